```python
import math
import jax, jax.numpy as jnp
from jax import lax
import numpy as np

D_MODEL = 1024
BATCH = 8
SEQ = 4096
DEPTH = 1

MEM_LEN = 256
EPS = 1e-6
ROPE_THETA = 10000.0
A_HEADS = 8
A_HEAD_DIM = 64
A_WIDTH = A_HEADS * A_HEAD_DIM
MOBA_BLOCK = 256
MOBA_TOPK = 3
Q_BLOCK = 128
R_HEADS = 4
R_QK_DIM = 128
R_V_DIM = 256
R_QK_WIDTH = R_HEADS * R_QK_DIM
R_V_WIDTH = R_HEADS * R_V_DIM
R_CHUNK = 128
X_HEADS = 4
X_HEAD_DIM = D_MODEL // X_HEADS
D_FF = -(-8 * D_MODEL // (3 * 256)) * 256
SPLIT_SIZES = (A_WIDTH, A_WIDTH, A_WIDTH, R_QK_WIDTH, R_QK_WIDTH, R_V_WIDTH, R_V_WIDTH, D_MODEL, D_MODEL)
IN_COLS = sum(SPLIT_SIZES)

kernel_name = "moba_retention_gated_hybrid"


def rms_norm(x, g):
    xf = x.astype(jnp.float32)
    y = xf * lax.rsqrt(jnp.mean(xf * xf, axis=-1, keepdims=True) + EPS)
    return (y * g.astype(jnp.float32)).astype(x.dtype)


def rotary(x, pos):
    d = x.shape[-1]
    inv = ROPE_THETA ** (-jnp.arange(0, d, 2, dtype=jnp.float32) / d)
    ang = pos.astype(jnp.float32)[:, None] * inv[None, :]
    cos, sin = jnp.cos(ang), jnp.sin(ang)
    xf = x.astype(jnp.float32)
    x1, x2 = xf[..., : d // 2], xf[..., d // 2:]
    return jnp.concatenate([x1 * cos - x2 * sin, x2 * cos + x1 * sin], axis=-1).astype(x.dtype)


def to_heads(t, n_heads, head_dim):
    b, s, _ = t.shape
    return t.reshape(b, s, n_heads, head_dim).transpose(0, 2, 1, 3)


def from_heads(t):
    b, h, s, d = t.shape
    return t.transpose(0, 2, 1, 3).reshape(b, s, h * d)


def moba_attention(q, k, v):
    B, H, S, dh = q.shape
    nb = -(-S // MOBA_BLOCK)
    pad = nb * MOBA_BLOCK - S
    kp = jnp.pad(k, ((0, 0), (0, 0), (0, pad), (0, 0)))
    vp = jnp.pad(v, ((0, 0), (0, 0), (0, pad), (0, 0)))
    kb = kp.reshape(B, H, nb, MOBA_BLOCK, dh)
    vb = vp.reshape(B, H, nb, MOBA_BLOCK, dh)
    n_sel = min(MOBA_TOPK, nb - 1)
    scale = dh ** -0.5
    nqc = S // Q_BLOCK
    pos = jnp.arange(S)
    q_blk = pos // MOBA_BLOCK

    qc = q.reshape(B, H, nqc, Q_BLOCK, dh).transpose(0, 2, 1, 3, 4).reshape(B * nqc, H, Q_BLOCK, dh)
    if n_sel > 0:
        kmean = jnp.mean(kb, axis=3, dtype=jnp.float32)
        bscore = jnp.einsum('bhsd,bhnd->bhsn', q.astype(jnp.float32), kmean)
        past = jnp.arange(nb)[None, :] < q_blk[:, None]
        bscore = jnp.where(past[None, None], bscore, -jnp.inf)
        _, sel = lax.top_k(bscore, n_sel)
        selc = sel.reshape(B, H, nqc, Q_BLOCK, n_sel).transpose(0, 2, 1, 3, 4).reshape(B * nqc, H, Q_BLOCK, n_sel)
    else:
        selc = jnp.zeros((B * nqc, H, Q_BLOCK, 0), jnp.int32)
    hidx = jnp.arange(H)[:, None, None]

    def step(args):
        sid, qb, selb = args
        bi = sid // nqc
        ci = sid % nqc
        kb_b = kb[bi]
        vb_b = vb[bi]
        qpos = ci * Q_BLOCK + jnp.arange(Q_BLOCK)
        own = (ci * Q_BLOCK) // MOBA_BLOCK
        k_own = lax.dynamic_index_in_dim(kb_b, own, axis=1, keepdims=False)
        v_own = lax.dynamic_index_in_dim(vb_b, own, axis=1, keepdims=False)
        kpos_own = own * MOBA_BLOCK + jnp.arange(MOBA_BLOCK)
        s_own = jnp.einsum('hqd,hkd->hqk', qb, k_own, preferred_element_type=jnp.float32) * scale
        s_own = jnp.where((kpos_own[None, :] <= qpos[:, None])[None], s_own, -jnp.inf)
        if n_sel > 0:
            k_sel = kb_b[hidx, selb]
            v_sel = vb_b[hidx, selb]
            s_sel = jnp.einsum('hqd,hqrkd->hqrk', qb, k_sel, preferred_element_type=jnp.float32) * scale
            valid = jnp.arange(n_sel)[None, :] < (qpos // MOBA_BLOCK)[:, None]
            s_sel = jnp.where(valid[None, :, :, None], s_sel, -jnp.inf)
            s = jnp.concatenate([s_sel.reshape(H, Q_BLOCK, n_sel * MOBA_BLOCK), s_own], axis=-1)
            p = jax.nn.softmax(s, axis=-1).astype(v.dtype)
            p_sel = p[..., : n_sel * MOBA_BLOCK].reshape(H, Q_BLOCK, n_sel, MOBA_BLOCK)
            p_own = p[..., n_sel * MOBA_BLOCK:]
            o = (jnp.einsum('hqrk,hqrkd->hqd', p_sel, v_sel)
                 + jnp.einsum('hqk,hkd->hqd', p_own, v_own))
        else:
            p_own = jax.nn.softmax(s_own, axis=-1).astype(v.dtype)
            o = jnp.einsum('hqk,hkd->hqd', p_own, v_own)
        return o.astype(q.dtype)

    out = lax.map(step, (jnp.arange(B * nqc), qc, selc))
    return out.reshape(B, nqc, H, Q_BLOCK, dh).transpose(0, 2, 1, 3, 4).reshape(B, H, S, dh)


def retention(q, k, v):
    B, H, S, dk = q.shape
    dv = v.shape[-1]
    C = R_CHUNK
    n = S // C
    q = q.astype(jnp.float32)
    k = k.astype(jnp.float32)
    v = v.astype(jnp.float32)
    log_g = jnp.log(1.0 - jnp.exp2(-5.0 - jnp.arange(H, dtype=jnp.float32)))
    i = jnp.arange(C, dtype=jnp.float32)
    diff = i[:, None] - i[None, :]
    dmask = jnp.where(diff >= 0, jnp.exp(log_g[:, None, None] * jnp.maximum(diff, 0.0)), 0.0)
    q_dec = jnp.exp(log_g[:, None] * (i + 1.0))
    k_dec = jnp.exp(log_g[:, None] * (C - 1.0 - i))
    chunk_dec = jnp.exp(log_g * C)

    qc = q.reshape(B, H, n, C, dk)
    kc = k.reshape(B, H, n, C, dk)
    vc = v.reshape(B, H, n, C, dv)
    att = jnp.einsum('bhnid,bhnjd->bhnij', qc, kc) * dmask[None, :, None]
    inner = jnp.einsum('bhnij,bhnje->bhnie', att, vc)
    kv = jnp.einsum('bhnjd,bhnje->bhnde', kc * k_dec[None, :, None, :, None], vc)

    def scan_fn(state, kv_n):
        return chunk_dec[None, :, None, None] * state + kv_n, state

    _, prev = lax.scan(scan_fn, jnp.zeros((B, H, dk, dv), jnp.float32), kv.transpose(2, 0, 1, 3, 4))
    prev = prev.transpose(1, 2, 0, 3, 4)
    cross = jnp.einsum('bhnid,bhnde->bhnie', qc * q_dec[None, :, None, :, None], prev)
    return (inner + cross).reshape(B, H, S, dv)


def head_group_norm(y):
    mu = jnp.mean(y, axis=-1, keepdims=True)
    var = jnp.mean(jnp.square(y - mu), axis=-1, keepdims=True)
    return (y - mu) * lax.rsqrt(var + EPS)


def cross_attention(h, m, w_q, w_kv, w_o):
    q = to_heads(h @ w_q, X_HEADS, X_HEAD_DIM)
    kv = m @ w_kv
    k = to_heads(kv[..., :D_MODEL], X_HEADS, X_HEAD_DIM)
    v = to_heads(kv[..., D_MODEL:], X_HEADS, X_HEAD_DIM)
    s = jnp.einsum('bhsd,bhmd->bhsm', q, k, preferred_element_type=jnp.float32) * (X_HEAD_DIM ** -0.5)
    p = jax.nn.softmax(s, axis=-1).astype(v.dtype)
    o = jnp.einsum('bhsm,bhmd->bhsd', p, v)
    return from_heads(o) @ w_o


def setup_inputs(seed: int = 0) -> dict:
    key = jax.random.key(seed)
    ks = jax.random.split(key, 20)

    def w(k, shape, fan_in):
        return jax.random.normal(k, shape, jnp.float32) * (fan_in ** -0.5)

    def gain(k, shape):
        return 1.0 + 0.02 * jax.random.normal(k, shape, jnp.float32)

    return {
        "x": jax.random.normal(ks[0], (BATCH, SEQ, D_MODEL), jnp.float32),
        "mem": jax.random.normal(ks[1], (BATCH, MEM_LEN, D_MODEL), jnp.float32),
        "norm_mix": gain(ks[2], (DEPTH, D_MODEL)),
        "w_in": w(ks[3], (DEPTH, D_MODEL, IN_COLS), D_MODEL),
        "w_branch_a": w(ks[4], (DEPTH, A_WIDTH, D_MODEL), A_WIDTH),
        "w_branch_b": w(ks[5], (DEPTH, R_V_WIDTH, D_MODEL), R_V_WIDTH),
        "w_out": w(ks[6], (DEPTH, D_MODEL, D_MODEL), D_MODEL),
        "norm_cross": gain(ks[7], (DEPTH, D_MODEL)),
        "norm_mem": gain(ks[8], (DEPTH, D_MODEL)),
        "w_xq": w(ks[9], (DEPTH, D_MODEL, D_MODEL), D_MODEL),
        "w_xkv": w(ks[10], (DEPTH, D_MODEL, 2 * D_MODEL), D_MODEL),
        "w_xo": w(ks[11], (DEPTH, D_MODEL, D_MODEL), D_MODEL),
        "norm_ffn": gain(ks[12], (DEPTH, D_MODEL)),
        "w_gate": w(ks[13], (DEPTH, D_MODEL, D_FF), D_MODEL),
        "w_up": w(ks[14], (DEPTH, D_MODEL, D_FF), D_MODEL),
        "w_down": w(ks[15], (DEPTH, D_FF, D_MODEL), D_FF),
        "norm_final": gain(ks[16], (D_MODEL,)),
    }


def reference(x, mem, norm_mix, w_in, w_branch_a, w_branch_b, w_out, norm_cross, norm_mem,
              w_xq, w_xkv, w_xo, norm_ffn, w_gate, w_up, w_down, norm_final):
    B, S, _ = x.shape
    pos = jnp.arange(S)
    split_points = [int(p) for p in np.cumsum(SPLIT_SIZES)[:-1]]
    for l in range(DEPTH):
        h = rms_norm(x, norm_mix[l])
        proj = h @ w_in[l]
        aq, ak, av, rq, rk, rv, rg, ga, gb = jnp.split(proj, split_points, axis=-1)

        aq = rotary(to_heads(aq, A_HEADS, A_HEAD_DIM), pos)
        ak = rotary(to_heads(ak, A_HEADS, A_HEAD_DIM), pos)
        av = to_heads(av, A_HEADS, A_HEAD_DIM)
        y_a = from_heads(moba_attention(aq, ak, av))

        rq = rotary(to_heads(rq, R_HEADS, R_QK_DIM), pos)
        rk = rotary(to_heads(rk, R_HEADS, R_QK_DIM), pos) * (R_QK_DIM ** -0.5)
        rv = to_heads(rv, R_HEADS, R_V_DIM)
        y_r = head_group_norm(retention(rq, rk, rv))
        y_b = from_heads(y_r).astype(x.dtype) * jax.nn.silu(rg)

        merged = (jax.nn.sigmoid(ga) * (y_a @ w_branch_a[l])
                  + jax.nn.sigmoid(gb) * (y_b @ w_branch_b[l]))
        x = x + merged @ w_out[l]

        x = x + cross_attention(rms_norm(x, norm_cross[l]), rms_norm(mem, norm_mem[l]),
                                w_xq[l], w_xkv[l], w_xo[l])

        h = rms_norm(x, norm_ffn[l])
        x = x + (jax.nn.silu(h @ w_gate[l]) * (h @ w_up[l])) @ w_down[l]
    return rms_norm(x, norm_final)
```

```python
import functools
import math

import numpy as np
import jax
import jax.numpy as jnp
from jax import lax
from jax.experimental import pallas as pl
from jax.experimental.pallas import tpu as pltpu

F32 = jnp.float32
BF16 = jnp.bfloat16

EPS = 1e-6
ROPE_THETA = 10000.0
A_HEADS = 8
A_HEAD_DIM = 64
A_WIDTH = A_HEADS * A_HEAD_DIM
MOBA_BLOCK = 256
MOBA_TOPK = 3
R_HEADS = 4
R_QK_DIM = 128
R_V_DIM = 256
R_QK_WIDTH = R_HEADS * R_QK_DIM
R_V_WIDTH = R_HEADS * R_V_DIM
R_CHUNK = 128
X_HEADS = 4
FF_CHUNK = 256
TOKEN_TILE = MOBA_BLOCK
NEG = -1e30
VMEM_LIMIT = 56 * 1024 * 1024


def _dot(a, b):
    return jnp.dot(a, b, preferred_element_type=F32)


def _rms(x, g):
    return x * lax.rsqrt(jnp.mean(x * x, axis=-1, keepdims=True) + EPS) * g


def _sigmoid(x):
    return 1.0 / (1.0 + jnp.exp(-x))


def _params(*sem):
    return pltpu.CompilerParams(dimension_semantics=sem, vmem_limit_bytes=VMEM_LIMIT)


def _full(shape):
    n = len(shape)
    return pl.BlockSpec(shape, lambda *_: (0,) * n)


def _qkv_kernel(x_ref, g_ref, wn_ref, wt_ref, cosa_ref, sina_ref, cosr_ref, sinr_ref,
                costa_ref, sinta_ref, costr_ref, sintr_ref, kdec_ref,
                k_ref, kmean_ref, qt_ref, vt_ref, rq_ref, rv_ref, rkt_ref, rktd_ref):
    h = _rms(x_ref[0], g_ref[...])
    hb = h.astype(BF16)
    ht = h.T.astype(BF16)
    t = h.shape[0]
    lane = lax.broadcasted_iota(jnp.int32, (t, 128), 1)
    first_half = (lane & 32) == 0

    ak = _dot(hb, wn_ref[:, 0:A_WIDTH])
    cosa, sina = cosa_ref[...], sina_ref[...]
    for c in range(A_WIDTH // 128):
        blk = ak[:, c * 128:(c + 1) * 128]
        rot = jnp.where(first_half, pltpu.roll(blk, 96, 1), pltpu.roll(blk, 32, 1))
        kr = blk * cosa + rot * sina
        k_ref[0, 0, :, c * 128:(c + 1) * 128] = kr.astype(BF16)
        kmean_ref[0, 0, :, c * 128:(c + 1) * 128] = jnp.mean(kr, axis=0, keepdims=True)

    rq = _dot(hb, wn_ref[:, A_WIDTH:A_WIDTH + R_QK_WIDTH])
    cosr, sinr = cosr_ref[...], sinr_ref[...]
    for c in range(R_HEADS):
        blk = rq[:, c * 128:(c + 1) * 128]
        rq_ref[0, :, c * 128:(c + 1) * 128] = (blk * cosr + pltpu.roll(blk, 64, 1) * sinr).astype(BF16)

    rv_ref[0] = _dot(hb, wn_ref[:, A_WIDTH + R_QK_WIDTH:]).astype(BF16)

    qt = _dot(wt_ref[0:A_WIDTH, :], ht)
    cost, sint = costa_ref[...], sinta_ref[...]
    half = A_HEAD_DIM // 2
    scale_a = A_HEAD_DIM ** -0.5
    for hd in range(A_HEADS):
        x1 = qt[hd * A_HEAD_DIM:hd * A_HEAD_DIM + half]
        x2 = qt[hd * A_HEAD_DIM + half:(hd + 1) * A_HEAD_DIM]
        qt_ref[0, 0, hd * A_HEAD_DIM:hd * A_HEAD_DIM + half, :] = ((x1 * cost - x2 * sint) * scale_a).astype(BF16)
        qt_ref[0, 0, hd * A_HEAD_DIM + half:(hd + 1) * A_HEAD_DIM, :] = ((x2 * cost + x1 * sint) * scale_a).astype(BF16)

    vt_ref[0, 0] = _dot(wt_ref[A_WIDTH:2 * A_WIDTH, :], ht).astype(BF16)

    rkt = _dot(wt_ref[2 * A_WIDTH:, :], ht)
    cost, sint = costr_ref[...], sintr_ref[...]
    half = R_QK_DIM // 2
    scale_r = R_QK_DIM ** -0.5
    for hd in range(R_HEADS):
        x1 = rkt[hd * R_QK_DIM:hd * R_QK_DIM + half]
        x2 = rkt[hd * R_QK_DIM + half:(hd + 1) * R_QK_DIM]
        o1 = (x1 * cost - x2 * sint) * scale_r
        o2 = (x2 * cost + x1 * sint) * scale_r
        dec = kdec_ref[hd:hd + 1, :]
        rkt_ref[0, 0, hd * R_QK_DIM:hd * R_QK_DIM + half, :] = o1.astype(BF16)
        rkt_ref[0, 0, hd * R_QK_DIM + half:(hd + 1) * R_QK_DIM, :] = o2.astype(BF16)
        rktd_ref[0, 0, hd * R_QK_DIM:hd * R_QK_DIM + half, :] = (o1 * dec).astype(BF16)
        rktd_ref[0, 0, hd * R_QK_DIM + half:(hd + 1) * R_QK_DIM, :] = (o2 * dec).astype(BF16)


def _qkv(x, g, w_nat, w_t, tabs):
    b, s, d = x.shape
    t = TOKEN_TILE
    nb = s // t
    tok = lambda w: pl.BlockSpec((1, t, w), lambda bi, i: (bi, i, 0))
    blk_t = lambda r: pl.BlockSpec((1, 1, r, t), lambda bi, i: (bi, i, 0, 0))
    nat_tab = pl.BlockSpec((t, 128), lambda bi, i: (i, 0))
    tr_tab = lambda r: pl.BlockSpec((r, t), lambda bi, i: (0, i))
    return pl.pallas_call(
        _qkv_kernel,
        grid=(b, nb),
        in_specs=[tok(d), _full((1, d)), _full(w_nat.shape), _full(w_t.shape),
                  nat_tab, nat_tab, nat_tab, nat_tab,
                  tr_tab(A_HEAD_DIM // 2), tr_tab(A_HEAD_DIM // 2),
                  tr_tab(R_QK_DIM // 2), tr_tab(R_QK_DIM // 2), tr_tab(8)],
        out_specs=[pl.BlockSpec((1, 1, t, A_WIDTH), lambda bi, i: (bi, i, 0, 0)),
                   pl.BlockSpec((1, 1, 1, A_WIDTH), lambda bi, i: (bi, i, 0, 0)),
                   blk_t(A_WIDTH), blk_t(A_WIDTH), tok(R_QK_WIDTH), tok(R_V_WIDTH),
                   blk_t(R_QK_WIDTH), blk_t(R_QK_WIDTH)],
        out_shape=[jax.ShapeDtypeStruct((b, nb, t, A_WIDTH), BF16),
                   jax.ShapeDtypeStruct((b, nb, 1, A_WIDTH), F32),
                   jax.ShapeDtypeStruct((b, nb, A_WIDTH, t), BF16),
                   jax.ShapeDtypeStruct((b, nb, A_WIDTH, t), BF16),
                   jax.ShapeDtypeStruct((b, s, R_QK_WIDTH), BF16),
                   jax.ShapeDtypeStruct((b, s, R_V_WIDTH), BF16),
                   jax.ShapeDtypeStruct((b, nb, R_QK_WIDTH, t), BF16),
                   jax.ShapeDtypeStruct((b, nb, R_QK_WIDTH, t), BF16)],
        compiler_params=_params("parallel", "parallel"),
        name="qkv",
    )(x, g, w_nat, w_t, *tabs)


def _moba_kernel(qt_ref, k_ref, vt_ref, km_ref, o_ref, sel_ref, ot_ref):
    i = pl.program_id(1)
    nb = km_ref.shape[1]
    t = MOBA_BLOCK
    row = lax.broadcasted_iota(jnp.int32, (2 * A_HEAD_DIM, t), 0)
    blk = lax.broadcasted_iota(jnp.int32, (nb, t), 0)
    causal = (lax.broadcasted_iota(jnp.int32, (t, t), 0) <= lax.broadcasted_iota(jnp.int32, (t, t), 1))

    for p in range(A_HEADS // 2):
        lanes = slice(p * 128, (p + 1) * 128)
        qt_pair = qt_ref[0, 0, lanes, :]
        km = km_ref[0, :, lanes]
        km_hi = km.astype(BF16)
        km_lo = (km - km_hi.astype(F32)).astype(BF16)
        for hh in range(2):
            hd = 2 * p + hh
            mine = (row >= hh * A_HEAD_DIM) & (row < (hh + 1) * A_HEAD_DIM)
            qt = jnp.where(mine, qt_pair, jnp.zeros_like(qt_pair))

            bs = _dot(km_hi, qt) + _dot(km_lo, qt)
            cnt = jnp.zeros((nb, t), F32)
            for m in range(nb):
                r = bs[m:m + 1, :]
                beats = (r > bs) | ((r == bs) & (m < blk))
                cnt = cnt + jnp.where(beats & (m < i), 1.0, 0.0)
            sel = (blk < i) & (cnt < MOBA_TOPK)
            sel_ref[...] = jnp.where(sel, 0.0, NEG)

            head_rows = slice(hd * A_HEAD_DIM, (hd + 1) * A_HEAD_DIM)
            st = jnp.where(causal, _dot(k_ref[0, i, :, lanes], qt), NEG)
            m0 = jnp.max(st, axis=0, keepdims=True)
            pt = jnp.exp(st - m0)
            l0 = jnp.sum(pt, axis=0, keepdims=True)
            acc0 = _dot(vt_ref[0, i, head_rows, :], pt.astype(BF16))

            def body(j, carry, lanes=lanes, head_rows=head_rows, qt=qt):
                m, l, acc = carry
                s = _dot(k_ref[0, j, :, lanes], qt) + sel_ref[pl.ds(j, 1), :]
                m_new = jnp.maximum(m, jnp.max(s, axis=0, keepdims=True))
                alpha = jnp.exp(m - m_new)
                pj = jnp.exp(s - m_new)
                l = alpha * l + jnp.sum(pj, axis=0, keepdims=True)
                acc = alpha * acc + _dot(vt_ref[0, j, head_rows, :], pj.astype(BF16))
                return m_new, l, acc

            _, l, acc = lax.fori_loop(0, i, body, (m0, l0, acc0))
            ot_ref[head_rows, :] = acc / l
    o_ref[0] = ot_ref[...].T.astype(BF16)


def _moba(qt, k, vt, kmean):
    b, nb, t, w = k.shape
    return pl.pallas_call(
        _moba_kernel,
        grid=(b, nb),
        in_specs=[pl.BlockSpec((1, 1, w, t), lambda bi, i: (bi, i, 0, 0)),
                  pl.BlockSpec((1, nb, t, w), lambda bi, i: (bi, 0, 0, 0)),
                  pl.BlockSpec((1, nb, w, t), lambda bi, i: (bi, 0, 0, 0)),
                  pl.BlockSpec((1, nb, w), lambda bi, i: (bi, 0, 0))],
        out_specs=pl.BlockSpec((1, t, w), lambda bi, i: (bi, i, 0)),
        out_shape=jax.ShapeDtypeStruct((b, nb * t, w), BF16),
        scratch_shapes=[pltpu.VMEM((nb, t), F32), pltpu.VMEM((w, t), F32)],
        compiler_params=_params("parallel", "parallel"),
        name="moba",
    )(qt, k, vt, kmean)


def _retention_kernel(chunk_dec, rq_ref, rkt_ref, rktd_ref, rv_ref, dmask_ref, qdec_ref, o_ref, state_ref):
    @pl.when(pl.program_id(1) == 0)
    def _():
        state_ref[...] = jnp.zeros_like(state_ref)

    c = R_CHUNK
    for ci in range(TOKEN_TILE // c):
        rows = slice(ci * c, (ci + 1) * c)
        for hd in range(R_HEADS):
            qk = slice(hd * R_QK_DIM, (hd + 1) * R_QK_DIM)
            vv = slice(hd * R_V_DIM, (hd + 1) * R_V_DIM)
            q = rq_ref[0, rows, qk]
            v = rv_ref[0, rows, vv]
            att = _dot(q, rkt_ref[0, 0, qk, rows]) * dmask_ref[hd]
            st = state_ref[hd]
            y = _dot(att.astype(BF16), v) + _dot(q, st.astype(BF16)) * qdec_ref[hd]
            state_ref[hd] = chunk_dec[hd] * st + _dot(rktd_ref[0, 0, qk, rows], v)
            mu = jnp.mean(y, axis=-1, keepdims=True)
            yc = y - mu
            var = jnp.mean(yc * yc, axis=-1, keepdims=True)
            o_ref[0, rows, vv] = (yc * lax.rsqrt(var + EPS)).astype(BF16)


def _retention(rq, rkt, rktd, rv, dmask, qdec, chunk_dec):
    b, s, _ = rq.shape
    t = TOKEN_TILE
    tok = lambda w: pl.BlockSpec((1, t, w), lambda bi, i: (bi, i, 0))
    blk_t = pl.BlockSpec((1, 1, R_QK_WIDTH, t), lambda bi, i: (bi, i, 0, 0))
    return pl.pallas_call(
        functools.partial(_retention_kernel, chunk_dec),
        grid=(b, s // t),
        in_specs=[tok(R_QK_WIDTH), blk_t, blk_t, tok(R_V_WIDTH), _full(dmask.shape), _full(qdec.shape)],
        out_specs=tok(R_V_WIDTH),
        out_shape=jax.ShapeDtypeStruct((b, s, R_V_WIDTH), BF16),
        scratch_shapes=[pltpu.VMEM((R_HEADS, R_QK_DIM, R_V_DIM), F32)],
        compiler_params=_params("parallel", "arbitrary"),
        name="retention",
    )(rq, rkt, rktd, rv, dmask, qdec)


def _merge_kernel(x_ref, g_ref, gn_ref, ya_ref, wg_ref, wa_ref, wb_ref, wo_ref, o_ref):
    x = x_ref[...]
    d = x.shape[1]
    hb = _rms(x, g_ref[...]).astype(BF16)
    rg = _dot(hb, wg_ref[:, 0:R_V_WIDTH])
    yb = gn_ref[...].astype(F32) * (rg * _sigmoid(rg))
    tb = _dot(yb.astype(BF16), wb_ref[...])
    ta = _dot(ya_ref[...], wa_ref[...])
    ga = _dot(hb, wg_ref[:, R_V_WIDTH:R_V_WIDTH + d])
    gb = _dot(hb, wg_ref[:, R_V_WIDTH + d:])
    merged = _sigmoid(ga) * ta + _sigmoid(gb) * tb
    o_ref[...] = x + _dot(merged.astype(BF16), wo_ref[...])


def _merge(x2, g, gn2, ya2, wg, wa, wb, wo):
    n, d = x2.shape
    t = TOKEN_TILE
    tok = lambda w: pl.BlockSpec((t, w), lambda i: (i, 0))
    return pl.pallas_call(
        _merge_kernel,
        grid=(n // t,),
        in_specs=[tok(d), _full((1, d)), tok(R_V_WIDTH), tok(A_WIDTH),
                  _full(wg.shape), _full(wa.shape), _full(wb.shape), _full(wo.shape)],
        out_specs=tok(d),
        out_shape=jax.ShapeDtypeStruct((n, d), F32),
        compiler_params=_params("parallel"),
        name="merge",
    )(x2, g, gn2, ya2, wg, wa, wb, wo)


def _memkv_kernel(m_ref, g_ref, w_ref, kt_ref, v_ref):
    d = m_ref.shape[2]
    mb = _rms(m_ref[0], g_ref[...]).astype(BF16)
    kt_ref[0] = _dot(mb, w_ref[:, 0:d]).T.astype(BF16)
    v_ref[0] = _dot(mb, w_ref[:, d:]).astype(BF16)


def _memkv(mem, g, w_kv):
    b, m, d = mem.shape
    return pl.pallas_call(
        _memkv_kernel,
        grid=(b,),
        in_specs=[pl.BlockSpec((1, m, d), lambda bi: (bi, 0, 0)), _full((1, d)), _full(w_kv.shape)],
        out_specs=[pl.BlockSpec((1, d, m), lambda bi: (bi, 0, 0)), pl.BlockSpec((1, m, d), lambda bi: (bi, 0, 0))],
        out_shape=[jax.ShapeDtypeStruct((b, d, m), BF16), jax.ShapeDtypeStruct((b, m, d), BF16)],
        compiler_params=_params("parallel"),
        name="memkv",
    )(mem, g, w_kv)


def _cross_kernel(x_ref, g_ref, kt_ref, v_ref, wq_ref, wo_ref, o_ref, cat_ref):
    x = x_ref[0]
    d = x.shape[1]
    dh = d // X_HEADS
    hb = _rms(x, g_ref[...]).astype(BF16)
    q = (_dot(hb, wq_ref[...]) * dh ** -0.5).astype(BF16)
    for hd in range(X_HEADS):
        cols = slice(hd * dh, (hd + 1) * dh)
        s = _dot(q[:, cols], kt_ref[0, cols, :])
        p = jnp.exp(s - jnp.max(s, axis=-1, keepdims=True))
        p = p / jnp.sum(p, axis=-1, keepdims=True)
        cat_ref[:, cols] = _dot(p.astype(BF16), v_ref[0, :, cols]).astype(BF16)
    o_ref[0] = x + _dot(cat_ref[...], wo_ref[...])


def _cross(x, g, kt, v, wq, wo):
    b, s, d = x.shape
    m = v.shape[1]
    t = TOKEN_TILE
    tok = pl.BlockSpec((1, t, d), lambda bi, i: (bi, i, 0))
    return pl.pallas_call(
        _cross_kernel,
        grid=(b, s // t),
        in_specs=[tok, _full((1, d)), pl.BlockSpec((1, d, m), lambda bi, i: (bi, 0, 0)),
                  pl.BlockSpec((1, m, d), lambda bi, i: (bi, 0, 0)), _full(wq.shape), _full(wo.shape)],
        out_specs=tok,
        out_shape=jax.ShapeDtypeStruct((b, s, d), F32),
        scratch_shapes=[pltpu.VMEM((t, d), BF16)],
        compiler_params=_params("parallel", "parallel"),
        name="cross",
    )(x, g, kt, v, wq, wo)


def _ffn_kernel(x_ref, g_ref, wg_ref, wu_ref, wd_ref, gf_ref, o_ref):
    x = x_ref[...]
    hb = _rms(x, g_ref[...]).astype(BF16)
    acc = x
    for c in range(wg_ref.shape[1] // FF_CHUNK):
        cols = slice(c * FF_CHUNK, (c + 1) * FF_CHUNK)
        gate = _dot(hb, wg_ref[:, cols])
        act = gate * _sigmoid(gate) * _dot(hb, wu_ref[:, cols])
        acc = acc + _dot(act.astype(BF16), wd_ref[cols, :])
    o_ref[...] = _rms(acc, gf_ref[...])


def _ffn(x2, g, wg, wu, wd, gf):
    n, d = x2.shape
    t = TOKEN_TILE
    tok = pl.BlockSpec((t, d), lambda i: (i, 0))
    return pl.pallas_call(
        _ffn_kernel,
        grid=(n // t,),
        in_specs=[tok, _full((1, d)), _full(wg.shape), _full(wu.shape), _full(wd.shape), _full((1, d))],
        out_specs=tok,
        out_shape=jax.ShapeDtypeStruct((n, d), F32),
        compiler_params=_params("parallel"),
        name="ffn",
    )(x2, g, wg, wu, wd, gf)


def _rope_tables(s):
    def cs(dim):
        inv = ROPE_THETA ** (-jnp.arange(0, dim, 2, dtype=F32) / dim)
        ang = jnp.arange(s).astype(F32)[:, None] * inv[None, :]
        return jnp.cos(ang), jnp.sin(ang)

    ca, sa = cs(A_HEAD_DIM)
    cr, sr = cs(R_QK_DIM)
    return (jnp.concatenate([ca, ca, ca, ca], axis=1), jnp.concatenate([-sa, sa, -sa, sa], axis=1),
            jnp.concatenate([cr, cr], axis=1), jnp.concatenate([-sr, sr], axis=1),
            ca.T, sa.T, cr.T, sr.T)


def _decay_tables(s):
    c = R_CHUNK
    log_g = np.log(1.0 - np.exp2(-5.0 - np.arange(R_HEADS, dtype=np.float64)))
    i = np.arange(c, dtype=np.float64)
    diff = i[:, None] - i[None, :]
    dmask = np.where(diff >= 0, np.exp(log_g[:, None, None] * np.maximum(diff, 0.0)), 0.0)
    q_dec = np.exp(log_g[:, None] * (i + 1.0))
    k_dec = np.exp(log_g[:, None] * (c - 1.0 - i))
    kdec_t = np.zeros((8, s), np.float64)
    kdec_t[:R_HEADS] = np.tile(k_dec, (1, s // c))
    qdec = np.broadcast_to(q_dec[:, :, None], (R_HEADS, c, R_V_DIM))
    chunk_dec = tuple(float(v) for v in np.exp(log_g * c))
    return (jnp.asarray(dmask, F32), jnp.asarray(qdec, F32), jnp.asarray(kdec_t, F32), chunk_dec)


def kernel(x, mem, norm_mix, w_in, w_branch_a, w_branch_b, w_out, norm_cross, norm_mem,
           w_xq, w_xkv, w_xo, norm_ffn, w_gate, w_up, w_down, norm_final):
    b, s, d = x.shape
    assert s % TOKEN_TILE == 0 and d % 128 == 0
    rope = _rope_tables(s)
    dmask, qdec, kdec_t, chunk_dec = _decay_tables(s)
    row = lambda v: v.reshape(1, d)
    o_aq, o_ak, o_av = 0, A_WIDTH, 2 * A_WIDTH
    o_rq = 3 * A_WIDTH
    o_rk = o_rq + R_QK_WIDTH
    o_rv = o_rk + R_QK_WIDTH
    o_g = o_rv + R_V_WIDTH

    for l in range(w_in.shape[0]):
        w = w_in[l]
        w_nat = jnp.concatenate([w[:, o_ak:o_av], w[:, o_rq:o_rk], w[:, o_rv:o_g]], axis=1).astype(BF16)
        w_t = jnp.concatenate([w[:, o_aq:o_ak], w[:, o_av:o_rq], w[:, o_rk:o_rv]], axis=1).T.astype(BF16)
        k_a, kmean, qt_a, vt_a, rq, rv, rkt, rktd = _qkv(x, row(norm_mix[l]), w_nat, w_t, rope + (kdec_t,))
        y_a = _moba(qt_a, k_a, vt_a, kmean.reshape(b, s // MOBA_BLOCK, A_WIDTH))
        gn = _retention(rq, rkt, rktd, rv, dmask, qdec, chunk_dec)
        x = _merge(x.reshape(b * s, d), row(norm_mix[l]), gn.reshape(b * s, R_V_WIDTH),
                   y_a.reshape(b * s, A_WIDTH), w[:, o_g:].astype(BF16), w_branch_a[l].astype(BF16),
                   w_branch_b[l].astype(BF16), w_out[l].astype(BF16)).reshape(b, s, d)
        kt_m, v_m = _memkv(mem, row(norm_mem[l]), w_xkv[l].astype(BF16))
        x = _cross(x, row(norm_cross[l]), kt_m, v_m, w_xq[l].astype(BF16), w_xo[l].astype(BF16))
        last = l == w_in.shape[0] - 1
        gf = row(norm_final) if last else jnp.ones((1, d), F32)
        x = _ffn(x.reshape(b * s, d), row(norm_ffn[l]), w_gate[l].astype(BF16), w_up[l].astype(BF16),
                 w_down[l].astype(BF16), gf).reshape(b, s, d)
        if not last:
            raise NotImplementedError("the fused final norm assumes a single layer")
    return x
```

```python
import functools
import math

import numpy as np
import jax
import jax.numpy as jnp
from jax import lax
from jax.experimental import pallas as pl
from jax.experimental.pallas import tpu as pltpu

F32 = jnp.float32
BF16 = jnp.bfloat16

EPS = 1e-6
ROPE_THETA = 10000.0
A_HEADS = 8
A_HEAD_DIM = 64
A_WIDTH = A_HEADS * A_HEAD_DIM
MOBA_BLOCK = 256
MOBA_TOPK = 3
R_HEADS = 4
R_QK_DIM = 128
R_V_DIM = 256
R_QK_WIDTH = R_HEADS * R_QK_DIM
R_V_WIDTH = R_HEADS * R_V_DIM
R_CHUNK = 128
X_HEADS = 4
FF_CHUNK = 256
TOKEN_TILE = MOBA_BLOCK
NEG = -1e30
MOBA_LOOKAHEAD = 3
MOBA_ONES_ROWS = 16
VMEM_LIMIT = 56 * 1024 * 1024


def _dot(a, b):
    return jnp.dot(a, b, preferred_element_type=F32)


def _rms(x, g):
    return x * lax.rsqrt(jnp.mean(x * x, axis=-1, keepdims=True) + EPS) * g


def _sigmoid(x):
    return 1.0 / (1.0 + jnp.exp(-x))


def _params(*sem):
    return pltpu.CompilerParams(dimension_semantics=sem, vmem_limit_bytes=VMEM_LIMIT)


def _full(shape):
    n = len(shape)
    return pl.BlockSpec(shape, lambda *_: (0,) * n)


def _qkv_kernel(x_ref, g_ref, wn_ref, wt_ref, cosa_ref, sina_ref, cosr_ref, sinr_ref,
                costa_ref, sinta_ref, costr_ref, sintr_ref, kdec_ref,
                k_ref, kmean_ref, qt_ref, vt_ref, rq_ref, rv_ref, rkt_ref, rktd_ref):
    h = _rms(x_ref[0], g_ref[...])
    hb = h.astype(BF16)
    ht = h.T.astype(BF16)
    t = h.shape[0]
    lane = lax.broadcasted_iota(jnp.int32, (t, 128), 1)
    first_half = (lane & 32) == 0

    ak = _dot(hb, wn_ref[:, 0:A_WIDTH])
    cosa, sina = cosa_ref[...], sina_ref[...]
    for c in range(A_WIDTH // 128):
        blk = ak[:, c * 128:(c + 1) * 128]
        rot = jnp.where(first_half, pltpu.roll(blk, 96, 1), pltpu.roll(blk, 32, 1))
        kr = blk * cosa + rot * sina
        k_ref[0, 0, :, c * 128:(c + 1) * 128] = kr.astype(BF16)
        kmean_ref[0, 0, :, c * 128:(c + 1) * 128] = jnp.mean(kr, axis=0, keepdims=True)

    rq = _dot(hb, wn_ref[:, A_WIDTH:A_WIDTH + R_QK_WIDTH])
    cosr, sinr = cosr_ref[...], sinr_ref[...]
    for c in range(R_HEADS):
        blk = rq[:, c * 128:(c + 1) * 128]
        rq_ref[0, :, c * 128:(c + 1) * 128] = (blk * cosr + pltpu.roll(blk, 64, 1) * sinr).astype(BF16)

    rv_ref[0] = _dot(hb, wn_ref[:, A_WIDTH + R_QK_WIDTH:]).astype(BF16)

    qt = _dot(wt_ref[0:A_WIDTH, :], ht)
    cost, sint = costa_ref[...], sinta_ref[...]
    half = A_HEAD_DIM // 2
    scale_a = A_HEAD_DIM ** -0.5
    for hd in range(A_HEADS):
        x1 = qt[hd * A_HEAD_DIM:hd * A_HEAD_DIM + half]
        x2 = qt[hd * A_HEAD_DIM + half:(hd + 1) * A_HEAD_DIM]
        qt_ref[0, 0, hd * A_HEAD_DIM:hd * A_HEAD_DIM + half, :] = ((x1 * cost - x2 * sint) * scale_a).astype(BF16)
        qt_ref[0, 0, hd * A_HEAD_DIM + half:(hd + 1) * A_HEAD_DIM, :] = ((x2 * cost + x1 * sint) * scale_a).astype(BF16)

    vt_ref[0, 0] = _dot(wt_ref[A_WIDTH:2 * A_WIDTH, :], ht).astype(BF16)

    rkt = _dot(wt_ref[2 * A_WIDTH:, :], ht)
    cost, sint = costr_ref[...], sintr_ref[...]
    half = R_QK_DIM // 2
    scale_r = R_QK_DIM ** -0.5
    for hd in range(R_HEADS):
        x1 = rkt[hd * R_QK_DIM:hd * R_QK_DIM + half]
        x2 = rkt[hd * R_QK_DIM + half:(hd + 1) * R_QK_DIM]
        o1 = (x1 * cost - x2 * sint) * scale_r
        o2 = (x2 * cost + x1 * sint) * scale_r
        dec = kdec_ref[hd:hd + 1, :]
        rkt_ref[0, 0, hd * R_QK_DIM:hd * R_QK_DIM + half, :] = o1.astype(BF16)
        rkt_ref[0, 0, hd * R_QK_DIM + half:(hd + 1) * R_QK_DIM, :] = o2.astype(BF16)
        rktd_ref[0, 0, hd * R_QK_DIM:hd * R_QK_DIM + half, :] = (o1 * dec).astype(BF16)
        rktd_ref[0, 0, hd * R_QK_DIM + half:(hd + 1) * R_QK_DIM, :] = (o2 * dec).astype(BF16)


def _qkv(x, g, w_nat, w_t, tabs):
    b, s, d = x.shape
    t = TOKEN_TILE
    nb = s // t
    tok = lambda w: pl.BlockSpec((1, t, w), lambda bi, i: (bi, i, 0))
    blk_t = lambda r: pl.BlockSpec((1, 1, r, t), lambda bi, i: (bi, i, 0, 0))
    nat_tab = pl.BlockSpec((t, 128), lambda bi, i: (i, 0))
    tr_tab = lambda r: pl.BlockSpec((r, t), lambda bi, i: (0, i))
    return pl.pallas_call(
        _qkv_kernel,
        grid=(b, nb),
        in_specs=[tok(d), _full((1, d)), _full(w_nat.shape), _full(w_t.shape),
                  nat_tab, nat_tab, nat_tab, nat_tab,
                  tr_tab(A_HEAD_DIM // 2), tr_tab(A_HEAD_DIM // 2),
                  tr_tab(R_QK_DIM // 2), tr_tab(R_QK_DIM // 2), tr_tab(8)],
        out_specs=[pl.BlockSpec((1, 1, t, A_WIDTH), lambda bi, i: (bi, i, 0, 0)),
                   pl.BlockSpec((1, 1, 1, A_WIDTH), lambda bi, i: (bi, i, 0, 0)),
                   blk_t(A_WIDTH), blk_t(A_WIDTH), tok(R_QK_WIDTH), tok(R_V_WIDTH),
                   blk_t(R_QK_WIDTH), blk_t(R_QK_WIDTH)],
        out_shape=[jax.ShapeDtypeStruct((b, nb, t, A_WIDTH), BF16),
                   jax.ShapeDtypeStruct((b, nb, 1, A_WIDTH), F32),
                   jax.ShapeDtypeStruct((b, nb, A_WIDTH, t), BF16),
                   jax.ShapeDtypeStruct((b, nb, A_WIDTH, t), BF16),
                   jax.ShapeDtypeStruct((b, s, R_QK_WIDTH), BF16),
                   jax.ShapeDtypeStruct((b, s, R_V_WIDTH), BF16),
                   jax.ShapeDtypeStruct((b, nb, R_QK_WIDTH, t), BF16),
                   jax.ShapeDtypeStruct((b, nb, R_QK_WIDTH, t), BF16)],
        compiler_params=_params("parallel", "parallel"),
        name="qkv",
    )(x, g, w_nat, w_t, *tabs)


def _moba_kernel(qt_ref, k_ref, vt_ref, km_ref, o_ref, qm_ref, sel_ref, m_ref, acc_ref, ot_ref):
    i = pl.program_id(1)
    nb = km_ref.shape[1]
    t = MOBA_BLOCK
    dh = A_HEAD_DIM
    row = lax.broadcasted_iota(jnp.int32, (2 * dh, t), 0)
    blk = lax.broadcasted_iota(jnp.int32, (nb, t), 0)
    past = blk < i
    causal = (lax.broadcasted_iota(jnp.int32, (t, t), 0) <= lax.broadcasted_iota(jnp.int32, (t, t), 1))
    ones = jnp.ones((MOBA_ONES_ROWS, t), BF16)
    lanes = lambda hd: slice((hd // 2) * 128, (hd // 2 + 1) * 128)
    rows = lambda hd: slice(hd * dh, (hd + 1) * dh)

    for hd in range(A_HEADS):
        hh = hd % 2
        qt_pair = qt_ref[0, 0, lanes(hd), :]
        mine = (row >= hh * dh) & (row < (hh + 1) * dh)
        qt = jnp.where(mine, qt_pair, jnp.zeros_like(qt_pair))
        qm_ref[hd] = qt

        km = km_ref[0, :, lanes(hd)]
        km_hi = km.astype(BF16)
        km_lo = (km - km_hi.astype(F32)).astype(BF16)
        bs = jnp.where(past, _dot(km_hi, qt) + _dot(km_lo, qt), -jnp.inf)
        cnt = jnp.zeros((nb, t), F32)
        for m in range(nb):
            r = bs[m:m + 1, :]
            tie = jnp.where(blk > m, 1.0, 0.0)
            cnt = cnt + jnp.where(r > bs, 1.0, jnp.where(r == bs, tie, 0.0))
        sel_ref[hd] = jnp.where(past & (cnt < MOBA_TOPK), 0.0, NEG)

    def pv(j, hd, p):
        vt_aug = jnp.concatenate([vt_ref[0, j, rows(hd), :], ones], axis=0)
        return _dot(vt_aug, p.astype(BF16))

    def own_scores(hd):
        return jnp.where(causal, _dot(k_ref[0, i, :, lanes(hd)], qm_ref[hd]), NEG)

    def own_update(hd, s):
        m0 = jnp.max(s, axis=0, keepdims=True)
        m_ref[hd] = jnp.broadcast_to(m0, (8, t))
        acc_ref[hd] = pv(i, hd, jnp.exp(s - m0))

    def past_scores(j, hd):
        return _dot(k_ref[0, j, :, lanes(hd)], qm_ref[hd]) + sel_ref[hd, pl.ds(j, 1), :]

    def past_update(j, hd, s):
        m_old = m_ref[hd]
        m_new = jnp.maximum(m_old, jnp.max(s, axis=0, keepdims=True))
        alpha = jnp.exp(m_old - m_new)
        m_ref[hd] = m_new
        acc_ref[hd] = alpha[0:1, :] * acc_ref[hd] + pv(j, hd, jnp.exp(s - m_new[0:1, :]))

    def pipelined(scores, update):
        s = {}
        for step in range(A_HEADS + MOBA_LOOKAHEAD):
            if step < A_HEADS:
                s[step] = scores(step)
            if step >= MOBA_LOOKAHEAD:
                update(step - MOBA_LOOKAHEAD, s.pop(step - MOBA_LOOKAHEAD))

    pipelined(own_scores, own_update)

    def body(j, carry):
        pipelined(functools.partial(past_scores, j), functools.partial(past_update, j))
        return carry

    lax.fori_loop(0, i, body, 0)
    for hd in range(A_HEADS):
        ot_ref[rows(hd), :] = acc_ref[hd, 0:dh, :] / acc_ref[hd, dh:dh + 1, :]
    o_ref[0] = ot_ref[...].T.astype(BF16)


def _moba(qt, k, vt, kmean):
    b, nb, t, w = k.shape
    return pl.pallas_call(
        _moba_kernel,
        grid=(b, nb),
        in_specs=[pl.BlockSpec((1, 1, w, t), lambda bi, i: (bi, i, 0, 0)),
                  pl.BlockSpec((1, nb, t, w), lambda bi, i: (bi, 0, 0, 0)),
                  pl.BlockSpec((1, nb, w, t), lambda bi, i: (bi, 0, 0, 0)),
                  pl.BlockSpec((1, nb, w), lambda bi, i: (bi, 0, 0))],
        out_specs=pl.BlockSpec((1, t, w), lambda bi, i: (bi, i, 0)),
        out_shape=jax.ShapeDtypeStruct((b, nb * t, w), BF16),
        scratch_shapes=[pltpu.VMEM((A_HEADS, 2 * A_HEAD_DIM, t), BF16),
                        pltpu.VMEM((A_HEADS, nb, t), F32),
                        pltpu.VMEM((A_HEADS, 8, t), F32),
                        pltpu.VMEM((A_HEADS, A_HEAD_DIM + MOBA_ONES_ROWS, t), F32),
                        pltpu.VMEM((w, t), F32)],
        compiler_params=_params("parallel", "parallel"),
        name="moba",
    )(qt, k, vt, kmean)


def _retention_kernel(chunk_dec, rq_ref, rkt_ref, rktd_ref, rv_ref, dmask_ref, qdec_ref, o_ref, state_ref):
    @pl.when(pl.program_id(1) == 0)
    def _():
        state_ref[...] = jnp.zeros_like(state_ref)

    c = R_CHUNK
    for ci in range(TOKEN_TILE // c):
        rows = slice(ci * c, (ci + 1) * c)
        for hd in range(R_HEADS):
            qk = slice(hd * R_QK_DIM, (hd + 1) * R_QK_DIM)
            vv = slice(hd * R_V_DIM, (hd + 1) * R_V_DIM)
            q = rq_ref[0, rows, qk]
            v = rv_ref[0, rows, vv]
            att = _dot(q, rkt_ref[0, 0, qk, rows]) * dmask_ref[hd]
            st = state_ref[hd]
            y = _dot(att.astype(BF16), v) + _dot(q, st.astype(BF16)) * qdec_ref[hd]
            state_ref[hd] = chunk_dec[hd] * st + _dot(rktd_ref[0, 0, qk, rows], v)
            mu = jnp.mean(y, axis=-1, keepdims=True)
            yc = y - mu
            var = jnp.mean(yc * yc, axis=-1, keepdims=True)
            o_ref[0, rows, vv] = (yc * lax.rsqrt(var + EPS)).astype(BF16)


def _retention(rq, rkt, rktd, rv, dmask, qdec, chunk_dec):
    b, s, _ = rq.shape
    t = TOKEN_TILE
    tok = lambda w: pl.BlockSpec((1, t, w), lambda bi, i: (bi, i, 0))
    blk_t = pl.BlockSpec((1, 1, R_QK_WIDTH, t), lambda bi, i: (bi, i, 0, 0))
    return pl.pallas_call(
        functools.partial(_retention_kernel, chunk_dec),
        grid=(b, s // t),
        in_specs=[tok(R_QK_WIDTH), blk_t, blk_t, tok(R_V_WIDTH), _full(dmask.shape), _full(qdec.shape)],
        out_specs=tok(R_V_WIDTH),
        out_shape=jax.ShapeDtypeStruct((b, s, R_V_WIDTH), BF16),
        scratch_shapes=[pltpu.VMEM((R_HEADS, R_QK_DIM, R_V_DIM), F32)],
        compiler_params=_params("parallel", "arbitrary"),
        name="retention",
    )(rq, rkt, rktd, rv, dmask, qdec)


def _merge_kernel(x_ref, g_ref, gn_ref, ya_ref, wg_ref, wa_ref, wb_ref, wo_ref, o_ref):
    x = x_ref[...]
    d = x.shape[1]
    hb = _rms(x, g_ref[...]).astype(BF16)
    rg = _dot(hb, wg_ref[:, 0:R_V_WIDTH])
    yb = gn_ref[...].astype(F32) * (rg * _sigmoid(rg))
    tb = _dot(yb.astype(BF16), wb_ref[...])
    ta = _dot(ya_ref[...], wa_ref[...])
    ga = _dot(hb, wg_ref[:, R_V_WIDTH:R_V_WIDTH + d])
    gb = _dot(hb, wg_ref[:, R_V_WIDTH + d:])
    merged = _sigmoid(ga) * ta + _sigmoid(gb) * tb
    o_ref[...] = x + _dot(merged.astype(BF16), wo_ref[...])


def _merge(x2, g, gn2, ya2, wg, wa, wb, wo):
    n, d = x2.shape
    t = TOKEN_TILE
    tok = lambda w: pl.BlockSpec((t, w), lambda i: (i, 0))
    return pl.pallas_call(
        _merge_kernel,
        grid=(n // t,),
        in_specs=[tok(d), _full((1, d)), tok(R_V_WIDTH), tok(A_WIDTH),
                  _full(wg.shape), _full(wa.shape), _full(wb.shape), _full(wo.shape)],
        out_specs=tok(d),
        out_shape=jax.ShapeDtypeStruct((n, d), F32),
        compiler_params=_params("parallel"),
        name="merge",
    )(x2, g, gn2, ya2, wg, wa, wb, wo)


def _memkv_kernel(m_ref, g_ref, w_ref, kt_ref, v_ref):
    d = m_ref.shape[2]
    mb = _rms(m_ref[0], g_ref[...]).astype(BF16)
    kt_ref[0] = _dot(mb, w_ref[:, 0:d]).T.astype(BF16)
    v_ref[0] = _dot(mb, w_ref[:, d:]).astype(BF16)


def _memkv(mem, g, w_kv):
    b, m, d = mem.shape
    return pl.pallas_call(
        _memkv_kernel,
        grid=(b,),
        in_specs=[pl.BlockSpec((1, m, d), lambda bi: (bi, 0, 0)), _full((1, d)), _full(w_kv.shape)],
        out_specs=[pl.BlockSpec((1, d, m), lambda bi: (bi, 0, 0)), pl.BlockSpec((1, m, d), lambda bi: (bi, 0, 0))],
        out_shape=[jax.ShapeDtypeStruct((b, d, m), BF16), jax.ShapeDtypeStruct((b, m, d), BF16)],
        compiler_params=_params("parallel"),
        name="memkv",
    )(mem, g, w_kv)


def _cross_kernel(x_ref, g_ref, kt_ref, v_ref, wq_ref, wo_ref, o_ref, cat_ref):
    x = x_ref[0]
    d = x.shape[1]
    dh = d // X_HEADS
    hb = _rms(x, g_ref[...]).astype(BF16)
    q = (_dot(hb, wq_ref[...]) * dh ** -0.5).astype(BF16)
    for hd in range(X_HEADS):
        cols = slice(hd * dh, (hd + 1) * dh)
        s = _dot(q[:, cols], kt_ref[0, cols, :])
        p = jnp.exp(s - jnp.max(s, axis=-1, keepdims=True))
        p = p / jnp.sum(p, axis=-1, keepdims=True)
        cat_ref[:, cols] = _dot(p.astype(BF16), v_ref[0, :, cols]).astype(BF16)
    o_ref[0] = x + _dot(cat_ref[...], wo_ref[...])


def _cross(x, g, kt, v, wq, wo):
    b, s, d = x.shape
    m = v.shape[1]
    t = TOKEN_TILE
    tok = pl.BlockSpec((1, t, d), lambda bi, i: (bi, i, 0))
    return pl.pallas_call(
        _cross_kernel,
        grid=(b, s // t),
        in_specs=[tok, _full((1, d)), pl.BlockSpec((1, d, m), lambda bi, i: (bi, 0, 0)),
                  pl.BlockSpec((1, m, d), lambda bi, i: (bi, 0, 0)), _full(wq.shape), _full(wo.shape)],
        out_specs=tok,
        out_shape=jax.ShapeDtypeStruct((b, s, d), F32),
        scratch_shapes=[pltpu.VMEM((t, d), BF16)],
        compiler_params=_params("parallel", "parallel"),
        name="cross",
    )(x, g, kt, v, wq, wo)


def _ffn_kernel(x_ref, g_ref, wg_ref, wu_ref, wd_ref, gf_ref, o_ref):
    x = x_ref[...]
    hb = _rms(x, g_ref[...]).astype(BF16)
    acc = x
    for c in range(wg_ref.shape[1] // FF_CHUNK):
        cols = slice(c * FF_CHUNK, (c + 1) * FF_CHUNK)
        gate = _dot(hb, wg_ref[:, cols])
        act = gate * _sigmoid(gate) * _dot(hb, wu_ref[:, cols])
        acc = acc + _dot(act.astype(BF16), wd_ref[cols, :])
    o_ref[...] = _rms(acc, gf_ref[...])


def _ffn(x2, g, wg, wu, wd, gf):
    n, d = x2.shape
    t = TOKEN_TILE
    tok = pl.BlockSpec((t, d), lambda i: (i, 0))
    return pl.pallas_call(
        _ffn_kernel,
        grid=(n // t,),
        in_specs=[tok, _full((1, d)), _full(wg.shape), _full(wu.shape), _full(wd.shape), _full((1, d))],
        out_specs=tok,
        out_shape=jax.ShapeDtypeStruct((n, d), F32),
        compiler_params=_params("parallel"),
        name="ffn",
    )(x2, g, wg, wu, wd, gf)


def _rope_tables(s):
    def cs(dim):
        inv = ROPE_THETA ** (-jnp.arange(0, dim, 2, dtype=F32) / dim)
        ang = jnp.arange(s).astype(F32)[:, None] * inv[None, :]
        return jnp.cos(ang), jnp.sin(ang)

    ca, sa = cs(A_HEAD_DIM)
    cr, sr = cs(R_QK_DIM)
    return (jnp.concatenate([ca, ca, ca, ca], axis=1), jnp.concatenate([-sa, sa, -sa, sa], axis=1),
            jnp.concatenate([cr, cr], axis=1), jnp.concatenate([-sr, sr], axis=1),
            ca.T, sa.T, cr.T, sr.T)


def _decay_tables(s):
    c = R_CHUNK
    log_g = np.log(1.0 - np.exp2(-5.0 - np.arange(R_HEADS, dtype=np.float64)))
    i = np.arange(c, dtype=np.float64)
    diff = i[:, None] - i[None, :]
    dmask = np.where(diff >= 0, np.exp(log_g[:, None, None] * np.maximum(diff, 0.0)), 0.0)
    q_dec = np.exp(log_g[:, None] * (i + 1.0))
    k_dec = np.exp(log_g[:, None] * (c - 1.0 - i))
    kdec_t = np.zeros((8, s), np.float64)
    kdec_t[:R_HEADS] = np.tile(k_dec, (1, s // c))
    qdec = np.broadcast_to(q_dec[:, :, None], (R_HEADS, c, R_V_DIM))
    chunk_dec = tuple(float(v) for v in np.exp(log_g * c))
    return (jnp.asarray(dmask, F32), jnp.asarray(qdec, F32), jnp.asarray(kdec_t, F32), chunk_dec)


def kernel(x, mem, norm_mix, w_in, w_branch_a, w_branch_b, w_out, norm_cross, norm_mem,
           w_xq, w_xkv, w_xo, norm_ffn, w_gate, w_up, w_down, norm_final):
    b, s, d = x.shape
    assert s % TOKEN_TILE == 0 and d % 128 == 0
    rope = _rope_tables(s)
    dmask, qdec, kdec_t, chunk_dec = _decay_tables(s)
    row = lambda v: v.reshape(1, d)
    o_aq, o_ak, o_av = 0, A_WIDTH, 2 * A_WIDTH
    o_rq = 3 * A_WIDTH
    o_rk = o_rq + R_QK_WIDTH
    o_rv = o_rk + R_QK_WIDTH
    o_g = o_rv + R_V_WIDTH

    for l in range(w_in.shape[0]):
        w = w_in[l]
        w_nat = jnp.concatenate([w[:, o_ak:o_av], w[:, o_rq:o_rk], w[:, o_rv:o_g]], axis=1).astype(BF16)
        w_t = jnp.concatenate([w[:, o_aq:o_ak], w[:, o_av:o_rq], w[:, o_rk:o_rv]], axis=1).T.astype(BF16)
        k_a, kmean, qt_a, vt_a, rq, rv, rkt, rktd = _qkv(x, row(norm_mix[l]), w_nat, w_t, rope + (kdec_t,))
        y_a = _moba(qt_a, k_a, vt_a, kmean.reshape(b, s // MOBA_BLOCK, A_WIDTH))
        gn = _retention(rq, rkt, rktd, rv, dmask, qdec, chunk_dec)
        x = _merge(x.reshape(b * s, d), row(norm_mix[l]), gn.reshape(b * s, R_V_WIDTH),
                   y_a.reshape(b * s, A_WIDTH), w[:, o_g:].astype(BF16), w_branch_a[l].astype(BF16),
                   w_branch_b[l].astype(BF16), w_out[l].astype(BF16)).reshape(b, s, d)
        kt_m, v_m = _memkv(mem, row(norm_mem[l]), w_xkv[l].astype(BF16))
        x = _cross(x, row(norm_cross[l]), kt_m, v_m, w_xq[l].astype(BF16), w_xo[l].astype(BF16))
        last = l == w_in.shape[0] - 1
        gf = row(norm_final) if last else jnp.ones((1, d), F32)
        x = _ffn(x.reshape(b * s, d), row(norm_ffn[l]), w_gate[l].astype(BF16), w_up[l].astype(BF16),
                 w_down[l].astype(BF16), gf).reshape(b, s, d)
        if not last:
            raise NotImplementedError("the fused final norm assumes a single layer")
    return x
```

```python
import functools
import math

import numpy as np
import jax
import jax.numpy as jnp
from jax import lax
from jax.experimental import pallas as pl
from jax.experimental.pallas import tpu as pltpu

F32 = jnp.float32
BF16 = jnp.bfloat16

EPS = 1e-6
ROPE_THETA = 10000.0
A_HEADS = 8
A_HEAD_DIM = 64
A_WIDTH = A_HEADS * A_HEAD_DIM
MOBA_BLOCK = 256
MOBA_TOPK = 3
R_HEADS = 4
R_QK_DIM = 128
R_V_DIM = 256
R_QK_WIDTH = R_HEADS * R_QK_DIM
R_V_WIDTH = R_HEADS * R_V_DIM
R_CHUNK = 128
X_HEADS = 4
FF_CHUNK = 256
TOKEN_TILE = MOBA_BLOCK
ROW_TILE = 512
NEG = -1e30
MOBA_LOOKAHEAD = 3
MOBA_UNROLL = 4
MOBA_ONES_ROWS = 16
VMEM_LIMIT = 56 * 1024 * 1024


def _dot(a, b):
    return jnp.dot(a, b, preferred_element_type=F32)


def _rms(x, g):
    return x * lax.rsqrt(jnp.mean(x * x, axis=-1, keepdims=True) + EPS) * g


def _sigmoid(x):
    return 1.0 / (1.0 + jnp.exp(-x))


def _params(*sem):
    return pltpu.CompilerParams(dimension_semantics=sem, vmem_limit_bytes=VMEM_LIMIT)


def _full(shape):
    n = len(shape)
    return pl.BlockSpec(shape, lambda *_: (0,) * n)


def _qkv_kernel(x_ref, g_ref, wn_ref, wt_ref, cosa_ref, sina_ref, cosr_ref, sinr_ref,
                costa_ref, sinta_ref, costr_ref, sintr_ref, kdec_ref,
                k_ref, kmean_ref, qt_ref, vt_ref, rq_ref, rv_ref, rkt_ref, rktd_ref):
    h = _rms(x_ref[0], g_ref[...])
    hb = h.astype(BF16)
    ht = h.T.astype(BF16)
    t = h.shape[0]
    lane = lax.broadcasted_iota(jnp.int32, (t, 128), 1)
    first_half = (lane & 32) == 0

    ak = _dot(hb, wn_ref[:, 0:A_WIDTH])
    cosa, sina = cosa_ref[...], sina_ref[...]
    for c in range(A_WIDTH // 128):
        blk = ak[:, c * 128:(c + 1) * 128]
        rot = jnp.where(first_half, pltpu.roll(blk, 96, 1), pltpu.roll(blk, 32, 1))
        kr = blk * cosa + rot * sina
        k_ref[0, 0, :, c * 128:(c + 1) * 128] = kr.astype(BF16)
        kmean_ref[0, 0, :, c * 128:(c + 1) * 128] = jnp.mean(kr, axis=0, keepdims=True)

    rq = _dot(hb, wn_ref[:, A_WIDTH:A_WIDTH + R_QK_WIDTH])
    cosr, sinr = cosr_ref[...], sinr_ref[...]
    for c in range(R_HEADS):
        blk = rq[:, c * 128:(c + 1) * 128]
        rq_ref[0, :, c * 128:(c + 1) * 128] = (blk * cosr + pltpu.roll(blk, 64, 1) * sinr).astype(BF16)

    rv_ref[0] = _dot(hb, wn_ref[:, A_WIDTH + R_QK_WIDTH:]).astype(BF16)

    qt = _dot(wt_ref[0:A_WIDTH, :], ht)
    cost, sint = costa_ref[...], sinta_ref[...]
    half = A_HEAD_DIM // 2
    scale_a = A_HEAD_DIM ** -0.5
    for hd in range(A_HEADS):
        x1 = qt[hd * A_HEAD_DIM:hd * A_HEAD_DIM + half]
        x2 = qt[hd * A_HEAD_DIM + half:(hd + 1) * A_HEAD_DIM]
        qt_ref[0, 0, hd * A_HEAD_DIM:hd * A_HEAD_DIM + half, :] = ((x1 * cost - x2 * sint) * scale_a).astype(BF16)
        qt_ref[0, 0, hd * A_HEAD_DIM + half:(hd + 1) * A_HEAD_DIM, :] = ((x2 * cost + x1 * sint) * scale_a).astype(BF16)

    vt_ref[0, 0] = _dot(wt_ref[A_WIDTH:2 * A_WIDTH, :], ht).astype(BF16)

    rkt = _dot(wt_ref[2 * A_WIDTH:, :], ht)
    cost, sint = costr_ref[...], sintr_ref[...]
    half = R_QK_DIM // 2
    scale_r = R_QK_DIM ** -0.5
    for hd in range(R_HEADS):
        x1 = rkt[hd * R_QK_DIM:hd * R_QK_DIM + half]
        x2 = rkt[hd * R_QK_DIM + half:(hd + 1) * R_QK_DIM]
        o1 = (x1 * cost - x2 * sint) * scale_r
        o2 = (x2 * cost + x1 * sint) * scale_r
        dec = kdec_ref[hd:hd + 1, :]
        rkt_ref[0, 0, hd * R_QK_DIM:hd * R_QK_DIM + half, :] = o1.astype(BF16)
        rkt_ref[0, 0, hd * R_QK_DIM + half:(hd + 1) * R_QK_DIM, :] = o2.astype(BF16)
        rktd_ref[0, 0, hd * R_QK_DIM:hd * R_QK_DIM + half, :] = (o1 * dec).astype(BF16)
        rktd_ref[0, 0, hd * R_QK_DIM + half:(hd + 1) * R_QK_DIM, :] = (o2 * dec).astype(BF16)


def _qkv(x, g, w_nat, w_t, tabs):
    b, s, d = x.shape
    t = TOKEN_TILE
    nb = s // t
    tok = lambda w: pl.BlockSpec((1, t, w), lambda bi, i: (bi, i, 0))
    blk_t = lambda r: pl.BlockSpec((1, 1, r, t), lambda bi, i: (bi, i, 0, 0))
    nat_tab = pl.BlockSpec((t, 128), lambda bi, i: (i, 0))
    tr_tab = lambda r: pl.BlockSpec((r, t), lambda bi, i: (0, i))
    return pl.pallas_call(
        _qkv_kernel,
        grid=(b, nb),
        in_specs=[tok(d), _full((1, d)), _full(w_nat.shape), _full(w_t.shape),
                  nat_tab, nat_tab, nat_tab, nat_tab,
                  tr_tab(A_HEAD_DIM // 2), tr_tab(A_HEAD_DIM // 2),
                  tr_tab(R_QK_DIM // 2), tr_tab(R_QK_DIM // 2), tr_tab(8)],
        out_specs=[pl.BlockSpec((1, 1, t, A_WIDTH), lambda bi, i: (bi, i, 0, 0)),
                   pl.BlockSpec((1, 1, 1, A_WIDTH), lambda bi, i: (bi, i, 0, 0)),
                   blk_t(A_WIDTH), blk_t(A_WIDTH), tok(R_QK_WIDTH), tok(R_V_WIDTH),
                   blk_t(R_QK_WIDTH), blk_t(R_QK_WIDTH)],
        out_shape=[jax.ShapeDtypeStruct((b, nb, t, A_WIDTH), BF16),
                   jax.ShapeDtypeStruct((b, nb, 1, A_WIDTH), F32),
                   jax.ShapeDtypeStruct((b, nb, A_WIDTH, t), BF16),
                   jax.ShapeDtypeStruct((b, nb, A_WIDTH, t), BF16),
                   jax.ShapeDtypeStruct((b, s, R_QK_WIDTH), BF16),
                   jax.ShapeDtypeStruct((b, s, R_V_WIDTH), BF16),
                   jax.ShapeDtypeStruct((b, nb, R_QK_WIDTH, t), BF16),
                   jax.ShapeDtypeStruct((b, nb, R_QK_WIDTH, t), BF16)],
        compiler_params=_params("parallel", "parallel"),
        name="qkv",
    )(x, g, w_nat, w_t, *tabs)


def _moba_kernel(qt_ref, k_ref, vt_ref, km_ref, o_ref, qm_ref, sel_ref, m_ref, acc_ref, ot_ref):
    i = pl.program_id(1)
    nb = km_ref.shape[1]
    t = MOBA_BLOCK
    dh = A_HEAD_DIM
    row = lax.broadcasted_iota(jnp.int32, (2 * dh, t), 0)
    blk = lax.broadcasted_iota(jnp.int32, (nb, t), 0)
    past = blk < i
    causal = (lax.broadcasted_iota(jnp.int32, (t, t), 0) <= lax.broadcasted_iota(jnp.int32, (t, t), 1))
    ones = jnp.ones((MOBA_ONES_ROWS, t), BF16)
    lanes = lambda hd: slice((hd // 2) * 128, (hd // 2 + 1) * 128)
    rows = lambda hd: slice(hd * dh, (hd + 1) * dh)

    for hd in range(A_HEADS):
        hh = hd % 2
        qt_pair = qt_ref[0, 0, lanes(hd), :]
        mine = (row >= hh * dh) & (row < (hh + 1) * dh)
        qt = jnp.where(mine, qt_pair, jnp.zeros_like(qt_pair))
        qm_ref[hd] = qt

        km = km_ref[0, :, lanes(hd)]
        km_hi = km.astype(BF16)
        km_lo = (km - km_hi.astype(F32)).astype(BF16)
        bs = jnp.where(past, _dot(km_hi, qt) + _dot(km_lo, qt), -jnp.inf)
        cnt = jnp.zeros((nb, t), F32)
        for m in range(nb):
            r = bs[m:m + 1, :]
            tie = jnp.where(blk > m, 1.0, 0.0)
            cnt = cnt + jnp.where(r > bs, 1.0, jnp.where(r == bs, tie, 0.0))
        sel_ref[hd] = jnp.where(past & (cnt < MOBA_TOPK), 0.0, NEG)

    def pv(j, hd, p):
        vt_aug = jnp.concatenate([vt_ref[0, j, rows(hd), :], ones], axis=0)
        return _dot(vt_aug, p.astype(BF16))

    def own_scores(hd):
        return jnp.where(causal, _dot(k_ref[0, i, :, lanes(hd)], qm_ref[hd]), NEG)

    def own_update(hd, s):
        m0 = jnp.max(s, axis=0, keepdims=True)
        m_ref[hd] = jnp.broadcast_to(m0, (8, t))
        acc_ref[hd] = pv(i, hd, jnp.exp(s - m0))

    def past_scores(j, hd):
        return _dot(k_ref[0, j, :, lanes(hd)], qm_ref[hd]) + sel_ref[hd, pl.ds(j, 1), :]

    def past_update(j, hd, s):
        m_old = m_ref[hd]
        m_new = jnp.maximum(m_old, jnp.max(s, axis=0, keepdims=True))
        alpha = jnp.exp(m_old - m_new)
        m_ref[hd] = m_new
        acc_ref[hd] = alpha[0:1, :] * acc_ref[hd] + pv(j, hd, jnp.exp(s - m_new[0:1, :]))

    def pipelined(units, scores, update):
        s = {}
        for step in range(len(units) + MOBA_LOOKAHEAD):
            if step < len(units):
                s[step] = scores(*units[step])
            if step >= MOBA_LOOKAHEAD:
                update(*units[step - MOBA_LOOKAHEAD], s.pop(step - MOBA_LOOKAHEAD))

    heads = range(A_HEADS)
    pipelined([(hd,) for hd in heads], own_scores, own_update)

    def group_body(jj, carry):
        js = [MOBA_UNROLL * jj + u for u in range(MOBA_UNROLL)]
        pipelined([(j, hd) for j in js for hd in heads], past_scores, past_update)
        return carry

    def single_body(j, carry):
        pipelined([(j, hd) for hd in heads], past_scores, past_update)
        return carry

    n_groups = i // MOBA_UNROLL
    lax.fori_loop(0, n_groups, group_body, 0)
    lax.fori_loop(n_groups * MOBA_UNROLL, i, single_body, 0)

    for hd in range(A_HEADS):
        ot_ref[rows(hd), :] = acc_ref[hd, 0:dh, :] / acc_ref[hd, dh:dh + 1, :]
    o_ref[0] = ot_ref[...].T.astype(BF16)


def _moba(qt, k, vt, kmean):
    b, nb, t, w = k.shape
    return pl.pallas_call(
        _moba_kernel,
        grid=(b, nb),
        in_specs=[pl.BlockSpec((1, 1, w, t), lambda bi, i: (bi, i, 0, 0)),
                  pl.BlockSpec((1, nb, t, w), lambda bi, i: (bi, 0, 0, 0)),
                  pl.BlockSpec((1, nb, w, t), lambda bi, i: (bi, 0, 0, 0)),
                  pl.BlockSpec((1, nb, w), lambda bi, i: (bi, 0, 0))],
        out_specs=pl.BlockSpec((1, t, w), lambda bi, i: (bi, i, 0)),
        out_shape=jax.ShapeDtypeStruct((b, nb * t, w), BF16),
        scratch_shapes=[pltpu.VMEM((A_HEADS, 2 * A_HEAD_DIM, t), BF16),
                        pltpu.VMEM((A_HEADS, nb, t), F32),
                        pltpu.VMEM((A_HEADS, 8, t), F32),
                        pltpu.VMEM((A_HEADS, A_HEAD_DIM + MOBA_ONES_ROWS, t), F32),
                        pltpu.VMEM((w, t), F32)],
        compiler_params=_params("parallel", "parallel"),
        name="moba",
    )(qt, k, vt, kmean)


def _retention_kernel(chunk_dec, rq_ref, rkt_ref, rktd_ref, rv_ref, dmask_ref, qdec_ref, o_ref, state_ref):
    @pl.when(pl.program_id(1) == 0)
    def _():
        state_ref[...] = jnp.zeros_like(state_ref)

    c = R_CHUNK
    for ci in range(TOKEN_TILE // c):
        rows = slice(ci * c, (ci + 1) * c)
        for hd in range(R_HEADS):
            qk = slice(hd * R_QK_DIM, (hd + 1) * R_QK_DIM)
            vv = slice(hd * R_V_DIM, (hd + 1) * R_V_DIM)
            q = rq_ref[0, rows, qk]
            v = rv_ref[0, rows, vv]
            att = _dot(q, rkt_ref[0, 0, qk, rows]) * dmask_ref[hd]
            st = state_ref[hd]
            y = _dot(att.astype(BF16), v) + _dot(q, st.astype(BF16)) * qdec_ref[hd]
            state_ref[hd] = chunk_dec[hd] * st + _dot(rktd_ref[0, 0, qk, rows], v)
            mu = jnp.mean(y, axis=-1, keepdims=True)
            yc = y - mu
            var = jnp.mean(yc * yc, axis=-1, keepdims=True)
            o_ref[0, rows, vv] = (yc * lax.rsqrt(var + EPS)).astype(BF16)


def _retention(rq, rkt, rktd, rv, dmask, qdec, chunk_dec):
    b, s, _ = rq.shape
    t = TOKEN_TILE
    tok = lambda w: pl.BlockSpec((1, t, w), lambda bi, i: (bi, i, 0))
    blk_t = pl.BlockSpec((1, 1, R_QK_WIDTH, t), lambda bi, i: (bi, i, 0, 0))
    return pl.pallas_call(
        functools.partial(_retention_kernel, chunk_dec),
        grid=(b, s // t),
        in_specs=[tok(R_QK_WIDTH), blk_t, blk_t, tok(R_V_WIDTH), _full(dmask.shape), _full(qdec.shape)],
        out_specs=tok(R_V_WIDTH),
        out_shape=jax.ShapeDtypeStruct((b, s, R_V_WIDTH), BF16),
        scratch_shapes=[pltpu.VMEM((R_HEADS, R_QK_DIM, R_V_DIM), F32)],
        compiler_params=_params("parallel", "arbitrary"),
        name="retention",
    )(rq, rkt, rktd, rv, dmask, qdec)


def _merge_kernel(x_ref, g_ref, gn_ref, ya_ref, wg_ref, wa_ref, wb_ref, wo_ref, o_ref):
    x = x_ref[...]
    d = x.shape[1]
    hb = _rms(x, g_ref[...]).astype(BF16)
    rg = _dot(hb, wg_ref[:, 0:R_V_WIDTH])
    ta = _dot(ya_ref[...], wa_ref[...])
    ga = _dot(hb, wg_ref[:, R_V_WIDTH:R_V_WIDTH + d])
    gb = _dot(hb, wg_ref[:, R_V_WIDTH + d:])
    yb = gn_ref[...].astype(F32) * (rg * _sigmoid(rg))
    tb = _dot(yb.astype(BF16), wb_ref[...])
    merged = _sigmoid(ga) * ta + _sigmoid(gb) * tb
    o_ref[...] = x + _dot(merged.astype(BF16), wo_ref[...])


def _merge(x2, g, gn2, ya2, wg, wa, wb, wo):
    n, d = x2.shape
    t = ROW_TILE
    tok = lambda w: pl.BlockSpec((t, w), lambda i: (i, 0))
    return pl.pallas_call(
        _merge_kernel,
        grid=(n // t,),
        in_specs=[tok(d), _full((1, d)), tok(R_V_WIDTH), tok(A_WIDTH),
                  _full(wg.shape), _full(wa.shape), _full(wb.shape), _full(wo.shape)],
        out_specs=tok(d),
        out_shape=jax.ShapeDtypeStruct((n, d), F32),
        compiler_params=_params("parallel"),
        name="merge",
    )(x2, g, gn2, ya2, wg, wa, wb, wo)


def _memkv_kernel(m_ref, g_ref, w_ref, kt_ref, v_ref):
    d = m_ref.shape[2]
    mb = _rms(m_ref[0], g_ref[...]).astype(BF16)
    kt_ref[0] = _dot(mb, w_ref[:, 0:d]).T.astype(BF16)
    v_ref[0] = _dot(mb, w_ref[:, d:]).astype(BF16)


def _memkv(mem, g, w_kv):
    b, m, d = mem.shape
    return pl.pallas_call(
        _memkv_kernel,
        grid=(b,),
        in_specs=[pl.BlockSpec((1, m, d), lambda bi: (bi, 0, 0)), _full((1, d)), _full(w_kv.shape)],
        out_specs=[pl.BlockSpec((1, d, m), lambda bi: (bi, 0, 0)), pl.BlockSpec((1, m, d), lambda bi: (bi, 0, 0))],
        out_shape=[jax.ShapeDtypeStruct((b, d, m), BF16), jax.ShapeDtypeStruct((b, m, d), BF16)],
        compiler_params=_params("parallel"),
        name="memkv",
    )(mem, g, w_kv)


def _cross_kernel(x_ref, g_ref, kt_ref, v_ref, wq_ref, wo_ref, o_ref, cat_ref):
    x = x_ref[0]
    d = x.shape[1]
    dh = d // X_HEADS
    hb = _rms(x, g_ref[...]).astype(BF16)
    q = (_dot(hb, wq_ref[...]) * dh ** -0.5).astype(BF16)
    cols = lambda hd: slice(hd * dh, (hd + 1) * dh)
    scores = [_dot(q[:, cols(hd)], kt_ref[0, cols(hd), :]) for hd in range(X_HEADS)]
    for hd, s in enumerate(scores):
        p = jnp.exp(s - jnp.max(s, axis=-1, keepdims=True))
        p = p / jnp.sum(p, axis=-1, keepdims=True)
        cat_ref[:, cols(hd)] = _dot(p.astype(BF16), v_ref[0, :, cols(hd)]).astype(BF16)
    o_ref[0] = x + _dot(cat_ref[...], wo_ref[...])


def _cross(x, g, kt, v, wq, wo):
    b, s, d = x.shape
    m = v.shape[1]
    t = ROW_TILE
    tok = pl.BlockSpec((1, t, d), lambda bi, i: (bi, i, 0))
    return pl.pallas_call(
        _cross_kernel,
        grid=(b, s // t),
        in_specs=[tok, _full((1, d)), pl.BlockSpec((1, d, m), lambda bi, i: (bi, 0, 0)),
                  pl.BlockSpec((1, m, d), lambda bi, i: (bi, 0, 0)), _full(wq.shape), _full(wo.shape)],
        out_specs=tok,
        out_shape=jax.ShapeDtypeStruct((b, s, d), F32),
        scratch_shapes=[pltpu.VMEM((t, d), BF16)],
        compiler_params=_params("parallel", "parallel"),
        name="cross",
    )(x, g, kt, v, wq, wo)


def _ffn_kernel(x_ref, g_ref, wg_ref, wu_ref, wd_ref, gf_ref, o_ref):
    x = x_ref[...]
    hb = _rms(x, g_ref[...]).astype(BF16)
    n_chunks = wg_ref.shape[1] // FF_CHUNK
    cols = lambda c: slice(c * FF_CHUNK, (c + 1) * FF_CHUNK)

    def up(c):
        return _dot(hb, wg_ref[:, cols(c)]), _dot(hb, wu_ref[:, cols(c)])

    acc = x
    nxt = up(0)
    for c in range(n_chunks):
        gate, upv = nxt
        if c + 1 < n_chunks:
            nxt = up(c + 1)
        act = gate * _sigmoid(gate) * upv
        acc = acc + _dot(act.astype(BF16), wd_ref[cols(c), :])
    o_ref[...] = _rms(acc, gf_ref[...])


def _ffn(x2, g, wg, wu, wd, gf):
    n, d = x2.shape
    t = ROW_TILE
    tok = pl.BlockSpec((t, d), lambda i: (i, 0))
    return pl.pallas_call(
        _ffn_kernel,
        grid=(n // t,),
        in_specs=[tok, _full((1, d)), _full(wg.shape), _full(wu.shape), _full(wd.shape), _full((1, d))],
        out_specs=tok,
        out_shape=jax.ShapeDtypeStruct((n, d), F32),
        compiler_params=_params("parallel"),
        name="ffn",
    )(x2, g, wg, wu, wd, gf)


def _rope_tables(s):
    def cs(dim):
        inv = ROPE_THETA ** (-jnp.arange(0, dim, 2, dtype=F32) / dim)
        ang = jnp.arange(s).astype(F32)[:, None] * inv[None, :]
        return jnp.cos(ang), jnp.sin(ang)

    ca, sa = cs(A_HEAD_DIM)
    cr, sr = cs(R_QK_DIM)
    return (jnp.concatenate([ca, ca, ca, ca], axis=1), jnp.concatenate([-sa, sa, -sa, sa], axis=1),
            jnp.concatenate([cr, cr], axis=1), jnp.concatenate([-sr, sr], axis=1),
            ca.T, sa.T, cr.T, sr.T)


def _decay_tables(s):
    c = R_CHUNK
    log_g = np.log(1.0 - np.exp2(-5.0 - np.arange(R_HEADS, dtype=np.float64)))
    i = np.arange(c, dtype=np.float64)
    diff = i[:, None] - i[None, :]
    dmask = np.where(diff >= 0, np.exp(log_g[:, None, None] * np.maximum(diff, 0.0)), 0.0)
    q_dec = np.exp(log_g[:, None] * (i + 1.0))
    k_dec = np.exp(log_g[:, None] * (c - 1.0 - i))
    kdec_t = np.zeros((8, s), np.float64)
    kdec_t[:R_HEADS] = np.tile(k_dec, (1, s // c))
    qdec = np.broadcast_to(q_dec[:, :, None], (R_HEADS, c, R_V_DIM))
    chunk_dec = tuple(float(v) for v in np.exp(log_g * c))
    return (jnp.asarray(dmask, F32), jnp.asarray(qdec, F32), jnp.asarray(kdec_t, F32), chunk_dec)


def kernel(x, mem, norm_mix, w_in, w_branch_a, w_branch_b, w_out, norm_cross, norm_mem,
           w_xq, w_xkv, w_xo, norm_ffn, w_gate, w_up, w_down, norm_final):
    b, s, d = x.shape
    assert s % TOKEN_TILE == 0 and s % ROW_TILE == 0 and d % 128 == 0
    rope = _rope_tables(s)
    dmask, qdec, kdec_t, chunk_dec = _decay_tables(s)
    row = lambda v: v.reshape(1, d)
    o_aq, o_ak, o_av = 0, A_WIDTH, 2 * A_WIDTH
    o_rq = 3 * A_WIDTH
    o_rk = o_rq + R_QK_WIDTH
    o_rv = o_rk + R_QK_WIDTH
    o_g = o_rv + R_V_WIDTH

    for l in range(w_in.shape[0]):
        w = w_in[l]
        w_nat = jnp.concatenate([w[:, o_ak:o_av], w[:, o_rq:o_rk], w[:, o_rv:o_g]], axis=1).astype(BF16)
        w_t = jnp.concatenate([w[:, o_aq:o_ak], w[:, o_av:o_rq], w[:, o_rk:o_rv]], axis=1).T.astype(BF16)
        k_a, kmean, qt_a, vt_a, rq, rv, rkt, rktd = _qkv(x, row(norm_mix[l]), w_nat, w_t, rope + (kdec_t,))
        y_a = _moba(qt_a, k_a, vt_a, kmean.reshape(b, s // MOBA_BLOCK, A_WIDTH))
        gn = _retention(rq, rkt, rktd, rv, dmask, qdec, chunk_dec)
        x = _merge(x.reshape(b * s, d), row(norm_mix[l]), gn.reshape(b * s, R_V_WIDTH),
                   y_a.reshape(b * s, A_WIDTH), w[:, o_g:].astype(BF16), w_branch_a[l].astype(BF16),
                   w_branch_b[l].astype(BF16), w_out[l].astype(BF16)).reshape(b, s, d)
        kt_m, v_m = _memkv(mem, row(norm_mem[l]), w_xkv[l].astype(BF16))
        x = _cross(x, row(norm_cross[l]), kt_m, v_m, w_xq[l].astype(BF16), w_xo[l].astype(BF16))
        last = l == w_in.shape[0] - 1
        gf = row(norm_final) if last else jnp.ones((1, d), F32)
        x = _ffn(x.reshape(b * s, d), row(norm_ffn[l]), w_gate[l].astype(BF16), w_up[l].astype(BF16),
                 w_down[l].astype(BF16), gf).reshape(b, s, d)
        if not last:
            raise NotImplementedError("the fused final norm assumes a single layer")
    return x
```

```python
import functools
import math

import numpy as np
import jax
import jax.numpy as jnp
from jax import lax
from jax.experimental import pallas as pl
from jax.experimental.pallas import tpu as pltpu

F32 = jnp.float32
BF16 = jnp.bfloat16

EPS = 1e-6
ROPE_THETA = 10000.0
A_HEADS = 8
A_HEAD_DIM = 64
A_WIDTH = A_HEADS * A_HEAD_DIM
MOBA_BLOCK = 256
MOBA_TOPK = 3
R_HEADS = 4
R_QK_DIM = 128
R_V_DIM = 256
R_QK_WIDTH = R_HEADS * R_QK_DIM
R_V_WIDTH = R_HEADS * R_V_DIM
R_CHUNK = 128
X_HEADS = 4
FF_CHUNK = 256
TOKEN_TILE = MOBA_BLOCK
ROW_TILE = 512
NEG = -1e30
LOG2_E = math.log2(math.e)
MOBA_LOOKAHEAD = 5
MOBA_UNROLL = 4
MOBA_ONES_ROWS = 16
VMEM_LIMIT = 56 * 1024 * 1024


def _dot(a, b):
    return jnp.dot(a, b, preferred_element_type=F32)


def _rms(x, g):
    return x * lax.rsqrt(jnp.mean(x * x, axis=-1, keepdims=True) + EPS) * g


def _sigmoid(x):
    return 1.0 / (1.0 + jnp.exp(-x))


def _params(*sem):
    return pltpu.CompilerParams(dimension_semantics=sem, vmem_limit_bytes=VMEM_LIMIT)


def _full(shape):
    n = len(shape)
    return pl.BlockSpec(shape, lambda *_: (0,) * n)


def _qkv_kernel(x_ref, g_ref, wn_ref, wt_ref, cosa_ref, sina_ref, cosr_ref, sinr_ref,
                costa_ref, sinta_ref, costr_ref, sintr_ref, kdec_ref,
                k_ref, kmean_ref, qt_ref, vt_ref, rq_ref, rv_ref, rkt_ref, rktd_ref):
    h = _rms(x_ref[0], g_ref[...])
    hb = h.astype(BF16)
    ht = h.T.astype(BF16)
    t = h.shape[0]
    lane = lax.broadcasted_iota(jnp.int32, (t, 128), 1)
    first_half = (lane & 32) == 0

    ak = _dot(hb, wn_ref[:, 0:A_WIDTH])
    cosa, sina = cosa_ref[...], sina_ref[...]
    for c in range(A_WIDTH // 128):
        blk = ak[:, c * 128:(c + 1) * 128]
        rot = jnp.where(first_half, pltpu.roll(blk, 96, 1), pltpu.roll(blk, 32, 1))
        kr = blk * cosa + rot * sina
        k_ref[0, 0, :, c * 128:(c + 1) * 128] = kr.astype(BF16)
        kmean_ref[0, 0, :, c * 128:(c + 1) * 128] = jnp.mean(kr, axis=0, keepdims=True)

    rq = _dot(hb, wn_ref[:, A_WIDTH:A_WIDTH + R_QK_WIDTH])
    cosr, sinr = cosr_ref[...], sinr_ref[...]
    for c in range(R_HEADS):
        blk = rq[:, c * 128:(c + 1) * 128]
        rq_ref[0, :, c * 128:(c + 1) * 128] = (blk * cosr + pltpu.roll(blk, 64, 1) * sinr).astype(BF16)

    rv_ref[0] = _dot(hb, wn_ref[:, A_WIDTH + R_QK_WIDTH:]).astype(BF16)

    qt = _dot(wt_ref[0:A_WIDTH, :], ht)
    cost, sint = costa_ref[...], sinta_ref[...]
    half = A_HEAD_DIM // 2
    scale_a = A_HEAD_DIM ** -0.5 * LOG2_E
    for hd in range(A_HEADS):
        x1 = qt[hd * A_HEAD_DIM:hd * A_HEAD_DIM + half]
        x2 = qt[hd * A_HEAD_DIM + half:(hd + 1) * A_HEAD_DIM]
        qt_ref[0, 0, hd * A_HEAD_DIM:hd * A_HEAD_DIM + half, :] = ((x1 * cost - x2 * sint) * scale_a).astype(BF16)
        qt_ref[0, 0, hd * A_HEAD_DIM + half:(hd + 1) * A_HEAD_DIM, :] = ((x2 * cost + x1 * sint) * scale_a).astype(BF16)

    vt_ref[0, 0] = _dot(wt_ref[A_WIDTH:2 * A_WIDTH, :], ht).astype(BF16)

    rkt = _dot(wt_ref[2 * A_WIDTH:, :], ht)
    cost, sint = costr_ref[...], sintr_ref[...]
    half = R_QK_DIM // 2
    scale_r = R_QK_DIM ** -0.5
    for hd in range(R_HEADS):
        x1 = rkt[hd * R_QK_DIM:hd * R_QK_DIM + half]
        x2 = rkt[hd * R_QK_DIM + half:(hd + 1) * R_QK_DIM]
        o1 = (x1 * cost - x2 * sint) * scale_r
        o2 = (x2 * cost + x1 * sint) * scale_r
        dec = kdec_ref[hd:hd + 1, :]
        rkt_ref[0, 0, hd * R_QK_DIM:hd * R_QK_DIM + half, :] = o1.astype(BF16)
        rkt_ref[0, 0, hd * R_QK_DIM + half:(hd + 1) * R_QK_DIM, :] = o2.astype(BF16)
        rktd_ref[0, 0, hd * R_QK_DIM:hd * R_QK_DIM + half, :] = (o1 * dec).astype(BF16)
        rktd_ref[0, 0, hd * R_QK_DIM + half:(hd + 1) * R_QK_DIM, :] = (o2 * dec).astype(BF16)


def _qkv(x, g, w_nat, w_t, tabs):
    b, s, d = x.shape
    t = TOKEN_TILE
    nb = s // t
    tok = lambda w: pl.BlockSpec((1, t, w), lambda bi, i: (bi, i, 0))
    blk_t = lambda r: pl.BlockSpec((1, 1, r, t), lambda bi, i: (bi, i, 0, 0))
    nat_tab = pl.BlockSpec((t, 128), lambda bi, i: (i, 0))
    tr_tab = lambda r: pl.BlockSpec((r, t), lambda bi, i: (0, i))
    return pl.pallas_call(
        _qkv_kernel,
        grid=(b, nb),
        in_specs=[tok(d), _full((1, d)), _full(w_nat.shape), _full(w_t.shape),
                  nat_tab, nat_tab, nat_tab, nat_tab,
                  tr_tab(A_HEAD_DIM // 2), tr_tab(A_HEAD_DIM // 2),
                  tr_tab(R_QK_DIM // 2), tr_tab(R_QK_DIM // 2), tr_tab(8)],
        out_specs=[pl.BlockSpec((1, 1, t, A_WIDTH), lambda bi, i: (bi, i, 0, 0)),
                   pl.BlockSpec((1, 1, 1, A_WIDTH), lambda bi, i: (bi, i, 0, 0)),
                   blk_t(A_WIDTH), blk_t(A_WIDTH), tok(R_QK_WIDTH), tok(R_V_WIDTH),
                   blk_t(R_QK_WIDTH), blk_t(R_QK_WIDTH)],
        out_shape=[jax.ShapeDtypeStruct((b, nb, t, A_WIDTH), BF16),
                   jax.ShapeDtypeStruct((b, nb, 1, A_WIDTH), F32),
                   jax.ShapeDtypeStruct((b, nb, A_WIDTH, t), BF16),
                   jax.ShapeDtypeStruct((b, nb, A_WIDTH, t), BF16),
                   jax.ShapeDtypeStruct((b, s, R_QK_WIDTH), BF16),
                   jax.ShapeDtypeStruct((b, s, R_V_WIDTH), BF16),
                   jax.ShapeDtypeStruct((b, nb, R_QK_WIDTH, t), BF16),
                   jax.ShapeDtypeStruct((b, nb, R_QK_WIDTH, t), BF16)],
        compiler_params=_params("parallel", "parallel"),
        name="qkv",
    )(x, g, w_nat, w_t, *tabs)


def _moba_kernel(qt_ref, k_ref, vt_ref, km_ref, o_ref, qm_ref, m_ref, acc_ref, ot_ref, s_ref):
    i = pl.program_id(1)
    nb = km_ref.shape[1]
    t = MOBA_BLOCK
    dh = A_HEAD_DIM
    row = lax.broadcasted_iota(jnp.int32, (2 * dh, t), 0)
    blk = lax.broadcasted_iota(jnp.int32, (nb, t), 0)
    past = blk < i
    causal = (lax.broadcasted_iota(jnp.int32, (t, t), 0) <= lax.broadcasted_iota(jnp.int32, (t, t), 1))
    ones = jnp.ones((MOBA_ONES_ROWS, t), BF16)
    lanes = lambda hd: slice((hd // 2) * 128, (hd // 2 + 1) * 128)
    rows = lambda hd: slice(hd * dh, (hd + 1) * dh)

    for hd in range(A_HEADS):
        hh = hd % 2
        qt_pair = qt_ref[0, 0, lanes(hd), :]
        mine = (row >= hh * dh) & (row < (hh + 1) * dh)
        qt = jnp.where(mine, qt_pair, jnp.zeros_like(qt_pair))
        qm_ref[hd, 0:2 * dh, :] = qt

        km = km_ref[0, :, lanes(hd)]
        km_hi = km.astype(BF16)
        km_lo = (km - km_hi.astype(F32)).astype(BF16)
        bs = jnp.where(past, _dot(km_hi, qt) + _dot(km_lo, qt), -jnp.inf)
        cnt = jnp.zeros((nb, t), F32)
        for m in range(nb):
            r = bs[m:m + 1, :]
            tie = jnp.where(blk > m, 1.0, 0.0)
            cnt = cnt + jnp.where(r > bs, 1.0, jnp.where(r == bs, tie, 0.0))
        bias = jnp.where(past & (cnt < MOBA_TOPK), 0.0, NEG).astype(BF16)
        qm_ref[hd, 2 * dh:, :] = jnp.concatenate([bias, jnp.zeros((2 * dh - nb, t), BF16)], axis=0)

    key_lane = lax.broadcasted_iota(jnp.int32, (t, 2 * dh), 1)

    def keys_with_block_column(j):
        onehot = jnp.where(key_lane == j, 1.0, 0.0).astype(BF16)
        return [jnp.concatenate([k_ref[0, j, :, lanes(2 * p)], onehot], axis=1) for p in range(A_HEADS // 2)]

    def pv(j, hd, p):
        vt_aug = jnp.concatenate([vt_ref[0, j, rows(hd), :], ones], axis=0)
        return _dot(vt_aug, p.astype(BF16))

    def own_unit(hd):
        def scores():
            return jnp.where(causal, _dot(k_ref[0, i, :, lanes(hd)], qm_ref[hd, 0:2 * dh, :]), NEG)

        def update(s, m0):
            m_ref[hd] = jnp.broadcast_to(m0, (8, t))
            acc_ref[hd] = pv(i, hd, jnp.exp2(s - m0))

        return scores, update

    def past_units(j):
        keys = keys_with_block_column(j)

        def unit(hd):
            def scores():
                return _dot(keys[hd // 2], qm_ref[hd])

            def update(s, m_blk):
                m_old = m_ref[hd]
                m_new = jnp.maximum(m_old, m_blk)
                alpha = jnp.exp2(m_old - m_new)
                m_ref[hd] = m_new
                acc_ref[hd] = alpha[0:1, :] * acc_ref[hd] + pv(j, hd, jnp.exp2(s - m_new[0:1, :]))

            return scores, update

        return [unit(hd) for hd in range(A_HEADS)]

    def pipelined(units):
        blk_max = {}
        slots = MOBA_LOOKAHEAD + 1
        for step in range(len(units) + MOBA_LOOKAHEAD):
            if step < len(units):
                sc = units[step][0]()
                s_ref[step % slots] = sc
                blk_max[step] = jnp.max(sc, axis=0, keepdims=True)
            if step >= MOBA_LOOKAHEAD:
                u = step - MOBA_LOOKAHEAD
                units[u][1](s_ref[u % slots], blk_max.pop(u))

    pipelined([own_unit(hd) for hd in range(A_HEADS)])

    def group_body(jj, carry):
        pipelined([u for r in range(MOBA_UNROLL) for u in past_units(MOBA_UNROLL * jj + r)])
        return carry

    def single_body(j, carry):
        pipelined(past_units(j))
        return carry

    n_groups = i // MOBA_UNROLL
    lax.fori_loop(0, n_groups, group_body, 0)
    lax.fori_loop(n_groups * MOBA_UNROLL, i, single_body, 0)

    for hd in range(A_HEADS):
        ot_ref[rows(hd), :] = acc_ref[hd, 0:dh, :] / acc_ref[hd, dh:dh + 1, :]
    o_ref[0] = ot_ref[...].T.astype(BF16)


def _moba(qt, k, vt, kmean):
    b, nb, t, w = k.shape
    return pl.pallas_call(
        _moba_kernel,
        grid=(b, nb),
        in_specs=[pl.BlockSpec((1, 1, w, t), lambda bi, i: (bi, i, 0, 0)),
                  pl.BlockSpec((1, nb, t, w), lambda bi, i: (bi, 0, 0, 0)),
                  pl.BlockSpec((1, nb, w, t), lambda bi, i: (bi, 0, 0, 0)),
                  pl.BlockSpec((1, nb, w), lambda bi, i: (bi, 0, 0))],
        out_specs=pl.BlockSpec((1, t, w), lambda bi, i: (bi, i, 0)),
        out_shape=jax.ShapeDtypeStruct((b, nb * t, w), BF16),
        scratch_shapes=[pltpu.VMEM((A_HEADS, 4 * A_HEAD_DIM, t), BF16),
                        pltpu.VMEM((A_HEADS, 8, t), F32),
                        pltpu.VMEM((A_HEADS, A_HEAD_DIM + MOBA_ONES_ROWS, t), F32),
                        pltpu.VMEM((w, t), F32),
                        pltpu.VMEM((MOBA_LOOKAHEAD + 1, t, t), F32)],
        compiler_params=_params("parallel", "parallel"),
        name="moba",
    )(qt, k, vt, kmean)


def _retention_kernel(chunk_dec, rq_ref, rkt_ref, rktd_ref, rv_ref, dmask_ref, qdec_ref, o_ref, state_ref):
    @pl.when(pl.program_id(1) == 0)
    def _():
        state_ref[...] = jnp.zeros_like(state_ref)

    c = R_CHUNK
    for ci in range(TOKEN_TILE // c):
        rows = slice(ci * c, (ci + 1) * c)
        for hd in range(R_HEADS):
            qk = slice(hd * R_QK_DIM, (hd + 1) * R_QK_DIM)
            vv = slice(hd * R_V_DIM, (hd + 1) * R_V_DIM)
            q = rq_ref[0, rows, qk]
            v = rv_ref[0, rows, vv]
            att = _dot(q, rkt_ref[0, 0, qk, rows]) * dmask_ref[hd]
            st = state_ref[hd]
            y = _dot(att.astype(BF16), v) + _dot(q, st.astype(BF16)) * qdec_ref[hd]
            state_ref[hd] = chunk_dec[hd] * st + _dot(rktd_ref[0, 0, qk, rows], v)
            mu = jnp.mean(y, axis=-1, keepdims=True)
            yc = y - mu
            var = jnp.mean(yc * yc, axis=-1, keepdims=True)
            o_ref[0, rows, vv] = (yc * lax.rsqrt(var + EPS)).astype(BF16)


def _retention(rq, rkt, rktd, rv, dmask, qdec, chunk_dec):
    b, s, _ = rq.shape
    t = TOKEN_TILE
    tok = lambda w: pl.BlockSpec((1, t, w), lambda bi, i: (bi, i, 0))
    blk_t = pl.BlockSpec((1, 1, R_QK_WIDTH, t), lambda bi, i: (bi, i, 0, 0))
    return pl.pallas_call(
        functools.partial(_retention_kernel, chunk_dec),
        grid=(b, s // t),
        in_specs=[tok(R_QK_WIDTH), blk_t, blk_t, tok(R_V_WIDTH), _full(dmask.shape), _full(qdec.shape)],
        out_specs=tok(R_V_WIDTH),
        out_shape=jax.ShapeDtypeStruct((b, s, R_V_WIDTH), BF16),
        scratch_shapes=[pltpu.VMEM((R_HEADS, R_QK_DIM, R_V_DIM), F32)],
        compiler_params=_params("parallel", "arbitrary"),
        name="retention",
    )(rq, rkt, rktd, rv, dmask, qdec)


def _merge_kernel(x_ref, g_ref, gn_ref, ya_ref, wg_ref, wa_ref, wb_ref, wo_ref, o_ref):
    x = x_ref[...]
    d = x.shape[1]
    hb = _rms(x, g_ref[...]).astype(BF16)
    rg = _dot(hb, wg_ref[:, 0:R_V_WIDTH])
    ta = _dot(ya_ref[...], wa_ref[...])
    ga = _dot(hb, wg_ref[:, R_V_WIDTH:R_V_WIDTH + d])
    gb = _dot(hb, wg_ref[:, R_V_WIDTH + d:])
    yb = gn_ref[...].astype(F32) * (rg * _sigmoid(rg))
    tb = _dot(yb.astype(BF16), wb_ref[...])
    merged = _sigmoid(ga) * ta + _sigmoid(gb) * tb
    o_ref[...] = x + _dot(merged.astype(BF16), wo_ref[...])


def _merge(x2, g, gn2, ya2, wg, wa, wb, wo):
    n, d = x2.shape
    t = ROW_TILE
    tok = lambda w: pl.BlockSpec((t, w), lambda i: (i, 0))
    return pl.pallas_call(
        _merge_kernel,
        grid=(n // t,),
        in_specs=[tok(d), _full((1, d)), tok(R_V_WIDTH), tok(A_WIDTH),
                  _full(wg.shape), _full(wa.shape), _full(wb.shape), _full(wo.shape)],
        out_specs=tok(d),
        out_shape=jax.ShapeDtypeStruct((n, d), F32),
        compiler_params=_params("parallel"),
        name="merge",
    )(x2, g, gn2, ya2, wg, wa, wb, wo)


def _memkv_kernel(m_ref, g_ref, w_ref, kt_ref, v_ref):
    d = m_ref.shape[2]
    mb = _rms(m_ref[0], g_ref[...]).astype(BF16)
    kt_ref[0] = _dot(mb, w_ref[:, 0:d]).T.astype(BF16)
    v_ref[0] = _dot(mb, w_ref[:, d:]).astype(BF16)


def _memkv(mem, g, w_kv):
    b, m, d = mem.shape
    return pl.pallas_call(
        _memkv_kernel,
        grid=(b,),
        in_specs=[pl.BlockSpec((1, m, d), lambda bi: (bi, 0, 0)), _full((1, d)), _full(w_kv.shape)],
        out_specs=[pl.BlockSpec((1, d, m), lambda bi: (bi, 0, 0)), pl.BlockSpec((1, m, d), lambda bi: (bi, 0, 0))],
        out_shape=[jax.ShapeDtypeStruct((b, d, m), BF16), jax.ShapeDtypeStruct((b, m, d), BF16)],
        compiler_params=_params("parallel"),
        name="memkv",
    )(mem, g, w_kv)


def _cross_kernel(x_ref, g_ref, kt_ref, v_ref, wq_ref, wo_ref, o_ref, cat_ref):
    x = x_ref[0]
    d = x.shape[1]
    dh = d // X_HEADS
    hb = _rms(x, g_ref[...]).astype(BF16)
    q = (_dot(hb, wq_ref[...]) * dh ** -0.5).astype(BF16)
    cols = lambda hd: slice(hd * dh, (hd + 1) * dh)
    scores = [_dot(q[:, cols(hd)], kt_ref[0, cols(hd), :]) for hd in range(X_HEADS)]
    for hd, s in enumerate(scores):
        p = jnp.exp(s - jnp.max(s, axis=-1, keepdims=True))
        p = p / jnp.sum(p, axis=-1, keepdims=True)
        cat_ref[:, cols(hd)] = _dot(p.astype(BF16), v_ref[0, :, cols(hd)]).astype(BF16)
    o_ref[0] = x + _dot(cat_ref[...], wo_ref[...])


def _cross(x, g, kt, v, wq, wo):
    b, s, d = x.shape
    m = v.shape[1]
    t = ROW_TILE
    tok = pl.BlockSpec((1, t, d), lambda bi, i: (bi, i, 0))
    return pl.pallas_call(
        _cross_kernel,
        grid=(b, s // t),
        in_specs=[tok, _full((1, d)), pl.BlockSpec((1, d, m), lambda bi, i: (bi, 0, 0)),
                  pl.BlockSpec((1, m, d), lambda bi, i: (bi, 0, 0)), _full(wq.shape), _full(wo.shape)],
        out_specs=tok,
        out_shape=jax.ShapeDtypeStruct((b, s, d), F32),
        scratch_shapes=[pltpu.VMEM((t, d), BF16)],
        compiler_params=_params("parallel", "parallel"),
        name="cross",
    )(x, g, kt, v, wq, wo)


def _ffn_kernel(x_ref, g_ref, wg_ref, wu_ref, wd_ref, gf_ref, o_ref):
    x = x_ref[...]
    hb = _rms(x, g_ref[...]).astype(BF16)
    n_chunks = wg_ref.shape[1] // FF_CHUNK
    cols = lambda c: slice(c * FF_CHUNK, (c + 1) * FF_CHUNK)

    def up(c):
        return _dot(hb, wg_ref[:, cols(c)]), _dot(hb, wu_ref[:, cols(c)])

    acc = x
    nxt = up(0)
    for c in range(n_chunks):
        gate, upv = nxt
        if c + 1 < n_chunks:
            nxt = up(c + 1)
        act = gate * _sigmoid(gate) * upv
        acc = acc + _dot(act.astype(BF16), wd_ref[cols(c), :])
    o_ref[...] = _rms(acc, gf_ref[...])


def _ffn(x2, g, wg, wu, wd, gf):
    n, d = x2.shape
    t = ROW_TILE
    tok = pl.BlockSpec((t, d), lambda i: (i, 0))
    return pl.pallas_call(
        _ffn_kernel,
        grid=(n // t,),
        in_specs=[tok, _full((1, d)), _full(wg.shape), _full(wu.shape), _full(wd.shape), _full((1, d))],
        out_specs=tok,
        out_shape=jax.ShapeDtypeStruct((n, d), F32),
        compiler_params=_params("parallel"),
        name="ffn",
    )(x2, g, wg, wu, wd, gf)


def _rope_tables(s):
    def cs(dim):
        inv = ROPE_THETA ** (-jnp.arange(0, dim, 2, dtype=F32) / dim)
        ang = jnp.arange(s).astype(F32)[:, None] * inv[None, :]
        return jnp.cos(ang), jnp.sin(ang)

    ca, sa = cs(A_HEAD_DIM)
    cr, sr = cs(R_QK_DIM)
    return (jnp.concatenate([ca, ca, ca, ca], axis=1), jnp.concatenate([-sa, sa, -sa, sa], axis=1),
            jnp.concatenate([cr, cr], axis=1), jnp.concatenate([-sr, sr], axis=1),
            ca.T, sa.T, cr.T, sr.T)


def _decay_tables(s):
    c = R_CHUNK
    log_g = np.log(1.0 - np.exp2(-5.0 - np.arange(R_HEADS, dtype=np.float64)))
    i = np.arange(c, dtype=np.float64)
    diff = i[:, None] - i[None, :]
    dmask = np.where(diff >= 0, np.exp(log_g[:, None, None] * np.maximum(diff, 0.0)), 0.0)
    q_dec = np.exp(log_g[:, None] * (i + 1.0))
    k_dec = np.exp(log_g[:, None] * (c - 1.0 - i))
    kdec_t = np.zeros((8, s), np.float64)
    kdec_t[:R_HEADS] = np.tile(k_dec, (1, s // c))
    qdec = np.broadcast_to(q_dec[:, :, None], (R_HEADS, c, R_V_DIM))
    chunk_dec = tuple(float(v) for v in np.exp(log_g * c))
    return (jnp.asarray(dmask, F32), jnp.asarray(qdec, F32), jnp.asarray(kdec_t, F32), chunk_dec)


def kernel(x, mem, norm_mix, w_in, w_branch_a, w_branch_b, w_out, norm_cross, norm_mem,
           w_xq, w_xkv, w_xo, norm_ffn, w_gate, w_up, w_down, norm_final):
    b, s, d = x.shape
    assert s % TOKEN_TILE == 0 and s % ROW_TILE == 0 and d % 128 == 0
    rope = _rope_tables(s)
    dmask, qdec, kdec_t, chunk_dec = _decay_tables(s)
    row = lambda v: v.reshape(1, d)
    o_aq, o_ak, o_av = 0, A_WIDTH, 2 * A_WIDTH
    o_rq = 3 * A_WIDTH
    o_rk = o_rq + R_QK_WIDTH
    o_rv = o_rk + R_QK_WIDTH
    o_g = o_rv + R_V_WIDTH

    for l in range(w_in.shape[0]):
        w = w_in[l]
        w_nat = jnp.concatenate([w[:, o_ak:o_av], w[:, o_rq:o_rk], w[:, o_rv:o_g]], axis=1).astype(BF16)
        w_t = jnp.concatenate([w[:, o_aq:o_ak], w[:, o_av:o_rq], w[:, o_rk:o_rv]], axis=1).T.astype(BF16)
        k_a, kmean, qt_a, vt_a, rq, rv, rkt, rktd = _qkv(x, row(norm_mix[l]), w_nat, w_t, rope + (kdec_t,))
        y_a = _moba(qt_a, k_a, vt_a, kmean.reshape(b, s // MOBA_BLOCK, A_WIDTH))
        gn = _retention(rq, rkt, rktd, rv, dmask, qdec, chunk_dec)
        x = _merge(x.reshape(b * s, d), row(norm_mix[l]), gn.reshape(b * s, R_V_WIDTH),
                   y_a.reshape(b * s, A_WIDTH), w[:, o_g:].astype(BF16), w_branch_a[l].astype(BF16),
                   w_branch_b[l].astype(BF16), w_out[l].astype(BF16)).reshape(b, s, d)
        kt_m, v_m = _memkv(mem, row(norm_mem[l]), w_xkv[l].astype(BF16))
        x = _cross(x, row(norm_cross[l]), kt_m, v_m, w_xq[l].astype(BF16), w_xo[l].astype(BF16))
        last = l == w_in.shape[0] - 1
        gf = row(norm_final) if last else jnp.ones((1, d), F32)
        x = _ffn(x.reshape(b * s, d), row(norm_ffn[l]), w_gate[l].astype(BF16), w_up[l].astype(BF16),
                 w_down[l].astype(BF16), gf).reshape(b, s, d)
        if not last:
            raise NotImplementedError("the fused final norm assumes a single layer")
    return x
```

```python
import functools
import math

import numpy as np
import jax
import jax.numpy as jnp
from jax import lax
from jax.experimental import pallas as pl
from jax.experimental.pallas import tpu as pltpu

F32 = jnp.float32
BF16 = jnp.bfloat16

EPS = 1e-6
ROPE_THETA = 10000.0
A_HEADS = 8
A_HEAD_DIM = 64
A_WIDTH = A_HEADS * A_HEAD_DIM
MOBA_BLOCK = 256
MOBA_TOPK = 3
R_HEADS = 4
R_QK_DIM = 128
R_V_DIM = 256
R_QK_WIDTH = R_HEADS * R_QK_DIM
R_V_WIDTH = R_HEADS * R_V_DIM
R_CHUNK = 128
X_HEADS = 4
FF_CHUNK = 256
TOKEN_TILE = MOBA_BLOCK
ROW_TILE = 1024
QKV_BLOCKS = 2
RET_BLOCKS = 2
NEG = -1e30
LOG2_E = math.log2(math.e)
MOBA_LOOKAHEAD = 5
MOBA_UNROLL = 4
MOBA_ONES_ROWS = 16
VMEM_LIMIT = 56 * 1024 * 1024


def _dot(a, b):
    return jnp.dot(a, b, preferred_element_type=F32)


def _rms(x, g):
    return x * lax.rsqrt(jnp.mean(x * x, axis=-1, keepdims=True) + EPS) * g


def _sigmoid(x):
    return 1.0 / (1.0 + jnp.exp(-x))


def _params(*sem):
    return pltpu.CompilerParams(dimension_semantics=sem, vmem_limit_bytes=VMEM_LIMIT)


def _full(shape):
    n = len(shape)
    return pl.BlockSpec(shape, lambda *_: (0,) * n, pipeline_mode=pl.Buffered(1))


def _qkv_kernel(*refs):
    for sb in range(QKV_BLOCKS):
        _qkv_block(sb, *refs)


def _qkv_block(sb, x_ref, g_ref, wn_ref, wt_ref, cosa_ref, sina_ref, cosr_ref, sinr_ref,
               costa_ref, sinta_ref, costr_ref, sintr_ref, kdec_ref,
               k_ref, kmean_ref, qt_ref, vt_ref, rq_ref, rv_ref, rkt_ref, rktd_ref):
    t = TOKEN_TILE
    tok = slice(sb * t, (sb + 1) * t)
    h = _rms(x_ref[0, tok, :], g_ref[...])
    hb = h.astype(BF16)
    ht = h.T.astype(BF16)
    lane = lax.broadcasted_iota(jnp.int32, (t, 128), 1)
    first_half = (lane & 32) == 0

    ak = _dot(hb, wn_ref[:, 0:A_WIDTH])
    cosa, sina = cosa_ref[tok, :], sina_ref[tok, :]
    for c in range(A_WIDTH // 128):
        blk = ak[:, c * 128:(c + 1) * 128]
        rot = jnp.where(first_half, pltpu.roll(blk, 96, 1), pltpu.roll(blk, 32, 1))
        kr = blk * cosa + rot * sina
        k_ref[0, sb, :, c * 128:(c + 1) * 128] = kr.astype(BF16)
        kmean_ref[0, sb, :, c * 128:(c + 1) * 128] = jnp.mean(kr, axis=0, keepdims=True)

    rq = _dot(hb, wn_ref[:, A_WIDTH:A_WIDTH + R_QK_WIDTH])
    cosr, sinr = cosr_ref[tok, :], sinr_ref[tok, :]
    for c in range(R_HEADS):
        blk = rq[:, c * 128:(c + 1) * 128]
        rq_ref[0, tok, c * 128:(c + 1) * 128] = (blk * cosr + pltpu.roll(blk, 64, 1) * sinr).astype(BF16)

    rv_ref[0, tok, :] = _dot(hb, wn_ref[:, A_WIDTH + R_QK_WIDTH:]).astype(BF16)

    qt = _dot(wt_ref[0:A_WIDTH, :], ht)
    cost, sint = costa_ref[:, tok], sinta_ref[:, tok]
    half = A_HEAD_DIM // 2
    scale_a = A_HEAD_DIM ** -0.5 * LOG2_E
    for hd in range(A_HEADS):
        x1 = qt[hd * A_HEAD_DIM:hd * A_HEAD_DIM + half]
        x2 = qt[hd * A_HEAD_DIM + half:(hd + 1) * A_HEAD_DIM]
        qt_ref[0, sb, hd * A_HEAD_DIM:hd * A_HEAD_DIM + half, :] = ((x1 * cost - x2 * sint) * scale_a).astype(BF16)
        qt_ref[0, sb, hd * A_HEAD_DIM + half:(hd + 1) * A_HEAD_DIM, :] = ((x2 * cost + x1 * sint) * scale_a).astype(BF16)

    vt_ref[0, sb] = _dot(wt_ref[A_WIDTH:2 * A_WIDTH, :], ht).astype(BF16)

    rkt = _dot(wt_ref[2 * A_WIDTH:, :], ht)
    cost, sint = costr_ref[:, tok], sintr_ref[:, tok]
    half = R_QK_DIM // 2
    scale_r = R_QK_DIM ** -0.5
    for hd in range(R_HEADS):
        x1 = rkt[hd * R_QK_DIM:hd * R_QK_DIM + half]
        x2 = rkt[hd * R_QK_DIM + half:(hd + 1) * R_QK_DIM]
        o1 = (x1 * cost - x2 * sint) * scale_r
        o2 = (x2 * cost + x1 * sint) * scale_r
        dec = kdec_ref[hd:hd + 1, tok]
        rkt_ref[0, sb, hd * R_QK_DIM:hd * R_QK_DIM + half, :] = o1.astype(BF16)
        rkt_ref[0, sb, hd * R_QK_DIM + half:(hd + 1) * R_QK_DIM, :] = o2.astype(BF16)
        rktd_ref[0, sb, hd * R_QK_DIM:hd * R_QK_DIM + half, :] = (o1 * dec).astype(BF16)
        rktd_ref[0, sb, hd * R_QK_DIM + half:(hd + 1) * R_QK_DIM, :] = (o2 * dec).astype(BF16)


def _qkv(x, g, w_nat, w_t, tabs):
    b, s, d = x.shape
    nb = s // TOKEN_TILE
    qb = QKV_BLOCKS
    t = qb * TOKEN_TILE
    tok = lambda w: pl.BlockSpec((1, t, w), lambda bi, i: (bi, i, 0))
    blk_t = lambda r: pl.BlockSpec((1, qb, r, TOKEN_TILE), lambda bi, i: (bi, i, 0, 0))
    nat_tab = pl.BlockSpec((t, 128), lambda bi, i: (i, 0))
    tr_tab = lambda r: pl.BlockSpec((r, t), lambda bi, i: (0, i))
    return pl.pallas_call(
        _qkv_kernel,
        grid=(b, nb // qb),
        in_specs=[tok(d), _full((1, d)), _full(w_nat.shape), _full(w_t.shape),
                  nat_tab, nat_tab, nat_tab, nat_tab,
                  tr_tab(A_HEAD_DIM // 2), tr_tab(A_HEAD_DIM // 2),
                  tr_tab(R_QK_DIM // 2), tr_tab(R_QK_DIM // 2), tr_tab(8)],
        out_specs=[pl.BlockSpec((1, qb, TOKEN_TILE, A_WIDTH), lambda bi, i: (bi, i, 0, 0)),
                   pl.BlockSpec((1, qb, 1, A_WIDTH), lambda bi, i: (bi, i, 0, 0)),
                   blk_t(A_WIDTH), blk_t(A_WIDTH), tok(R_QK_WIDTH), tok(R_V_WIDTH),
                   blk_t(R_QK_WIDTH), blk_t(R_QK_WIDTH)],
        out_shape=[jax.ShapeDtypeStruct((b, nb, TOKEN_TILE, A_WIDTH), BF16),
                   jax.ShapeDtypeStruct((b, nb, 1, A_WIDTH), F32),
                   jax.ShapeDtypeStruct((b, nb, A_WIDTH, TOKEN_TILE), BF16),
                   jax.ShapeDtypeStruct((b, nb, A_WIDTH, TOKEN_TILE), BF16),
                   jax.ShapeDtypeStruct((b, s, R_QK_WIDTH), BF16),
                   jax.ShapeDtypeStruct((b, s, R_V_WIDTH), BF16),
                   jax.ShapeDtypeStruct((b, nb, R_QK_WIDTH, TOKEN_TILE), BF16),
                   jax.ShapeDtypeStruct((b, nb, R_QK_WIDTH, TOKEN_TILE), BF16)],
        compiler_params=_params("parallel", "parallel"),
        name="qkv",
    )(x, g, w_nat, w_t, *tabs)


def _moba_kernel(qt_ref, k_ref, vt_ref, km_ref, o_ref, qm_ref, m_ref, acc_ref, ot_ref, s_ref):
    i = pl.program_id(1)
    nb = km_ref.shape[1]
    t = MOBA_BLOCK
    dh = A_HEAD_DIM
    row = lax.broadcasted_iota(jnp.int32, (2 * dh, t), 0)
    blk = lax.broadcasted_iota(jnp.int32, (nb, t), 0)
    past = blk < i
    causal = (lax.broadcasted_iota(jnp.int32, (t, t), 0) <= lax.broadcasted_iota(jnp.int32, (t, t), 1))
    ones = jnp.ones((MOBA_ONES_ROWS, t), BF16)
    lanes = lambda hd: slice((hd // 2) * 128, (hd // 2 + 1) * 128)
    rows = lambda hd: slice(hd * dh, (hd + 1) * dh)

    for hd in range(A_HEADS):
        hh = hd % 2
        qt_pair = qt_ref[0, 0, lanes(hd), :]
        mine = (row >= hh * dh) & (row < (hh + 1) * dh)
        qt = jnp.where(mine, qt_pair, jnp.zeros_like(qt_pair))
        qm_ref[hd, 0:2 * dh, :] = qt

        km = km_ref[0, :, lanes(hd)]
        km_hi = km.astype(BF16)
        km_lo = (km - km_hi.astype(F32)).astype(BF16)
        bs = jnp.where(past, _dot(km_hi, qt) + _dot(km_lo, qt), -jnp.inf)
        cnt = jnp.zeros((nb, t), F32)
        for m in range(nb):
            r = bs[m:m + 1, :]
            tie = jnp.where(blk > m, 1.0, 0.0)
            cnt = cnt + jnp.where(r > bs, 1.0, jnp.where(r == bs, tie, 0.0))
        bias = jnp.where(past & (cnt < MOBA_TOPK), 0.0, NEG).astype(BF16)
        qm_ref[hd, 2 * dh:, :] = jnp.concatenate([bias, jnp.zeros((2 * dh - nb, t), BF16)], axis=0)

    key_lane = lax.broadcasted_iota(jnp.int32, (t, 2 * dh), 1)

    def keys_with_block_column(j):
        onehot = jnp.where(key_lane == j, 1.0, 0.0).astype(BF16)
        return [jnp.concatenate([k_ref[0, j, :, lanes(2 * p)], onehot], axis=1) for p in range(A_HEADS // 2)]

    def pv(j, hd, p):
        vt_aug = jnp.concatenate([vt_ref[0, j, rows(hd), :], ones], axis=0)
        return _dot(vt_aug, p.astype(BF16))

    def own_unit(hd):
        def scores():
            return jnp.where(causal, _dot(k_ref[0, i, :, lanes(hd)], qm_ref[hd, 0:2 * dh, :]), NEG)

        def update(s, m0):
            m_ref[hd] = jnp.broadcast_to(m0, (8, t))
            acc_ref[hd] = pv(i, hd, jnp.exp2(s - m0))

        return scores, update

    def past_units(j):
        keys = keys_with_block_column(j)

        def unit(hd):
            def scores():
                return _dot(keys[hd // 2], qm_ref[hd])

            def update(s, m_blk):
                m_old = m_ref[hd]
                m_new = jnp.maximum(m_old, m_blk)
                alpha = jnp.exp2(m_old - m_new)
                m_ref[hd] = m_new
                acc_ref[hd] = alpha[0:1, :] * acc_ref[hd] + pv(j, hd, jnp.exp2(s - m_new[0:1, :]))

            return scores, update

        return [unit(hd) for hd in range(A_HEADS)]

    def pipelined(units):
        blk_max = {}
        slots = MOBA_LOOKAHEAD + 1
        for step in range(len(units) + MOBA_LOOKAHEAD):
            if step < len(units):
                sc = units[step][0]()
                s_ref[step % slots] = sc
                blk_max[step] = jnp.max(sc, axis=0, keepdims=True)
            if step >= MOBA_LOOKAHEAD:
                u = step - MOBA_LOOKAHEAD
                units[u][1](s_ref[u % slots], blk_max.pop(u))

    pipelined([own_unit(hd) for hd in range(A_HEADS)])

    def group_body(jj, carry):
        pipelined([u for r in range(MOBA_UNROLL) for u in past_units(MOBA_UNROLL * jj + r)])
        return carry

    def single_body(j, carry):
        pipelined(past_units(j))
        return carry

    n_groups = i // MOBA_UNROLL
    lax.fori_loop(0, n_groups, group_body, 0)
    lax.fori_loop(n_groups * MOBA_UNROLL, i, single_body, 0)

    for hd in range(A_HEADS):
        ot_ref[rows(hd), :] = acc_ref[hd, 0:dh, :] / acc_ref[hd, dh:dh + 1, :]
    o_ref[0] = ot_ref[...].T.astype(BF16)


def _moba(qt, k, vt, kmean):
    b, nb, t, w = k.shape
    return pl.pallas_call(
        _moba_kernel,
        grid=(b, nb),
        in_specs=[pl.BlockSpec((1, 1, w, t), lambda bi, i: (bi, i, 0, 0)),
                  pl.BlockSpec((1, nb, t, w), lambda bi, i: (bi, 0, 0, 0)),
                  pl.BlockSpec((1, nb, w, t), lambda bi, i: (bi, 0, 0, 0)),
                  pl.BlockSpec((1, nb, w), lambda bi, i: (bi, 0, 0))],
        out_specs=pl.BlockSpec((1, t, w), lambda bi, i: (bi, i, 0)),
        out_shape=jax.ShapeDtypeStruct((b, nb * t, w), BF16),
        scratch_shapes=[pltpu.VMEM((A_HEADS, 4 * A_HEAD_DIM, t), BF16),
                        pltpu.VMEM((A_HEADS, 8, t), F32),
                        pltpu.VMEM((A_HEADS, A_HEAD_DIM + MOBA_ONES_ROWS, t), F32),
                        pltpu.VMEM((w, t), F32),
                        pltpu.VMEM((MOBA_LOOKAHEAD + 1, t, t), F32)],
        compiler_params=_params("parallel", "parallel"),
        name="moba",
    )(qt, k, vt, kmean)


def _retention_kernel(chunk_dec, rq_ref, rkt_ref, rktd_ref, rv_ref, dmask_ref, qdec_ref, o_ref, state_ref):
    @pl.when(pl.program_id(1) == 0)
    def _():
        state_ref[...] = jnp.zeros_like(state_ref)

    c = R_CHUNK
    per_blk = TOKEN_TILE // c
    n_chunks = RET_BLOCKS * per_blk
    rows = lambda ci: slice(ci * c, (ci + 1) * c)
    qk = lambda hd: slice(hd * R_QK_DIM, (hd + 1) * R_QK_DIM)
    vv = lambda hd: slice(hd * R_V_DIM, (hd + 1) * R_V_DIM)
    units = [(ci, hd) for ci in range(n_chunks) for hd in range(R_HEADS)]
    keys_t = lambda ref, ci, hd: ref[0, ci // per_blk, qk(hd), rows(ci % per_blk)]

    att = {u: _dot(rq_ref[0, rows(u[0]), qk(u[1])], keys_t(rkt_ref, *u)) for u in units}
    kv = {u: _dot(keys_t(rktd_ref, *u), rv_ref[0, rows(u[0]), vv(u[1])]) for u in units}

    cross = {}
    for hd in range(R_HEADS):
        st = state_ref[hd]
        for ci in range(n_chunks):
            cross[ci, hd] = _dot(rq_ref[0, rows(ci), qk(hd)], st.astype(BF16))
            st = chunk_dec[hd] * st + kv[ci, hd]
        state_ref[hd] = st

    for ci, hd in units:
        inner = _dot((att[ci, hd] * dmask_ref[hd]).astype(BF16), rv_ref[0, rows(ci), vv(hd)])
        y = inner + cross[ci, hd] * qdec_ref[hd]
        mu = jnp.mean(y, axis=-1, keepdims=True)
        yc = y - mu
        var = jnp.mean(yc * yc, axis=-1, keepdims=True)
        o_ref[0, rows(ci), vv(hd)] = (yc * lax.rsqrt(var + EPS)).astype(BF16)


def _retention(rq, rkt, rktd, rv, dmask, qdec, chunk_dec):
    b, s, _ = rq.shape
    t = RET_BLOCKS * TOKEN_TILE
    tok = lambda w: pl.BlockSpec((1, t, w), lambda bi, i: (bi, i, 0))
    blk_t = pl.BlockSpec((1, RET_BLOCKS, R_QK_WIDTH, TOKEN_TILE), lambda bi, i: (bi, i, 0, 0))
    return pl.pallas_call(
        functools.partial(_retention_kernel, chunk_dec),
        grid=(b, s // t),
        in_specs=[tok(R_QK_WIDTH), blk_t, blk_t, tok(R_V_WIDTH), _full(dmask.shape), _full(qdec.shape)],
        out_specs=tok(R_V_WIDTH),
        out_shape=jax.ShapeDtypeStruct((b, s, R_V_WIDTH), BF16),
        scratch_shapes=[pltpu.VMEM((R_HEADS, R_QK_DIM, R_V_DIM), F32)],
        compiler_params=_params("parallel", "arbitrary"),
        name="retention",
    )(rq, rkt, rktd, rv, dmask, qdec)


def _merge_kernel(x_ref, g_ref, gn_ref, ya_ref, wg_ref, wa_ref, wb_ref, wo_ref, o_ref):
    x = x_ref[...]
    d = x.shape[1]
    hb = _rms(x, g_ref[...]).astype(BF16)
    rg = _dot(hb, wg_ref[:, 0:R_V_WIDTH])
    ta = _dot(ya_ref[...], wa_ref[...])
    ga = _dot(hb, wg_ref[:, R_V_WIDTH:R_V_WIDTH + d])
    gb = _dot(hb, wg_ref[:, R_V_WIDTH + d:])
    yb = gn_ref[...].astype(F32) * (rg * _sigmoid(rg))
    tb = _dot(yb.astype(BF16), wb_ref[...])
    merged = _sigmoid(ga) * ta + _sigmoid(gb) * tb
    o_ref[...] = x + _dot(merged.astype(BF16), wo_ref[...])


def _merge(x2, g, gn2, ya2, wg, wa, wb, wo):
    n, d = x2.shape
    t = ROW_TILE
    tok = lambda w: pl.BlockSpec((t, w), lambda i: (i, 0))
    return pl.pallas_call(
        _merge_kernel,
        grid=(n // t,),
        in_specs=[tok(d), _full((1, d)), tok(R_V_WIDTH), tok(A_WIDTH),
                  _full(wg.shape), _full(wa.shape), _full(wb.shape), _full(wo.shape)],
        out_specs=tok(d),
        out_shape=jax.ShapeDtypeStruct((n, d), F32),
        compiler_params=_params("parallel"),
        name="merge",
    )(x2, g, gn2, ya2, wg, wa, wb, wo)


def _memkv_kernel(m_ref, g_ref, w_ref, kt_ref, v_ref):
    d = m_ref.shape[2]
    mb = _rms(m_ref[0], g_ref[...]).astype(BF16)
    kt_ref[0] = _dot(mb, w_ref[:, 0:d]).T.astype(BF16)
    v_ref[0] = _dot(mb, w_ref[:, d:]).astype(BF16)


def _memkv(mem, g, w_kv):
    b, m, d = mem.shape
    return pl.pallas_call(
        _memkv_kernel,
        grid=(b,),
        in_specs=[pl.BlockSpec((1, m, d), lambda bi: (bi, 0, 0)), _full((1, d)), _full(w_kv.shape)],
        out_specs=[pl.BlockSpec((1, d, m), lambda bi: (bi, 0, 0)), pl.BlockSpec((1, m, d), lambda bi: (bi, 0, 0))],
        out_shape=[jax.ShapeDtypeStruct((b, d, m), BF16), jax.ShapeDtypeStruct((b, m, d), BF16)],
        compiler_params=_params("parallel"),
        name="memkv",
    )(mem, g, w_kv)


def _cross_kernel(x_ref, g_ref, kt_ref, v_ref, wq_ref, wo_ref, o_ref, cat_ref):
    x = x_ref[0]
    d = x.shape[1]
    dh = d // X_HEADS
    hb = _rms(x, g_ref[...]).astype(BF16)
    q = (_dot(hb, wq_ref[...]) * dh ** -0.5).astype(BF16)
    cols = lambda hd: slice(hd * dh, (hd + 1) * dh)
    scores = [_dot(q[:, cols(hd)], kt_ref[0, cols(hd), :]) for hd in range(X_HEADS)]
    for hd, s in enumerate(scores):
        p = jnp.exp(s - jnp.max(s, axis=-1, keepdims=True))
        p = p / jnp.sum(p, axis=-1, keepdims=True)
        cat_ref[:, cols(hd)] = _dot(p.astype(BF16), v_ref[0, :, cols(hd)]).astype(BF16)
    o_ref[0] = x + _dot(cat_ref[...], wo_ref[...])


def _cross(x, g, kt, v, wq, wo):
    b, s, d = x.shape
    m = v.shape[1]
    t = ROW_TILE
    tok = pl.BlockSpec((1, t, d), lambda bi, i: (bi, i, 0))
    return pl.pallas_call(
        _cross_kernel,
        grid=(b, s // t),
        in_specs=[tok, _full((1, d)), pl.BlockSpec((1, d, m), lambda bi, i: (bi, 0, 0)),
                  pl.BlockSpec((1, m, d), lambda bi, i: (bi, 0, 0)), _full(wq.shape), _full(wo.shape)],
        out_specs=tok,
        out_shape=jax.ShapeDtypeStruct((b, s, d), F32),
        scratch_shapes=[pltpu.VMEM((t, d), BF16)],
        compiler_params=_params("parallel", "parallel"),
        name="cross",
    )(x, g, kt, v, wq, wo)


def _ffn_kernel(x_ref, g_ref, wg_ref, wu_ref, wd_ref, gf_ref, o_ref):
    x = x_ref[...]
    hb = _rms(x, g_ref[...]).astype(BF16)
    n_chunks = wg_ref.shape[1] // FF_CHUNK
    cols = lambda c: slice(c * FF_CHUNK, (c + 1) * FF_CHUNK)

    def up(c):
        return _dot(hb, wg_ref[:, cols(c)]), _dot(hb, wu_ref[:, cols(c)])

    acc = x
    nxt = up(0)
    for c in range(n_chunks):
        gate, upv = nxt
        if c + 1 < n_chunks:
            nxt = up(c + 1)
        act = gate * _sigmoid(gate) * upv
        acc = acc + _dot(act.astype(BF16), wd_ref[cols(c), :])
    o_ref[...] = _rms(acc, gf_ref[...])


def _ffn(x2, g, wg, wu, wd, gf):
    n, d = x2.shape
    t = ROW_TILE
    tok = pl.BlockSpec((t, d), lambda i: (i, 0))
    return pl.pallas_call(
        _ffn_kernel,
        grid=(n // t,),
        in_specs=[tok, _full((1, d)), _full(wg.shape), _full(wu.shape), _full(wd.shape), _full((1, d))],
        out_specs=tok,
        out_shape=jax.ShapeDtypeStruct((n, d), F32),
        compiler_params=_params("parallel"),
        name="ffn",
    )(x2, g, wg, wu, wd, gf)


def _rope_tables(s):
    def cs(dim):
        inv = ROPE_THETA ** (-jnp.arange(0, dim, 2, dtype=F32) / dim)
        ang = jnp.arange(s).astype(F32)[:, None] * inv[None, :]
        return jnp.cos(ang), jnp.sin(ang)

    ca, sa = cs(A_HEAD_DIM)
    cr, sr = cs(R_QK_DIM)
    return (jnp.concatenate([ca, ca, ca, ca], axis=1), jnp.concatenate([-sa, sa, -sa, sa], axis=1),
            jnp.concatenate([cr, cr], axis=1), jnp.concatenate([-sr, sr], axis=1),
            ca.T, sa.T, cr.T, sr.T)


def _decay_tables(s):
    c = R_CHUNK
    log_g = np.log(1.0 - np.exp2(-5.0 - np.arange(R_HEADS, dtype=np.float64)))
    i = np.arange(c, dtype=np.float64)
    diff = i[:, None] - i[None, :]
    dmask = np.where(diff >= 0, np.exp(log_g[:, None, None] * np.maximum(diff, 0.0)), 0.0)
    q_dec = np.exp(log_g[:, None] * (i + 1.0))
    k_dec = np.exp(log_g[:, None] * (c - 1.0 - i))
    kdec_t = np.zeros((8, s), np.float64)
    kdec_t[:R_HEADS] = np.tile(k_dec, (1, s // c))
    qdec = np.broadcast_to(q_dec[:, :, None], (R_HEADS, c, R_V_DIM))
    chunk_dec = tuple(float(v) for v in np.exp(log_g * c))
    return (jnp.asarray(dmask, F32), jnp.asarray(qdec, F32), jnp.asarray(kdec_t, F32), chunk_dec)


def kernel(x, mem, norm_mix, w_in, w_branch_a, w_branch_b, w_out, norm_cross, norm_mem,
           w_xq, w_xkv, w_xo, norm_ffn, w_gate, w_up, w_down, norm_final):
    b, s, d = x.shape
    assert s % TOKEN_TILE == 0 and s % ROW_TILE == 0 and d % 128 == 0
    rope = _rope_tables(s)
    dmask, qdec, kdec_t, chunk_dec = _decay_tables(s)
    row = lambda v: v.reshape(1, d)
    o_aq, o_ak, o_av = 0, A_WIDTH, 2 * A_WIDTH
    o_rq = 3 * A_WIDTH
    o_rk = o_rq + R_QK_WIDTH
    o_rv = o_rk + R_QK_WIDTH
    o_g = o_rv + R_V_WIDTH

    for l in range(w_in.shape[0]):
        w = w_in[l]
        w_nat = jnp.concatenate([w[:, o_ak:o_av], w[:, o_rq:o_rk], w[:, o_rv:o_g]], axis=1).astype(BF16)
        w_t = jnp.concatenate([w[:, o_aq:o_ak], w[:, o_av:o_rq], w[:, o_rk:o_rv]], axis=1).T.astype(BF16)
        k_a, kmean, qt_a, vt_a, rq, rv, rkt, rktd = _qkv(x, row(norm_mix[l]), w_nat, w_t, rope + (kdec_t,))
        y_a = _moba(qt_a, k_a, vt_a, kmean.reshape(b, s // MOBA_BLOCK, A_WIDTH))
        gn = _retention(rq, rkt, rktd, rv, dmask, qdec, chunk_dec)
        x = _merge(x.reshape(b * s, d), row(norm_mix[l]), gn.reshape(b * s, R_V_WIDTH),
                   y_a.reshape(b * s, A_WIDTH), w[:, o_g:].astype(BF16), w_branch_a[l].astype(BF16),
                   w_branch_b[l].astype(BF16), w_out[l].astype(BF16)).reshape(b, s, d)
        kt_m, v_m = _memkv(mem, row(norm_mem[l]), w_xkv[l].astype(BF16))
        x = _cross(x, row(norm_cross[l]), kt_m, v_m, w_xq[l].astype(BF16), w_xo[l].astype(BF16))
        last = l == w_in.shape[0] - 1
        gf = row(norm_final) if last else jnp.ones((1, d), F32)
        x = _ffn(x.reshape(b * s, d), row(norm_ffn[l]), w_gate[l].astype(BF16), w_up[l].astype(BF16),
                 w_down[l].astype(BF16), gf).reshape(b, s, d)
        if not last:
            raise NotImplementedError("the fused final norm assumes a single layer")
    return x
```

```python
import functools
import math

import numpy as np
import jax
import jax.numpy as jnp
from jax import lax
from jax.experimental import pallas as pl
from jax.experimental.pallas import tpu as pltpu

F32 = jnp.float32
BF16 = jnp.bfloat16

EPS = 1e-6
ROPE_THETA = 10000.0
A_HEADS = 8
A_HEAD_DIM = 64
A_WIDTH = A_HEADS * A_HEAD_DIM
MOBA_BLOCK = 256
MOBA_TOPK = 3
R_HEADS = 4
R_QK_DIM = 128
R_V_DIM = 256
R_QK_WIDTH = R_HEADS * R_QK_DIM
R_V_WIDTH = R_HEADS * R_V_DIM
R_CHUNK = 128
X_HEADS = 4
FF_CHUNK = 256
TOKEN_TILE = MOBA_BLOCK
ROW_TILE = 1024
QKV_BLOCKS = 2
RET_BLOCKS = 2
NEG = -1e30
LOG2_E = math.log2(math.e)
MOBA_LOOKAHEAD = 5
MOBA_UNROLL = 4
MOBA_ONES_ROWS = 16
VMEM_LIMIT = 56 * 1024 * 1024


def _dot(a, b):
    return jnp.dot(a, b, preferred_element_type=F32)


def _rms(x, g):
    return x * lax.rsqrt(jnp.mean(x * x, axis=-1, keepdims=True) + EPS) * g


def _sigmoid(x):
    return 1.0 / (1.0 + jnp.exp(-x))


def _params(*sem):
    return pltpu.CompilerParams(dimension_semantics=sem, vmem_limit_bytes=VMEM_LIMIT)


def _full(shape):
    n = len(shape)
    return pl.BlockSpec(shape, lambda *_: (0,) * n, pipeline_mode=pl.Buffered(1))


def _qkv_kernel(*refs):
    for sb in range(QKV_BLOCKS):
        _qkv_block(sb, *refs)


def _qkv_block(sb, x_ref, g_ref, wn_ref, wt_ref, cosa_ref, sina_ref, cosr_ref, sinr_ref,
               costa_ref, sinta_ref, costr_ref, sintr_ref, kdec_ref,
               k_ref, kmean_ref, qt_ref, vt_ref, rq_ref, rv_ref, rkt_ref, rktd_ref):
    t = TOKEN_TILE
    tok = slice(sb * t, (sb + 1) * t)
    h = _rms(x_ref[0, tok, :], g_ref[...])
    hb = h.astype(BF16)
    ht = h.T.astype(BF16)
    lane = lax.broadcasted_iota(jnp.int32, (t, 128), 1)
    first_half = (lane & 32) == 0

    ak = _dot(hb, wn_ref[:, 0:A_WIDTH])
    cosa, sina = cosa_ref[tok, :], sina_ref[tok, :]
    for c in range(A_WIDTH // 128):
        blk = ak[:, c * 128:(c + 1) * 128]
        rot = jnp.where(first_half, pltpu.roll(blk, 96, 1), pltpu.roll(blk, 32, 1))
        kr = blk * cosa + rot * sina
        k_ref[0, sb, :, c * 128:(c + 1) * 128] = kr.astype(BF16)
        kmean_ref[0, sb, :, c * 128:(c + 1) * 128] = jnp.mean(kr, axis=0, keepdims=True)

    rq = _dot(hb, wn_ref[:, A_WIDTH:A_WIDTH + R_QK_WIDTH])
    cosr, sinr = cosr_ref[tok, :], sinr_ref[tok, :]
    for c in range(R_HEADS):
        blk = rq[:, c * 128:(c + 1) * 128]
        rq_ref[0, tok, c * 128:(c + 1) * 128] = (blk * cosr + pltpu.roll(blk, 64, 1) * sinr).astype(BF16)

    rv_ref[0, tok, :] = _dot(hb, wn_ref[:, A_WIDTH + R_QK_WIDTH:]).astype(BF16)

    qt = _dot(wt_ref[0:A_WIDTH, :], ht)
    cost, sint = costa_ref[:, tok], sinta_ref[:, tok]
    half = A_HEAD_DIM // 2
    scale_a = A_HEAD_DIM ** -0.5 * LOG2_E
    for hd in range(A_HEADS):
        x1 = qt[hd * A_HEAD_DIM:hd * A_HEAD_DIM + half]
        x2 = qt[hd * A_HEAD_DIM + half:(hd + 1) * A_HEAD_DIM]
        qt_ref[0, sb, hd * A_HEAD_DIM:hd * A_HEAD_DIM + half, :] = ((x1 * cost - x2 * sint) * scale_a).astype(BF16)
        qt_ref[0, sb, hd * A_HEAD_DIM + half:(hd + 1) * A_HEAD_DIM, :] = ((x2 * cost + x1 * sint) * scale_a).astype(BF16)

    vt_ref[0, sb] = _dot(wt_ref[A_WIDTH:2 * A_WIDTH, :], ht).astype(BF16)

    rkt = _dot(wt_ref[2 * A_WIDTH:, :], ht)
    cost, sint = costr_ref[:, tok], sintr_ref[:, tok]
    half = R_QK_DIM // 2
    scale_r = R_QK_DIM ** -0.5
    for hd in range(R_HEADS):
        x1 = rkt[hd * R_QK_DIM:hd * R_QK_DIM + half]
        x2 = rkt[hd * R_QK_DIM + half:(hd + 1) * R_QK_DIM]
        o1 = (x1 * cost - x2 * sint) * scale_r
        o2 = (x2 * cost + x1 * sint) * scale_r
        dec = kdec_ref[hd:hd + 1, tok]
        rkt_ref[0, sb, hd * R_QK_DIM:hd * R_QK_DIM + half, :] = o1.astype(BF16)
        rkt_ref[0, sb, hd * R_QK_DIM + half:(hd + 1) * R_QK_DIM, :] = o2.astype(BF16)
        rktd_ref[0, sb, hd * R_QK_DIM:hd * R_QK_DIM + half, :] = (o1 * dec).astype(BF16)
        rktd_ref[0, sb, hd * R_QK_DIM + half:(hd + 1) * R_QK_DIM, :] = (o2 * dec).astype(BF16)


def _qkv(x, g, w_nat, w_t, tabs):
    b, s, d = x.shape
    nb = s // TOKEN_TILE
    qb = QKV_BLOCKS
    t = qb * TOKEN_TILE
    tok = lambda w: pl.BlockSpec((1, t, w), lambda bi, i: (bi, i, 0))
    blk_t = lambda r: pl.BlockSpec((1, qb, r, TOKEN_TILE), lambda bi, i: (bi, i, 0, 0))
    nat_tab = pl.BlockSpec((t, 128), lambda bi, i: (i, 0))
    tr_tab = lambda r: pl.BlockSpec((r, t), lambda bi, i: (0, i))
    return pl.pallas_call(
        _qkv_kernel,
        grid=(b, nb // qb),
        in_specs=[tok(d), _full((1, d)), _full(w_nat.shape), _full(w_t.shape),
                  nat_tab, nat_tab, nat_tab, nat_tab,
                  tr_tab(A_HEAD_DIM // 2), tr_tab(A_HEAD_DIM // 2),
                  tr_tab(R_QK_DIM // 2), tr_tab(R_QK_DIM // 2), tr_tab(8)],
        out_specs=[pl.BlockSpec((1, qb, TOKEN_TILE, A_WIDTH), lambda bi, i: (bi, i, 0, 0)),
                   pl.BlockSpec((1, qb, 1, A_WIDTH), lambda bi, i: (bi, i, 0, 0)),
                   blk_t(A_WIDTH), blk_t(A_WIDTH), tok(R_QK_WIDTH), tok(R_V_WIDTH),
                   blk_t(R_QK_WIDTH), blk_t(R_QK_WIDTH)],
        out_shape=[jax.ShapeDtypeStruct((b, nb, TOKEN_TILE, A_WIDTH), BF16),
                   jax.ShapeDtypeStruct((b, nb, 1, A_WIDTH), F32),
                   jax.ShapeDtypeStruct((b, nb, A_WIDTH, TOKEN_TILE), BF16),
                   jax.ShapeDtypeStruct((b, nb, A_WIDTH, TOKEN_TILE), BF16),
                   jax.ShapeDtypeStruct((b, s, R_QK_WIDTH), BF16),
                   jax.ShapeDtypeStruct((b, s, R_V_WIDTH), BF16),
                   jax.ShapeDtypeStruct((b, nb, R_QK_WIDTH, TOKEN_TILE), BF16),
                   jax.ShapeDtypeStruct((b, nb, R_QK_WIDTH, TOKEN_TILE), BF16)],
        compiler_params=_params("parallel", "parallel"),
        name="qkv",
    )(x, g, w_nat, w_t, *tabs)


def _moba_kernel(qt_ref, k_ref, vt_ref, km_ref, o_ref, qm_ref, m_ref, acc_ref, ot_ref, s_ref):
    i = pl.program_id(1)
    nb = km_ref.shape[1]
    t = MOBA_BLOCK
    dh = A_HEAD_DIM
    row = lax.broadcasted_iota(jnp.int32, (2 * dh, t), 0)
    blk = lax.broadcasted_iota(jnp.int32, (nb, t), 0)
    past = blk < i
    causal = (lax.broadcasted_iota(jnp.int32, (t, t), 0) <= lax.broadcasted_iota(jnp.int32, (t, t), 1))
    ones = jnp.ones((MOBA_ONES_ROWS, t), BF16)
    lanes = lambda hd: slice((hd // 2) * 128, (hd // 2 + 1) * 128)
    rows = lambda hd: slice(hd * dh, (hd + 1) * dh)

    def select_blocks(hd):
        hh = hd % 2
        qt_pair = qt_ref[0, 0, lanes(hd), :]
        mine = (row >= hh * dh) & (row < (hh + 1) * dh)
        qt = jnp.where(mine, qt_pair, jnp.zeros_like(qt_pair))
        qm_ref[hd, 0:2 * dh, :] = qt

        km = km_ref[0, :, lanes(hd)]
        km_hi = km.astype(BF16)
        km_lo = (km - km_hi.astype(F32)).astype(BF16)
        bs = jnp.where(past, _dot(km_hi, qt) + _dot(km_lo, qt), -jnp.inf)
        cnt = jnp.zeros((nb, t), F32)
        for m in range(nb):
            r = bs[m:m + 1, :]
            tie = jnp.where(blk > m, 1.0, 0.0)
            cnt = cnt + jnp.where(r > bs, 1.0, jnp.where(r == bs, tie, 0.0))
        bias = jnp.where(past & (cnt < MOBA_TOPK), 0.0, NEG).astype(BF16)
        qm_ref[hd, 2 * dh:, :] = jnp.concatenate([bias, jnp.zeros((2 * dh - nb, t), BF16)], axis=0)

    key_lane = lax.broadcasted_iota(jnp.int32, (t, 2 * dh), 1)

    def keys_with_block_column(j):
        onehot = jnp.where(key_lane == j, 1.0, 0.0).astype(BF16)
        return [jnp.concatenate([k_ref[0, j, :, lanes(2 * p)], onehot], axis=1) for p in range(A_HEADS // 2)]

    def pv(j, hd, p):
        vt_aug = jnp.concatenate([vt_ref[0, j, rows(hd), :], ones], axis=0)
        return _dot(vt_aug, p.astype(BF16))

    def own_unit(hd):
        def scores():
            return jnp.where(causal, _dot(k_ref[0, i, :, lanes(hd)], qm_ref[hd, 0:2 * dh, :]), NEG)

        def update(s, m0):
            m_ref[hd] = jnp.broadcast_to(m0, (8, t))
            acc_ref[hd] = pv(i, hd, jnp.exp2(s - m0))

        return scores, update

    def past_units(j):
        keys = keys_with_block_column(j)

        def unit(hd):
            def scores():
                return _dot(keys[hd // 2], qm_ref[hd])

            def update(s, m_blk):
                m_old = m_ref[hd]
                m_new = jnp.maximum(m_old, m_blk)
                alpha = jnp.exp2(m_old - m_new)
                m_ref[hd] = m_new
                acc_ref[hd] = alpha[0:1, :] * acc_ref[hd] + pv(j, hd, jnp.exp2(s - m_new[0:1, :]))

            return scores, update

        return [unit(hd) for hd in range(A_HEADS)]

    def pipelined(units):
        blk_max = {}
        slots = MOBA_LOOKAHEAD + 1
        for step in range(len(units) + MOBA_LOOKAHEAD):
            if step < len(units):
                sc = units[step][0]()
                s_ref[step % slots] = sc
                blk_max[step] = jnp.max(sc, axis=0, keepdims=True)
            if step >= MOBA_LOOKAHEAD:
                u = step - MOBA_LOOKAHEAD
                units[u][1](s_ref[u % slots], blk_max.pop(u))

    n_groups = i // MOBA_UNROLL
    for left in range(MOBA_UNROLL):
        @pl.when(i % MOBA_UNROLL == left)
        def _(left=left):
            for hd in range(A_HEADS):
                select_blocks(hd)
            units = [own_unit(hd) for hd in range(A_HEADS)]
            for r in range(left):
                units += past_units(n_groups * MOBA_UNROLL + r)
            pipelined(units)

    def group_body(jj, carry):
        pipelined([u for r in range(MOBA_UNROLL) for u in past_units(MOBA_UNROLL * jj + r)])
        return carry

    lax.fori_loop(0, n_groups, group_body, 0)

    for hd in range(A_HEADS):
        ot_ref[rows(hd), :] = acc_ref[hd, 0:dh, :] / acc_ref[hd, dh:dh + 1, :]
    o_ref[0] = ot_ref[...].T.astype(BF16)


def _moba(qt, k, vt, kmean):
    b, nb, t, w = k.shape
    return pl.pallas_call(
        _moba_kernel,
        grid=(b, nb),
        in_specs=[pl.BlockSpec((1, 1, w, t), lambda bi, i: (bi, i, 0, 0)),
                  pl.BlockSpec((1, nb, t, w), lambda bi, i: (bi, 0, 0, 0)),
                  pl.BlockSpec((1, nb, w, t), lambda bi, i: (bi, 0, 0, 0)),
                  pl.BlockSpec((1, nb, w), lambda bi, i: (bi, 0, 0))],
        out_specs=pl.BlockSpec((1, t, w), lambda bi, i: (bi, i, 0)),
        out_shape=jax.ShapeDtypeStruct((b, nb * t, w), BF16),
        scratch_shapes=[pltpu.VMEM((A_HEADS, 4 * A_HEAD_DIM, t), BF16),
                        pltpu.VMEM((A_HEADS, 8, t), F32),
                        pltpu.VMEM((A_HEADS, A_HEAD_DIM + MOBA_ONES_ROWS, t), F32),
                        pltpu.VMEM((w, t), F32),
                        pltpu.VMEM((MOBA_LOOKAHEAD + 1, t, t), F32)],
        compiler_params=_params("parallel", "parallel"),
        name="moba",
    )(qt, k, vt, kmean)


def _retention_kernel(chunk_dec, rq_ref, rkt_ref, rktd_ref, rv_ref, dmask_ref, qdec_ref, o_ref, state_ref):
    @pl.when(pl.program_id(1) == 0)
    def _():
        state_ref[...] = jnp.zeros_like(state_ref)

    c = R_CHUNK
    per_blk = TOKEN_TILE // c
    n_chunks = RET_BLOCKS * per_blk
    rows = lambda ci: slice(ci * c, (ci + 1) * c)
    qk = lambda hd: slice(hd * R_QK_DIM, (hd + 1) * R_QK_DIM)
    vv = lambda hd: slice(hd * R_V_DIM, (hd + 1) * R_V_DIM)
    units = [(ci, hd) for ci in range(n_chunks) for hd in range(R_HEADS)]
    keys_t = lambda ref, ci, hd: ref[0, ci // per_blk, qk(hd), rows(ci % per_blk)]

    att = {u: _dot(rq_ref[0, rows(u[0]), qk(u[1])], keys_t(rkt_ref, *u)) for u in units}
    kv = {u: _dot(keys_t(rktd_ref, *u), rv_ref[0, rows(u[0]), vv(u[1])]) for u in units}

    cross = {}
    for hd in range(R_HEADS):
        st = state_ref[hd]
        for ci in range(n_chunks):
            cross[ci, hd] = _dot(rq_ref[0, rows(ci), qk(hd)], st.astype(BF16))
            st = chunk_dec[hd] * st + kv[ci, hd]
        state_ref[hd] = st

    for ci, hd in units:
        inner = _dot((att[ci, hd] * dmask_ref[hd]).astype(BF16), rv_ref[0, rows(ci), vv(hd)])
        y = inner + cross[ci, hd] * qdec_ref[hd]
        mu = jnp.mean(y, axis=-1, keepdims=True)
        yc = y - mu
        var = jnp.mean(yc * yc, axis=-1, keepdims=True)
        o_ref[0, rows(ci), vv(hd)] = (yc * lax.rsqrt(var + EPS)).astype(BF16)


def _retention(rq, rkt, rktd, rv, dmask, qdec, chunk_dec):
    b, s, _ = rq.shape
    t = RET_BLOCKS * TOKEN_TILE
    tok = lambda w: pl.BlockSpec((1, t, w), lambda bi, i: (bi, i, 0))
    blk_t = pl.BlockSpec((1, RET_BLOCKS, R_QK_WIDTH, TOKEN_TILE), lambda bi, i: (bi, i, 0, 0))
    return pl.pallas_call(
        functools.partial(_retention_kernel, chunk_dec),
        grid=(b, s // t),
        in_specs=[tok(R_QK_WIDTH), blk_t, blk_t, tok(R_V_WIDTH), _full(dmask.shape), _full(qdec.shape)],
        out_specs=tok(R_V_WIDTH),
        out_shape=jax.ShapeDtypeStruct((b, s, R_V_WIDTH), BF16),
        scratch_shapes=[pltpu.VMEM((R_HEADS, R_QK_DIM, R_V_DIM), F32)],
        compiler_params=_params("parallel", "arbitrary"),
        name="retention",
    )(rq, rkt, rktd, rv, dmask, qdec)


def _merge_kernel(x_ref, g_ref, gn_ref, ya_ref, wg_ref, wa_ref, wb_ref, wo_ref, o_ref):
    x = x_ref[...]
    d = x.shape[1]
    hb = _rms(x, g_ref[...]).astype(BF16)
    rg = _dot(hb, wg_ref[:, 0:R_V_WIDTH])
    ta = _dot(ya_ref[...], wa_ref[...])
    ga = _dot(hb, wg_ref[:, R_V_WIDTH:R_V_WIDTH + d])
    gb = _dot(hb, wg_ref[:, R_V_WIDTH + d:])
    yb = gn_ref[...].astype(F32) * (rg * _sigmoid(rg))
    tb = _dot(yb.astype(BF16), wb_ref[...])
    merged = _sigmoid(ga) * ta + _sigmoid(gb) * tb
    o_ref[...] = x + _dot(merged.astype(BF16), wo_ref[...])


def _merge(x2, g, gn2, ya2, wg, wa, wb, wo):
    n, d = x2.shape
    t = ROW_TILE
    tok = lambda w: pl.BlockSpec((t, w), lambda i: (i, 0))
    return pl.pallas_call(
        _merge_kernel,
        grid=(n // t,),
        in_specs=[tok(d), _full((1, d)), tok(R_V_WIDTH), tok(A_WIDTH),
                  _full(wg.shape), _full(wa.shape), _full(wb.shape), _full(wo.shape)],
        out_specs=tok(d),
        out_shape=jax.ShapeDtypeStruct((n, d), F32),
        compiler_params=_params("parallel"),
        name="merge",
    )(x2, g, gn2, ya2, wg, wa, wb, wo)


def _memkv_kernel(m_ref, g_ref, w_ref, kt_ref, v_ref):
    d = m_ref.shape[2]
    mb = _rms(m_ref[0], g_ref[...]).astype(BF16)
    kt_ref[0] = _dot(mb, w_ref[:, 0:d]).T.astype(BF16)
    v_ref[0] = _dot(mb, w_ref[:, d:]).astype(BF16)


def _memkv(mem, g, w_kv):
    b, m, d = mem.shape
    return pl.pallas_call(
        _memkv_kernel,
        grid=(b,),
        in_specs=[pl.BlockSpec((1, m, d), lambda bi: (bi, 0, 0)), _full((1, d)), _full(w_kv.shape)],
        out_specs=[pl.BlockSpec((1, d, m), lambda bi: (bi, 0, 0)), pl.BlockSpec((1, m, d), lambda bi: (bi, 0, 0))],
        out_shape=[jax.ShapeDtypeStruct((b, d, m), BF16), jax.ShapeDtypeStruct((b, m, d), BF16)],
        compiler_params=_params("parallel"),
        name="memkv",
    )(mem, g, w_kv)


def _cross_kernel(x_ref, g_ref, kt_ref, v_ref, wq_ref, wo_ref, o_ref, cat_ref):
    x = x_ref[0]
    d = x.shape[1]
    dh = d // X_HEADS
    hb = _rms(x, g_ref[...]).astype(BF16)
    q = (_dot(hb, wq_ref[...]) * dh ** -0.5).astype(BF16)
    cols = lambda hd: slice(hd * dh, (hd + 1) * dh)
    scores = [_dot(q[:, cols(hd)], kt_ref[0, cols(hd), :]) for hd in range(X_HEADS)]
    for hd, s in enumerate(scores):
        p = jnp.exp(s - jnp.max(s, axis=-1, keepdims=True))
        p = p / jnp.sum(p, axis=-1, keepdims=True)
        cat_ref[:, cols(hd)] = _dot(p.astype(BF16), v_ref[0, :, cols(hd)]).astype(BF16)
    o_ref[0] = x + _dot(cat_ref[...], wo_ref[...])


def _cross(x, g, kt, v, wq, wo):
    b, s, d = x.shape
    m = v.shape[1]
    t = ROW_TILE
    tok = pl.BlockSpec((1, t, d), lambda bi, i: (bi, i, 0))
    return pl.pallas_call(
        _cross_kernel,
        grid=(b, s // t),
        in_specs=[tok, _full((1, d)), pl.BlockSpec((1, d, m), lambda bi, i: (bi, 0, 0)),
                  pl.BlockSpec((1, m, d), lambda bi, i: (bi, 0, 0)), _full(wq.shape), _full(wo.shape)],
        out_specs=tok,
        out_shape=jax.ShapeDtypeStruct((b, s, d), F32),
        scratch_shapes=[pltpu.VMEM((t, d), BF16)],
        compiler_params=_params("parallel", "parallel"),
        name="cross",
    )(x, g, kt, v, wq, wo)


def _ffn_kernel(x_ref, g_ref, wg_ref, wu_ref, wd_ref, gf_ref, o_ref):
    x = x_ref[...]
    hb = _rms(x, g_ref[...]).astype(BF16)
    n_chunks = wg_ref.shape[1] // FF_CHUNK
    cols = lambda c: slice(c * FF_CHUNK, (c + 1) * FF_CHUNK)

    def up(c):
        return _dot(hb, wg_ref[:, cols(c)]), _dot(hb, wu_ref[:, cols(c)])

    acc = x
    nxt = up(0)
    for c in range(n_chunks):
        gate, upv = nxt
        if c + 1 < n_chunks:
            nxt = up(c + 1)
        act = gate * _sigmoid(gate) * upv
        acc = acc + _dot(act.astype(BF16), wd_ref[cols(c), :])
    o_ref[...] = _rms(acc, gf_ref[...])


def _ffn(x2, g, wg, wu, wd, gf):
    n, d = x2.shape
    t = ROW_TILE
    tok = pl.BlockSpec((t, d), lambda i: (i, 0))
    return pl.pallas_call(
        _ffn_kernel,
        grid=(n // t,),
        in_specs=[tok, _full((1, d)), _full(wg.shape), _full(wu.shape), _full(wd.shape), _full((1, d))],
        out_specs=tok,
        out_shape=jax.ShapeDtypeStruct((n, d), F32),
        compiler_params=_params("parallel"),
        name="ffn",
    )(x2, g, wg, wu, wd, gf)


def _rope_tables(s):
    def cs(dim):
        inv = ROPE_THETA ** (-jnp.arange(0, dim, 2, dtype=F32) / dim)
        ang = jnp.arange(s).astype(F32)[:, None] * inv[None, :]
        return jnp.cos(ang), jnp.sin(ang)

    ca, sa = cs(A_HEAD_DIM)
    cr, sr = cs(R_QK_DIM)
    return (jnp.concatenate([ca, ca, ca, ca], axis=1), jnp.concatenate([-sa, sa, -sa, sa], axis=1),
            jnp.concatenate([cr, cr], axis=1), jnp.concatenate([-sr, sr], axis=1),
            ca.T, sa.T, cr.T, sr.T)


def _decay_tables(s):
    c = R_CHUNK
    log_g = np.log(1.0 - np.exp2(-5.0 - np.arange(R_HEADS, dtype=np.float64)))
    i = np.arange(c, dtype=np.float64)
    diff = i[:, None] - i[None, :]
    dmask = np.where(diff >= 0, np.exp(log_g[:, None, None] * np.maximum(diff, 0.0)), 0.0)
    q_dec = np.exp(log_g[:, None] * (i + 1.0))
    k_dec = np.exp(log_g[:, None] * (c - 1.0 - i))
    kdec_t = np.zeros((8, s), np.float64)
    kdec_t[:R_HEADS] = np.tile(k_dec, (1, s // c))
    qdec = np.broadcast_to(q_dec[:, :, None], (R_HEADS, c, R_V_DIM))
    chunk_dec = tuple(float(v) for v in np.exp(log_g * c))
    return (jnp.asarray(dmask, F32), jnp.asarray(qdec, F32), jnp.asarray(kdec_t, F32), chunk_dec)


def kernel(x, mem, norm_mix, w_in, w_branch_a, w_branch_b, w_out, norm_cross, norm_mem,
           w_xq, w_xkv, w_xo, norm_ffn, w_gate, w_up, w_down, norm_final):
    b, s, d = x.shape
    assert s % TOKEN_TILE == 0 and s % ROW_TILE == 0 and d % 128 == 0
    rope = _rope_tables(s)
    dmask, qdec, kdec_t, chunk_dec = _decay_tables(s)
    row = lambda v: v.reshape(1, d)
    o_aq, o_ak, o_av = 0, A_WIDTH, 2 * A_WIDTH
    o_rq = 3 * A_WIDTH
    o_rk = o_rq + R_QK_WIDTH
    o_rv = o_rk + R_QK_WIDTH
    o_g = o_rv + R_V_WIDTH

    for l in range(w_in.shape[0]):
        w = w_in[l]
        w_nat = jnp.concatenate([w[:, o_ak:o_av], w[:, o_rq:o_rk], w[:, o_rv:o_g]], axis=1).astype(BF16)
        w_t = jnp.concatenate([w[:, o_aq:o_ak], w[:, o_av:o_rq], w[:, o_rk:o_rv]], axis=1).T.astype(BF16)
        k_a, kmean, qt_a, vt_a, rq, rv, rkt, rktd = _qkv(x, row(norm_mix[l]), w_nat, w_t, rope + (kdec_t,))
        y_a = _moba(qt_a, k_a, vt_a, kmean.reshape(b, s // MOBA_BLOCK, A_WIDTH))
        gn = _retention(rq, rkt, rktd, rv, dmask, qdec, chunk_dec)
        x = _merge(x.reshape(b * s, d), row(norm_mix[l]), gn.reshape(b * s, R_V_WIDTH),
                   y_a.reshape(b * s, A_WIDTH), w[:, o_g:].astype(BF16), w_branch_a[l].astype(BF16),
                   w_branch_b[l].astype(BF16), w_out[l].astype(BF16)).reshape(b, s, d)
        kt_m, v_m = _memkv(mem, row(norm_mem[l]), w_xkv[l].astype(BF16))
        x = _cross(x, row(norm_cross[l]), kt_m, v_m, w_xq[l].astype(BF16), w_xo[l].astype(BF16))
        last = l == w_in.shape[0] - 1
        gf = row(norm_final) if last else jnp.ones((1, d), F32)
        x = _ffn(x.reshape(b * s, d), row(norm_ffn[l]), w_gate[l].astype(BF16), w_up[l].astype(BF16),
                 w_down[l].astype(BF16), gf).reshape(b, s, d)
        if not last:
            raise NotImplementedError("the fused final norm assumes a single layer")
    return x
```

```python
import functools
import math

import numpy as np
import jax
import jax.numpy as jnp
from jax import lax
from jax.experimental import pallas as pl
from jax.experimental.pallas import tpu as pltpu

F32 = jnp.float32
BF16 = jnp.bfloat16

EPS = 1e-6
ROPE_THETA = 10000.0
A_HEADS = 8
A_HEAD_DIM = 64
A_WIDTH = A_HEADS * A_HEAD_DIM
MOBA_BLOCK = 256
MOBA_TOPK = 3
R_HEADS = 4
R_QK_DIM = 128
R_V_DIM = 256
R_QK_WIDTH = R_HEADS * R_QK_DIM
R_V_WIDTH = R_HEADS * R_V_DIM
R_CHUNK = 128
X_HEADS = 4
FF_CHUNK = 256
TOKEN_TILE = MOBA_BLOCK
ROW_TILE = 1024
QKV_BLOCKS = 2
MIX_LATE_COLS = 256
MIX_BLOCKS = 2
NEG = -1e30
LOG2_E = math.log2(math.e)
MOBA_LOOKAHEAD = 5
MOBA_UNROLL = 4
MOBA_ONES_ROWS = 16
VMEM_LIMIT = 56 * 1024 * 1024


def _dot(a, b):
    return jnp.dot(a, b, preferred_element_type=F32)


def _rms(x, g):
    return x * lax.rsqrt(jnp.mean(x * x, axis=-1, keepdims=True) + EPS) * g


def _sigmoid(x):
    return 1.0 / (1.0 + jnp.exp(-x))


def _params(*sem):
    return pltpu.CompilerParams(dimension_semantics=sem, vmem_limit_bytes=VMEM_LIMIT)


def _full(shape):
    n = len(shape)
    return pl.BlockSpec(shape, lambda *_: (0,) * n, pipeline_mode=pl.Buffered(1))


def _qkv_kernel(*refs):
    for sb in range(QKV_BLOCKS):
        _qkv_block(sb, *refs)


def _qkv_block(sb, x_ref, g_ref, wn_ref, wt_ref, cosa_ref, sina_ref, cosr_ref, sinr_ref,
               costa_ref, sinta_ref, costr_ref, sintr_ref, kdec_ref,
               k_ref, kmean_ref, qt_ref, vt_ref, rq_ref, rv_ref, rkt_ref, rktd_ref):
    t = TOKEN_TILE
    tok = slice(sb * t, (sb + 1) * t)
    h = _rms(x_ref[0, tok, :], g_ref[...])
    hb = h.astype(BF16)
    ht = h.T.astype(BF16)
    lane = lax.broadcasted_iota(jnp.int32, (t, 128), 1)
    first_half = (lane & 32) == 0

    ak = _dot(hb, wn_ref[:, 0:A_WIDTH])
    cosa, sina = cosa_ref[tok, :], sina_ref[tok, :]
    for c in range(A_WIDTH // 128):
        blk = ak[:, c * 128:(c + 1) * 128]
        rot = jnp.where(first_half, pltpu.roll(blk, 96, 1), pltpu.roll(blk, 32, 1))
        kr = blk * cosa + rot * sina
        k_ref[0, sb, :, c * 128:(c + 1) * 128] = kr.astype(BF16)
        kmean_ref[0, sb, :, c * 128:(c + 1) * 128] = jnp.mean(kr, axis=0, keepdims=True)

    rq = _dot(hb, wn_ref[:, A_WIDTH:A_WIDTH + R_QK_WIDTH])
    cosr, sinr = cosr_ref[tok, :], sinr_ref[tok, :]
    for c in range(R_HEADS):
        blk = rq[:, c * 128:(c + 1) * 128]
        rq_ref[0, tok, c * 128:(c + 1) * 128] = (blk * cosr + pltpu.roll(blk, 64, 1) * sinr).astype(BF16)

    rv_ref[0, tok, :] = _dot(hb, wn_ref[:, A_WIDTH + R_QK_WIDTH:]).astype(BF16)

    qt = _dot(wt_ref[0:A_WIDTH, :], ht)
    cost, sint = costa_ref[:, tok], sinta_ref[:, tok]
    half = A_HEAD_DIM // 2
    scale_a = A_HEAD_DIM ** -0.5 * LOG2_E
    for hd in range(A_HEADS):
        x1 = qt[hd * A_HEAD_DIM:hd * A_HEAD_DIM + half]
        x2 = qt[hd * A_HEAD_DIM + half:(hd + 1) * A_HEAD_DIM]
        qt_ref[0, sb, hd * A_HEAD_DIM:hd * A_HEAD_DIM + half, :] = ((x1 * cost - x2 * sint) * scale_a).astype(BF16)
        qt_ref[0, sb, hd * A_HEAD_DIM + half:(hd + 1) * A_HEAD_DIM, :] = ((x2 * cost + x1 * sint) * scale_a).astype(BF16)

    vt_ref[0, sb] = _dot(wt_ref[A_WIDTH:2 * A_WIDTH, :], ht).astype(BF16)

    rkt = _dot(wt_ref[2 * A_WIDTH:, :], ht)
    cost, sint = costr_ref[:, tok], sintr_ref[:, tok]
    half = R_QK_DIM // 2
    scale_r = R_QK_DIM ** -0.5
    for hd in range(R_HEADS):
        x1 = rkt[hd * R_QK_DIM:hd * R_QK_DIM + half]
        x2 = rkt[hd * R_QK_DIM + half:(hd + 1) * R_QK_DIM]
        o1 = (x1 * cost - x2 * sint) * scale_r
        o2 = (x2 * cost + x1 * sint) * scale_r
        dec = kdec_ref[hd:hd + 1, tok]
        rkt_ref[0, sb, hd * R_QK_DIM:hd * R_QK_DIM + half, :] = o1.astype(BF16)
        rkt_ref[0, sb, hd * R_QK_DIM + half:(hd + 1) * R_QK_DIM, :] = o2.astype(BF16)
        rktd_ref[0, sb, hd * R_QK_DIM:hd * R_QK_DIM + half, :] = (o1 * dec).astype(BF16)
        rktd_ref[0, sb, hd * R_QK_DIM + half:(hd + 1) * R_QK_DIM, :] = (o2 * dec).astype(BF16)


def _qkv(x, g, w_nat, w_t, tabs):
    b, s, d = x.shape
    nb = s // TOKEN_TILE
    qb = QKV_BLOCKS
    t = qb * TOKEN_TILE
    tok = lambda w: pl.BlockSpec((1, t, w), lambda bi, i: (bi, i, 0))
    blk_t = lambda r: pl.BlockSpec((1, qb, r, TOKEN_TILE), lambda bi, i: (bi, i, 0, 0))
    nat_tab = pl.BlockSpec((t, 128), lambda bi, i: (i, 0))
    tr_tab = lambda r: pl.BlockSpec((r, t), lambda bi, i: (0, i))
    return pl.pallas_call(
        _qkv_kernel,
        grid=(b, nb // qb),
        in_specs=[tok(d), _full((1, d)), _full(w_nat.shape), _full(w_t.shape),
                  nat_tab, nat_tab, nat_tab, nat_tab,
                  tr_tab(A_HEAD_DIM // 2), tr_tab(A_HEAD_DIM // 2),
                  tr_tab(R_QK_DIM // 2), tr_tab(R_QK_DIM // 2), tr_tab(8)],
        out_specs=[pl.BlockSpec((1, qb, TOKEN_TILE, A_WIDTH), lambda bi, i: (bi, i, 0, 0)),
                   pl.BlockSpec((1, qb, 1, A_WIDTH), lambda bi, i: (bi, i, 0, 0)),
                   blk_t(A_WIDTH), blk_t(A_WIDTH), tok(R_QK_WIDTH), tok(R_V_WIDTH),
                   blk_t(R_QK_WIDTH), blk_t(R_QK_WIDTH)],
        out_shape=[jax.ShapeDtypeStruct((b, nb, TOKEN_TILE, A_WIDTH), BF16),
                   jax.ShapeDtypeStruct((b, nb, 1, A_WIDTH), F32),
                   jax.ShapeDtypeStruct((b, nb, A_WIDTH, TOKEN_TILE), BF16),
                   jax.ShapeDtypeStruct((b, nb, A_WIDTH, TOKEN_TILE), BF16),
                   jax.ShapeDtypeStruct((b, s, R_QK_WIDTH), BF16),
                   jax.ShapeDtypeStruct((b, s, R_V_WIDTH), BF16),
                   jax.ShapeDtypeStruct((b, nb, R_QK_WIDTH, TOKEN_TILE), BF16),
                   jax.ShapeDtypeStruct((b, nb, R_QK_WIDTH, TOKEN_TILE), BF16)],
        compiler_params=_params("parallel", "parallel"),
        name="qkv",
    )(x, g, w_nat, w_t, *tabs)


def _moba_kernel(qt_ref, k_ref, vt_ref, km_ref, o_ref, qm_ref, m_ref, acc_ref, ot_ref, s_ref):
    i = pl.program_id(1)
    nb = km_ref.shape[1]
    t = MOBA_BLOCK
    dh = A_HEAD_DIM
    row = lax.broadcasted_iota(jnp.int32, (2 * dh, t), 0)
    blk = lax.broadcasted_iota(jnp.int32, (nb, t), 0)
    past = blk < i
    causal = (lax.broadcasted_iota(jnp.int32, (t, t), 0) <= lax.broadcasted_iota(jnp.int32, (t, t), 1))
    ones = jnp.ones((MOBA_ONES_ROWS, t), BF16)
    lanes = lambda hd: slice((hd // 2) * 128, (hd // 2 + 1) * 128)
    rows = lambda hd: slice(hd * dh, (hd + 1) * dh)

    def select_blocks(hd):
        hh = hd % 2
        qt_pair = qt_ref[0, 0, lanes(hd), :]
        mine = (row >= hh * dh) & (row < (hh + 1) * dh)
        qt = jnp.where(mine, qt_pair, jnp.zeros_like(qt_pair))
        qm_ref[hd, 0:2 * dh, :] = qt

        km = km_ref[0, :, lanes(hd)]
        km_hi = km.astype(BF16)
        km_lo = (km - km_hi.astype(F32)).astype(BF16)
        bs = jnp.where(past, _dot(km_hi, qt) + _dot(km_lo, qt), -jnp.inf)
        cnt = jnp.zeros((nb, t), F32)
        for m in range(nb):
            r = bs[m:m + 1, :]
            tie = jnp.where(blk > m, 1.0, 0.0)
            cnt = cnt + jnp.where(r > bs, 1.0, jnp.where(r == bs, tie, 0.0))
        bias = jnp.where(past & (cnt < MOBA_TOPK), 0.0, NEG).astype(BF16)
        qm_ref[hd, 2 * dh:, :] = jnp.concatenate([bias, jnp.zeros((2 * dh - nb, t), BF16)], axis=0)

    key_lane = lax.broadcasted_iota(jnp.int32, (t, 2 * dh), 1)

    def keys_with_block_column(j):
        onehot = jnp.where(key_lane == j, 1.0, 0.0).astype(BF16)
        return [jnp.concatenate([k_ref[0, j, :, lanes(2 * p)], onehot], axis=1) for p in range(A_HEADS // 2)]

    def pv(j, hd, p):
        vt_aug = jnp.concatenate([vt_ref[0, j, rows(hd), :], ones], axis=0)
        return _dot(vt_aug, p.astype(BF16))

    def own_unit(hd):
        def scores():
            return jnp.where(causal, _dot(k_ref[0, i, :, lanes(hd)], qm_ref[hd, 0:2 * dh, :]), NEG)

        def update(s, m0):
            m_ref[hd] = jnp.broadcast_to(m0, (8, t))
            acc_ref[hd] = pv(i, hd, jnp.exp2(s - m0))

        return scores, update

    def past_units(j):
        keys = keys_with_block_column(j)

        def unit(hd):
            def scores():
                return _dot(keys[hd // 2], qm_ref[hd])

            def update(s, m_blk):
                m_old = m_ref[hd]
                m_new = jnp.maximum(m_old, m_blk)
                alpha = jnp.exp2(m_old - m_new)
                m_ref[hd] = m_new
                acc_ref[hd] = alpha[0:1, :] * acc_ref[hd] + pv(j, hd, jnp.exp2(s - m_new[0:1, :]))

            return scores, update

        return [unit(hd) for hd in range(A_HEADS)]

    def pipelined(units):
        blk_max = {}
        slots = MOBA_LOOKAHEAD + 1
        for step in range(len(units) + MOBA_LOOKAHEAD):
            if step < len(units):
                sc = units[step][0]()
                s_ref[step % slots] = sc
                blk_max[step] = jnp.max(sc, axis=0, keepdims=True)
            if step >= MOBA_LOOKAHEAD:
                u = step - MOBA_LOOKAHEAD
                units[u][1](s_ref[u % slots], blk_max.pop(u))

    n_groups = i // MOBA_UNROLL
    for left in range(MOBA_UNROLL):
        @pl.when(i % MOBA_UNROLL == left)
        def _(left=left):
            for hd in range(A_HEADS):
                select_blocks(hd)
            units = [own_unit(hd) for hd in range(A_HEADS)]
            for r in range(left):
                units += past_units(n_groups * MOBA_UNROLL + r)
            pipelined(units)

    def group_body(jj, carry):
        pipelined([u for r in range(MOBA_UNROLL) for u in past_units(MOBA_UNROLL * jj + r)])
        return carry

    lax.fori_loop(0, n_groups, group_body, 0)

    for hd in range(A_HEADS):
        ot_ref[rows(hd), :] = acc_ref[hd, 0:dh, :] / acc_ref[hd, dh:dh + 1, :]
    o_ref[0] = ot_ref[...].T.astype(BF16)


def _moba(qt, k, vt, kmean):
    b, nb, t, w = k.shape
    return pl.pallas_call(
        _moba_kernel,
        grid=(b, nb),
        in_specs=[pl.BlockSpec((1, 1, w, t), lambda bi, i: (bi, i, 0, 0)),
                  pl.BlockSpec((1, nb, t, w), lambda bi, i: (bi, 0, 0, 0)),
                  pl.BlockSpec((1, nb, w, t), lambda bi, i: (bi, 0, 0, 0)),
                  pl.BlockSpec((1, nb, w), lambda bi, i: (bi, 0, 0))],
        out_specs=pl.BlockSpec((1, t, w), lambda bi, i: (bi, i, 0)),
        out_shape=jax.ShapeDtypeStruct((b, nb * t, w), BF16),
        scratch_shapes=[pltpu.VMEM((A_HEADS, 4 * A_HEAD_DIM, t), BF16),
                        pltpu.VMEM((A_HEADS, 8, t), F32),
                        pltpu.VMEM((A_HEADS, A_HEAD_DIM + MOBA_ONES_ROWS, t), F32),
                        pltpu.VMEM((w, t), F32),
                        pltpu.VMEM((MOBA_LOOKAHEAD + 1, t, t), F32)],
        compiler_params=_params("parallel", "parallel"),
        name="moba",
    )(qt, k, vt, kmean)


def _mix_kernel(chunk_dec, x_ref, g_ref, rq_ref, rkt_ref, rktd_ref, rv_ref, ya_ref, dmask_ref, qdec_ref,
                wg_ref, wa_ref, wb_ref, wo_ref, o_ref, state_ref, yb_ref):
    @pl.when(pl.program_id(1) == 0)
    def _():
        state_ref[...] = jnp.zeros_like(state_ref)

    x = x_ref[0]
    d = x.shape[1]
    hb = _rms(x, g_ref[...]).astype(BF16)
    c = R_CHUNK
    per_blk = TOKEN_TILE // c
    n_chunks = MIX_BLOCKS * per_blk
    rows = lambda ci: slice(ci * c, (ci + 1) * c)
    qk = lambda hd: slice(hd * R_QK_DIM, (hd + 1) * R_QK_DIM)
    vv = lambda hd: slice(hd * R_V_DIM, (hd + 1) * R_V_DIM)
    units = [(ci, hd) for ci in range(n_chunks) for hd in range(R_HEADS)]
    keys_t = lambda ref, ci, hd: ref[0, ci // per_blk, qk(hd), rows(ci % per_blk)]

    att = {u: _dot(rq_ref[0, rows(u[0]), qk(u[1])], keys_t(rkt_ref, *u)) for u in units}
    kv = {u: _dot(keys_t(rktd_ref, *u), rv_ref[0, rows(u[0]), vv(u[1])]) for u in units}
    rg = _dot(hb, wg_ref[:, 0:R_V_WIDTH])
    w = MIX_LATE_COLS
    late = ([lambda c=c: _dot(ya_ref[0], wa_ref[:, c:c + w]) for c in range(0, d, w)]
            + [lambda c=c: _dot(hb, wg_ref[:, R_V_WIDTH + c:R_V_WIDTH + c + w]) for c in range(0, 2 * d, w)])
    late_out = []

    cross = {}
    for hd in range(R_HEADS):
        st = state_ref[hd]
        for ci in range(n_chunks):
            cross[ci, hd] = _dot(rq_ref[0, rows(ci), qk(hd)], st.astype(BF16))
            st = chunk_dec[hd] * st + kv[ci, hd]
        state_ref[hd] = st

    gate = rg * _sigmoid(rg)
    for n, (ci, hd) in enumerate(units):
        inner = _dot((att[ci, hd] * dmask_ref[hd]).astype(BF16), rv_ref[0, rows(ci), vv(hd)])
        y = inner + cross[ci, hd] * qdec_ref[hd]
        mu = jnp.mean(y, axis=-1, keepdims=True)
        yc = y - mu
        var = jnp.mean(yc * yc, axis=-1, keepdims=True)
        yb_ref[rows(ci), vv(hd)] = (yc * lax.rsqrt(var + EPS) * gate[rows(ci), vv(hd)]).astype(BF16)
        if n < len(late):
            late_out.append(late[n]())

    pieces = d // w
    ta, ga, gb = (jnp.concatenate(late_out[k * pieces:(k + 1) * pieces], axis=1) for k in range(3))
    tb = _dot(yb_ref[...], wb_ref[...])
    merged = _sigmoid(ga) * ta + _sigmoid(gb) * tb
    o_ref[0] = x + _dot(merged.astype(BF16), wo_ref[...])


def _mix(x, g, rq, rkt, rktd, rv, ya, dmask, qdec, chunk_dec, wg, wa, wb, wo):
    b, s, d = x.shape
    t = MIX_BLOCKS * TOKEN_TILE
    tok = lambda w: pl.BlockSpec((1, t, w), lambda bi, i: (bi, i, 0))
    blk_t = pl.BlockSpec((1, MIX_BLOCKS, R_QK_WIDTH, TOKEN_TILE), lambda bi, i: (bi, i, 0, 0))
    return pl.pallas_call(
        functools.partial(_mix_kernel, chunk_dec),
        grid=(b, s // t),
        in_specs=[tok(d), _full((1, d)), tok(R_QK_WIDTH), blk_t, blk_t, tok(R_V_WIDTH), tok(A_WIDTH),
                  _full(dmask.shape), _full(qdec.shape),
                  _full(wg.shape), _full(wa.shape), _full(wb.shape), _full(wo.shape)],
        out_specs=tok(d),
        out_shape=jax.ShapeDtypeStruct((b, s, d), F32),
        scratch_shapes=[pltpu.VMEM((R_HEADS, R_QK_DIM, R_V_DIM), F32),
                        pltpu.VMEM((t, R_V_WIDTH), BF16)],
        compiler_params=_params("parallel", "arbitrary"),
        name="mix",
    )(x, g, rq, rkt, rktd, rv, ya, dmask, qdec, wg, wa, wb, wo)


def _memkv_kernel(m_ref, g_ref, w_ref, kt_ref, v_ref):
    d = m_ref.shape[2]
    mb = _rms(m_ref[0], g_ref[...]).astype(BF16)
    kt_ref[0] = _dot(mb, w_ref[:, 0:d]).T.astype(BF16)
    v_ref[0] = _dot(mb, w_ref[:, d:]).astype(BF16)


def _memkv(mem, g, w_kv):
    b, m, d = mem.shape
    return pl.pallas_call(
        _memkv_kernel,
        grid=(b,),
        in_specs=[pl.BlockSpec((1, m, d), lambda bi: (bi, 0, 0)), _full((1, d)), _full(w_kv.shape)],
        out_specs=[pl.BlockSpec((1, d, m), lambda bi: (bi, 0, 0)), pl.BlockSpec((1, m, d), lambda bi: (bi, 0, 0))],
        out_shape=[jax.ShapeDtypeStruct((b, d, m), BF16), jax.ShapeDtypeStruct((b, m, d), BF16)],
        compiler_params=_params("parallel"),
        name="memkv",
    )(mem, g, w_kv)


def _cross_kernel(x_ref, g_ref, kt_ref, v_ref, wq_ref, wo_ref, o_ref, cat_ref):
    x = x_ref[0]
    d = x.shape[1]
    dh = d // X_HEADS
    hb = _rms(x, g_ref[...]).astype(BF16)
    q = (_dot(hb, wq_ref[...]) * dh ** -0.5).astype(BF16)
    cols = lambda hd: slice(hd * dh, (hd + 1) * dh)
    scores = [_dot(q[:, cols(hd)], kt_ref[0, cols(hd), :]) for hd in range(X_HEADS)]
    for hd, s in enumerate(scores):
        p = jnp.exp(s - jnp.max(s, axis=-1, keepdims=True))
        p = p / jnp.sum(p, axis=-1, keepdims=True)
        cat_ref[:, cols(hd)] = _dot(p.astype(BF16), v_ref[0, :, cols(hd)]).astype(BF16)
    o_ref[0] = x + _dot(cat_ref[...], wo_ref[...])


def _cross(x, g, kt, v, wq, wo):
    b, s, d = x.shape
    m = v.shape[1]
    t = ROW_TILE
    tok = pl.BlockSpec((1, t, d), lambda bi, i: (bi, i, 0))
    return pl.pallas_call(
        _cross_kernel,
        grid=(b, s // t),
        in_specs=[tok, _full((1, d)), pl.BlockSpec((1, d, m), lambda bi, i: (bi, 0, 0)),
                  pl.BlockSpec((1, m, d), lambda bi, i: (bi, 0, 0)), _full(wq.shape), _full(wo.shape)],
        out_specs=tok,
        out_shape=jax.ShapeDtypeStruct((b, s, d), F32),
        scratch_shapes=[pltpu.VMEM((t, d), BF16)],
        compiler_params=_params("parallel", "parallel"),
        name="cross",
    )(x, g, kt, v, wq, wo)


def _ffn_kernel(x_ref, g_ref, wg_ref, wu_ref, wd_ref, gf_ref, o_ref):
    x = x_ref[...]
    hb = _rms(x, g_ref[...]).astype(BF16)
    n_chunks = wg_ref.shape[1] // FF_CHUNK
    cols = lambda c: slice(c * FF_CHUNK, (c + 1) * FF_CHUNK)

    def up(c):
        return _dot(hb, wg_ref[:, cols(c)]), _dot(hb, wu_ref[:, cols(c)])

    acc = x
    nxt = up(0)
    for c in range(n_chunks):
        gate, upv = nxt
        if c + 1 < n_chunks:
            nxt = up(c + 1)
        act = gate * _sigmoid(gate) * upv
        acc = acc + _dot(act.astype(BF16), wd_ref[cols(c), :])
    o_ref[...] = _rms(acc, gf_ref[...])


def _ffn(x2, g, wg, wu, wd, gf):
    n, d = x2.shape
    t = ROW_TILE
    tok = pl.BlockSpec((t, d), lambda i: (i, 0))
    return pl.pallas_call(
        _ffn_kernel,
        grid=(n // t,),
        in_specs=[tok, _full((1, d)), _full(wg.shape), _full(wu.shape), _full(wd.shape), _full((1, d))],
        out_specs=tok,
        out_shape=jax.ShapeDtypeStruct((n, d), F32),
        compiler_params=_params("parallel"),
        name="ffn",
    )(x2, g, wg, wu, wd, gf)


def _rope_tables(s):
    def cs(dim):
        inv = ROPE_THETA ** (-jnp.arange(0, dim, 2, dtype=F32) / dim)
        ang = jnp.arange(s).astype(F32)[:, None] * inv[None, :]
        return jnp.cos(ang), jnp.sin(ang)

    ca, sa = cs(A_HEAD_DIM)
    cr, sr = cs(R_QK_DIM)
    return (jnp.concatenate([ca, ca, ca, ca], axis=1), jnp.concatenate([-sa, sa, -sa, sa], axis=1),
            jnp.concatenate([cr, cr], axis=1), jnp.concatenate([-sr, sr], axis=1),
            ca.T, sa.T, cr.T, sr.T)


def _decay_tables(s):
    c = R_CHUNK
    log_g = np.log(1.0 - np.exp2(-5.0 - np.arange(R_HEADS, dtype=np.float64)))
    i = np.arange(c, dtype=np.float64)
    diff = i[:, None] - i[None, :]
    dmask = np.where(diff >= 0, np.exp(log_g[:, None, None] * np.maximum(diff, 0.0)), 0.0)
    q_dec = np.exp(log_g[:, None] * (i + 1.0))
    k_dec = np.exp(log_g[:, None] * (c - 1.0 - i))
    kdec_t = np.zeros((8, s), np.float64)
    kdec_t[:R_HEADS] = np.tile(k_dec, (1, s // c))
    qdec = np.broadcast_to(q_dec[:, :, None], (R_HEADS, c, R_V_DIM))
    chunk_dec = tuple(float(v) for v in np.exp(log_g * c))
    return (jnp.asarray(dmask, F32), jnp.asarray(qdec, F32), jnp.asarray(kdec_t, F32), chunk_dec)


def kernel(x, mem, norm_mix, w_in, w_branch_a, w_branch_b, w_out, norm_cross, norm_mem,
           w_xq, w_xkv, w_xo, norm_ffn, w_gate, w_up, w_down, norm_final):
    b, s, d = x.shape
    assert s % TOKEN_TILE == 0 and s % ROW_TILE == 0 and d % 128 == 0
    rope = _rope_tables(s)
    dmask, qdec, kdec_t, chunk_dec = _decay_tables(s)
    row = lambda v: v.reshape(1, d)
    o_aq, o_ak, o_av = 0, A_WIDTH, 2 * A_WIDTH
    o_rq = 3 * A_WIDTH
    o_rk = o_rq + R_QK_WIDTH
    o_rv = o_rk + R_QK_WIDTH
    o_g = o_rv + R_V_WIDTH

    for l in range(w_in.shape[0]):
        w = w_in[l]
        w_nat = jnp.concatenate([w[:, o_ak:o_av], w[:, o_rq:o_rk], w[:, o_rv:o_g]], axis=1).astype(BF16)
        w_t = jnp.concatenate([w[:, o_aq:o_ak], w[:, o_av:o_rq], w[:, o_rk:o_rv]], axis=1).T.astype(BF16)
        k_a, kmean, qt_a, vt_a, rq, rv, rkt, rktd = _qkv(x, row(norm_mix[l]), w_nat, w_t, rope + (kdec_t,))
        y_a = _moba(qt_a, k_a, vt_a, kmean.reshape(b, s // MOBA_BLOCK, A_WIDTH))
        x = _mix(x, row(norm_mix[l]), rq, rkt, rktd, rv, y_a, dmask, qdec, chunk_dec,
                 w[:, o_g:].astype(BF16), w_branch_a[l].astype(BF16), w_branch_b[l].astype(BF16),
                 w_out[l].astype(BF16))
        kt_m, v_m = _memkv(mem, row(norm_mem[l]), w_xkv[l].astype(BF16))
        x = _cross(x, row(norm_cross[l]), kt_m, v_m, w_xq[l].astype(BF16), w_xo[l].astype(BF16))
        last = l == w_in.shape[0] - 1
        gf = row(norm_final) if last else jnp.ones((1, d), F32)
        x = _ffn(x.reshape(b * s, d), row(norm_ffn[l]), w_gate[l].astype(BF16), w_up[l].astype(BF16),
                 w_down[l].astype(BF16), gf).reshape(b, s, d)
        if not last:
            raise NotImplementedError("the fused final norm assumes a single layer")
    return x
```

```python
import functools
import math

import numpy as np
import jax
import jax.numpy as jnp
from jax import lax
from jax.experimental import pallas as pl
from jax.experimental.pallas import tpu as pltpu

F32 = jnp.float32
BF16 = jnp.bfloat16

EPS = 1e-6
ROPE_THETA = 10000.0
A_HEADS = 8
A_HEAD_DIM = 64
A_WIDTH = A_HEADS * A_HEAD_DIM
MOBA_BLOCK = 256
MOBA_TOPK = 3
R_HEADS = 4
R_QK_DIM = 128
R_V_DIM = 256
R_QK_WIDTH = R_HEADS * R_QK_DIM
R_V_WIDTH = R_HEADS * R_V_DIM
R_CHUNK = 128
X_HEADS = 4
FF_CHUNK = 256
TOKEN_TILE = MOBA_BLOCK
ROW_TILE = 1024
QKV_BLOCKS = 2
MIX_LATE_COLS = 256
MIX_BLOCKS = 2
NEG = -1e30
LOG2_E = math.log2(math.e)
MOBA_LOOKAHEAD = 5
MOBA_ROWS = 2
MOBA_UNROLL = 4
MOBA_ONES_ROWS = 16
VMEM_LIMIT = 56 * 1024 * 1024


def _dot(a, b):
    return jnp.dot(a, b, preferred_element_type=F32)


def _rms(x, g):
    return x * lax.rsqrt(jnp.mean(x * x, axis=-1, keepdims=True) + EPS) * g


def _sigmoid(x):
    return 1.0 / (1.0 + jnp.exp(-x))


def _params(*sem):
    return pltpu.CompilerParams(dimension_semantics=sem, vmem_limit_bytes=VMEM_LIMIT)


def _full(shape):
    n = len(shape)
    return pl.BlockSpec(shape, lambda *_: (0,) * n, pipeline_mode=pl.Buffered(1))


def _qkv_kernel(*refs):
    for sb in range(QKV_BLOCKS):
        _qkv_block(sb, *refs)


def _qkv_block(sb, x_ref, g_ref, wn_ref, wt_ref, cosa_ref, sina_ref, cosr_ref, sinr_ref,
               costa_ref, sinta_ref, costr_ref, sintr_ref, kdec_ref,
               k_ref, kmean_ref, qt_ref, vt_ref, rq_ref, rv_ref, rkt_ref, rktd_ref):
    t = TOKEN_TILE
    tok = slice(sb * t, (sb + 1) * t)
    h = _rms(x_ref[0, tok, :], g_ref[...])
    hb = h.astype(BF16)
    ht = h.T.astype(BF16)
    lane = lax.broadcasted_iota(jnp.int32, (t, 128), 1)
    first_half = (lane & 32) == 0

    ak = _dot(hb, wn_ref[:, 0:A_WIDTH])
    cosa, sina = cosa_ref[tok, :], sina_ref[tok, :]
    for c in range(A_WIDTH // 128):
        blk = ak[:, c * 128:(c + 1) * 128]
        rot = jnp.where(first_half, pltpu.roll(blk, 96, 1), pltpu.roll(blk, 32, 1))
        kr = blk * cosa + rot * sina
        k_ref[0, sb, :, c * 128:(c + 1) * 128] = kr.astype(BF16)
        kmean_ref[0, sb, :, c * 128:(c + 1) * 128] = jnp.mean(kr, axis=0, keepdims=True)

    rq = _dot(hb, wn_ref[:, A_WIDTH:A_WIDTH + R_QK_WIDTH])
    cosr, sinr = cosr_ref[tok, :], sinr_ref[tok, :]
    for c in range(R_HEADS):
        blk = rq[:, c * 128:(c + 1) * 128]
        rq_ref[0, tok, c * 128:(c + 1) * 128] = (blk * cosr + pltpu.roll(blk, 64, 1) * sinr).astype(BF16)

    rv_ref[0, tok, :] = _dot(hb, wn_ref[:, A_WIDTH + R_QK_WIDTH:]).astype(BF16)

    qt = _dot(wt_ref[0:A_WIDTH, :], ht)
    cost, sint = costa_ref[:, tok], sinta_ref[:, tok]
    half = A_HEAD_DIM // 2
    scale_a = A_HEAD_DIM ** -0.5 * LOG2_E
    for hd in range(A_HEADS):
        x1 = qt[hd * A_HEAD_DIM:hd * A_HEAD_DIM + half]
        x2 = qt[hd * A_HEAD_DIM + half:(hd + 1) * A_HEAD_DIM]
        qt_ref[0, sb, hd * A_HEAD_DIM:hd * A_HEAD_DIM + half, :] = ((x1 * cost - x2 * sint) * scale_a).astype(BF16)
        qt_ref[0, sb, hd * A_HEAD_DIM + half:(hd + 1) * A_HEAD_DIM, :] = ((x2 * cost + x1 * sint) * scale_a).astype(BF16)

    vt_ref[0, sb] = _dot(wt_ref[A_WIDTH:2 * A_WIDTH, :], ht).astype(BF16)

    rkt = _dot(wt_ref[2 * A_WIDTH:, :], ht)
    cost, sint = costr_ref[:, tok], sintr_ref[:, tok]
    half = R_QK_DIM // 2
    scale_r = R_QK_DIM ** -0.5
    for hd in range(R_HEADS):
        x1 = rkt[hd * R_QK_DIM:hd * R_QK_DIM + half]
        x2 = rkt[hd * R_QK_DIM + half:(hd + 1) * R_QK_DIM]
        o1 = (x1 * cost - x2 * sint) * scale_r
        o2 = (x2 * cost + x1 * sint) * scale_r
        dec = kdec_ref[hd:hd + 1, tok]
        rkt_ref[0, sb, hd * R_QK_DIM:hd * R_QK_DIM + half, :] = o1.astype(BF16)
        rkt_ref[0, sb, hd * R_QK_DIM + half:(hd + 1) * R_QK_DIM, :] = o2.astype(BF16)
        rktd_ref[0, sb, hd * R_QK_DIM:hd * R_QK_DIM + half, :] = (o1 * dec).astype(BF16)
        rktd_ref[0, sb, hd * R_QK_DIM + half:(hd + 1) * R_QK_DIM, :] = (o2 * dec).astype(BF16)


def _qkv(x, g, w_nat, w_t, tabs):
    b, s, d = x.shape
    nb = s // TOKEN_TILE
    qb = QKV_BLOCKS
    t = qb * TOKEN_TILE
    tok = lambda w: pl.BlockSpec((1, t, w), lambda bi, i: (bi, i, 0))
    blk_t = lambda r: pl.BlockSpec((1, qb, r, TOKEN_TILE), lambda bi, i: (bi, i, 0, 0))
    nat_tab = pl.BlockSpec((t, 128), lambda bi, i: (i, 0))
    tr_tab = lambda r: pl.BlockSpec((r, t), lambda bi, i: (0, i))
    return pl.pallas_call(
        _qkv_kernel,
        grid=(b, nb // qb),
        in_specs=[tok(d), _full((1, d)), _full(w_nat.shape), _full(w_t.shape),
                  nat_tab, nat_tab, nat_tab, nat_tab,
                  tr_tab(A_HEAD_DIM // 2), tr_tab(A_HEAD_DIM // 2),
                  tr_tab(R_QK_DIM // 2), tr_tab(R_QK_DIM // 2), tr_tab(8)],
        out_specs=[pl.BlockSpec((1, qb, TOKEN_TILE, A_WIDTH), lambda bi, i: (bi, i, 0, 0)),
                   pl.BlockSpec((1, qb, 1, A_WIDTH), lambda bi, i: (bi, i, 0, 0)),
                   blk_t(A_WIDTH), blk_t(A_WIDTH), tok(R_QK_WIDTH), tok(R_V_WIDTH),
                   blk_t(R_QK_WIDTH), blk_t(R_QK_WIDTH)],
        out_shape=[jax.ShapeDtypeStruct((b, nb, TOKEN_TILE, A_WIDTH), BF16),
                   jax.ShapeDtypeStruct((b, nb, 1, A_WIDTH), F32),
                   jax.ShapeDtypeStruct((b, nb, A_WIDTH, TOKEN_TILE), BF16),
                   jax.ShapeDtypeStruct((b, nb, A_WIDTH, TOKEN_TILE), BF16),
                   jax.ShapeDtypeStruct((b, s, R_QK_WIDTH), BF16),
                   jax.ShapeDtypeStruct((b, s, R_V_WIDTH), BF16),
                   jax.ShapeDtypeStruct((b, nb, R_QK_WIDTH, TOKEN_TILE), BF16),
                   jax.ShapeDtypeStruct((b, nb, R_QK_WIDTH, TOKEN_TILE), BF16)],
        compiler_params=_params("parallel", "parallel"),
        name="qkv",
    )(x, g, w_nat, w_t, *tabs)


def _moba_kernel(qt_ref, k_ref, vt_ref, km_ref, o_ref, qm_ref, m_ref, acc_ref, ot_ref, s_ref):
    i = pl.program_id(1)
    nb = km_ref.shape[1]
    t = MOBA_BLOCK
    dh = A_HEAD_DIM
    row = lax.broadcasted_iota(jnp.int32, (2 * dh, t), 0)
    blk = lax.broadcasted_iota(jnp.int32, (nb, t), 0)
    past = blk < i
    causal = (lax.broadcasted_iota(jnp.int32, (t, t), 0) <= lax.broadcasted_iota(jnp.int32, (t, t), 1))
    ones = jnp.ones((MOBA_ONES_ROWS, t), BF16)
    lanes = lambda hd: slice((hd // 2) * 128, (hd // 2 + 1) * 128)
    rows = lambda hd: slice(hd * dh, (hd + 1) * dh)
    row_heads = [(b, hd) for b in range(MOBA_ROWS) for hd in range(A_HEADS)]

    def select_blocks(b, hd):
        hh = hd % 2
        qt_pair = qt_ref[b, 0, lanes(hd), :]
        mine = (row >= hh * dh) & (row < (hh + 1) * dh)
        qt = jnp.where(mine, qt_pair, jnp.zeros_like(qt_pair))
        qm_ref[b, hd, 0:2 * dh, :] = qt

        km = km_ref[b, :, lanes(hd)]
        km_hi = km.astype(BF16)
        km_lo = (km - km_hi.astype(F32)).astype(BF16)
        bs = jnp.where(past, _dot(km_hi, qt) + _dot(km_lo, qt), -jnp.inf)
        cnt = jnp.zeros((nb, t), F32)
        for m in range(nb):
            r = bs[m:m + 1, :]
            tie = jnp.where(blk > m, 1.0, 0.0)
            cnt = cnt + jnp.where(r > bs, 1.0, jnp.where(r == bs, tie, 0.0))
        bias = jnp.where(past & (cnt < MOBA_TOPK), 0.0, NEG).astype(BF16)
        qm_ref[b, hd, 2 * dh:, :] = jnp.concatenate([bias, jnp.zeros((2 * dh - nb, t), BF16)], axis=0)

    key_lane = lax.broadcasted_iota(jnp.int32, (t, 2 * dh), 1)

    def keys_with_block_column(b, j):
        onehot = jnp.where(key_lane == j, 1.0, 0.0).astype(BF16)
        return [jnp.concatenate([k_ref[b, j, :, lanes(2 * p)], onehot], axis=1) for p in range(A_HEADS // 2)]

    def pv(b, j, hd, p):
        vt_aug = jnp.concatenate([vt_ref[b, j, rows(hd), :], ones], axis=0)
        return _dot(vt_aug, p.astype(BF16))

    def own_unit(b, hd):
        def scores():
            return jnp.where(causal, _dot(k_ref[b, i, :, lanes(hd)], qm_ref[b, hd, 0:2 * dh, :]), NEG)

        def update(s, m0):
            m_ref[b, hd] = jnp.broadcast_to(m0, (8, t))
            acc_ref[b, hd] = pv(b, i, hd, jnp.exp2(s - m0))

        return scores, update

    def past_units(b, j):
        keys = keys_with_block_column(b, j)

        def unit(hd):
            def scores():
                return _dot(keys[hd // 2], qm_ref[b, hd])

            def update(s, m_blk):
                m_old = m_ref[b, hd]
                m_new = jnp.maximum(m_old, m_blk)
                alpha = jnp.exp2(m_old - m_new)
                m_ref[b, hd] = m_new
                acc_ref[b, hd] = alpha[0:1, :] * acc_ref[b, hd] + pv(b, j, hd, jnp.exp2(s - m_new[0:1, :]))

            return scores, update

        return [unit(hd) for hd in range(A_HEADS)]

    def pipelined(units):
        blk_max = {}
        slots = MOBA_LOOKAHEAD + 1
        for step in range(len(units) + MOBA_LOOKAHEAD):
            if step < len(units):
                sc = units[step][0]()
                s_ref[step % slots] = sc
                blk_max[step] = jnp.max(sc, axis=0, keepdims=True)
            if step >= MOBA_LOOKAHEAD:
                u = step - MOBA_LOOKAHEAD
                units[u][1](s_ref[u % slots], blk_max.pop(u))

    n_groups = i // MOBA_UNROLL
    for left in range(MOBA_UNROLL):
        @pl.when(i % MOBA_UNROLL == left)
        def _(left=left):
            for b, hd in row_heads:
                select_blocks(b, hd)
            units = [own_unit(b, hd) for b, hd in row_heads]
            for r in range(left):
                for b in range(MOBA_ROWS):
                    units += past_units(b, n_groups * MOBA_UNROLL + r)
            pipelined(units)

    def group_body(jj, carry):
        pipelined([u for r in range(MOBA_UNROLL) for b in range(MOBA_ROWS)
                   for u in past_units(b, MOBA_UNROLL * jj + r)])
        return carry

    lax.fori_loop(0, n_groups, group_body, 0)

    for b in range(MOBA_ROWS):
        for hd in range(A_HEADS):
            ot_ref[rows(hd), :] = acc_ref[b, hd, 0:dh, :] / acc_ref[b, hd, dh:dh + 1, :]
        o_ref[b] = ot_ref[...].T.astype(BF16)


def _moba(qt, k, vt, kmean):
    b, nb, t, w = k.shape
    r = MOBA_ROWS
    assert b % r == 0
    return pl.pallas_call(
        _moba_kernel,
        grid=(b // r, nb),
        in_specs=[pl.BlockSpec((r, 1, w, t), lambda bi, i: (bi, i, 0, 0)),
                  pl.BlockSpec((r, nb, t, w), lambda bi, i: (bi, 0, 0, 0)),
                  pl.BlockSpec((r, nb, w, t), lambda bi, i: (bi, 0, 0, 0)),
                  pl.BlockSpec((r, nb, w), lambda bi, i: (bi, 0, 0))],
        out_specs=pl.BlockSpec((r, t, w), lambda bi, i: (bi, i, 0)),
        out_shape=jax.ShapeDtypeStruct((b, nb * t, w), BF16),
        scratch_shapes=[pltpu.VMEM((r, A_HEADS, 4 * A_HEAD_DIM, t), BF16),
                        pltpu.VMEM((r, A_HEADS, 8, t), F32),
                        pltpu.VMEM((r, A_HEADS, A_HEAD_DIM + MOBA_ONES_ROWS, t), F32),
                        pltpu.VMEM((w, t), F32),
                        pltpu.VMEM((MOBA_LOOKAHEAD + 1, t, t), F32)],
        compiler_params=_params("parallel", "parallel"),
        name="moba",
    )(qt, k, vt, kmean)


def _mix_kernel(chunk_dec, x_ref, g_ref, rq_ref, rkt_ref, rktd_ref, rv_ref, ya_ref, dmask_ref, qdec_ref,
                wg_ref, wa_ref, wb_ref, wo_ref, o_ref, state_ref, yb_ref):
    @pl.when(pl.program_id(1) == 0)
    def _():
        state_ref[...] = jnp.zeros_like(state_ref)

    x = x_ref[0]
    d = x.shape[1]
    hb = _rms(x, g_ref[...]).astype(BF16)
    c = R_CHUNK
    per_blk = TOKEN_TILE // c
    n_chunks = MIX_BLOCKS * per_blk
    rows = lambda ci: slice(ci * c, (ci + 1) * c)
    qk = lambda hd: slice(hd * R_QK_DIM, (hd + 1) * R_QK_DIM)
    vv = lambda hd: slice(hd * R_V_DIM, (hd + 1) * R_V_DIM)
    units = [(ci, hd) for ci in range(n_chunks) for hd in range(R_HEADS)]
    keys_t = lambda ref, ci, hd: ref[0, ci // per_blk, qk(hd), rows(ci % per_blk)]

    att = {u: _dot(rq_ref[0, rows(u[0]), qk(u[1])], keys_t(rkt_ref, *u)) for u in units}
    kv = {u: _dot(keys_t(rktd_ref, *u), rv_ref[0, rows(u[0]), vv(u[1])]) for u in units}
    rg = _dot(hb, wg_ref[:, 0:R_V_WIDTH])
    w = MIX_LATE_COLS
    late = ([lambda c=c: _dot(ya_ref[0], wa_ref[:, c:c + w]) for c in range(0, d, w)]
            + [lambda c=c: _dot(hb, wg_ref[:, R_V_WIDTH + c:R_V_WIDTH + c + w]) for c in range(0, 2 * d, w)])
    late_out = []

    cross = {}
    for hd in range(R_HEADS):
        st = state_ref[hd]
        for ci in range(n_chunks):
            cross[ci, hd] = _dot(rq_ref[0, rows(ci), qk(hd)], st.astype(BF16))
            st = chunk_dec[hd] * st + kv[ci, hd]
        state_ref[hd] = st

    gate = rg * _sigmoid(rg)
    for n, (ci, hd) in enumerate(units):
        inner = _dot((att[ci, hd] * dmask_ref[hd]).astype(BF16), rv_ref[0, rows(ci), vv(hd)])
        y = inner + cross[ci, hd] * qdec_ref[hd]
        mu = jnp.mean(y, axis=-1, keepdims=True)
        yc = y - mu
        var = jnp.mean(yc * yc, axis=-1, keepdims=True)
        yb_ref[rows(ci), vv(hd)] = (yc * lax.rsqrt(var + EPS) * gate[rows(ci), vv(hd)]).astype(BF16)
        if n < len(late):
            late_out.append(late[n]())

    pieces = d // w
    ta, ga, gb = (jnp.concatenate(late_out[k * pieces:(k + 1) * pieces], axis=1) for k in range(3))
    tb = _dot(yb_ref[...], wb_ref[...])
    merged = _sigmoid(ga) * ta + _sigmoid(gb) * tb
    o_ref[0] = x + _dot(merged.astype(BF16), wo_ref[...])


def _mix(x, g, rq, rkt, rktd, rv, ya, dmask, qdec, chunk_dec, wg, wa, wb, wo):
    b, s, d = x.shape
    t = MIX_BLOCKS * TOKEN_TILE
    tok = lambda w: pl.BlockSpec((1, t, w), lambda bi, i: (bi, i, 0))
    blk_t = pl.BlockSpec((1, MIX_BLOCKS, R_QK_WIDTH, TOKEN_TILE), lambda bi, i: (bi, i, 0, 0))
    return pl.pallas_call(
        functools.partial(_mix_kernel, chunk_dec),
        grid=(b, s // t),
        in_specs=[tok(d), _full((1, d)), tok(R_QK_WIDTH), blk_t, blk_t, tok(R_V_WIDTH), tok(A_WIDTH),
                  _full(dmask.shape), _full(qdec.shape),
                  _full(wg.shape), _full(wa.shape), _full(wb.shape), _full(wo.shape)],
        out_specs=tok(d),
        out_shape=jax.ShapeDtypeStruct((b, s, d), F32),
        scratch_shapes=[pltpu.VMEM((R_HEADS, R_QK_DIM, R_V_DIM), F32),
                        pltpu.VMEM((t, R_V_WIDTH), BF16)],
        compiler_params=_params("parallel", "arbitrary"),
        name="mix",
    )(x, g, rq, rkt, rktd, rv, ya, dmask, qdec, wg, wa, wb, wo)


def _memkv_kernel(m_ref, g_ref, w_ref, kt_ref, v_ref):
    d = m_ref.shape[2]
    mb = _rms(m_ref[0], g_ref[...]).astype(BF16)
    kt_ref[0] = _dot(mb, w_ref[:, 0:d]).T.astype(BF16)
    v_ref[0] = _dot(mb, w_ref[:, d:]).astype(BF16)


def _memkv(mem, g, w_kv):
    b, m, d = mem.shape
    return pl.pallas_call(
        _memkv_kernel,
        grid=(b,),
        in_specs=[pl.BlockSpec((1, m, d), lambda bi: (bi, 0, 0)), _full((1, d)), _full(w_kv.shape)],
        out_specs=[pl.BlockSpec((1, d, m), lambda bi: (bi, 0, 0)), pl.BlockSpec((1, m, d), lambda bi: (bi, 0, 0))],
        out_shape=[jax.ShapeDtypeStruct((b, d, m), BF16), jax.ShapeDtypeStruct((b, m, d), BF16)],
        compiler_params=_params("parallel"),
        name="memkv",
    )(mem, g, w_kv)


def _cross_kernel(x_ref, g_ref, kt_ref, v_ref, wq_ref, wo_ref, o_ref, cat_ref):
    x = x_ref[0]
    d = x.shape[1]
    dh = d // X_HEADS
    hb = _rms(x, g_ref[...]).astype(BF16)
    q = (_dot(hb, wq_ref[...]) * dh ** -0.5).astype(BF16)
    cols = lambda hd: slice(hd * dh, (hd + 1) * dh)
    scores = [_dot(q[:, cols(hd)], kt_ref[0, cols(hd), :]) for hd in range(X_HEADS)]
    for hd, s in enumerate(scores):
        p = jnp.exp(s - jnp.max(s, axis=-1, keepdims=True))
        p = p / jnp.sum(p, axis=-1, keepdims=True)
        cat_ref[:, cols(hd)] = _dot(p.astype(BF16), v_ref[0, :, cols(hd)]).astype(BF16)
    o_ref[0] = x + _dot(cat_ref[...], wo_ref[...])


def _cross(x, g, kt, v, wq, wo):
    b, s, d = x.shape
    m = v.shape[1]
    t = ROW_TILE
    tok = pl.BlockSpec((1, t, d), lambda bi, i: (bi, i, 0))
    return pl.pallas_call(
        _cross_kernel,
        grid=(b, s // t),
        in_specs=[tok, _full((1, d)), pl.BlockSpec((1, d, m), lambda bi, i: (bi, 0, 0)),
                  pl.BlockSpec((1, m, d), lambda bi, i: (bi, 0, 0)), _full(wq.shape), _full(wo.shape)],
        out_specs=tok,
        out_shape=jax.ShapeDtypeStruct((b, s, d), F32),
        scratch_shapes=[pltpu.VMEM((t, d), BF16)],
        compiler_params=_params("parallel", "parallel"),
        name="cross",
    )(x, g, kt, v, wq, wo)


def _ffn_kernel(x_ref, g_ref, wg_ref, wu_ref, wd_ref, gf_ref, o_ref):
    x = x_ref[...]
    hb = _rms(x, g_ref[...]).astype(BF16)
    n_chunks = wg_ref.shape[1] // FF_CHUNK
    cols = lambda c: slice(c * FF_CHUNK, (c + 1) * FF_CHUNK)

    def up(c):
        return _dot(hb, wg_ref[:, cols(c)]), _dot(hb, wu_ref[:, cols(c)])

    acc = x
    nxt = up(0)
    for c in range(n_chunks):
        gate, upv = nxt
        if c + 1 < n_chunks:
            nxt = up(c + 1)
        act = gate * _sigmoid(gate) * upv
        acc = acc + _dot(act.astype(BF16), wd_ref[cols(c), :])
    o_ref[...] = _rms(acc, gf_ref[...])


def _ffn(x2, g, wg, wu, wd, gf):
    n, d = x2.shape
    t = ROW_TILE
    tok = pl.BlockSpec((t, d), lambda i: (i, 0))
    return pl.pallas_call(
        _ffn_kernel,
        grid=(n // t,),
        in_specs=[tok, _full((1, d)), _full(wg.shape), _full(wu.shape), _full(wd.shape), _full((1, d))],
        out_specs=tok,
        out_shape=jax.ShapeDtypeStruct((n, d), F32),
        compiler_params=_params("parallel"),
        name="ffn",
    )(x2, g, wg, wu, wd, gf)


def _rope_tables(s):
    def cs(dim):
        inv = ROPE_THETA ** (-jnp.arange(0, dim, 2, dtype=F32) / dim)
        ang = jnp.arange(s).astype(F32)[:, None] * inv[None, :]
        return jnp.cos(ang), jnp.sin(ang)

    ca, sa = cs(A_HEAD_DIM)
    cr, sr = cs(R_QK_DIM)
    return (jnp.concatenate([ca, ca, ca, ca], axis=1), jnp.concatenate([-sa, sa, -sa, sa], axis=1),
            jnp.concatenate([cr, cr], axis=1), jnp.concatenate([-sr, sr], axis=1),
            ca.T, sa.T, cr.T, sr.T)


def _decay_tables(s):
    c = R_CHUNK
    log_g = np.log(1.0 - np.exp2(-5.0 - np.arange(R_HEADS, dtype=np.float64)))
    i = np.arange(c, dtype=np.float64)
    diff = i[:, None] - i[None, :]
    dmask = np.where(diff >= 0, np.exp(log_g[:, None, None] * np.maximum(diff, 0.0)), 0.0)
    q_dec = np.exp(log_g[:, None] * (i + 1.0))
    k_dec = np.exp(log_g[:, None] * (c - 1.0 - i))
    kdec_t = np.zeros((8, s), np.float64)
    kdec_t[:R_HEADS] = np.tile(k_dec, (1, s // c))
    qdec = np.broadcast_to(q_dec[:, :, None], (R_HEADS, c, R_V_DIM))
    chunk_dec = tuple(float(v) for v in np.exp(log_g * c))
    return (jnp.asarray(dmask, F32), jnp.asarray(qdec, F32), jnp.asarray(kdec_t, F32), chunk_dec)


def kernel(x, mem, norm_mix, w_in, w_branch_a, w_branch_b, w_out, norm_cross, norm_mem,
           w_xq, w_xkv, w_xo, norm_ffn, w_gate, w_up, w_down, norm_final):
    b, s, d = x.shape
    assert s % TOKEN_TILE == 0 and s % ROW_TILE == 0 and d % 128 == 0
    rope = _rope_tables(s)
    dmask, qdec, kdec_t, chunk_dec = _decay_tables(s)
    row = lambda v: v.reshape(1, d)
    o_aq, o_ak, o_av = 0, A_WIDTH, 2 * A_WIDTH
    o_rq = 3 * A_WIDTH
    o_rk = o_rq + R_QK_WIDTH
    o_rv = o_rk + R_QK_WIDTH
    o_g = o_rv + R_V_WIDTH

    for l in range(w_in.shape[0]):
        w = w_in[l]
        w_nat = jnp.concatenate([w[:, o_ak:o_av], w[:, o_rq:o_rk], w[:, o_rv:o_g]], axis=1).astype(BF16)
        w_t = jnp.concatenate([w[:, o_aq:o_ak], w[:, o_av:o_rq], w[:, o_rk:o_rv]], axis=1).T.astype(BF16)
        k_a, kmean, qt_a, vt_a, rq, rv, rkt, rktd = _qkv(x, row(norm_mix[l]), w_nat, w_t, rope + (kdec_t,))
        y_a = _moba(qt_a, k_a, vt_a, kmean.reshape(b, s // MOBA_BLOCK, A_WIDTH))
        x = _mix(x, row(norm_mix[l]), rq, rkt, rktd, rv, y_a, dmask, qdec, chunk_dec,
                 w[:, o_g:].astype(BF16), w_branch_a[l].astype(BF16), w_branch_b[l].astype(BF16),
                 w_out[l].astype(BF16))
        kt_m, v_m = _memkv(mem, row(norm_mem[l]), w_xkv[l].astype(BF16))
        x = _cross(x, row(norm_cross[l]), kt_m, v_m, w_xq[l].astype(BF16), w_xo[l].astype(BF16))
        last = l == w_in.shape[0] - 1
        gf = row(norm_final) if last else jnp.ones((1, d), F32)
        x = _ffn(x.reshape(b * s, d), row(norm_ffn[l]), w_gate[l].astype(BF16), w_up[l].astype(BF16),
                 w_down[l].astype(BF16), gf).reshape(b, s, d)
        if not last:
            raise NotImplementedError("the fused final norm assumes a single layer")
    return x
```

```python
import functools
import math

import numpy as np
import jax
import jax.numpy as jnp
from jax import lax
from jax.experimental import pallas as pl
from jax.experimental.pallas import tpu as pltpu

F32 = jnp.float32
BF16 = jnp.bfloat16

EPS = 1e-6
ROPE_THETA = 10000.0
A_HEADS = 8
A_HEAD_DIM = 64
A_WIDTH = A_HEADS * A_HEAD_DIM
MOBA_BLOCK = 256
MOBA_TOPK = 3
R_HEADS = 4
R_QK_DIM = 128
R_V_DIM = 256
R_QK_WIDTH = R_HEADS * R_QK_DIM
R_V_WIDTH = R_HEADS * R_V_DIM
R_CHUNK = 128
X_HEADS = 4
FF_CHUNK = 256
TOKEN_TILE = MOBA_BLOCK
ROW_TILE = 1024
CROSS_PARTS = 4
QKV_BLOCKS = 2
MIX_LATE_COLS = 256
MIX_BLOCKS = 2
NEG = -1e30
LOG2_E = math.log2(math.e)
MOBA_LOOKAHEAD = 7
MOBA_ROWS = 2
MOBA_UNROLL = 4
MOBA_ONES_ROWS = 16
VMEM_LIMIT = 56 * 1024 * 1024


def _dot(a, b):
    return jnp.dot(a, b, preferred_element_type=F32)


def _rms(x, g):
    return x * lax.rsqrt(jnp.mean(x * x, axis=-1, keepdims=True) + EPS) * g


def _sigmoid(x):
    return 1.0 / (1.0 + jnp.exp(-x))


def _params(*sem):
    return pltpu.CompilerParams(dimension_semantics=sem, vmem_limit_bytes=VMEM_LIMIT)


def _full(shape):
    n = len(shape)
    return pl.BlockSpec(shape, lambda *_: (0,) * n, pipeline_mode=pl.Buffered(1))


def _qkv_kernel(*refs):
    for sb in range(QKV_BLOCKS):
        _qkv_block(sb, *refs)


def _qkv_block(sb, x_ref, g_ref, wn_ref, wt_ref, cosa_ref, sina_ref, cosr_ref, sinr_ref,
               costa_ref, sinta_ref, costr_ref, sintr_ref, kdec_ref,
               k_ref, kmean_ref, qt_ref, vt_ref, rq_ref, rv_ref, rkt_ref, rktd_ref):
    t = TOKEN_TILE
    tok = slice(sb * t, (sb + 1) * t)
    h = _rms(x_ref[0, tok, :], g_ref[...])
    hb = h.astype(BF16)
    ht = h.T.astype(BF16)
    lane = lax.broadcasted_iota(jnp.int32, (t, 128), 1)
    first_half = (lane & 32) == 0

    ak = _dot(hb, wn_ref[:, 0:A_WIDTH])
    cosa, sina = cosa_ref[tok, :], sina_ref[tok, :]
    for c in range(A_WIDTH // 128):
        blk = ak[:, c * 128:(c + 1) * 128]
        rot = jnp.where(first_half, pltpu.roll(blk, 96, 1), pltpu.roll(blk, 32, 1))
        kr = blk * cosa + rot * sina
        k_ref[0, sb, :, c * 128:(c + 1) * 128] = kr.astype(BF16)
        kmean_ref[0, sb, :, c * 128:(c + 1) * 128] = jnp.mean(kr, axis=0, keepdims=True)

    rq = _dot(hb, wn_ref[:, A_WIDTH:A_WIDTH + R_QK_WIDTH])
    cosr, sinr = cosr_ref[tok, :], sinr_ref[tok, :]
    for c in range(R_HEADS):
        blk = rq[:, c * 128:(c + 1) * 128]
        rq_ref[0, tok, c * 128:(c + 1) * 128] = (blk * cosr + pltpu.roll(blk, 64, 1) * sinr).astype(BF16)

    rv_ref[0, tok, :] = _dot(hb, wn_ref[:, A_WIDTH + R_QK_WIDTH:]).astype(BF16)

    qt = _dot(wt_ref[0:A_WIDTH, :], ht)
    cost, sint = costa_ref[:, tok], sinta_ref[:, tok]
    half = A_HEAD_DIM // 2
    scale_a = A_HEAD_DIM ** -0.5 * LOG2_E
    for hd in range(A_HEADS):
        x1 = qt[hd * A_HEAD_DIM:hd * A_HEAD_DIM + half]
        x2 = qt[hd * A_HEAD_DIM + half:(hd + 1) * A_HEAD_DIM]
        qt_ref[0, sb, hd * A_HEAD_DIM:hd * A_HEAD_DIM + half, :] = ((x1 * cost - x2 * sint) * scale_a).astype(BF16)
        qt_ref[0, sb, hd * A_HEAD_DIM + half:(hd + 1) * A_HEAD_DIM, :] = ((x2 * cost + x1 * sint) * scale_a).astype(BF16)

    vt_ref[0, sb] = _dot(wt_ref[A_WIDTH:2 * A_WIDTH, :], ht).astype(BF16)

    rkt = _dot(wt_ref[2 * A_WIDTH:, :], ht)
    cost, sint = costr_ref[:, tok], sintr_ref[:, tok]
    half = R_QK_DIM // 2
    scale_r = R_QK_DIM ** -0.5
    for hd in range(R_HEADS):
        x1 = rkt[hd * R_QK_DIM:hd * R_QK_DIM + half]
        x2 = rkt[hd * R_QK_DIM + half:(hd + 1) * R_QK_DIM]
        o1 = (x1 * cost - x2 * sint) * scale_r
        o2 = (x2 * cost + x1 * sint) * scale_r
        dec = kdec_ref[hd:hd + 1, tok]
        rkt_ref[0, sb, hd * R_QK_DIM:hd * R_QK_DIM + half, :] = o1.astype(BF16)
        rkt_ref[0, sb, hd * R_QK_DIM + half:(hd + 1) * R_QK_DIM, :] = o2.astype(BF16)
        rktd_ref[0, sb, hd * R_QK_DIM:hd * R_QK_DIM + half, :] = (o1 * dec).astype(BF16)
        rktd_ref[0, sb, hd * R_QK_DIM + half:(hd + 1) * R_QK_DIM, :] = (o2 * dec).astype(BF16)


def _qkv(x, g, w_nat, w_t, tabs):
    b, s, d = x.shape
    nb = s // TOKEN_TILE
    qb = QKV_BLOCKS
    t = qb * TOKEN_TILE
    tok = lambda w: pl.BlockSpec((1, t, w), lambda bi, i: (bi, i, 0))
    blk_t = lambda r: pl.BlockSpec((1, qb, r, TOKEN_TILE), lambda bi, i: (bi, i, 0, 0))
    nat_tab = pl.BlockSpec((t, 128), lambda bi, i: (i, 0))
    tr_tab = lambda r: pl.BlockSpec((r, t), lambda bi, i: (0, i))
    return pl.pallas_call(
        _qkv_kernel,
        grid=(b, nb // qb),
        in_specs=[tok(d), _full((1, d)), _full(w_nat.shape), _full(w_t.shape),
                  nat_tab, nat_tab, nat_tab, nat_tab,
                  tr_tab(A_HEAD_DIM // 2), tr_tab(A_HEAD_DIM // 2),
                  tr_tab(R_QK_DIM // 2), tr_tab(R_QK_DIM // 2), tr_tab(8)],
        out_specs=[pl.BlockSpec((1, qb, TOKEN_TILE, A_WIDTH), lambda bi, i: (bi, i, 0, 0)),
                   pl.BlockSpec((1, qb, 1, A_WIDTH), lambda bi, i: (bi, i, 0, 0)),
                   blk_t(A_WIDTH), blk_t(A_WIDTH), tok(R_QK_WIDTH), tok(R_V_WIDTH),
                   blk_t(R_QK_WIDTH), blk_t(R_QK_WIDTH)],
        out_shape=[jax.ShapeDtypeStruct((b, nb, TOKEN_TILE, A_WIDTH), BF16),
                   jax.ShapeDtypeStruct((b, nb, 1, A_WIDTH), F32),
                   jax.ShapeDtypeStruct((b, nb, A_WIDTH, TOKEN_TILE), BF16),
                   jax.ShapeDtypeStruct((b, nb, A_WIDTH, TOKEN_TILE), BF16),
                   jax.ShapeDtypeStruct((b, s, R_QK_WIDTH), BF16),
                   jax.ShapeDtypeStruct((b, s, R_V_WIDTH), BF16),
                   jax.ShapeDtypeStruct((b, nb, R_QK_WIDTH, TOKEN_TILE), BF16),
                   jax.ShapeDtypeStruct((b, nb, R_QK_WIDTH, TOKEN_TILE), BF16)],
        compiler_params=_params("parallel", "parallel"),
        name="qkv",
    )(x, g, w_nat, w_t, *tabs)


def _moba_kernel(qt_ref, k_ref, vt_ref, km_ref, o_ref, qm_ref, m_ref, acc_ref, ot_ref, s_ref):
    i = pl.program_id(1)
    nb = km_ref.shape[1]
    t = MOBA_BLOCK
    dh = A_HEAD_DIM
    row = lax.broadcasted_iota(jnp.int32, (2 * dh, t), 0)
    blk = lax.broadcasted_iota(jnp.int32, (nb, t), 0)
    past = blk < i
    causal = (lax.broadcasted_iota(jnp.int32, (t, t), 0) <= lax.broadcasted_iota(jnp.int32, (t, t), 1))
    ones = jnp.ones((MOBA_ONES_ROWS, t), BF16)
    lanes = lambda hd: slice((hd // 2) * 128, (hd // 2 + 1) * 128)
    rows = lambda hd: slice(hd * dh, (hd + 1) * dh)
    row_heads = [(b, hd) for b in range(MOBA_ROWS) for hd in range(A_HEADS)]

    def select_blocks(b, hd):
        hh = hd % 2
        qt_pair = qt_ref[b, 0, lanes(hd), :]
        mine = (row >= hh * dh) & (row < (hh + 1) * dh)
        qt = jnp.where(mine, qt_pair, jnp.zeros_like(qt_pair))
        qm_ref[b, hd, 0:2 * dh, :] = qt

        km = km_ref[b, :, lanes(hd)]
        km_hi = km.astype(BF16)
        km_lo = (km - km_hi.astype(F32)).astype(BF16)
        bs = jnp.where(past, _dot(km_hi, qt) + _dot(km_lo, qt), -jnp.inf)
        cnt = jnp.zeros((nb, t), F32)
        for m in range(nb):
            r = bs[m:m + 1, :]
            tie = jnp.where(blk > m, 1.0, 0.0)
            cnt = cnt + jnp.where(r > bs, 1.0, jnp.where(r == bs, tie, 0.0))
        bias = jnp.where(past & (cnt < MOBA_TOPK), 0.0, NEG).astype(BF16)
        qm_ref[b, hd, 2 * dh:, :] = jnp.concatenate([bias, jnp.zeros((2 * dh - nb, t), BF16)], axis=0)

    key_lane = lax.broadcasted_iota(jnp.int32, (t, 2 * dh), 1)

    def keys_with_block_column(b, j):
        onehot = jnp.where(key_lane == j, 1.0, 0.0).astype(BF16)
        return [jnp.concatenate([k_ref[b, j, :, lanes(2 * p)], onehot], axis=1) for p in range(A_HEADS // 2)]

    def pv(b, j, hd, p):
        vt_aug = jnp.concatenate([vt_ref[b, j, rows(hd), :], ones], axis=0)
        return _dot(vt_aug, p.astype(BF16))

    def own_unit(b, hd):
        def scores():
            return jnp.where(causal, _dot(k_ref[b, i, :, lanes(hd)], qm_ref[b, hd, 0:2 * dh, :]), NEG)

        def update(s, m0):
            m_ref[b, hd] = jnp.broadcast_to(m0, (8, t))
            acc_ref[b, hd] = pv(b, i, hd, jnp.exp2(s - m0))

        return scores, update

    def past_units(b, j):
        keys = keys_with_block_column(b, j)

        def unit(hd):
            def scores():
                return _dot(keys[hd // 2], qm_ref[b, hd])

            def update(s, m_blk):
                m_old = m_ref[b, hd]
                m_new = jnp.maximum(m_old, m_blk)
                alpha = jnp.exp2(m_old - m_new)
                m_ref[b, hd] = m_new
                acc_ref[b, hd] = alpha[0:1, :] * acc_ref[b, hd] + pv(b, j, hd, jnp.exp2(s - m_new[0:1, :]))

            return scores, update

        return [unit(hd) for hd in range(A_HEADS)]

    def pipelined(units):
        blk_max = {}
        slots = MOBA_LOOKAHEAD + 1
        for step in range(len(units) + MOBA_LOOKAHEAD):
            if step < len(units):
                sc = units[step][0]()
                s_ref[step % slots] = sc
                blk_max[step] = jnp.max(sc, axis=0, keepdims=True)
            if step >= MOBA_LOOKAHEAD:
                u = step - MOBA_LOOKAHEAD
                units[u][1](s_ref[u % slots], blk_max.pop(u))

    n_groups = i // MOBA_UNROLL
    for left in range(MOBA_UNROLL):
        @pl.when(i % MOBA_UNROLL == left)
        def _(left=left):
            for b, hd in row_heads:
                select_blocks(b, hd)
            units = [own_unit(b, hd) for b, hd in row_heads]
            for r in range(left):
                for b in range(MOBA_ROWS):
                    units += past_units(b, n_groups * MOBA_UNROLL + r)
            pipelined(units)

    def group_body(jj, carry):
        pipelined([u for r in range(MOBA_UNROLL) for b in range(MOBA_ROWS)
                   for u in past_units(b, MOBA_UNROLL * jj + r)])
        return carry

    lax.fori_loop(0, n_groups, group_body, 0)

    for b in range(MOBA_ROWS):
        for hd in range(A_HEADS):
            ot_ref[rows(hd), :] = acc_ref[b, hd, 0:dh, :] / acc_ref[b, hd, dh:dh + 1, :]
        o_ref[b] = ot_ref[...].T.astype(BF16)


def _moba(qt, k, vt, kmean):
    b, nb, t, w = k.shape
    r = MOBA_ROWS
    assert b % r == 0
    return pl.pallas_call(
        _moba_kernel,
        grid=(b // r, nb),
        in_specs=[pl.BlockSpec((r, 1, w, t), lambda bi, i: (bi, i, 0, 0)),
                  pl.BlockSpec((r, nb, t, w), lambda bi, i: (bi, 0, 0, 0)),
                  pl.BlockSpec((r, nb, w, t), lambda bi, i: (bi, 0, 0, 0)),
                  pl.BlockSpec((r, nb, w), lambda bi, i: (bi, 0, 0))],
        out_specs=pl.BlockSpec((r, t, w), lambda bi, i: (bi, i, 0)),
        out_shape=jax.ShapeDtypeStruct((b, nb * t, w), BF16),
        scratch_shapes=[pltpu.VMEM((r, A_HEADS, 4 * A_HEAD_DIM, t), BF16),
                        pltpu.VMEM((r, A_HEADS, 8, t), F32),
                        pltpu.VMEM((r, A_HEADS, A_HEAD_DIM + MOBA_ONES_ROWS, t), F32),
                        pltpu.VMEM((w, t), F32),
                        pltpu.VMEM((MOBA_LOOKAHEAD + 1, t, t), F32)],
        compiler_params=_params("parallel", "parallel"),
        name="moba",
    )(qt, k, vt, kmean)


def _mix_kernel(chunk_dec, x_ref, g_ref, rq_ref, rkt_ref, rktd_ref, rv_ref, ya_ref, dmask_ref, qdec_ref,
                wg_ref, wa_ref, wb_ref, wo_ref, o_ref, state_ref, yb_ref):
    @pl.when(pl.program_id(1) == 0)
    def _():
        state_ref[...] = jnp.zeros_like(state_ref)

    x = x_ref[0]
    d = x.shape[1]
    hb = _rms(x, g_ref[...]).astype(BF16)
    c = R_CHUNK
    per_blk = TOKEN_TILE // c
    n_chunks = MIX_BLOCKS * per_blk
    rows = lambda ci: slice(ci * c, (ci + 1) * c)
    qk = lambda hd: slice(hd * R_QK_DIM, (hd + 1) * R_QK_DIM)
    vv = lambda hd: slice(hd * R_V_DIM, (hd + 1) * R_V_DIM)
    units = [(ci, hd) for ci in range(n_chunks) for hd in range(R_HEADS)]
    keys_t = lambda ref, ci, hd: ref[0, ci // per_blk, qk(hd), rows(ci % per_blk)]

    att = {u: _dot(rq_ref[0, rows(u[0]), qk(u[1])], keys_t(rkt_ref, *u)) for u in units}
    kv = {u: _dot(keys_t(rktd_ref, *u), rv_ref[0, rows(u[0]), vv(u[1])]) for u in units}
    rg = _dot(hb, wg_ref[:, 0:R_V_WIDTH])
    w = MIX_LATE_COLS
    late = ([lambda c=c: _dot(ya_ref[0], wa_ref[:, c:c + w]) for c in range(0, d, w)]
            + [lambda c=c: _dot(hb, wg_ref[:, R_V_WIDTH + c:R_V_WIDTH + c + w]) for c in range(0, 2 * d, w)])
    late_out = []

    cross = {}
    for hd in range(R_HEADS):
        st = state_ref[hd]
        for ci in range(n_chunks):
            cross[ci, hd] = _dot(rq_ref[0, rows(ci), qk(hd)], st.astype(BF16))
            st = chunk_dec[hd] * st + kv[ci, hd]
        state_ref[hd] = st

    gate = rg * _sigmoid(rg)
    for n, (ci, hd) in enumerate(units):
        inner = _dot((att[ci, hd] * dmask_ref[hd]).astype(BF16), rv_ref[0, rows(ci), vv(hd)])
        y = inner + cross[ci, hd] * qdec_ref[hd]
        mu = jnp.mean(y, axis=-1, keepdims=True)
        yc = y - mu
        var = jnp.mean(yc * yc, axis=-1, keepdims=True)
        yb_ref[rows(ci), vv(hd)] = (yc * lax.rsqrt(var + EPS) * gate[rows(ci), vv(hd)]).astype(BF16)
        if n < len(late):
            late_out.append(late[n]())

    pieces = d // w
    ta, ga, gb = (jnp.concatenate(late_out[k * pieces:(k + 1) * pieces], axis=1) for k in range(3))
    tb = _dot(yb_ref[...], wb_ref[...])
    merged = _sigmoid(ga) * ta + _sigmoid(gb) * tb
    o_ref[0] = x + _dot(merged.astype(BF16), wo_ref[...])


def _mix(x, g, rq, rkt, rktd, rv, ya, dmask, qdec, chunk_dec, wg, wa, wb, wo):
    b, s, d = x.shape
    t = MIX_BLOCKS * TOKEN_TILE
    tok = lambda w: pl.BlockSpec((1, t, w), lambda bi, i: (bi, i, 0))
    blk_t = pl.BlockSpec((1, MIX_BLOCKS, R_QK_WIDTH, TOKEN_TILE), lambda bi, i: (bi, i, 0, 0))
    return pl.pallas_call(
        functools.partial(_mix_kernel, chunk_dec),
        grid=(b, s // t),
        in_specs=[tok(d), _full((1, d)), tok(R_QK_WIDTH), blk_t, blk_t, tok(R_V_WIDTH), tok(A_WIDTH),
                  _full(dmask.shape), _full(qdec.shape),
                  _full(wg.shape), _full(wa.shape), _full(wb.shape), _full(wo.shape)],
        out_specs=tok(d),
        out_shape=jax.ShapeDtypeStruct((b, s, d), F32),
        scratch_shapes=[pltpu.VMEM((R_HEADS, R_QK_DIM, R_V_DIM), F32),
                        pltpu.VMEM((t, R_V_WIDTH), BF16)],
        compiler_params=_params("parallel", "arbitrary"),
        name="mix",
    )(x, g, rq, rkt, rktd, rv, ya, dmask, qdec, wg, wa, wb, wo)


def _cross_kernel(x_ref, g_ref, m_ref, gm_ref, wkv_ref, wq_ref, wo_ref, o_ref, kt_ref, v_ref, cat_ref):
    d = x_ref.shape[2]
    dh = d // X_HEADS

    @pl.when(pl.program_id(1) == 0)
    def _():
        mb = _rms(m_ref[0], gm_ref[...]).astype(BF16)
        kt_ref[...] = _dot(mb, wkv_ref[:, 0:d]).T.astype(BF16)
        v_ref[...] = _dot(mb, wkv_ref[:, d:]).astype(BF16)

    cols = lambda hd: slice(hd * dh, (hd + 1) * dh)
    n_parts = CROSS_PARTS
    t = x_ref.shape[1] // n_parts
    part = lambda r: slice(r * t, (r + 1) * t)

    def project(r):
        hb = _rms(x_ref[0, part(r), :], g_ref[...]).astype(BF16)
        q = (_dot(hb, wq_ref[...]) * dh ** -0.5).astype(BF16)
        return [_dot(q[:, cols(hd)], kt_ref[cols(hd), :]) for hd in range(X_HEADS)]

    nxt = project(0)
    for r in range(n_parts):
        scores = nxt
        if r + 1 < n_parts:
            nxt = project(r + 1)
        for hd, s in enumerate(scores):
            p = jnp.exp(s - jnp.max(s, axis=-1, keepdims=True))
            p = p / jnp.sum(p, axis=-1, keepdims=True)
            cat_ref[part(r), cols(hd)] = _dot(p.astype(BF16), v_ref[:, cols(hd)]).astype(BF16)
        o_ref[0, part(r), :] = x_ref[0, part(r), :] + _dot(cat_ref[part(r), :], wo_ref[...])


def _cross(x, g, mem, gm, wkv, wq, wo):
    b, s, d = x.shape
    m = mem.shape[1]
    t = ROW_TILE
    tok = pl.BlockSpec((1, t, d), lambda bi, i: (bi, i, 0))
    return pl.pallas_call(
        _cross_kernel,
        grid=(b, s // t),
        in_specs=[tok, _full((1, d)), pl.BlockSpec((1, m, d), lambda bi, i: (bi, 0, 0)), _full((1, d)),
                  _full(wkv.shape), _full(wq.shape), _full(wo.shape)],
        out_specs=tok,
        out_shape=jax.ShapeDtypeStruct((b, s, d), F32),
        scratch_shapes=[pltpu.VMEM((d, m), BF16), pltpu.VMEM((m, d), BF16),
                        pltpu.VMEM((t, d), BF16)],
        compiler_params=_params("parallel", "arbitrary"),
        name="cross",
    )(x, g, mem, gm, wkv, wq, wo)


def _ffn_kernel(x_ref, g_ref, wg_ref, wu_ref, wd_ref, *rest):
    o_ref = rest[-1]
    x = x_ref[...]
    hb = _rms(x, g_ref[...]).astype(BF16)
    n_chunks = wg_ref.shape[1] // FF_CHUNK
    cols = lambda c: slice(c * FF_CHUNK, (c + 1) * FF_CHUNK)

    def up(c):
        return _dot(hb, wg_ref[:, cols(c)]), _dot(hb, wu_ref[:, cols(c)])

    acc = x
    nxt = up(0)
    for c in range(n_chunks):
        gate, upv = nxt
        if c + 1 < n_chunks:
            nxt = up(c + 1)
        act = gate * _sigmoid(gate) * upv
        acc = acc + _dot(act.astype(BF16), wd_ref[cols(c), :])
    o_ref[...] = _rms(acc, rest[0][...]) if len(rest) == 2 else acc


def _ffn(x2, g, wg, wu, wd, final_gain=None):
    n, d = x2.shape
    t = ROW_TILE
    tok = pl.BlockSpec((t, d), lambda i: (i, 0))
    final = () if final_gain is None else (final_gain,)
    return pl.pallas_call(
        _ffn_kernel,
        grid=(n // t,),
        in_specs=[tok, _full((1, d)), _full(wg.shape), _full(wu.shape), _full(wd.shape)] + [_full((1, d))] * len(final),
        out_specs=tok,
        out_shape=jax.ShapeDtypeStruct((n, d), F32),
        compiler_params=_params("parallel"),
        name="ffn",
    )(x2, g, wg, wu, wd, *final)


def _rope_tables(s):
    def cs(dim):
        inv = ROPE_THETA ** (-jnp.arange(0, dim, 2, dtype=F32) / dim)
        ang = jnp.arange(s).astype(F32)[:, None] * inv[None, :]
        return jnp.cos(ang), jnp.sin(ang)

    ca, sa = cs(A_HEAD_DIM)
    cr, sr = cs(R_QK_DIM)
    return (jnp.concatenate([ca, ca, ca, ca], axis=1), jnp.concatenate([-sa, sa, -sa, sa], axis=1),
            jnp.concatenate([cr, cr], axis=1), jnp.concatenate([-sr, sr], axis=1),
            ca.T, sa.T, cr.T, sr.T)


def _decay_tables(s):
    c = R_CHUNK
    log_g = np.log(1.0 - np.exp2(-5.0 - np.arange(R_HEADS, dtype=np.float64)))
    i = np.arange(c, dtype=np.float64)
    diff = i[:, None] - i[None, :]
    dmask = np.where(diff >= 0, np.exp(log_g[:, None, None] * np.maximum(diff, 0.0)), 0.0)
    q_dec = np.exp(log_g[:, None] * (i + 1.0))
    k_dec = np.exp(log_g[:, None] * (c - 1.0 - i))
    kdec_t = np.zeros((8, s), np.float64)
    kdec_t[:R_HEADS] = np.tile(k_dec, (1, s // c))
    qdec = np.broadcast_to(q_dec[:, :, None], (R_HEADS, c, R_V_DIM))
    chunk_dec = tuple(float(v) for v in np.exp(log_g * c))
    return (jnp.asarray(dmask, F32), jnp.asarray(qdec, F32), jnp.asarray(kdec_t, F32), chunk_dec)


def kernel(x, mem, norm_mix, w_in, w_branch_a, w_branch_b, w_out, norm_cross, norm_mem,
           w_xq, w_xkv, w_xo, norm_ffn, w_gate, w_up, w_down, norm_final):
    b, s, d = x.shape
    assert s % TOKEN_TILE == 0 and s % ROW_TILE == 0 and d % 128 == 0
    rope = _rope_tables(s)
    dmask, qdec, kdec_t, chunk_dec = _decay_tables(s)
    row = lambda v: v.reshape(1, d)
    o_aq, o_ak, o_av = 0, A_WIDTH, 2 * A_WIDTH
    o_rq = 3 * A_WIDTH
    o_rk = o_rq + R_QK_WIDTH
    o_rv = o_rk + R_QK_WIDTH
    o_g = o_rv + R_V_WIDTH

    for l in range(w_in.shape[0]):
        w = w_in[l]
        w_nat = jnp.concatenate([w[:, o_ak:o_av], w[:, o_rq:o_rk], w[:, o_rv:o_g]], axis=1).astype(BF16)
        w_t = jnp.concatenate([w[:, o_aq:o_ak], w[:, o_av:o_rq], w[:, o_rk:o_rv]], axis=1).T.astype(BF16)
        k_a, kmean, qt_a, vt_a, rq, rv, rkt, rktd = _qkv(x, row(norm_mix[l]), w_nat, w_t, rope + (kdec_t,))
        y_a = _moba(qt_a, k_a, vt_a, kmean.reshape(b, s // MOBA_BLOCK, A_WIDTH))
        x = _mix(x, row(norm_mix[l]), rq, rkt, rktd, rv, y_a, dmask, qdec, chunk_dec,
                 w[:, o_g:].astype(BF16), w_branch_a[l].astype(BF16), w_branch_b[l].astype(BF16),
                 w_out[l].astype(BF16))
        x = _cross(x, row(norm_cross[l]), mem, row(norm_mem[l]), w_xkv[l].astype(BF16),
                   w_xq[l].astype(BF16), w_xo[l].astype(BF16))
        last = l == w_in.shape[0] - 1
        x = _ffn(x.reshape(b * s, d), row(norm_ffn[l]), w_gate[l].astype(BF16), w_up[l].astype(BF16),
                 w_down[l].astype(BF16), row(norm_final) if last else None).reshape(b, s, d)
    return x
```

```python
import functools
import math

import numpy as np
import jax
import jax.numpy as jnp
from jax import lax
from jax.experimental import pallas as pl
from jax.experimental.pallas import tpu as pltpu

F32 = jnp.float32
BF16 = jnp.bfloat16

EPS = 1e-6
ROPE_THETA = 10000.0
A_HEADS = 8
A_HEAD_DIM = 64
A_WIDTH = A_HEADS * A_HEAD_DIM
MOBA_BLOCK = 256
MOBA_TOPK = 3
R_HEADS = 4
R_QK_DIM = 128
R_V_DIM = 256
R_QK_WIDTH = R_HEADS * R_QK_DIM
R_V_WIDTH = R_HEADS * R_V_DIM
R_CHUNK = 128
X_HEADS = 4
COL_AQ, COL_AK, COL_AV = 0, A_WIDTH, 2 * A_WIDTH
COL_RQ = 3 * A_WIDTH
COL_RK = COL_RQ + R_QK_WIDTH
COL_RV = COL_RK + R_QK_WIDTH
COL_GATES = COL_RV + R_V_WIDTH
FF_CHUNK = 256
TOKEN_TILE = MOBA_BLOCK
ROW_TILE = 1024
CROSS_PARTS = 4
QKV_BLOCKS = 2
MIX_LATE_COLS = 256
MIX_BLOCKS = 2
NEG = -1e30
LOG2_E = math.log2(math.e)
MOBA_LOOKAHEAD = 7
MOBA_ROWS = 2
MOBA_UNROLL = 4
MOBA_ONES_ROWS = 16
VMEM_LIMIT = 56 * 1024 * 1024


def _dot(a, b):
    return jnp.dot(a, b, preferred_element_type=F32)


def _rms(x, g):
    return x * lax.rsqrt(jnp.mean(x * x, axis=-1, keepdims=True) + EPS) * g


def _sigmoid(x):
    return 1.0 / (1.0 + jnp.exp(-x))


def _params(*sem):
    return pltpu.CompilerParams(dimension_semantics=sem, vmem_limit_bytes=VMEM_LIMIT)


def _full(shape):
    n = len(shape)
    return pl.BlockSpec(shape, lambda *_: (0,) * n, pipeline_mode=pl.Buffered(1))


def _qkv_kernel(*refs):
    for sb in range(QKV_BLOCKS):
        _qkv_block(sb, *refs)


def _qkv_block(sb, x_ref, g_ref, w_ref, wqt_ref, wvt_ref, wkt_ref, cosa_ref, sina_ref, cosr_ref, sinr_ref,
               costa_ref, sinta_ref, costr_ref, sintr_ref, kdec_ref,
               k_ref, kmean_ref, qt_ref, vt_ref, rq_ref, rv_ref, rkt_ref, rktd_ref):
    t = TOKEN_TILE
    tok = slice(sb * t, (sb + 1) * t)
    h = _rms(x_ref[0, tok, :], g_ref[...])
    hb = h.astype(BF16)
    ht = h.T.astype(BF16)
    lane = lax.broadcasted_iota(jnp.int32, (t, 128), 1)
    first_half = (lane & 32) == 0

    ak = _dot(hb, w_ref[:, COL_AK:COL_AV])
    cosa, sina = cosa_ref[tok, :], sina_ref[tok, :]
    for c in range(A_WIDTH // 128):
        blk = ak[:, c * 128:(c + 1) * 128]
        rot = jnp.where(first_half, pltpu.roll(blk, 96, 1), pltpu.roll(blk, 32, 1))
        kr = blk * cosa + rot * sina
        k_ref[0, sb, :, c * 128:(c + 1) * 128] = kr.astype(BF16)
        kmean_ref[0, sb, :, c * 128:(c + 1) * 128] = jnp.mean(kr, axis=0, keepdims=True)

    rq = _dot(hb, w_ref[:, COL_RQ:COL_RK])
    cosr, sinr = cosr_ref[tok, :], sinr_ref[tok, :]
    for c in range(R_HEADS):
        blk = rq[:, c * 128:(c + 1) * 128]
        rq_ref[0, tok, c * 128:(c + 1) * 128] = (blk * cosr + pltpu.roll(blk, 64, 1) * sinr).astype(BF16)

    rv_ref[0, tok, :] = _dot(hb, w_ref[:, COL_RV:COL_GATES]).astype(BF16)

    qt = _dot(wqt_ref[...], ht)
    cost, sint = costa_ref[:, tok], sinta_ref[:, tok]
    half = A_HEAD_DIM // 2
    scale_a = A_HEAD_DIM ** -0.5 * LOG2_E
    for hd in range(A_HEADS):
        x1 = qt[hd * A_HEAD_DIM:hd * A_HEAD_DIM + half]
        x2 = qt[hd * A_HEAD_DIM + half:(hd + 1) * A_HEAD_DIM]
        qt_ref[0, sb, hd * A_HEAD_DIM:hd * A_HEAD_DIM + half, :] = ((x1 * cost - x2 * sint) * scale_a).astype(BF16)
        qt_ref[0, sb, hd * A_HEAD_DIM + half:(hd + 1) * A_HEAD_DIM, :] = ((x2 * cost + x1 * sint) * scale_a).astype(BF16)

    vt_ref[0, sb] = _dot(wvt_ref[...], ht).astype(BF16)

    rkt = _dot(wkt_ref[...], ht)
    cost, sint = costr_ref[:, tok], sintr_ref[:, tok]
    half = R_QK_DIM // 2
    scale_r = R_QK_DIM ** -0.5
    for hd in range(R_HEADS):
        x1 = rkt[hd * R_QK_DIM:hd * R_QK_DIM + half]
        x2 = rkt[hd * R_QK_DIM + half:(hd + 1) * R_QK_DIM]
        o1 = (x1 * cost - x2 * sint) * scale_r
        o2 = (x2 * cost + x1 * sint) * scale_r
        dec = kdec_ref[hd:hd + 1, tok]
        rkt_ref[0, sb, hd * R_QK_DIM:hd * R_QK_DIM + half, :] = o1.astype(BF16)
        rkt_ref[0, sb, hd * R_QK_DIM + half:(hd + 1) * R_QK_DIM, :] = o2.astype(BF16)
        rktd_ref[0, sb, hd * R_QK_DIM:hd * R_QK_DIM + half, :] = (o1 * dec).astype(BF16)
        rktd_ref[0, sb, hd * R_QK_DIM + half:(hd + 1) * R_QK_DIM, :] = (o2 * dec).astype(BF16)


def _qkv(x, g, w, wqt, wvt, wkt, tabs):
    b, s, d = x.shape
    nb = s // TOKEN_TILE
    qb = QKV_BLOCKS
    t = qb * TOKEN_TILE
    tok = lambda w: pl.BlockSpec((1, t, w), lambda bi, i: (bi, i, 0))
    blk_t = lambda r: pl.BlockSpec((1, qb, r, TOKEN_TILE), lambda bi, i: (bi, i, 0, 0))
    nat_tab = pl.BlockSpec((t, 128), lambda bi, i: (i, 0))
    tr_tab = lambda r: pl.BlockSpec((r, t), lambda bi, i: (0, i))
    return pl.pallas_call(
        _qkv_kernel,
        grid=(b, nb // qb),
        in_specs=[tok(d), _full((1, d)), _full(w.shape), _full(wqt.shape), _full(wvt.shape), _full(wkt.shape),
                  nat_tab, nat_tab, nat_tab, nat_tab,
                  tr_tab(A_HEAD_DIM // 2), tr_tab(A_HEAD_DIM // 2),
                  tr_tab(R_QK_DIM // 2), tr_tab(R_QK_DIM // 2), tr_tab(8)],
        out_specs=[pl.BlockSpec((1, qb, TOKEN_TILE, A_WIDTH), lambda bi, i: (bi, i, 0, 0)),
                   pl.BlockSpec((1, qb, 1, A_WIDTH), lambda bi, i: (bi, i, 0, 0)),
                   blk_t(A_WIDTH), blk_t(A_WIDTH), tok(R_QK_WIDTH), tok(R_V_WIDTH),
                   blk_t(R_QK_WIDTH), blk_t(R_QK_WIDTH)],
        out_shape=[jax.ShapeDtypeStruct((b, nb, TOKEN_TILE, A_WIDTH), BF16),
                   jax.ShapeDtypeStruct((b, nb, 1, A_WIDTH), F32),
                   jax.ShapeDtypeStruct((b, nb, A_WIDTH, TOKEN_TILE), BF16),
                   jax.ShapeDtypeStruct((b, nb, A_WIDTH, TOKEN_TILE), BF16),
                   jax.ShapeDtypeStruct((b, s, R_QK_WIDTH), BF16),
                   jax.ShapeDtypeStruct((b, s, R_V_WIDTH), BF16),
                   jax.ShapeDtypeStruct((b, nb, R_QK_WIDTH, TOKEN_TILE), BF16),
                   jax.ShapeDtypeStruct((b, nb, R_QK_WIDTH, TOKEN_TILE), BF16)],
        compiler_params=_params("parallel", "parallel"),
        name="qkv",
    )(x, g, w, wqt, wvt, wkt, *tabs)


def _moba_kernel(qt_ref, k_ref, vt_ref, km_ref, o_ref, qm_ref, m_ref, acc_ref, ot_ref, s_ref):
    i = pl.program_id(1)
    nb = km_ref.shape[1]
    t = MOBA_BLOCK
    dh = A_HEAD_DIM
    row = lax.broadcasted_iota(jnp.int32, (2 * dh, t), 0)
    blk = lax.broadcasted_iota(jnp.int32, (nb, t), 0)
    past = blk < i
    causal = (lax.broadcasted_iota(jnp.int32, (t, t), 0) <= lax.broadcasted_iota(jnp.int32, (t, t), 1))
    ones = jnp.ones((MOBA_ONES_ROWS, t), BF16)
    lanes = lambda hd: slice((hd // 2) * 128, (hd // 2 + 1) * 128)
    rows = lambda hd: slice(hd * dh, (hd + 1) * dh)
    row_heads = [(b, hd) for b in range(MOBA_ROWS) for hd in range(A_HEADS)]

    def select_blocks(b, hd):
        hh = hd % 2
        qt_pair = qt_ref[b, 0, lanes(hd), :]
        mine = (row >= hh * dh) & (row < (hh + 1) * dh)
        qt = jnp.where(mine, qt_pair, jnp.zeros_like(qt_pair))
        qm_ref[b, hd, 0:2 * dh, :] = qt

        km = km_ref[b, :, lanes(hd)]
        km_hi = km.astype(BF16)
        km_lo = (km - km_hi.astype(F32)).astype(BF16)
        bs = jnp.where(past, _dot(km_hi, qt) + _dot(km_lo, qt), -jnp.inf)
        cnt = jnp.zeros((nb, t), F32)
        for m in range(nb):
            r = bs[m:m + 1, :]
            tie = jnp.where(blk > m, 1.0, 0.0)
            cnt = cnt + jnp.where(r > bs, 1.0, jnp.where(r == bs, tie, 0.0))
        bias = jnp.where(past & (cnt < MOBA_TOPK), 0.0, NEG).astype(BF16)
        qm_ref[b, hd, 2 * dh:, :] = jnp.concatenate([bias, jnp.zeros((2 * dh - nb, t), BF16)], axis=0)

    key_lane = lax.broadcasted_iota(jnp.int32, (t, 2 * dh), 1)

    def keys_with_block_column(b, j):
        onehot = jnp.where(key_lane == j, 1.0, 0.0).astype(BF16)
        return [jnp.concatenate([k_ref[b, j, :, lanes(2 * p)], onehot], axis=1) for p in range(A_HEADS // 2)]

    def pv(b, j, hd, p):
        vt_aug = jnp.concatenate([vt_ref[b, j, rows(hd), :], ones], axis=0)
        return _dot(vt_aug, p.astype(BF16))

    def own_unit(b, hd):
        def scores():
            return jnp.where(causal, _dot(k_ref[b, i, :, lanes(hd)], qm_ref[b, hd, 0:2 * dh, :]), NEG)

        def update(s, m0):
            m_ref[b, hd] = jnp.broadcast_to(m0, (8, t))
            acc_ref[b, hd] = pv(b, i, hd, jnp.exp2(s - m0))

        return scores, update

    def past_units(b, j):
        keys = keys_with_block_column(b, j)

        def unit(hd):
            def scores():
                return _dot(keys[hd // 2], qm_ref[b, hd])

            def update(s, m_blk):
                m_old = m_ref[b, hd]
                m_new = jnp.maximum(m_old, m_blk)
                alpha = jnp.exp2(m_old - m_new)
                m_ref[b, hd] = m_new
                acc_ref[b, hd] = alpha[0:1, :] * acc_ref[b, hd] + pv(b, j, hd, jnp.exp2(s - m_new[0:1, :]))

            return scores, update

        return [unit(hd) for hd in range(A_HEADS)]

    def pipelined(units):
        blk_max = {}
        slots = MOBA_LOOKAHEAD + 1
        for step in range(len(units) + MOBA_LOOKAHEAD):
            if step < len(units):
                sc = units[step][0]()
                s_ref[step % slots] = sc
                blk_max[step] = jnp.max(sc, axis=0, keepdims=True)
            if step >= MOBA_LOOKAHEAD:
                u = step - MOBA_LOOKAHEAD
                units[u][1](s_ref[u % slots], blk_max.pop(u))

    n_groups = i // MOBA_UNROLL
    for left in range(MOBA_UNROLL):
        @pl.when(i % MOBA_UNROLL == left)
        def _(left=left):
            for b, hd in row_heads:
                select_blocks(b, hd)
            units = [own_unit(b, hd) for b, hd in row_heads]
            for r in range(left):
                for b in range(MOBA_ROWS):
                    units += past_units(b, n_groups * MOBA_UNROLL + r)
            pipelined(units)

    def group_body(jj, carry):
        pipelined([u for r in range(MOBA_UNROLL) for b in range(MOBA_ROWS)
                   for u in past_units(b, MOBA_UNROLL * jj + r)])
        return carry

    lax.fori_loop(0, n_groups, group_body, 0)

    for b in range(MOBA_ROWS):
        for hd in range(A_HEADS):
            ot_ref[rows(hd), :] = acc_ref[b, hd, 0:dh, :] / acc_ref[b, hd, dh:dh + 1, :]
        o_ref[b] = ot_ref[...].T.astype(BF16)


def _moba(qt, k, vt, kmean):
    b, nb, t, w = k.shape
    r = MOBA_ROWS
    assert b % r == 0
    return pl.pallas_call(
        _moba_kernel,
        grid=(b // r, nb),
        in_specs=[pl.BlockSpec((r, 1, w, t), lambda bi, i: (bi, i, 0, 0)),
                  pl.BlockSpec((r, nb, t, w), lambda bi, i: (bi, 0, 0, 0)),
                  pl.BlockSpec((r, nb, w, t), lambda bi, i: (bi, 0, 0, 0)),
                  pl.BlockSpec((r, nb, w), lambda bi, i: (bi, 0, 0))],
        out_specs=pl.BlockSpec((r, t, w), lambda bi, i: (bi, i, 0)),
        out_shape=jax.ShapeDtypeStruct((b, nb * t, w), BF16),
        scratch_shapes=[pltpu.VMEM((r, A_HEADS, 4 * A_HEAD_DIM, t), BF16),
                        pltpu.VMEM((r, A_HEADS, 8, t), F32),
                        pltpu.VMEM((r, A_HEADS, A_HEAD_DIM + MOBA_ONES_ROWS, t), F32),
                        pltpu.VMEM((w, t), F32),
                        pltpu.VMEM((MOBA_LOOKAHEAD + 1, t, t), F32)],
        compiler_params=_params("parallel", "parallel"),
        name="moba",
    )(qt, k, vt, kmean)


def _mix_kernel(chunk_dec, x_ref, g_ref, rq_ref, rkt_ref, rktd_ref, rv_ref, ya_ref, dmask_ref, qdec_ref,
                w_ref, wa_ref, wb_ref, wo_ref, o_ref, state_ref, yb_ref):
    @pl.when(pl.program_id(1) == 0)
    def _():
        state_ref[...] = jnp.zeros_like(state_ref)

    x = x_ref[0]
    d = x.shape[1]
    hb = _rms(x, g_ref[...]).astype(BF16)
    c = R_CHUNK
    per_blk = TOKEN_TILE // c
    n_chunks = MIX_BLOCKS * per_blk
    rows = lambda ci: slice(ci * c, (ci + 1) * c)
    qk = lambda hd: slice(hd * R_QK_DIM, (hd + 1) * R_QK_DIM)
    vv = lambda hd: slice(hd * R_V_DIM, (hd + 1) * R_V_DIM)
    units = [(ci, hd) for ci in range(n_chunks) for hd in range(R_HEADS)]
    keys_t = lambda ref, ci, hd: ref[0, ci // per_blk, qk(hd), rows(ci % per_blk)]

    att = {u: _dot(rq_ref[0, rows(u[0]), qk(u[1])], keys_t(rkt_ref, *u)) for u in units}
    kv = {u: _dot(keys_t(rktd_ref, *u), rv_ref[0, rows(u[0]), vv(u[1])]) for u in units}
    rg = _dot(hb, w_ref[:, COL_GATES:COL_GATES + R_V_WIDTH])
    w = MIX_LATE_COLS
    late = ([lambda c=c: _dot(ya_ref[0], wa_ref[:, c:c + w]) for c in range(0, d, w)]
            + [lambda c=c: _dot(hb, w_ref[:, c:c + w])
               for c in range(COL_GATES + R_V_WIDTH, COL_GATES + R_V_WIDTH + 2 * d, w)])
    late_out = []

    cross = {}
    for hd in range(R_HEADS):
        st = state_ref[hd]
        for ci in range(n_chunks):
            cross[ci, hd] = _dot(rq_ref[0, rows(ci), qk(hd)], st.astype(BF16))
            st = chunk_dec[hd] * st + kv[ci, hd]
        state_ref[hd] = st

    gate = rg * _sigmoid(rg)
    for n, (ci, hd) in enumerate(units):
        inner = _dot((att[ci, hd] * dmask_ref[hd]).astype(BF16), rv_ref[0, rows(ci), vv(hd)])
        y = inner + cross[ci, hd] * qdec_ref[hd]
        mu = jnp.mean(y, axis=-1, keepdims=True)
        yc = y - mu
        var = jnp.mean(yc * yc, axis=-1, keepdims=True)
        yb_ref[rows(ci), vv(hd)] = (yc * lax.rsqrt(var + EPS) * gate[rows(ci), vv(hd)]).astype(BF16)
        if n < len(late):
            late_out.append(late[n]())

    pieces = d // w
    ta, ga, gb = (jnp.concatenate(late_out[k * pieces:(k + 1) * pieces], axis=1) for k in range(3))
    tb = _dot(yb_ref[...], wb_ref[...])
    merged = _sigmoid(ga) * ta + _sigmoid(gb) * tb
    o_ref[0] = x + _dot(merged.astype(BF16), wo_ref[...])


def _mix(x, g, rq, rkt, rktd, rv, ya, dmask, qdec, chunk_dec, w, wa, wb, wo):
    b, s, d = x.shape
    t = MIX_BLOCKS * TOKEN_TILE
    tok = lambda w: pl.BlockSpec((1, t, w), lambda bi, i: (bi, i, 0))
    blk_t = pl.BlockSpec((1, MIX_BLOCKS, R_QK_WIDTH, TOKEN_TILE), lambda bi, i: (bi, i, 0, 0))
    return pl.pallas_call(
        functools.partial(_mix_kernel, chunk_dec),
        grid=(b, s // t),
        in_specs=[tok(d), _full((1, d)), tok(R_QK_WIDTH), blk_t, blk_t, tok(R_V_WIDTH), tok(A_WIDTH),
                  _full(dmask.shape), _full(qdec.shape),
                  _full(w.shape), _full(wa.shape), _full(wb.shape), _full(wo.shape)],
        out_specs=tok(d),
        out_shape=jax.ShapeDtypeStruct((b, s, d), F32),
        scratch_shapes=[pltpu.VMEM((R_HEADS, R_QK_DIM, R_V_DIM), F32),
                        pltpu.VMEM((t, R_V_WIDTH), BF16)],
        compiler_params=_params("parallel", "arbitrary"),
        name="mix",
    )(x, g, rq, rkt, rktd, rv, ya, dmask, qdec, w, wa, wb, wo)


def _cross_kernel(x_ref, g_ref, m_ref, gm_ref, wkv_ref, wq_ref, wo_ref, o_ref, kt_ref, v_ref, cat_ref):
    d = x_ref.shape[2]
    dh = d // X_HEADS

    @pl.when(pl.program_id(1) == 0)
    def _():
        mb = _rms(m_ref[0], gm_ref[...]).astype(BF16)
        kt_ref[...] = _dot(mb, wkv_ref[:, 0:d]).T.astype(BF16)
        v_ref[...] = _dot(mb, wkv_ref[:, d:]).astype(BF16)

    cols = lambda hd: slice(hd * dh, (hd + 1) * dh)
    n_parts = CROSS_PARTS
    t = x_ref.shape[1] // n_parts
    part = lambda r: slice(r * t, (r + 1) * t)

    def project(r):
        hb = _rms(x_ref[0, part(r), :], g_ref[...]).astype(BF16)
        q = (_dot(hb, wq_ref[...]) * dh ** -0.5).astype(BF16)
        return [_dot(q[:, cols(hd)], kt_ref[cols(hd), :]) for hd in range(X_HEADS)]

    nxt = project(0)
    for r in range(n_parts):
        scores = nxt
        if r + 1 < n_parts:
            nxt = project(r + 1)
        for hd, s in enumerate(scores):
            p = jnp.exp(s - jnp.max(s, axis=-1, keepdims=True))
            p = p / jnp.sum(p, axis=-1, keepdims=True)
            cat_ref[part(r), cols(hd)] = _dot(p.astype(BF16), v_ref[:, cols(hd)]).astype(BF16)
        o_ref[0, part(r), :] = x_ref[0, part(r), :] + _dot(cat_ref[part(r), :], wo_ref[...])


def _cross(x, g, mem, gm, wkv, wq, wo):
    b, s, d = x.shape
    m = mem.shape[1]
    t = ROW_TILE
    tok = pl.BlockSpec((1, t, d), lambda bi, i: (bi, i, 0))
    return pl.pallas_call(
        _cross_kernel,
        grid=(b, s // t),
        in_specs=[tok, _full((1, d)), pl.BlockSpec((1, m, d), lambda bi, i: (bi, 0, 0)), _full((1, d)),
                  _full(wkv.shape), _full(wq.shape), _full(wo.shape)],
        out_specs=tok,
        out_shape=jax.ShapeDtypeStruct((b, s, d), F32),
        scratch_shapes=[pltpu.VMEM((d, m), BF16), pltpu.VMEM((m, d), BF16),
                        pltpu.VMEM((t, d), BF16)],
        compiler_params=_params("parallel", "arbitrary"),
        name="cross",
    )(x, g, mem, gm, wkv, wq, wo)


def _ffn_kernel(x_ref, g_ref, wg_ref, wu_ref, wd_ref, *rest):
    o_ref = rest[-1]
    x = x_ref[...]
    hb = _rms(x, g_ref[...]).astype(BF16)
    n_chunks = wg_ref.shape[1] // FF_CHUNK
    cols = lambda c: slice(c * FF_CHUNK, (c + 1) * FF_CHUNK)

    def up(c):
        return _dot(hb, wg_ref[:, cols(c)]), _dot(hb, wu_ref[:, cols(c)])

    acc = x
    nxt = up(0)
    for c in range(n_chunks):
        gate, upv = nxt
        if c + 1 < n_chunks:
            nxt = up(c + 1)
        act = gate * _sigmoid(gate) * upv
        acc = acc + _dot(act.astype(BF16), wd_ref[cols(c), :])
    o_ref[...] = _rms(acc, rest[0][...]) if len(rest) == 2 else acc


def _ffn(x2, g, wg, wu, wd, final_gain=None):
    n, d = x2.shape
    t = ROW_TILE
    tok = pl.BlockSpec((t, d), lambda i: (i, 0))
    final = () if final_gain is None else (final_gain,)
    return pl.pallas_call(
        _ffn_kernel,
        grid=(n // t,),
        in_specs=[tok, _full((1, d)), _full(wg.shape), _full(wu.shape), _full(wd.shape)] + [_full((1, d))] * len(final),
        out_specs=tok,
        out_shape=jax.ShapeDtypeStruct((n, d), F32),
        compiler_params=_params("parallel"),
        name="ffn",
    )(x2, g, wg, wu, wd, *final)


def _rope_tables(s):
    def cs(dim):
        inv = ROPE_THETA ** (-jnp.arange(0, dim, 2, dtype=F32) / dim)
        ang = jnp.arange(s).astype(F32)[:, None] * inv[None, :]
        return jnp.cos(ang), jnp.sin(ang)

    ca, sa = cs(A_HEAD_DIM)
    cr, sr = cs(R_QK_DIM)
    sign = lambda half: jnp.asarray(np.tile(np.repeat([-1.0, 1.0], half), 128 // (2 * half))[None, :], F32)
    return (jnp.tile(ca, (1, 4)), jnp.tile(sa, (1, 4)) * sign(A_HEAD_DIM // 2),
            jnp.tile(cr, (1, 2)), jnp.tile(sr, (1, 2)) * sign(R_QK_DIM // 2),
            ca.T, sa.T, cr.T, sr.T)


def _decay_tables(s):
    c = R_CHUNK
    log_g = np.log(1.0 - np.exp2(-5.0 - np.arange(R_HEADS, dtype=np.float64)))
    i = np.arange(c, dtype=np.float64)
    diff = i[:, None] - i[None, :]
    dmask = np.where(diff >= 0, np.exp(log_g[:, None, None] * np.maximum(diff, 0.0)), 0.0)
    q_dec = np.exp(log_g[:, None] * (i + 1.0))
    k_dec = np.exp(log_g[:, None] * (c - 1.0 - i))
    kdec_t = np.zeros((8, s), np.float64)
    kdec_t[:R_HEADS] = np.tile(k_dec, (1, s // c))
    qdec = np.broadcast_to(q_dec[:, :, None], (R_HEADS, c, R_V_DIM))
    chunk_dec = tuple(float(v) for v in np.exp(log_g * c))
    return (jnp.asarray(dmask, F32), jnp.asarray(qdec, F32), jnp.asarray(kdec_t, F32), chunk_dec)


def kernel(x, mem, norm_mix, w_in, w_branch_a, w_branch_b, w_out, norm_cross, norm_mem,
           w_xq, w_xkv, w_xo, norm_ffn, w_gate, w_up, w_down, norm_final):
    b, s, d = x.shape
    assert s % TOKEN_TILE == 0 and s % ROW_TILE == 0 and d % 128 == 0
    rope = _rope_tables(s)
    dmask, qdec, kdec_t, chunk_dec = _decay_tables(s)
    row = lambda v: v.reshape(1, d)
    assert w_in.shape[2] == COL_GATES + R_V_WIDTH + 2 * d

    for l in range(w_in.shape[0]):
        w = w_in[l].astype(BF16)
        wqt, wvt, wkt = (w[:, c:c + A_WIDTH].T for c in (COL_AQ, COL_AV, COL_RK))
        k_a, kmean, qt_a, vt_a, rq, rv, rkt, rktd = _qkv(x, row(norm_mix[l]), w, wqt, wvt, wkt, rope + (kdec_t,))
        y_a = _moba(qt_a, k_a, vt_a, kmean.reshape(b, s // MOBA_BLOCK, A_WIDTH))
        x = _mix(x, row(norm_mix[l]), rq, rkt, rktd, rv, y_a, dmask, qdec, chunk_dec,
                 w, w_branch_a[l].astype(BF16), w_branch_b[l].astype(BF16), w_out[l].astype(BF16))
        x = _cross(x, row(norm_cross[l]), mem, row(norm_mem[l]), w_xkv[l].astype(BF16),
                   w_xq[l].astype(BF16), w_xo[l].astype(BF16))
        last = l == w_in.shape[0] - 1
        x = _ffn(x.reshape(b * s, d), row(norm_ffn[l]), w_gate[l].astype(BF16), w_up[l].astype(BF16),
                 w_down[l].astype(BF16), row(norm_final) if last else None).reshape(b, s, d)
    return x
```

```python
import functools
import math

import numpy as np
import jax
import jax.numpy as jnp
from jax import lax
from jax.experimental import pallas as pl
from jax.experimental.pallas import tpu as pltpu

F32 = jnp.float32
BF16 = jnp.bfloat16

EPS = 1e-6
ROPE_THETA = 10000.0
A_HEADS = 8
A_HEAD_DIM = 64
A_WIDTH = A_HEADS * A_HEAD_DIM
MOBA_BLOCK = 256
MOBA_TOPK = 3
R_HEADS = 4
R_QK_DIM = 128
R_V_DIM = 256
R_QK_WIDTH = R_HEADS * R_QK_DIM
R_V_WIDTH = R_HEADS * R_V_DIM
R_CHUNK = 128
X_HEADS = 4
COL_AQ, COL_AK, COL_AV = 0, A_WIDTH, 2 * A_WIDTH
COL_RQ = 3 * A_WIDTH
COL_RK = COL_RQ + R_QK_WIDTH
COL_RV = COL_RK + R_QK_WIDTH
COL_GATES = COL_RV + R_V_WIDTH
FF_CHUNK = 256
TOKEN_TILE = MOBA_BLOCK
ROW_TILE = 1024
CROSS_PARTS = 4
QKV_BLOCKS = 2
MIX_LATE_COLS = 256
MIX_BLOCKS = 2
NEG = -1e30
LOG2_E = math.log2(math.e)
MOBA_LOOKAHEAD = 7
MOBA_ROWS = 2
MOBA_UNROLL = 4
MOBA_ONES_ROWS = 16
LANES = 128
VMEM_LIMIT = 56 * 1024 * 1024
assert 2 * A_HEAD_DIM == LANES and R_QK_DIM == LANES


def _dot(a, b):
    return jnp.dot(a, b, preferred_element_type=F32)


def _rms(x, g):
    return x * lax.rsqrt(jnp.mean(x * x, axis=-1, keepdims=True) + EPS) * g


def _sigmoid(x):
    return 1.0 / (1.0 + jnp.exp(-x))


def _params(*sem):
    return pltpu.CompilerParams(dimension_semantics=sem, vmem_limit_bytes=VMEM_LIMIT)


def _full(shape):
    n = len(shape)
    return pl.BlockSpec(shape, lambda *_: (0,) * n, pipeline_mode=pl.Buffered(1))


def _qkv_kernel(*refs):
    for sb in range(QKV_BLOCKS):
        _qkv_block(sb, *refs)


def _qkv_block(sb, x_ref, g_ref, w_ref, wqt_ref, wvt_ref, wkt_ref, cosa_ref, sina_ref, cosr_ref, sinr_ref,
               costa_ref, sinta_ref, costr_ref, sintr_ref, kdec_ref,
               k_ref, kmean_ref, qt_ref, vt_ref, rq_ref, rv_ref, rkt_ref, rktd_ref):
    t = TOKEN_TILE
    tok = slice(sb * t, (sb + 1) * t)
    h = _rms(x_ref[0, tok, :], g_ref[...])
    hb = h.astype(BF16)
    ht = h.T.astype(BF16)
    lane = lax.broadcasted_iota(jnp.int32, (t, LANES), 1)
    a_half, r_half = A_HEAD_DIM // 2, R_QK_DIM // 2
    first_half = (lane & a_half) == 0

    ak = _dot(hb, w_ref[:, COL_AK:COL_AV])
    cosa, sina = cosa_ref[tok, :], sina_ref[tok, :]
    for c in range(A_WIDTH // LANES):
        blk = ak[:, c * LANES:(c + 1) * LANES]
        rot = jnp.where(first_half, pltpu.roll(blk, LANES - a_half, 1), pltpu.roll(blk, a_half, 1))
        kr = blk * cosa + rot * sina
        k_ref[0, sb, :, c * LANES:(c + 1) * LANES] = kr.astype(BF16)
        kmean_ref[0, sb, :, c * LANES:(c + 1) * LANES] = jnp.mean(kr, axis=0, keepdims=True)

    rq = _dot(hb, w_ref[:, COL_RQ:COL_RK])
    cosr, sinr = cosr_ref[tok, :], sinr_ref[tok, :]
    for c in range(R_HEADS):
        blk = rq[:, c * R_QK_DIM:(c + 1) * R_QK_DIM]
        rq_ref[0, tok, c * R_QK_DIM:(c + 1) * R_QK_DIM] = (blk * cosr + pltpu.roll(blk, r_half, 1) * sinr).astype(BF16)

    rv_ref[0, tok, :] = _dot(hb, w_ref[:, COL_RV:COL_GATES]).astype(BF16)

    qt = _dot(wqt_ref[...], ht)
    cost, sint = costa_ref[:, tok], sinta_ref[:, tok]
    half = A_HEAD_DIM // 2
    scale_a = A_HEAD_DIM ** -0.5 * LOG2_E
    for hd in range(A_HEADS):
        x1 = qt[hd * A_HEAD_DIM:hd * A_HEAD_DIM + half]
        x2 = qt[hd * A_HEAD_DIM + half:(hd + 1) * A_HEAD_DIM]
        qt_ref[0, sb, hd * A_HEAD_DIM:hd * A_HEAD_DIM + half, :] = ((x1 * cost - x2 * sint) * scale_a).astype(BF16)
        qt_ref[0, sb, hd * A_HEAD_DIM + half:(hd + 1) * A_HEAD_DIM, :] = ((x2 * cost + x1 * sint) * scale_a).astype(BF16)

    vt_ref[0, sb] = _dot(wvt_ref[...], ht).astype(BF16)

    rkt = _dot(wkt_ref[...], ht)
    cost, sint = costr_ref[:, tok], sintr_ref[:, tok]
    half = R_QK_DIM // 2
    scale_r = R_QK_DIM ** -0.5
    for hd in range(R_HEADS):
        x1 = rkt[hd * R_QK_DIM:hd * R_QK_DIM + half]
        x2 = rkt[hd * R_QK_DIM + half:(hd + 1) * R_QK_DIM]
        o1 = (x1 * cost - x2 * sint) * scale_r
        o2 = (x2 * cost + x1 * sint) * scale_r
        dec = kdec_ref[hd:hd + 1, tok]
        rkt_ref[0, sb, hd * R_QK_DIM:hd * R_QK_DIM + half, :] = o1.astype(BF16)
        rkt_ref[0, sb, hd * R_QK_DIM + half:(hd + 1) * R_QK_DIM, :] = o2.astype(BF16)
        rktd_ref[0, sb, hd * R_QK_DIM:hd * R_QK_DIM + half, :] = (o1 * dec).astype(BF16)
        rktd_ref[0, sb, hd * R_QK_DIM + half:(hd + 1) * R_QK_DIM, :] = (o2 * dec).astype(BF16)


def _qkv(x, g, w, wqt, wvt, wkt, tabs):
    b, s, d = x.shape
    nb = s // TOKEN_TILE
    qb = QKV_BLOCKS
    t = qb * TOKEN_TILE
    tok = lambda w: pl.BlockSpec((1, t, w), lambda bi, i: (bi, i, 0))
    blk_t = lambda r: pl.BlockSpec((1, qb, r, TOKEN_TILE), lambda bi, i: (bi, i, 0, 0))
    nat_tab = pl.BlockSpec((t, LANES), lambda bi, i: (i, 0))
    tr_tab = lambda r: pl.BlockSpec((r, t), lambda bi, i: (0, i))
    return pl.pallas_call(
        _qkv_kernel,
        grid=(b, nb // qb),
        in_specs=[tok(d), _full((1, d)), _full(w.shape), _full(wqt.shape), _full(wvt.shape), _full(wkt.shape),
                  nat_tab, nat_tab, nat_tab, nat_tab,
                  tr_tab(A_HEAD_DIM // 2), tr_tab(A_HEAD_DIM // 2),
                  tr_tab(R_QK_DIM // 2), tr_tab(R_QK_DIM // 2), tr_tab(8)],
        out_specs=[pl.BlockSpec((1, qb, TOKEN_TILE, A_WIDTH), lambda bi, i: (bi, i, 0, 0)),
                   pl.BlockSpec((1, qb, 1, A_WIDTH), lambda bi, i: (bi, i, 0, 0)),
                   blk_t(A_WIDTH), blk_t(A_WIDTH), tok(R_QK_WIDTH), tok(R_V_WIDTH),
                   blk_t(R_QK_WIDTH), blk_t(R_QK_WIDTH)],
        out_shape=[jax.ShapeDtypeStruct((b, nb, TOKEN_TILE, A_WIDTH), BF16),
                   jax.ShapeDtypeStruct((b, nb, 1, A_WIDTH), F32),
                   jax.ShapeDtypeStruct((b, nb, A_WIDTH, TOKEN_TILE), BF16),
                   jax.ShapeDtypeStruct((b, nb, A_WIDTH, TOKEN_TILE), BF16),
                   jax.ShapeDtypeStruct((b, s, R_QK_WIDTH), BF16),
                   jax.ShapeDtypeStruct((b, s, R_V_WIDTH), BF16),
                   jax.ShapeDtypeStruct((b, nb, R_QK_WIDTH, TOKEN_TILE), BF16),
                   jax.ShapeDtypeStruct((b, nb, R_QK_WIDTH, TOKEN_TILE), BF16)],
        compiler_params=_params("parallel", "parallel"),
        name="qkv",
    )(x, g, w, wqt, wvt, wkt, *tabs)


def _moba_kernel(qt_ref, k_ref, vt_ref, km_ref, o_ref, qm_ref, m_ref, acc_ref, ot_ref, s_ref):
    i = pl.program_id(1)
    nb = km_ref.shape[1]
    t = MOBA_BLOCK
    dh = A_HEAD_DIM
    row = lax.broadcasted_iota(jnp.int32, (2 * dh, t), 0)
    blk = lax.broadcasted_iota(jnp.int32, (nb, t), 0)
    past = blk < i
    causal = (lax.broadcasted_iota(jnp.int32, (t, t), 0) <= lax.broadcasted_iota(jnp.int32, (t, t), 1))
    ones = jnp.ones((MOBA_ONES_ROWS, t), BF16)
    lanes = lambda hd: slice((hd // 2) * LANES, (hd // 2 + 1) * LANES)
    rows = lambda hd: slice(hd * dh, (hd + 1) * dh)
    row_heads = [(b, hd) for b in range(MOBA_ROWS) for hd in range(A_HEADS)]

    def select_blocks(b, hd):
        hh = hd % 2
        qt_pair = qt_ref[b, 0, lanes(hd), :]
        mine = (row >= hh * dh) & (row < (hh + 1) * dh)
        qt = jnp.where(mine, qt_pair, jnp.zeros_like(qt_pair))
        qm_ref[b, hd, 0:2 * dh, :] = qt

        km = km_ref[b, :, lanes(hd)]
        km_hi = km.astype(BF16)
        km_lo = (km - km_hi.astype(F32)).astype(BF16)
        bs = jnp.where(past, _dot(km_hi, qt) + _dot(km_lo, qt), -jnp.inf)
        picked = jnp.zeros((nb, t), jnp.bool_)
        for _ in range(MOBA_TOPK):
            best = jnp.max(bs, axis=0, keepdims=True)
            first = jnp.min(jnp.where(bs == best, blk, nb), axis=0, keepdims=True)
            hit = blk == first
            picked = picked | hit
            bs = jnp.where(hit, -jnp.inf, bs)
        bias = jnp.where(past & picked, 0.0, NEG).astype(BF16)
        qm_ref[b, hd, 2 * dh:, :] = jnp.concatenate([bias, jnp.zeros((2 * dh - nb, t), BF16)], axis=0)

    key_lane = lax.broadcasted_iota(jnp.int32, (t, 2 * dh), 1)

    def keys_with_block_column(b, j):
        onehot = jnp.where(key_lane == j, 1.0, 0.0).astype(BF16)
        return [jnp.concatenate([k_ref[b, j, :, lanes(2 * p)], onehot], axis=1) for p in range(A_HEADS // 2)]

    def pv(b, j, hd, p):
        vt_aug = jnp.concatenate([vt_ref[b, j, rows(hd), :], ones], axis=0)
        return _dot(vt_aug, p.astype(BF16))

    def own_unit(b, hd):
        def scores():
            return jnp.where(causal, _dot(k_ref[b, i, :, lanes(hd)], qm_ref[b, hd, 0:2 * dh, :]), NEG)

        def update(s, m0):
            m_ref[b, hd] = jnp.broadcast_to(m0, (8, t))
            acc_ref[b, hd] = pv(b, i, hd, jnp.exp2(s - m0))

        return scores, update

    def past_units(b, j):
        keys = keys_with_block_column(b, j)

        def unit(hd):
            def scores():
                return _dot(keys[hd // 2], qm_ref[b, hd])

            def update(s, m_blk):
                m_old = m_ref[b, hd]
                m_new = jnp.maximum(m_old, m_blk)
                alpha = jnp.exp2(m_old - m_new)
                m_ref[b, hd] = m_new
                acc_ref[b, hd] = alpha[0:1, :] * acc_ref[b, hd] + pv(b, j, hd, jnp.exp2(s - m_new[0:1, :]))

            return scores, update

        return [unit(hd) for hd in range(A_HEADS)]

    def pipelined(units):
        blk_max = {}
        slots = MOBA_LOOKAHEAD + 1
        for step in range(len(units) + MOBA_LOOKAHEAD):
            if step < len(units):
                sc = units[step][0]()
                s_ref[step % slots] = sc
                blk_max[step] = jnp.max(sc, axis=0, keepdims=True)
            if step >= MOBA_LOOKAHEAD:
                u = step - MOBA_LOOKAHEAD
                units[u][1](s_ref[u % slots], blk_max.pop(u))

    n_groups = i // MOBA_UNROLL
    for left in range(MOBA_UNROLL):
        @pl.when(i % MOBA_UNROLL == left)
        def _(left=left):
            for b, hd in row_heads:
                select_blocks(b, hd)
            units = [own_unit(b, hd) for b, hd in row_heads]
            for r in range(left):
                for b in range(MOBA_ROWS):
                    units += past_units(b, n_groups * MOBA_UNROLL + r)
            pipelined(units)

    def group_body(jj, carry):
        pipelined([u for r in range(MOBA_UNROLL) for b in range(MOBA_ROWS)
                   for u in past_units(b, MOBA_UNROLL * jj + r)])
        return carry

    lax.fori_loop(0, n_groups, group_body, 0)

    for b in range(MOBA_ROWS):
        for hd in range(A_HEADS):
            ot_ref[rows(hd), :] = acc_ref[b, hd, 0:dh, :] / acc_ref[b, hd, dh:dh + 1, :]
        o_ref[b] = ot_ref[...].T.astype(BF16)


def _moba(qt, k, vt, kmean):
    b, nb, t, w = k.shape
    r = MOBA_ROWS
    assert b % r == 0
    return pl.pallas_call(
        _moba_kernel,
        grid=(b // r, nb),
        in_specs=[pl.BlockSpec((r, 1, w, t), lambda bi, i: (bi, i, 0, 0)),
                  pl.BlockSpec((r, nb, t, w), lambda bi, i: (bi, 0, 0, 0)),
                  pl.BlockSpec((r, nb, w, t), lambda bi, i: (bi, 0, 0, 0)),
                  pl.BlockSpec((r, nb, w), lambda bi, i: (bi, 0, 0))],
        out_specs=pl.BlockSpec((r, t, w), lambda bi, i: (bi, i, 0)),
        out_shape=jax.ShapeDtypeStruct((b, nb * t, w), BF16),
        scratch_shapes=[pltpu.VMEM((r, A_HEADS, 4 * A_HEAD_DIM, t), BF16),
                        pltpu.VMEM((r, A_HEADS, 8, t), F32),
                        pltpu.VMEM((r, A_HEADS, A_HEAD_DIM + MOBA_ONES_ROWS, t), F32),
                        pltpu.VMEM((w, t), F32),
                        pltpu.VMEM((MOBA_LOOKAHEAD + 1, t, t), F32)],
        compiler_params=_params("parallel", "parallel"),
        name="moba",
    )(qt, k, vt, kmean)


def _mix_kernel(chunk_dec, x_ref, g_ref, rq_ref, rkt_ref, rktd_ref, rv_ref, ya_ref, dmask_ref, qdec_ref,
                w_ref, wa_ref, wb_ref, wo_ref, o_ref, state_ref, yb_ref):
    @pl.when(pl.program_id(1) == 0)
    def _():
        state_ref[...] = jnp.zeros_like(state_ref)

    x = x_ref[0]
    d = x.shape[1]
    hb = _rms(x, g_ref[...]).astype(BF16)
    c = R_CHUNK
    per_blk = TOKEN_TILE // c
    n_chunks = MIX_BLOCKS * per_blk
    rows = lambda ci: slice(ci * c, (ci + 1) * c)
    qk = lambda hd: slice(hd * R_QK_DIM, (hd + 1) * R_QK_DIM)
    vv = lambda hd: slice(hd * R_V_DIM, (hd + 1) * R_V_DIM)
    units = [(ci, hd) for ci in range(n_chunks) for hd in range(R_HEADS)]
    keys_t = lambda ref, ci, hd: ref[0, ci // per_blk, qk(hd), rows(ci % per_blk)]

    att = {u: _dot(rq_ref[0, rows(u[0]), qk(u[1])], keys_t(rkt_ref, *u)) for u in units}
    kv = {u: _dot(keys_t(rktd_ref, *u), rv_ref[0, rows(u[0]), vv(u[1])]) for u in units}
    rg = _dot(hb, w_ref[:, COL_GATES:COL_GATES + R_V_WIDTH])
    w = MIX_LATE_COLS
    late = ([lambda c=c: _dot(ya_ref[0], wa_ref[:, c:c + w]) for c in range(0, d, w)]
            + [lambda c=c: _dot(hb, w_ref[:, c:c + w])
               for c in range(COL_GATES + R_V_WIDTH, COL_GATES + R_V_WIDTH + 2 * d, w)])
    late_out = []

    cross = {}
    for hd in range(R_HEADS):
        st = state_ref[hd]
        for ci in range(n_chunks):
            cross[ci, hd] = _dot(rq_ref[0, rows(ci), qk(hd)], st.astype(BF16))
            st = chunk_dec[hd] * st + kv[ci, hd]
        state_ref[hd] = st

    gate = rg * _sigmoid(rg)
    for n, (ci, hd) in enumerate(units):
        inner = _dot((att[ci, hd] * dmask_ref[hd]).astype(BF16), rv_ref[0, rows(ci), vv(hd)])
        y = inner + cross[ci, hd] * qdec_ref[hd]
        mu = jnp.mean(y, axis=-1, keepdims=True)
        yc = y - mu
        var = jnp.mean(yc * yc, axis=-1, keepdims=True)
        yb_ref[rows(ci), vv(hd)] = (yc * lax.rsqrt(var + EPS) * gate[rows(ci), vv(hd)]).astype(BF16)
        if n < len(late):
            late_out.append(late[n]())

    pieces = d // w
    ta, ga, gb = (jnp.concatenate(late_out[k * pieces:(k + 1) * pieces], axis=1) for k in range(3))
    tb = _dot(yb_ref[...], wb_ref[...])
    merged = _sigmoid(ga) * ta + _sigmoid(gb) * tb
    o_ref[0] = x + _dot(merged.astype(BF16), wo_ref[...])


def _mix(x, g, rq, rkt, rktd, rv, ya, dmask, qdec, chunk_dec, w, wa, wb, wo):
    b, s, d = x.shape
    t = MIX_BLOCKS * TOKEN_TILE
    tok = lambda w: pl.BlockSpec((1, t, w), lambda bi, i: (bi, i, 0))
    blk_t = pl.BlockSpec((1, MIX_BLOCKS, R_QK_WIDTH, TOKEN_TILE), lambda bi, i: (bi, i, 0, 0))
    return pl.pallas_call(
        functools.partial(_mix_kernel, chunk_dec),
        grid=(b, s // t),
        in_specs=[tok(d), _full((1, d)), tok(R_QK_WIDTH), blk_t, blk_t, tok(R_V_WIDTH), tok(A_WIDTH),
                  _full(dmask.shape), _full(qdec.shape),
                  _full(w.shape), _full(wa.shape), _full(wb.shape), _full(wo.shape)],
        out_specs=tok(d),
        out_shape=jax.ShapeDtypeStruct((b, s, d), F32),
        scratch_shapes=[pltpu.VMEM((R_HEADS, R_QK_DIM, R_V_DIM), F32),
                        pltpu.VMEM((t, R_V_WIDTH), BF16)],
        compiler_params=_params("parallel", "arbitrary"),
        name="mix",
    )(x, g, rq, rkt, rktd, rv, ya, dmask, qdec, w, wa, wb, wo)


def _cross_kernel(x_ref, g_ref, m_ref, gm_ref, wkv_ref, wq_ref, wo_ref, o_ref, kt_ref, v_ref, cat_ref):
    d = x_ref.shape[2]
    dh = d // X_HEADS

    @pl.when(pl.program_id(1) == 0)
    def _():
        mb = _rms(m_ref[0], gm_ref[...]).astype(BF16)
        kt_ref[...] = _dot(mb, wkv_ref[:, 0:d]).T.astype(BF16)
        v_ref[...] = _dot(mb, wkv_ref[:, d:]).astype(BF16)

    cols = lambda hd: slice(hd * dh, (hd + 1) * dh)
    n_parts = CROSS_PARTS
    t = x_ref.shape[1] // n_parts
    part = lambda r: slice(r * t, (r + 1) * t)

    def project(r):
        hb = _rms(x_ref[0, part(r), :], g_ref[...]).astype(BF16)
        q = (_dot(hb, wq_ref[...]) * dh ** -0.5).astype(BF16)
        return [_dot(q[:, cols(hd)], kt_ref[cols(hd), :]) for hd in range(X_HEADS)]

    nxt = project(0)
    for r in range(n_parts):
        scores = nxt
        if r + 1 < n_parts:
            nxt = project(r + 1)
        for hd, s in enumerate(scores):
            p = jnp.exp(s - jnp.max(s, axis=-1, keepdims=True))
            p = p / jnp.sum(p, axis=-1, keepdims=True)
            cat_ref[part(r), cols(hd)] = _dot(p.astype(BF16), v_ref[:, cols(hd)]).astype(BF16)
        o_ref[0, part(r), :] = x_ref[0, part(r), :] + _dot(cat_ref[part(r), :], wo_ref[...])


def _cross(x, g, mem, gm, wkv, wq, wo):
    b, s, d = x.shape
    m = mem.shape[1]
    t = ROW_TILE
    tok = pl.BlockSpec((1, t, d), lambda bi, i: (bi, i, 0))
    return pl.pallas_call(
        _cross_kernel,
        grid=(b, s // t),
        in_specs=[tok, _full((1, d)), pl.BlockSpec((1, m, d), lambda bi, i: (bi, 0, 0)), _full((1, d)),
                  _full(wkv.shape), _full(wq.shape), _full(wo.shape)],
        out_specs=tok,
        out_shape=jax.ShapeDtypeStruct((b, s, d), F32),
        scratch_shapes=[pltpu.VMEM((d, m), BF16), pltpu.VMEM((m, d), BF16),
                        pltpu.VMEM((t, d), BF16)],
        compiler_params=_params("parallel", "arbitrary"),
        name="cross",
    )(x, g, mem, gm, wkv, wq, wo)


def _ffn_kernel(x_ref, g_ref, wg_ref, wu_ref, wd_ref, *rest):
    o_ref = rest[-1]
    x = x_ref[...]
    hb = _rms(x, g_ref[...]).astype(BF16)
    n_chunks = wg_ref.shape[1] // FF_CHUNK
    cols = lambda c: slice(c * FF_CHUNK, (c + 1) * FF_CHUNK)

    def up(c):
        return _dot(hb, wg_ref[:, cols(c)]), _dot(hb, wu_ref[:, cols(c)])

    acc = x
    nxt = up(0)
    for c in range(n_chunks):
        gate, upv = nxt
        if c + 1 < n_chunks:
            nxt = up(c + 1)
        act = gate * _sigmoid(gate) * upv
        acc = acc + _dot(act.astype(BF16), wd_ref[cols(c), :])
    o_ref[...] = _rms(acc, rest[0][...]) if len(rest) == 2 else acc


def _ffn(x2, g, wg, wu, wd, final_gain=None):
    n, d = x2.shape
    t = ROW_TILE
    tok = pl.BlockSpec((t, d), lambda i: (i, 0))
    final = () if final_gain is None else (final_gain,)
    return pl.pallas_call(
        _ffn_kernel,
        grid=(n // t,),
        in_specs=[tok, _full((1, d)), _full(wg.shape), _full(wu.shape), _full(wd.shape)] + [_full((1, d))] * len(final),
        out_specs=tok,
        out_shape=jax.ShapeDtypeStruct((n, d), F32),
        compiler_params=_params("parallel"),
        name="ffn",
    )(x2, g, wg, wu, wd, *final)


def _rope_tables(s):
    def cs(dim):
        inv = ROPE_THETA ** (-jnp.arange(0, dim, 2, dtype=F32) / dim)
        ang = jnp.arange(s).astype(F32)[:, None] * inv[None, :]
        return jnp.cos(ang), jnp.sin(ang)

    ca, sa = cs(A_HEAD_DIM)
    cr, sr = cs(R_QK_DIM)
    sign = lambda half: jnp.asarray(np.tile(np.repeat([-1.0, 1.0], half), LANES // (2 * half))[None, :], F32)
    return (jnp.tile(ca, (1, 4)), jnp.tile(sa, (1, 4)) * sign(A_HEAD_DIM // 2),
            jnp.tile(cr, (1, 2)), jnp.tile(sr, (1, 2)) * sign(R_QK_DIM // 2),
            ca.T, sa.T, cr.T, sr.T)


def _decay_tables(s):
    c = R_CHUNK
    log_g = np.log(1.0 - np.exp2(-5.0 - np.arange(R_HEADS, dtype=np.float64)))
    i = np.arange(c, dtype=np.float64)
    diff = i[:, None] - i[None, :]
    dmask = np.where(diff >= 0, np.exp(log_g[:, None, None] * np.maximum(diff, 0.0)), 0.0)
    q_dec = np.exp(log_g[:, None] * (i + 1.0))
    k_dec = np.exp(log_g[:, None] * (c - 1.0 - i))
    kdec_t = np.zeros((8, s), np.float64)
    kdec_t[:R_HEADS] = np.tile(k_dec, (1, s // c))
    qdec = np.broadcast_to(q_dec[:, :, None], (R_HEADS, c, R_V_DIM))
    chunk_dec = tuple(float(v) for v in np.exp(log_g * c))
    return (jnp.asarray(dmask, F32), jnp.asarray(qdec, F32), jnp.asarray(kdec_t, F32), chunk_dec)


def kernel(x, mem, norm_mix, w_in, w_branch_a, w_branch_b, w_out, norm_cross, norm_mem,
           w_xq, w_xkv, w_xo, norm_ffn, w_gate, w_up, w_down, norm_final):
    b, s, d = x.shape
    assert s % TOKEN_TILE == 0 and s % ROW_TILE == 0 and d % LANES == 0
    rope = _rope_tables(s)
    dmask, qdec, kdec_t, chunk_dec = _decay_tables(s)
    row = lambda v: v.reshape(1, d)
    assert w_in.shape[2] == COL_GATES + R_V_WIDTH + 2 * d

    for l in range(w_in.shape[0]):
        w = w_in[l].astype(BF16)
        wqt, wvt, wkt = (w[:, c:c + A_WIDTH].T for c in (COL_AQ, COL_AV, COL_RK))
        k_a, kmean, qt_a, vt_a, rq, rv, rkt, rktd = _qkv(x, row(norm_mix[l]), w, wqt, wvt, wkt, rope + (kdec_t,))
        y_a = _moba(qt_a, k_a, vt_a, kmean.reshape(b, s // MOBA_BLOCK, A_WIDTH))
        x = _mix(x, row(norm_mix[l]), rq, rkt, rktd, rv, y_a, dmask, qdec, chunk_dec,
                 w, w_branch_a[l].astype(BF16), w_branch_b[l].astype(BF16), w_out[l].astype(BF16))
        x = _cross(x, row(norm_cross[l]), mem, row(norm_mem[l]), w_xkv[l].astype(BF16),
                   w_xq[l].astype(BF16), w_xo[l].astype(BF16))
        last = l == w_in.shape[0] - 1
        x = _ffn(x.reshape(b * s, d), row(norm_ffn[l]), w_gate[l].astype(BF16), w_up[l].astype(BF16),
                 w_down[l].astype(BF16), row(norm_final) if last else None).reshape(b, s, d)
    return x
```

```python
import functools
import math

import numpy as np
import jax
import jax.numpy as jnp
from jax import lax
from jax.experimental import pallas as pl
from jax.experimental.pallas import tpu as pltpu

F32 = jnp.float32
BF16 = jnp.bfloat16

EPS = 1e-6
ROPE_THETA = 10000.0
A_HEADS = 8
A_HEAD_DIM = 64
A_WIDTH = A_HEADS * A_HEAD_DIM
MOBA_BLOCK = 256
MOBA_TOPK = 3
R_HEADS = 4
R_QK_DIM = 128
R_V_DIM = 256
R_QK_WIDTH = R_HEADS * R_QK_DIM
R_V_WIDTH = R_HEADS * R_V_DIM
R_CHUNK = 256
X_HEADS = 4
COL_AQ, COL_AK, COL_AV = 0, A_WIDTH, 2 * A_WIDTH
COL_RQ = 3 * A_WIDTH
COL_RK = COL_RQ + R_QK_WIDTH
COL_RV = COL_RK + R_QK_WIDTH
COL_GATES = COL_RV + R_V_WIDTH
FF_CHUNK = 256
TOKEN_TILE = MOBA_BLOCK
ROW_TILE = 1024
CROSS_PARTS = 4
QKV_BLOCKS = 2
MIX_LATE_COLS = 256
MIX_BLOCKS = 2
NEG = -1e30
LOG2_E = math.log2(math.e)
MOBA_LOOKAHEAD = 7
MOBA_ROWS = 2
MOBA_UNROLL = 4
MOBA_ONES_ROWS = 16
LANES = 128
VMEM_LIMIT = 56 * 1024 * 1024
assert 2 * A_HEAD_DIM == LANES and R_QK_DIM == LANES


def _dot(a, b):
    return jnp.dot(a, b, preferred_element_type=F32)


def _rms(x, g):
    return x * lax.rsqrt(jnp.mean(x * x, axis=-1, keepdims=True) + EPS) * g


def _sigmoid(x):
    return 1.0 / (1.0 + jnp.exp(-x))


def _params(*sem):
    return pltpu.CompilerParams(dimension_semantics=sem, vmem_limit_bytes=VMEM_LIMIT)


def _full(shape):
    n = len(shape)
    return pl.BlockSpec(shape, lambda *_: (0,) * n, pipeline_mode=pl.Buffered(1))


def _qkv_kernel(*refs):
    for sb in range(QKV_BLOCKS):
        _qkv_block(sb, *refs)


def _qkv_block(sb, x_ref, g_ref, w_ref, wqt_ref, wvt_ref, wkt_ref, cosa_ref, sina_ref, cosr_ref, sinr_ref,
               costa_ref, sinta_ref, costr_ref, sintr_ref, kdec_ref,
               k_ref, kmean_ref, qt_ref, vt_ref, rq_ref, rv_ref, rkt_ref, rktd_ref):
    t = TOKEN_TILE
    tok = slice(sb * t, (sb + 1) * t)
    h = _rms(x_ref[0, tok, :], g_ref[...])
    hb = h.astype(BF16)
    ht = h.T.astype(BF16)
    lane = lax.broadcasted_iota(jnp.int32, (t, LANES), 1)
    a_half, r_half = A_HEAD_DIM // 2, R_QK_DIM // 2
    first_half = (lane & a_half) == 0

    ak = _dot(hb, w_ref[:, COL_AK:COL_AV])
    cosa, sina = cosa_ref[tok, :], sina_ref[tok, :]
    for c in range(A_WIDTH // LANES):
        blk = ak[:, c * LANES:(c + 1) * LANES]
        rot = jnp.where(first_half, pltpu.roll(blk, LANES - a_half, 1), pltpu.roll(blk, a_half, 1))
        kr = blk * cosa + rot * sina
        k_ref[0, sb, :, c * LANES:(c + 1) * LANES] = kr.astype(BF16)
        kmean_ref[0, sb, :, c * LANES:(c + 1) * LANES] = jnp.mean(kr, axis=0, keepdims=True)

    rq = _dot(hb, w_ref[:, COL_RQ:COL_RK])
    cosr, sinr = cosr_ref[tok, :], sinr_ref[tok, :]
    for c in range(R_HEADS):
        blk = rq[:, c * R_QK_DIM:(c + 1) * R_QK_DIM]
        rq_ref[0, tok, c * R_QK_DIM:(c + 1) * R_QK_DIM] = (blk * cosr + pltpu.roll(blk, r_half, 1) * sinr).astype(BF16)

    rv_ref[0, tok, :] = _dot(hb, w_ref[:, COL_RV:COL_GATES]).astype(BF16)

    qt = _dot(wqt_ref[...], ht)
    cost, sint = costa_ref[:, tok], sinta_ref[:, tok]
    half = A_HEAD_DIM // 2
    scale_a = A_HEAD_DIM ** -0.5 * LOG2_E
    for hd in range(A_HEADS):
        x1 = qt[hd * A_HEAD_DIM:hd * A_HEAD_DIM + half]
        x2 = qt[hd * A_HEAD_DIM + half:(hd + 1) * A_HEAD_DIM]
        qt_ref[0, sb, hd * A_HEAD_DIM:hd * A_HEAD_DIM + half, :] = ((x1 * cost - x2 * sint) * scale_a).astype(BF16)
        qt_ref[0, sb, hd * A_HEAD_DIM + half:(hd + 1) * A_HEAD_DIM, :] = ((x2 * cost + x1 * sint) * scale_a).astype(BF16)

    vt_ref[0, sb] = _dot(wvt_ref[...], ht).astype(BF16)

    rkt = _dot(wkt_ref[...], ht)
    cost, sint = costr_ref[:, tok], sintr_ref[:, tok]
    half = R_QK_DIM // 2
    scale_r = R_QK_DIM ** -0.5
    for hd in range(R_HEADS):
        x1 = rkt[hd * R_QK_DIM:hd * R_QK_DIM + half]
        x2 = rkt[hd * R_QK_DIM + half:(hd + 1) * R_QK_DIM]
        o1 = (x1 * cost - x2 * sint) * scale_r
        o2 = (x2 * cost + x1 * sint) * scale_r
        dec = kdec_ref[hd:hd + 1, tok]
        rkt_ref[0, sb, hd * R_QK_DIM:hd * R_QK_DIM + half, :] = o1.astype(BF16)
        rkt_ref[0, sb, hd * R_QK_DIM + half:(hd + 1) * R_QK_DIM, :] = o2.astype(BF16)
        rktd_ref[0, sb, hd * R_QK_DIM:hd * R_QK_DIM + half, :] = (o1 * dec).astype(BF16)
        rktd_ref[0, sb, hd * R_QK_DIM + half:(hd + 1) * R_QK_DIM, :] = (o2 * dec).astype(BF16)


def _qkv(x, g, w, wqt, wvt, wkt, tabs):
    b, s, d = x.shape
    nb = s // TOKEN_TILE
    qb = QKV_BLOCKS
    t = qb * TOKEN_TILE
    tok = lambda w: pl.BlockSpec((1, t, w), lambda bi, i: (bi, i, 0))
    blk_t = lambda r: pl.BlockSpec((1, qb, r, TOKEN_TILE), lambda bi, i: (bi, i, 0, 0))
    nat_tab = pl.BlockSpec((t, LANES), lambda bi, i: (i, 0))
    tr_tab = lambda r: pl.BlockSpec((r, t), lambda bi, i: (0, i))
    return pl.pallas_call(
        _qkv_kernel,
        grid=(b, nb // qb),
        in_specs=[tok(d), _full((1, d)), _full(w.shape), _full(wqt.shape), _full(wvt.shape), _full(wkt.shape),
                  nat_tab, nat_tab, nat_tab, nat_tab,
                  tr_tab(A_HEAD_DIM // 2), tr_tab(A_HEAD_DIM // 2),
                  tr_tab(R_QK_DIM // 2), tr_tab(R_QK_DIM // 2), tr_tab(8)],
        out_specs=[pl.BlockSpec((1, qb, TOKEN_TILE, A_WIDTH), lambda bi, i: (bi, i, 0, 0)),
                   pl.BlockSpec((1, qb, 1, A_WIDTH), lambda bi, i: (bi, i, 0, 0)),
                   blk_t(A_WIDTH), blk_t(A_WIDTH), tok(R_QK_WIDTH), tok(R_V_WIDTH),
                   blk_t(R_QK_WIDTH), blk_t(R_QK_WIDTH)],
        out_shape=[jax.ShapeDtypeStruct((b, nb, TOKEN_TILE, A_WIDTH), BF16),
                   jax.ShapeDtypeStruct((b, nb, 1, A_WIDTH), F32),
                   jax.ShapeDtypeStruct((b, nb, A_WIDTH, TOKEN_TILE), BF16),
                   jax.ShapeDtypeStruct((b, nb, A_WIDTH, TOKEN_TILE), BF16),
                   jax.ShapeDtypeStruct((b, s, R_QK_WIDTH), BF16),
                   jax.ShapeDtypeStruct((b, s, R_V_WIDTH), BF16),
                   jax.ShapeDtypeStruct((b, nb, R_QK_WIDTH, TOKEN_TILE), BF16),
                   jax.ShapeDtypeStruct((b, nb, R_QK_WIDTH, TOKEN_TILE), BF16)],
        compiler_params=_params("parallel", "parallel"),
        name="qkv",
    )(x, g, w, wqt, wvt, wkt, *tabs)


def _moba_kernel(qt_ref, k_ref, vt_ref, km_ref, o_ref, qm_ref, m_ref, acc_ref, ot_ref, s_ref):
    i = pl.program_id(1)
    nb = km_ref.shape[1]
    t = MOBA_BLOCK
    dh = A_HEAD_DIM
    row = lax.broadcasted_iota(jnp.int32, (2 * dh, t), 0)
    blk = lax.broadcasted_iota(jnp.int32, (nb, t), 0)
    past = blk < i
    causal = (lax.broadcasted_iota(jnp.int32, (t, t), 0) <= lax.broadcasted_iota(jnp.int32, (t, t), 1))
    ones = jnp.ones((MOBA_ONES_ROWS, t), BF16)
    lanes = lambda hd: slice((hd // 2) * LANES, (hd // 2 + 1) * LANES)
    rows = lambda hd: slice(hd * dh, (hd + 1) * dh)
    row_heads = [(b, hd) for b in range(MOBA_ROWS) for hd in range(A_HEADS)]

    def select_blocks(b, hd):
        hh = hd % 2
        qt_pair = qt_ref[b, 0, lanes(hd), :]
        mine = (row >= hh * dh) & (row < (hh + 1) * dh)
        qt = jnp.where(mine, qt_pair, jnp.zeros_like(qt_pair))
        qm_ref[b, hd, 0:2 * dh, :] = qt

        km = km_ref[b, :, lanes(hd)]
        km_hi = km.astype(BF16)
        km_lo = (km - km_hi.astype(F32)).astype(BF16)
        bs = jnp.where(past, _dot(km_hi, qt) + _dot(km_lo, qt), -jnp.inf)
        picked = jnp.zeros((nb, t), jnp.bool_)
        for _ in range(MOBA_TOPK):
            best = jnp.max(bs, axis=0, keepdims=True)
            first = jnp.min(jnp.where(bs == best, blk, nb), axis=0, keepdims=True)
            hit = blk == first
            picked = picked | hit
            bs = jnp.where(hit, -jnp.inf, bs)
        bias = jnp.where(past & picked, 0.0, NEG).astype(BF16)
        qm_ref[b, hd, 2 * dh:, :] = jnp.concatenate([bias, jnp.zeros((2 * dh - nb, t), BF16)], axis=0)

    key_lane = lax.broadcasted_iota(jnp.int32, (t, 2 * dh), 1)

    def keys_with_block_column(b, j):
        onehot = jnp.where(key_lane == j, 1.0, 0.0).astype(BF16)
        return [jnp.concatenate([k_ref[b, j, :, lanes(2 * p)], onehot], axis=1) for p in range(A_HEADS // 2)]

    def pv(b, j, hd, p):
        vt_aug = jnp.concatenate([vt_ref[b, j, rows(hd), :], ones], axis=0)
        return _dot(vt_aug, p.astype(BF16))

    def own_unit(b, hd):
        def scores():
            return jnp.where(causal, _dot(k_ref[b, i, :, lanes(hd)], qm_ref[b, hd, 0:2 * dh, :]), NEG)

        def update(s, m0):
            m_ref[b, hd] = jnp.broadcast_to(m0, (8, t))
            acc_ref[b, hd] = pv(b, i, hd, jnp.exp2(s - m0))

        return scores, update

    def past_units(b, j):
        keys = keys_with_block_column(b, j)

        def unit(hd):
            def scores():
                return _dot(keys[hd // 2], qm_ref[b, hd])

            def update(s, m_blk):
                m_old = m_ref[b, hd]
                m_new = jnp.maximum(m_old, m_blk)
                alpha = jnp.exp2(m_old - m_new)
                m_ref[b, hd] = m_new
                acc_ref[b, hd] = alpha[0:1, :] * acc_ref[b, hd] + pv(b, j, hd, jnp.exp2(s - m_new[0:1, :]))

            return scores, update

        return [unit(hd) for hd in range(A_HEADS)]

    def pipelined(units):
        blk_max = {}
        slots = MOBA_LOOKAHEAD + 1
        for step in range(len(units) + MOBA_LOOKAHEAD):
            if step < len(units):
                sc = units[step][0]()
                s_ref[step % slots] = sc
                blk_max[step] = jnp.max(sc, axis=0, keepdims=True)
            if step >= MOBA_LOOKAHEAD:
                u = step - MOBA_LOOKAHEAD
                units[u][1](s_ref[u % slots], blk_max.pop(u))

    n_groups = i // MOBA_UNROLL
    for left in range(MOBA_UNROLL):
        @pl.when(i % MOBA_UNROLL == left)
        def _(left=left):
            for b, hd in row_heads:
                select_blocks(b, hd)
            units = [own_unit(b, hd) for b, hd in row_heads]
            for r in range(left):
                for b in range(MOBA_ROWS):
                    units += past_units(b, n_groups * MOBA_UNROLL + r)
            pipelined(units)

    def group_body(jj, carry):
        pipelined([u for r in range(MOBA_UNROLL) for b in range(MOBA_ROWS)
                   for u in past_units(b, MOBA_UNROLL * jj + r)])
        return carry

    lax.fori_loop(0, n_groups, group_body, 0)

    for b in range(MOBA_ROWS):
        for hd in range(A_HEADS):
            ot_ref[rows(hd), :] = acc_ref[b, hd, 0:dh, :] / acc_ref[b, hd, dh:dh + 1, :]
        o_ref[b] = ot_ref[...].T.astype(BF16)


def _moba(qt, k, vt, kmean):
    b, nb, t, w = k.shape
    r = MOBA_ROWS
    assert b % r == 0
    return pl.pallas_call(
        _moba_kernel,
        grid=(b // r, nb),
        in_specs=[pl.BlockSpec((r, 1, w, t), lambda bi, i: (bi, i, 0, 0)),
                  pl.BlockSpec((r, nb, t, w), lambda bi, i: (bi, 0, 0, 0)),
                  pl.BlockSpec((r, nb, w, t), lambda bi, i: (bi, 0, 0, 0)),
                  pl.BlockSpec((r, nb, w), lambda bi, i: (bi, 0, 0))],
        out_specs=pl.BlockSpec((r, t, w), lambda bi, i: (bi, i, 0)),
        out_shape=jax.ShapeDtypeStruct((b, nb * t, w), BF16),
        scratch_shapes=[pltpu.VMEM((r, A_HEADS, 4 * A_HEAD_DIM, t), BF16),
                        pltpu.VMEM((r, A_HEADS, 8, t), F32),
                        pltpu.VMEM((r, A_HEADS, A_HEAD_DIM + MOBA_ONES_ROWS, t), F32),
                        pltpu.VMEM((w, t), F32),
                        pltpu.VMEM((MOBA_LOOKAHEAD + 1, t, t), F32)],
        compiler_params=_params("parallel", "parallel"),
        name="moba",
    )(qt, k, vt, kmean)


def _mix_kernel(chunk_dec, x_ref, g_ref, rq_ref, rkt_ref, rktd_ref, rv_ref, ya_ref, dmask_ref, qdec_ref,
                w_ref, wa_ref, wb_ref, wo_ref, o_ref, state_ref, yb_ref):
    @pl.when(pl.program_id(1) == 0)
    def _():
        state_ref[...] = jnp.zeros_like(state_ref)

    x = x_ref[0]
    d = x.shape[1]
    hb = _rms(x, g_ref[...]).astype(BF16)
    c = R_CHUNK
    per_blk = TOKEN_TILE // c
    n_chunks = MIX_BLOCKS * per_blk
    rows = lambda ci: slice(ci * c, (ci + 1) * c)
    qk = lambda hd: slice(hd * R_QK_DIM, (hd + 1) * R_QK_DIM)
    vv = lambda hd: slice(hd * R_V_DIM, (hd + 1) * R_V_DIM)
    units = [(ci, hd) for ci in range(n_chunks) for hd in range(R_HEADS)]
    keys_t = lambda ref, ci, hd: ref[0, ci // per_blk, qk(hd), rows(ci % per_blk)]

    att = {u: _dot(rq_ref[0, rows(u[0]), qk(u[1])], keys_t(rkt_ref, *u)) for u in units}
    kv = {u: _dot(keys_t(rktd_ref, *u), rv_ref[0, rows(u[0]), vv(u[1])]) for u in units}
    rg = _dot(hb, w_ref[:, COL_GATES:COL_GATES + R_V_WIDTH])
    w = MIX_LATE_COLS
    late = ([lambda c=c: _dot(ya_ref[0], wa_ref[:, c:c + w]) for c in range(0, d, w)]
            + [lambda c=c: _dot(hb, w_ref[:, c:c + w])
               for c in range(COL_GATES + R_V_WIDTH, COL_GATES + R_V_WIDTH + 2 * d, w)])
    late_out = []

    cross = {}
    for hd in range(R_HEADS):
        st = state_ref[hd]
        for ci in range(n_chunks):
            cross[ci, hd] = _dot(rq_ref[0, rows(ci), qk(hd)], st.astype(BF16))
            st = chunk_dec[hd] * st + kv[ci, hd]
        state_ref[hd] = st

    gate = rg * _sigmoid(rg)
    for n, (ci, hd) in enumerate(units):
        inner = _dot((att[ci, hd] * dmask_ref[hd]).astype(BF16), rv_ref[0, rows(ci), vv(hd)])
        y = inner + cross[ci, hd] * qdec_ref[hd]
        mu = jnp.mean(y, axis=-1, keepdims=True)
        yc = y - mu
        var = jnp.mean(yc * yc, axis=-1, keepdims=True)
        yb_ref[rows(ci), vv(hd)] = (yc * lax.rsqrt(var + EPS) * gate[rows(ci), vv(hd)]).astype(BF16)
        for k in range(n * len(late) // len(units), (n + 1) * len(late) // len(units)):
            late_out.append(late[k]())

    pieces = d // w
    ta, ga, gb = (jnp.concatenate(late_out[k * pieces:(k + 1) * pieces], axis=1) for k in range(3))
    tb = _dot(yb_ref[...], wb_ref[...])
    merged = _sigmoid(ga) * ta + _sigmoid(gb) * tb
    o_ref[0] = x + _dot(merged.astype(BF16), wo_ref[...])


def _mix(x, g, rq, rkt, rktd, rv, ya, dmask, qdec, chunk_dec, w, wa, wb, wo):
    b, s, d = x.shape
    t = MIX_BLOCKS * TOKEN_TILE
    tok = lambda w: pl.BlockSpec((1, t, w), lambda bi, i: (bi, i, 0))
    blk_t = pl.BlockSpec((1, MIX_BLOCKS, R_QK_WIDTH, TOKEN_TILE), lambda bi, i: (bi, i, 0, 0))
    return pl.pallas_call(
        functools.partial(_mix_kernel, chunk_dec),
        grid=(b, s // t),
        in_specs=[tok(d), _full((1, d)), tok(R_QK_WIDTH), blk_t, blk_t, tok(R_V_WIDTH), tok(A_WIDTH),
                  _full(dmask.shape), _full(qdec.shape),
                  _full(w.shape), _full(wa.shape), _full(wb.shape), _full(wo.shape)],
        out_specs=tok(d),
        out_shape=jax.ShapeDtypeStruct((b, s, d), F32),
        scratch_shapes=[pltpu.VMEM((R_HEADS, R_QK_DIM, R_V_DIM), F32),
                        pltpu.VMEM((t, R_V_WIDTH), BF16)],
        compiler_params=_params("parallel", "arbitrary"),
        name="mix",
    )(x, g, rq, rkt, rktd, rv, ya, dmask, qdec, w, wa, wb, wo)


def _cross_kernel(x_ref, g_ref, m_ref, gm_ref, wkv_ref, wq_ref, wo_ref, o_ref, kt_ref, v_ref, cat_ref):
    d = x_ref.shape[2]
    dh = d // X_HEADS

    @pl.when(pl.program_id(1) == 0)
    def _():
        mb = _rms(m_ref[0], gm_ref[...]).astype(BF16)
        kt_ref[...] = _dot(mb, wkv_ref[:, 0:d]).T.astype(BF16)
        v_ref[...] = _dot(mb, wkv_ref[:, d:]).astype(BF16)

    cols = lambda hd: slice(hd * dh, (hd + 1) * dh)
    n_parts = CROSS_PARTS
    t = x_ref.shape[1] // n_parts
    part = lambda r: slice(r * t, (r + 1) * t)

    def project(r):
        hb = _rms(x_ref[0, part(r), :], g_ref[...]).astype(BF16)
        q = (_dot(hb, wq_ref[...]) * dh ** -0.5).astype(BF16)
        return [_dot(q[:, cols(hd)], kt_ref[cols(hd), :]) for hd in range(X_HEADS)]

    nxt = project(0)
    for r in range(n_parts):
        scores = nxt
        if r + 1 < n_parts:
            nxt = project(r + 1)
        for hd, s in enumerate(scores):
            p = jnp.exp(s - jnp.max(s, axis=-1, keepdims=True))
            p = p / jnp.sum(p, axis=-1, keepdims=True)
            cat_ref[part(r), cols(hd)] = _dot(p.astype(BF16), v_ref[:, cols(hd)]).astype(BF16)
        o_ref[0, part(r), :] = x_ref[0, part(r), :] + _dot(cat_ref[part(r), :], wo_ref[...])


def _cross(x, g, mem, gm, wkv, wq, wo):
    b, s, d = x.shape
    m = mem.shape[1]
    t = ROW_TILE
    tok = pl.BlockSpec((1, t, d), lambda bi, i: (bi, i, 0))
    return pl.pallas_call(
        _cross_kernel,
        grid=(b, s // t),
        in_specs=[tok, _full((1, d)), pl.BlockSpec((1, m, d), lambda bi, i: (bi, 0, 0)), _full((1, d)),
                  _full(wkv.shape), _full(wq.shape), _full(wo.shape)],
        out_specs=tok,
        out_shape=jax.ShapeDtypeStruct((b, s, d), F32),
        scratch_shapes=[pltpu.VMEM((d, m), BF16), pltpu.VMEM((m, d), BF16),
                        pltpu.VMEM((t, d), BF16)],
        compiler_params=_params("parallel", "arbitrary"),
        name="cross",
    )(x, g, mem, gm, wkv, wq, wo)


def _ffn_kernel(x_ref, g_ref, wg_ref, wu_ref, wd_ref, *rest):
    o_ref = rest[-1]
    x = x_ref[...]
    hb = _rms(x, g_ref[...]).astype(BF16)
    n_chunks = wg_ref.shape[1] // FF_CHUNK
    cols = lambda c: slice(c * FF_CHUNK, (c + 1) * FF_CHUNK)

    def up(c):
        return _dot(hb, wg_ref[:, cols(c)]), _dot(hb, wu_ref[:, cols(c)])

    acc = x
    nxt = up(0)
    for c in range(n_chunks):
        gate, upv = nxt
        if c + 1 < n_chunks:
            nxt = up(c + 1)
        act = gate * _sigmoid(gate) * upv
        acc = acc + _dot(act.astype(BF16), wd_ref[cols(c), :])
    o_ref[...] = _rms(acc, rest[0][...]) if len(rest) == 2 else acc


def _ffn(x2, g, wg, wu, wd, final_gain=None):
    n, d = x2.shape
    t = ROW_TILE
    tok = pl.BlockSpec((t, d), lambda i: (i, 0))
    final = () if final_gain is None else (final_gain,)
    return pl.pallas_call(
        _ffn_kernel,
        grid=(n // t,),
        in_specs=[tok, _full((1, d)), _full(wg.shape), _full(wu.shape), _full(wd.shape)] + [_full((1, d))] * len(final),
        out_specs=tok,
        out_shape=jax.ShapeDtypeStruct((n, d), F32),
        compiler_params=_params("parallel"),
        name="ffn",
    )(x2, g, wg, wu, wd, *final)


def _rope_tables(s):
    def cs(dim):
        inv = ROPE_THETA ** (-jnp.arange(0, dim, 2, dtype=F32) / dim)
        ang = jnp.arange(s).astype(F32)[:, None] * inv[None, :]
        return jnp.cos(ang), jnp.sin(ang)

    ca, sa = cs(A_HEAD_DIM)
    cr, sr = cs(R_QK_DIM)
    sign = lambda half: jnp.asarray(np.tile(np.repeat([-1.0, 1.0], half), LANES // (2 * half))[None, :], F32)
    return (jnp.tile(ca, (1, 4)), jnp.tile(sa, (1, 4)) * sign(A_HEAD_DIM // 2),
            jnp.tile(cr, (1, 2)), jnp.tile(sr, (1, 2)) * sign(R_QK_DIM // 2),
            ca.T, sa.T, cr.T, sr.T)


def _decay_tables(s):
    c = R_CHUNK
    log_g = np.log(1.0 - np.exp2(-5.0 - np.arange(R_HEADS, dtype=np.float64)))
    i = np.arange(c, dtype=np.float64)
    diff = i[:, None] - i[None, :]
    dmask = np.where(diff >= 0, np.exp(log_g[:, None, None] * np.maximum(diff, 0.0)), 0.0)
    q_dec = np.exp(log_g[:, None] * (i + 1.0))
    k_dec = np.exp(log_g[:, None] * (c - 1.0 - i))
    kdec_t = np.zeros((8, s), np.float64)
    kdec_t[:R_HEADS] = np.tile(k_dec, (1, s // c))
    qdec = np.broadcast_to(q_dec[:, :, None], (R_HEADS, c, R_V_DIM))
    chunk_dec = tuple(float(v) for v in np.exp(log_g * c))
    return (jnp.asarray(dmask, F32), jnp.asarray(qdec, F32), jnp.asarray(kdec_t, F32), chunk_dec)


def kernel(x, mem, norm_mix, w_in, w_branch_a, w_branch_b, w_out, norm_cross, norm_mem,
           w_xq, w_xkv, w_xo, norm_ffn, w_gate, w_up, w_down, norm_final):
    b, s, d = x.shape
    assert s % TOKEN_TILE == 0 and s % ROW_TILE == 0 and d % LANES == 0
    rope = _rope_tables(s)
    dmask, qdec, kdec_t, chunk_dec = _decay_tables(s)
    row = lambda v: v.reshape(1, d)
    assert w_in.shape[2] == COL_GATES + R_V_WIDTH + 2 * d

    for l in range(w_in.shape[0]):
        w = w_in[l].astype(BF16)
        wqt, wvt, wkt = (w[:, c:c + A_WIDTH].T for c in (COL_AQ, COL_AV, COL_RK))
        k_a, kmean, qt_a, vt_a, rq, rv, rkt, rktd = _qkv(x, row(norm_mix[l]), w, wqt, wvt, wkt, rope + (kdec_t,))
        y_a = _moba(qt_a, k_a, vt_a, kmean.reshape(b, s // MOBA_BLOCK, A_WIDTH))
        x = _mix(x, row(norm_mix[l]), rq, rkt, rktd, rv, y_a, dmask, qdec, chunk_dec,
                 w, w_branch_a[l].astype(BF16), w_branch_b[l].astype(BF16), w_out[l].astype(BF16))
        x = _cross(x, row(norm_cross[l]), mem, row(norm_mem[l]), w_xkv[l].astype(BF16),
                   w_xq[l].astype(BF16), w_xo[l].astype(BF16))
        last = l == w_in.shape[0] - 1
        x = _ffn(x.reshape(b * s, d), row(norm_ffn[l]), w_gate[l].astype(BF16), w_up[l].astype(BF16),
                 w_down[l].astype(BF16), row(norm_final) if last else None).reshape(b, s, d)
    return x
```

```python
import functools
import math

import numpy as np
import jax
import jax.numpy as jnp
from jax import lax
from jax.experimental import pallas as pl
from jax.experimental.pallas import tpu as pltpu

F32 = jnp.float32
BF16 = jnp.bfloat16

EPS = 1e-6
ROPE_THETA = 10000.0
A_HEADS = 8
A_HEAD_DIM = 64
A_WIDTH = A_HEADS * A_HEAD_DIM
MOBA_BLOCK = 256
MOBA_TOPK = 3
R_HEADS = 4
R_QK_DIM = 128
R_V_DIM = 256
R_QK_WIDTH = R_HEADS * R_QK_DIM
R_V_WIDTH = R_HEADS * R_V_DIM
R_CHUNK = 256
X_HEADS = 4
COL_AQ, COL_AK, COL_AV = 0, A_WIDTH, 2 * A_WIDTH
COL_RQ = 3 * A_WIDTH
COL_RK = COL_RQ + R_QK_WIDTH
COL_RV = COL_RK + R_QK_WIDTH
COL_GATES = COL_RV + R_V_WIDTH
FF_CHUNK = 256
TOKEN_TILE = MOBA_BLOCK
ROW_TILE = 1024
CROSS_PARTS = 4
QKV_BLOCKS = 4
MIX_LATE_COLS = 256
MIX_BLOCKS = 2
NEG = -1e30
LOG2_E = math.log2(math.e)
MOBA_LOOKAHEAD = 7
MOBA_ROWS = 2
MOBA_UNROLL = 4
MOBA_ONES_ROWS = 16
LANES = 128
VMEM_LIMIT = 56 * 1024 * 1024
assert 2 * A_HEAD_DIM == LANES and R_QK_DIM == LANES


def _dot(a, b):
    return jnp.dot(a, b, preferred_element_type=F32)


def _rms(x, g):
    return x * lax.rsqrt(jnp.mean(x * x, axis=-1, keepdims=True) + EPS) * g


def _sigmoid(x):
    return 1.0 / (1.0 + jnp.exp(-x))


def _params(*sem):
    return pltpu.CompilerParams(dimension_semantics=sem, vmem_limit_bytes=VMEM_LIMIT)


def _full(shape):
    n = len(shape)
    return pl.BlockSpec(shape, lambda *_: (0,) * n, pipeline_mode=pl.Buffered(1))


def _qkv_kernel(*refs):
    for sb in range(QKV_BLOCKS):
        _qkv_block(sb, *refs)


def _qkv_block(sb, x_ref, g_ref, w_ref, wqt_ref, wvt_ref, wkt_ref, cosa_ref, sina_ref, cosr_ref, sinr_ref,
               costa_ref, sinta_ref, costr_ref, sintr_ref, kdec_ref,
               k_ref, kmean_ref, qt_ref, vt_ref, rq_ref, rv_ref, rkt_ref, rktd_ref):
    t = TOKEN_TILE
    tok = slice(sb * t, (sb + 1) * t)
    h = _rms(x_ref[0, tok, :], g_ref[...])
    hb = h.astype(BF16)
    ht = h.T.astype(BF16)
    lane = lax.broadcasted_iota(jnp.int32, (t, LANES), 1)
    a_half, r_half = A_HEAD_DIM // 2, R_QK_DIM // 2
    first_half = (lane & a_half) == 0

    ak = _dot(hb, w_ref[:, COL_AK:COL_AV])
    cosa, sina = cosa_ref[tok, :], sina_ref[tok, :]
    for c in range(A_WIDTH // LANES):
        blk = ak[:, c * LANES:(c + 1) * LANES]
        rot = jnp.where(first_half, pltpu.roll(blk, LANES - a_half, 1), pltpu.roll(blk, a_half, 1))
        kr = blk * cosa + rot * sina
        k_ref[0, sb, :, c * LANES:(c + 1) * LANES] = kr.astype(BF16)
        kmean_ref[0, sb, :, c * LANES:(c + 1) * LANES] = jnp.mean(kr, axis=0, keepdims=True)

    rq = _dot(hb, w_ref[:, COL_RQ:COL_RK])
    cosr, sinr = cosr_ref[tok, :], sinr_ref[tok, :]
    for c in range(R_HEADS):
        blk = rq[:, c * R_QK_DIM:(c + 1) * R_QK_DIM]
        rq_ref[0, tok, c * R_QK_DIM:(c + 1) * R_QK_DIM] = (blk * cosr + pltpu.roll(blk, r_half, 1) * sinr).astype(BF16)

    rv_ref[0, tok, :] = _dot(hb, w_ref[:, COL_RV:COL_GATES]).astype(BF16)

    qt = _dot(wqt_ref[...], ht)
    cost, sint = costa_ref[:, tok], sinta_ref[:, tok]
    half = A_HEAD_DIM // 2
    scale_a = A_HEAD_DIM ** -0.5 * LOG2_E
    for hd in range(A_HEADS):
        x1 = qt[hd * A_HEAD_DIM:hd * A_HEAD_DIM + half]
        x2 = qt[hd * A_HEAD_DIM + half:(hd + 1) * A_HEAD_DIM]
        qt_ref[0, sb, hd * A_HEAD_DIM:hd * A_HEAD_DIM + half, :] = ((x1 * cost - x2 * sint) * scale_a).astype(BF16)
        qt_ref[0, sb, hd * A_HEAD_DIM + half:(hd + 1) * A_HEAD_DIM, :] = ((x2 * cost + x1 * sint) * scale_a).astype(BF16)

    vt_ref[0, sb] = _dot(wvt_ref[...], ht).astype(BF16)

    rkt = _dot(wkt_ref[...], ht)
    cost, sint = costr_ref[:, tok], sintr_ref[:, tok]
    half = R_QK_DIM // 2
    scale_r = R_QK_DIM ** -0.5
    for hd in range(R_HEADS):
        x1 = rkt[hd * R_QK_DIM:hd * R_QK_DIM + half]
        x2 = rkt[hd * R_QK_DIM + half:(hd + 1) * R_QK_DIM]
        o1 = (x1 * cost - x2 * sint) * scale_r
        o2 = (x2 * cost + x1 * sint) * scale_r
        dec = kdec_ref[hd:hd + 1, tok]
        rkt_ref[0, sb, hd * R_QK_DIM:hd * R_QK_DIM + half, :] = o1.astype(BF16)
        rkt_ref[0, sb, hd * R_QK_DIM + half:(hd + 1) * R_QK_DIM, :] = o2.astype(BF16)
        rktd_ref[0, sb, hd * R_QK_DIM:hd * R_QK_DIM + half, :] = (o1 * dec).astype(BF16)
        rktd_ref[0, sb, hd * R_QK_DIM + half:(hd + 1) * R_QK_DIM, :] = (o2 * dec).astype(BF16)


def _qkv(x, g, w, wqt, wvt, wkt, tabs):
    b, s, d = x.shape
    nb = s // TOKEN_TILE
    qb = QKV_BLOCKS
    t = qb * TOKEN_TILE
    tok = lambda w: pl.BlockSpec((1, t, w), lambda bi, i: (bi, i, 0))
    blk_t = lambda r: pl.BlockSpec((1, qb, r, TOKEN_TILE), lambda bi, i: (bi, i, 0, 0))
    nat_tab = pl.BlockSpec((t, LANES), lambda bi, i: (i, 0))
    tr_tab = lambda r: pl.BlockSpec((r, t), lambda bi, i: (0, i))
    return pl.pallas_call(
        _qkv_kernel,
        grid=(b, nb // qb),
        in_specs=[tok(d), _full((1, d)), _full(w.shape), _full(wqt.shape), _full(wvt.shape), _full(wkt.shape),
                  nat_tab, nat_tab, nat_tab, nat_tab,
                  tr_tab(A_HEAD_DIM // 2), tr_tab(A_HEAD_DIM // 2),
                  tr_tab(R_QK_DIM // 2), tr_tab(R_QK_DIM // 2), tr_tab(8)],
        out_specs=[pl.BlockSpec((1, qb, TOKEN_TILE, A_WIDTH), lambda bi, i: (bi, i, 0, 0)),
                   pl.BlockSpec((1, qb, 1, A_WIDTH), lambda bi, i: (bi, i, 0, 0)),
                   blk_t(A_WIDTH), blk_t(A_WIDTH), tok(R_QK_WIDTH), tok(R_V_WIDTH),
                   blk_t(R_QK_WIDTH), blk_t(R_QK_WIDTH)],
        out_shape=[jax.ShapeDtypeStruct((b, nb, TOKEN_TILE, A_WIDTH), BF16),
                   jax.ShapeDtypeStruct((b, nb, 1, A_WIDTH), F32),
                   jax.ShapeDtypeStruct((b, nb, A_WIDTH, TOKEN_TILE), BF16),
                   jax.ShapeDtypeStruct((b, nb, A_WIDTH, TOKEN_TILE), BF16),
                   jax.ShapeDtypeStruct((b, s, R_QK_WIDTH), BF16),
                   jax.ShapeDtypeStruct((b, s, R_V_WIDTH), BF16),
                   jax.ShapeDtypeStruct((b, nb, R_QK_WIDTH, TOKEN_TILE), BF16),
                   jax.ShapeDtypeStruct((b, nb, R_QK_WIDTH, TOKEN_TILE), BF16)],
        compiler_params=_params("parallel", "parallel"),
        name="qkv",
    )(x, g, w, wqt, wvt, wkt, *tabs)


def _moba_kernel(qt_ref, k_ref, vt_ref, km_ref, o_ref, qm_ref, m_ref, acc_ref, ot_ref, s_ref):
    i = pl.program_id(1)
    nb = km_ref.shape[1]
    t = MOBA_BLOCK
    dh = A_HEAD_DIM
    row = lax.broadcasted_iota(jnp.int32, (2 * dh, t), 0)
    blk = lax.broadcasted_iota(jnp.int32, (nb, t), 0)
    past = blk < i
    causal = (lax.broadcasted_iota(jnp.int32, (t, t), 0) <= lax.broadcasted_iota(jnp.int32, (t, t), 1))
    ones = jnp.ones((MOBA_ONES_ROWS, t), BF16)
    lanes = lambda hd: slice((hd // 2) * LANES, (hd // 2 + 1) * LANES)
    rows = lambda hd: slice(hd * dh, (hd + 1) * dh)
    row_heads = [(b, hd) for b in range(MOBA_ROWS) for hd in range(A_HEADS)]

    def select_blocks(b, hd):
        hh = hd % 2
        qt_pair = qt_ref[b, 0, lanes(hd), :]
        mine = (row >= hh * dh) & (row < (hh + 1) * dh)
        qt = jnp.where(mine, qt_pair, jnp.zeros_like(qt_pair))
        qm_ref[b, hd, 0:2 * dh, :] = qt

        km = km_ref[b, :, lanes(hd)]
        km_hi = km.astype(BF16)
        km_lo = (km - km_hi.astype(F32)).astype(BF16)
        bs = jnp.where(past, _dot(km_hi, qt) + _dot(km_lo, qt), -jnp.inf)
        picked = jnp.zeros((nb, t), jnp.bool_)
        for _ in range(MOBA_TOPK):
            best = jnp.max(bs, axis=0, keepdims=True)
            first = jnp.min(jnp.where(bs == best, blk, nb), axis=0, keepdims=True)
            hit = blk == first
            picked = picked | hit
            bs = jnp.where(hit, -jnp.inf, bs)
        bias = jnp.where(past & picked, 0.0, NEG).astype(BF16)
        qm_ref[b, hd, 2 * dh:, :] = jnp.concatenate([bias, jnp.zeros((2 * dh - nb, t), BF16)], axis=0)

    key_lane = lax.broadcasted_iota(jnp.int32, (t, 2 * dh), 1)

    def keys_with_block_column(b, j):
        onehot = jnp.where(key_lane == j, 1.0, 0.0).astype(BF16)
        return [jnp.concatenate([k_ref[b, j, :, lanes(2 * p)], onehot], axis=1) for p in range(A_HEADS // 2)]

    def pv(b, j, hd, p):
        vt_aug = jnp.concatenate([vt_ref[b, j, rows(hd), :], ones], axis=0)
        return _dot(vt_aug, p.astype(BF16))

    def own_unit(b, hd):
        def scores():
            return jnp.where(causal, _dot(k_ref[b, i, :, lanes(hd)], qm_ref[b, hd, 0:2 * dh, :]), NEG)

        def update(s, m0):
            m_ref[b, hd] = jnp.broadcast_to(m0, (8, t))
            acc_ref[b, hd] = pv(b, i, hd, jnp.exp2(s - m0))

        return scores, update

    def past_units(b, j):
        keys = keys_with_block_column(b, j)

        def unit(hd):
            def scores():
                return _dot(keys[hd // 2], qm_ref[b, hd])

            def update(s, m_blk):
                m_old = m_ref[b, hd]
                m_new = jnp.maximum(m_old, m_blk)
                alpha = jnp.exp2(m_old - m_new)
                m_ref[b, hd] = m_new
                acc_ref[b, hd] = alpha[0:1, :] * acc_ref[b, hd] + pv(b, j, hd, jnp.exp2(s - m_new[0:1, :]))

            return scores, update

        return [unit(hd) for hd in range(A_HEADS)]

    def pipelined(units):
        blk_max = {}
        slots = MOBA_LOOKAHEAD + 1
        for step in range(len(units) + MOBA_LOOKAHEAD):
            if step < len(units):
                sc = units[step][0]()
                s_ref[step % slots] = sc
                blk_max[step] = jnp.max(sc, axis=0, keepdims=True)
            if step >= MOBA_LOOKAHEAD:
                u = step - MOBA_LOOKAHEAD
                units[u][1](s_ref[u % slots], blk_max.pop(u))

    n_groups = i // MOBA_UNROLL
    for left in range(MOBA_UNROLL):
        @pl.when(i % MOBA_UNROLL == left)
        def _(left=left):
            for b, hd in row_heads:
                select_blocks(b, hd)
            units = [own_unit(b, hd) for b, hd in row_heads]
            for r in range(left):
                for b in range(MOBA_ROWS):
                    units += past_units(b, n_groups * MOBA_UNROLL + r)
            pipelined(units)

    def group_body(jj, carry):
        pipelined([u for r in range(MOBA_UNROLL) for b in range(MOBA_ROWS)
                   for u in past_units(b, MOBA_UNROLL * jj + r)])
        return carry

    lax.fori_loop(0, n_groups, group_body, 0)

    for b in range(MOBA_ROWS):
        for hd in range(A_HEADS):
            ot_ref[rows(hd), :] = acc_ref[b, hd, 0:dh, :] / acc_ref[b, hd, dh:dh + 1, :]
        o_ref[b] = ot_ref[...].T.astype(BF16)


def _moba(qt, k, vt, kmean):
    b, nb, t, w = k.shape
    r = MOBA_ROWS
    assert b % r == 0
    return pl.pallas_call(
        _moba_kernel,
        grid=(b // r, nb),
        in_specs=[pl.BlockSpec((r, 1, w, t), lambda bi, i: (bi, i, 0, 0)),
                  pl.BlockSpec((r, nb, t, w), lambda bi, i: (bi, 0, 0, 0)),
                  pl.BlockSpec((r, nb, w, t), lambda bi, i: (bi, 0, 0, 0)),
                  pl.BlockSpec((r, nb, w), lambda bi, i: (bi, 0, 0))],
        out_specs=pl.BlockSpec((r, t, w), lambda bi, i: (bi, i, 0)),
        out_shape=jax.ShapeDtypeStruct((b, nb * t, w), BF16),
        scratch_shapes=[pltpu.VMEM((r, A_HEADS, 4 * A_HEAD_DIM, t), BF16),
                        pltpu.VMEM((r, A_HEADS, 8, t), F32),
                        pltpu.VMEM((r, A_HEADS, A_HEAD_DIM + MOBA_ONES_ROWS, t), F32),
                        pltpu.VMEM((w, t), F32),
                        pltpu.VMEM((MOBA_LOOKAHEAD + 1, t, t), F32)],
        compiler_params=_params("parallel", "parallel"),
        name="moba",
    )(qt, k, vt, kmean)


def _mix_kernel(chunk_dec, x_ref, g_ref, rq_ref, rkt_ref, rktd_ref, rv_ref, ya_ref, dmask_ref, qdec_ref,
                w_ref, wa_ref, wb_ref, wo_ref, o_ref, state_ref, yb_ref):
    @pl.when(pl.program_id(1) == 0)
    def _():
        state_ref[...] = jnp.zeros_like(state_ref)

    x = x_ref[0]
    d = x.shape[1]
    hb = _rms(x, g_ref[...]).astype(BF16)
    c = R_CHUNK
    per_blk = TOKEN_TILE // c
    n_chunks = MIX_BLOCKS * per_blk
    rows = lambda ci: slice(ci * c, (ci + 1) * c)
    qk = lambda hd: slice(hd * R_QK_DIM, (hd + 1) * R_QK_DIM)
    vv = lambda hd: slice(hd * R_V_DIM, (hd + 1) * R_V_DIM)
    units = [(ci, hd) for ci in range(n_chunks) for hd in range(R_HEADS)]
    keys_t = lambda ref, ci, hd: ref[0, ci // per_blk, qk(hd), rows(ci % per_blk)]

    att = {u: _dot(rq_ref[0, rows(u[0]), qk(u[1])], keys_t(rkt_ref, *u)) for u in units}
    kv = {u: _dot(keys_t(rktd_ref, *u), rv_ref[0, rows(u[0]), vv(u[1])]) for u in units}
    rg = _dot(hb, w_ref[:, COL_GATES:COL_GATES + R_V_WIDTH])
    w = MIX_LATE_COLS
    late = ([lambda c=c: _dot(ya_ref[0], wa_ref[:, c:c + w]) for c in range(0, d, w)]
            + [lambda c=c: _dot(hb, w_ref[:, c:c + w])
               for c in range(COL_GATES + R_V_WIDTH, COL_GATES + R_V_WIDTH + 2 * d, w)])
    late_out = []

    cross = {}
    for hd in range(R_HEADS):
        st = state_ref[hd]
        for ci in range(n_chunks):
            cross[ci, hd] = _dot(rq_ref[0, rows(ci), qk(hd)], st.astype(BF16))
            st = chunk_dec[hd] * st + kv[ci, hd]
        state_ref[hd] = st

    gate = rg * _sigmoid(rg)
    for n, (ci, hd) in enumerate(units):
        inner = _dot((att[ci, hd] * dmask_ref[hd]).astype(BF16), rv_ref[0, rows(ci), vv(hd)])
        y = inner + cross[ci, hd] * qdec_ref[hd]
        mu = jnp.mean(y, axis=-1, keepdims=True)
        yc = y - mu
        var = jnp.mean(yc * yc, axis=-1, keepdims=True)
        yb_ref[rows(ci), vv(hd)] = (yc * lax.rsqrt(var + EPS) * gate[rows(ci), vv(hd)]).astype(BF16)
        for k in range(n * len(late) // len(units), (n + 1) * len(late) // len(units)):
            late_out.append(late[k]())

    pieces = d // w
    ta, ga, gb = (jnp.concatenate(late_out[k * pieces:(k + 1) * pieces], axis=1) for k in range(3))
    tb = _dot(yb_ref[...], wb_ref[...])
    merged = _sigmoid(ga) * ta + _sigmoid(gb) * tb
    o_ref[0] = x + _dot(merged.astype(BF16), wo_ref[...])


def _mix(x, g, rq, rkt, rktd, rv, ya, dmask, qdec, chunk_dec, w, wa, wb, wo):
    b, s, d = x.shape
    t = MIX_BLOCKS * TOKEN_TILE
    tok = lambda w: pl.BlockSpec((1, t, w), lambda bi, i: (bi, i, 0))
    blk_t = pl.BlockSpec((1, MIX_BLOCKS, R_QK_WIDTH, TOKEN_TILE), lambda bi, i: (bi, i, 0, 0))
    return pl.pallas_call(
        functools.partial(_mix_kernel, chunk_dec),
        grid=(b, s // t),
        in_specs=[tok(d), _full((1, d)), tok(R_QK_WIDTH), blk_t, blk_t, tok(R_V_WIDTH), tok(A_WIDTH),
                  _full(dmask.shape), _full(qdec.shape),
                  _full(w.shape), _full(wa.shape), _full(wb.shape), _full(wo.shape)],
        out_specs=tok(d),
        out_shape=jax.ShapeDtypeStruct((b, s, d), F32),
        scratch_shapes=[pltpu.VMEM((R_HEADS, R_QK_DIM, R_V_DIM), F32),
                        pltpu.VMEM((t, R_V_WIDTH), BF16)],
        compiler_params=_params("parallel", "arbitrary"),
        name="mix",
    )(x, g, rq, rkt, rktd, rv, ya, dmask, qdec, w, wa, wb, wo)


def _cross_kernel(x_ref, g_ref, m_ref, gm_ref, wkv_ref, wq_ref, wo_ref, o_ref, kt_ref, v_ref, cat_ref):
    d = x_ref.shape[2]
    dh = d // X_HEADS

    @pl.when(pl.program_id(1) == 0)
    def _():
        mb = _rms(m_ref[0], gm_ref[...]).astype(BF16)
        kt_ref[...] = _dot(mb, wkv_ref[:, 0:d]).T.astype(BF16)
        v_ref[...] = _dot(mb, wkv_ref[:, d:]).astype(BF16)

    cols = lambda hd: slice(hd * dh, (hd + 1) * dh)
    n_parts = CROSS_PARTS
    t = x_ref.shape[1] // n_parts
    part = lambda r: slice(r * t, (r + 1) * t)

    def project(r):
        hb = _rms(x_ref[0, part(r), :], g_ref[...]).astype(BF16)
        q = (_dot(hb, wq_ref[...]) * dh ** -0.5).astype(BF16)
        return [_dot(q[:, cols(hd)], kt_ref[cols(hd), :]) for hd in range(X_HEADS)]

    nxt = project(0)
    for r in range(n_parts):
        scores = nxt
        if r + 1 < n_parts:
            nxt = project(r + 1)
        for hd, s in enumerate(scores):
            p = jnp.exp(s - jnp.max(s, axis=-1, keepdims=True))
            p = p / jnp.sum(p, axis=-1, keepdims=True)
            cat_ref[part(r), cols(hd)] = _dot(p.astype(BF16), v_ref[:, cols(hd)]).astype(BF16)
        o_ref[0, part(r), :] = x_ref[0, part(r), :] + _dot(cat_ref[part(r), :], wo_ref[...])


def _cross(x, g, mem, gm, wkv, wq, wo):
    b, s, d = x.shape
    m = mem.shape[1]
    t = ROW_TILE
    tok = pl.BlockSpec((1, t, d), lambda bi, i: (bi, i, 0))
    return pl.pallas_call(
        _cross_kernel,
        grid=(b, s // t),
        in_specs=[tok, _full((1, d)), pl.BlockSpec((1, m, d), lambda bi, i: (bi, 0, 0)), _full((1, d)),
                  _full(wkv.shape), _full(wq.shape), _full(wo.shape)],
        out_specs=tok,
        out_shape=jax.ShapeDtypeStruct((b, s, d), F32),
        scratch_shapes=[pltpu.VMEM((d, m), BF16), pltpu.VMEM((m, d), BF16),
                        pltpu.VMEM((t, d), BF16)],
        compiler_params=_params("parallel", "arbitrary"),
        name="cross",
    )(x, g, mem, gm, wkv, wq, wo)


def _ffn_kernel(x_ref, g_ref, wg_ref, wu_ref, wd_ref, *rest):
    o_ref = rest[-1]
    x = x_ref[...]
    hb = _rms(x, g_ref[...]).astype(BF16)
    n_chunks = wg_ref.shape[1] // FF_CHUNK
    cols = lambda c: slice(c * FF_CHUNK, (c + 1) * FF_CHUNK)

    def up(c):
        return _dot(hb, wg_ref[:, cols(c)]), _dot(hb, wu_ref[:, cols(c)])

    acc = x
    nxt = up(0)
    for c in range(n_chunks):
        gate, upv = nxt
        if c + 1 < n_chunks:
            nxt = up(c + 1)
        act = gate * _sigmoid(gate) * upv
        acc = acc + _dot(act.astype(BF16), wd_ref[cols(c), :])
    o_ref[...] = _rms(acc, rest[0][...]) if len(rest) == 2 else acc


def _ffn(x2, g, wg, wu, wd, final_gain=None):
    n, d = x2.shape
    t = ROW_TILE
    tok = pl.BlockSpec((t, d), lambda i: (i, 0))
    final = () if final_gain is None else (final_gain,)
    return pl.pallas_call(
        _ffn_kernel,
        grid=(n // t,),
        in_specs=[tok, _full((1, d)), _full(wg.shape), _full(wu.shape), _full(wd.shape)] + [_full((1, d))] * len(final),
        out_specs=tok,
        out_shape=jax.ShapeDtypeStruct((n, d), F32),
        compiler_params=_params("parallel"),
        name="ffn",
    )(x2, g, wg, wu, wd, *final)


def _rope_tables(s):
    def cs(dim):
        inv = ROPE_THETA ** (-jnp.arange(0, dim, 2, dtype=F32) / dim)
        ang = jnp.arange(s).astype(F32)[:, None] * inv[None, :]
        return jnp.cos(ang), jnp.sin(ang)

    ca, sa = cs(A_HEAD_DIM)
    cr, sr = cs(R_QK_DIM)
    sign = lambda half: jnp.asarray(np.tile(np.repeat([-1.0, 1.0], half), LANES // (2 * half))[None, :], F32)
    return (jnp.tile(ca, (1, 4)), jnp.tile(sa, (1, 4)) * sign(A_HEAD_DIM // 2),
            jnp.tile(cr, (1, 2)), jnp.tile(sr, (1, 2)) * sign(R_QK_DIM // 2),
            ca.T, sa.T, cr.T, sr.T)


def _decay_tables(s):
    c = R_CHUNK
    log_g = np.log(1.0 - np.exp2(-5.0 - np.arange(R_HEADS, dtype=np.float64)))
    i = np.arange(c, dtype=np.float64)
    diff = i[:, None] - i[None, :]
    dmask = np.where(diff >= 0, np.exp(log_g[:, None, None] * np.maximum(diff, 0.0)), 0.0)
    q_dec = np.exp(log_g[:, None] * (i + 1.0))
    k_dec = np.exp(log_g[:, None] * (c - 1.0 - i))
    kdec_t = np.zeros((8, s), np.float64)
    kdec_t[:R_HEADS] = np.tile(k_dec, (1, s // c))
    qdec = np.broadcast_to(q_dec[:, :, None], (R_HEADS, c, R_V_DIM))
    chunk_dec = tuple(float(v) for v in np.exp(log_g * c))
    return (jnp.asarray(dmask, F32), jnp.asarray(qdec, F32), jnp.asarray(kdec_t, F32), chunk_dec)


def kernel(x, mem, norm_mix, w_in, w_branch_a, w_branch_b, w_out, norm_cross, norm_mem,
           w_xq, w_xkv, w_xo, norm_ffn, w_gate, w_up, w_down, norm_final):
    b, s, d = x.shape
    steps = (TOKEN_TILE * QKV_BLOCKS, TOKEN_TILE * MIX_BLOCKS, ROW_TILE)
    assert all(s % t == 0 for t in steps) and b % MOBA_ROWS == 0 and d % LANES == 0
    rope = _rope_tables(s)
    dmask, qdec, kdec_t, chunk_dec = _decay_tables(s)
    row = lambda v: v.reshape(1, d)
    assert w_in.shape[2] == COL_GATES + R_V_WIDTH + 2 * d

    for l in range(w_in.shape[0]):
        w = w_in[l].astype(BF16)
        wqt, wvt, wkt = (w[:, c:c + A_WIDTH].T for c in (COL_AQ, COL_AV, COL_RK))
        k_a, kmean, qt_a, vt_a, rq, rv, rkt, rktd = _qkv(x, row(norm_mix[l]), w, wqt, wvt, wkt, rope + (kdec_t,))
        y_a = _moba(qt_a, k_a, vt_a, kmean.reshape(b, s // MOBA_BLOCK, A_WIDTH))
        x = _mix(x, row(norm_mix[l]), rq, rkt, rktd, rv, y_a, dmask, qdec, chunk_dec,
                 w, w_branch_a[l].astype(BF16), w_branch_b[l].astype(BF16), w_out[l].astype(BF16))
        x = _cross(x, row(norm_cross[l]), mem, row(norm_mem[l]), w_xkv[l].astype(BF16),
                   w_xq[l].astype(BF16), w_xo[l].astype(BF16))
        last = l == w_in.shape[0] - 1
        x = _ffn(x.reshape(b * s, d), row(norm_ffn[l]), w_gate[l].astype(BF16), w_up[l].astype(BF16),
                 w_down[l].astype(BF16), row(norm_final) if last else None).reshape(b, s, d)
    return x
```

```python
import functools
import math

import numpy as np
import jax
import jax.numpy as jnp
from jax import lax
from jax.experimental import pallas as pl
from jax.experimental.pallas import tpu as pltpu

F32 = jnp.float32
BF16 = jnp.bfloat16

EPS = 1e-6
ROPE_THETA = 10000.0
A_HEADS = 8
A_HEAD_DIM = 64
A_WIDTH = A_HEADS * A_HEAD_DIM
MOBA_BLOCK = 256
MOBA_TOPK = 3
R_HEADS = 4
R_QK_DIM = 128
R_V_DIM = 256
R_QK_WIDTH = R_HEADS * R_QK_DIM
R_V_WIDTH = R_HEADS * R_V_DIM
R_CHUNK = 256
X_HEADS = 4
COL_AQ, COL_AK, COL_AV = 0, A_WIDTH, 2 * A_WIDTH
COL_RQ = 3 * A_WIDTH
COL_RK = COL_RQ + R_QK_WIDTH
COL_RV = COL_RK + R_QK_WIDTH
COL_GATES = COL_RV + R_V_WIDTH
FF_CHUNK = 256
TOKEN_TILE = MOBA_BLOCK
ROW_TILE = 1024
CROSS_PARTS = 4
QKV_BLOCKS = 4
MIX_LATE_COLS = 256
MIX_BLOCKS = 2
NEG = -1e30
LOG2_E = math.log2(math.e)
MOBA_LOOKAHEAD = 7
MOBA_ROWS = 2
MOBA_UNROLL = 4
MOBA_ONES_ROWS = 16
LANES = 128
VMEM_LIMIT = 56 * 1024 * 1024
assert 2 * A_HEAD_DIM == LANES and R_QK_DIM == LANES


def _dot(a, b):
    return jnp.dot(a, b, preferred_element_type=F32)


def _rms(x, g):
    return x * lax.rsqrt(jnp.mean(x * x, axis=-1, keepdims=True) + EPS) * g


def _sigmoid(x):
    return 1.0 / (1.0 + jnp.exp(-x))


def _params(*sem):
    return pltpu.CompilerParams(dimension_semantics=sem, vmem_limit_bytes=VMEM_LIMIT)


def _full(shape):
    n = len(shape)
    return pl.BlockSpec(shape, lambda *_: (0,) * n, pipeline_mode=pl.Buffered(1))


def _qkv_kernel(*refs):
    for sb in range(QKV_BLOCKS):
        _qkv_block(sb, *refs)


def _qkv_block(sb, x_ref, g_ref, w_ref, wqt_ref, wvt_ref, wkt_ref, cosa_ref, sina_ref, cosr_ref, sinr_ref,
               costa_ref, sinta_ref, costr_ref, sintr_ref, kdec_ref,
               k_ref, kmean_ref, qt_ref, vt_ref, rq_ref, rv_ref, rkt_ref, rktd_ref):
    t = TOKEN_TILE
    tok = slice(sb * t, (sb + 1) * t)
    h = _rms(x_ref[0, tok, :], g_ref[...])
    hb = h.astype(BF16)
    ht = h.T.astype(BF16)
    lane = lax.broadcasted_iota(jnp.int32, (t, LANES), 1)
    a_half, r_half = A_HEAD_DIM // 2, R_QK_DIM // 2
    first_half = (lane & a_half) == 0

    ak = _dot(hb, w_ref[:, COL_AK:COL_AV])
    cosa, sina = cosa_ref[tok, :], sina_ref[tok, :]
    for c in range(A_WIDTH // LANES):
        blk = ak[:, c * LANES:(c + 1) * LANES]
        rot = jnp.where(first_half, pltpu.roll(blk, LANES - a_half, 1), pltpu.roll(blk, a_half, 1))
        kr = blk * cosa + rot * sina
        k_ref[0, sb, :, c * LANES:(c + 1) * LANES] = kr.astype(BF16)
        kmean_ref[0, sb, :, c * LANES:(c + 1) * LANES] = jnp.mean(kr, axis=0, keepdims=True)

    rq = _dot(hb, w_ref[:, COL_RQ:COL_RK])
    cosr, sinr = cosr_ref[tok, :], sinr_ref[tok, :]
    for c in range(R_HEADS):
        blk = rq[:, c * R_QK_DIM:(c + 1) * R_QK_DIM]
        rq_ref[0, tok, c * R_QK_DIM:(c + 1) * R_QK_DIM] = (blk * cosr + pltpu.roll(blk, r_half, 1) * sinr).astype(BF16)

    rv_ref[0, tok, :] = _dot(hb, w_ref[:, COL_RV:COL_GATES]).astype(BF16)

    qt = _dot(wqt_ref[...], ht)
    cost, sint = costa_ref[:, tok], sinta_ref[:, tok]
    half = A_HEAD_DIM // 2
    scale_a = A_HEAD_DIM ** -0.5 * LOG2_E
    for hd in range(A_HEADS):
        x1 = qt[hd * A_HEAD_DIM:hd * A_HEAD_DIM + half]
        x2 = qt[hd * A_HEAD_DIM + half:(hd + 1) * A_HEAD_DIM]
        qt_ref[0, sb, hd * A_HEAD_DIM:hd * A_HEAD_DIM + half, :] = ((x1 * cost - x2 * sint) * scale_a).astype(BF16)
        qt_ref[0, sb, hd * A_HEAD_DIM + half:(hd + 1) * A_HEAD_DIM, :] = ((x2 * cost + x1 * sint) * scale_a).astype(BF16)

    vt_ref[0, sb] = _dot(wvt_ref[...], ht).astype(BF16)

    rkt = _dot(wkt_ref[...], ht)
    cost, sint = costr_ref[:, tok], sintr_ref[:, tok]
    half = R_QK_DIM // 2
    scale_r = R_QK_DIM ** -0.5
    for hd in range(R_HEADS):
        x1 = rkt[hd * R_QK_DIM:hd * R_QK_DIM + half]
        x2 = rkt[hd * R_QK_DIM + half:(hd + 1) * R_QK_DIM]
        o1 = (x1 * cost - x2 * sint) * scale_r
        o2 = (x2 * cost + x1 * sint) * scale_r
        dec = kdec_ref[hd:hd + 1, tok]
        rkt_ref[0, sb, hd * R_QK_DIM:hd * R_QK_DIM + half, :] = o1.astype(BF16)
        rkt_ref[0, sb, hd * R_QK_DIM + half:(hd + 1) * R_QK_DIM, :] = o2.astype(BF16)
        rktd_ref[0, sb, hd * R_QK_DIM:hd * R_QK_DIM + half, :] = (o1 * dec).astype(BF16)
        rktd_ref[0, sb, hd * R_QK_DIM + half:(hd + 1) * R_QK_DIM, :] = (o2 * dec).astype(BF16)


def _qkv(x, g, w, wqt, wvt, wkt, tabs):
    b, s, d = x.shape
    nb = s // TOKEN_TILE
    qb = QKV_BLOCKS
    t = qb * TOKEN_TILE
    tok = lambda w: pl.BlockSpec((1, t, w), lambda bi, i: (bi, i, 0))
    blk_t = lambda r: pl.BlockSpec((1, qb, r, TOKEN_TILE), lambda bi, i: (bi, i, 0, 0))
    nat_tab = pl.BlockSpec((t, LANES), lambda bi, i: (i, 0))
    tr_tab = lambda r: pl.BlockSpec((r, t), lambda bi, i: (0, i))
    return pl.pallas_call(
        _qkv_kernel,
        grid=(b, nb // qb),
        in_specs=[tok(d), _full((1, d)), _full(w.shape), _full(wqt.shape), _full(wvt.shape), _full(wkt.shape),
                  nat_tab, nat_tab, nat_tab, nat_tab,
                  tr_tab(A_HEAD_DIM // 2), tr_tab(A_HEAD_DIM // 2),
                  tr_tab(R_QK_DIM // 2), tr_tab(R_QK_DIM // 2), tr_tab(8)],
        out_specs=[pl.BlockSpec((1, qb, TOKEN_TILE, A_WIDTH), lambda bi, i: (bi, i, 0, 0)),
                   pl.BlockSpec((1, qb, 1, A_WIDTH), lambda bi, i: (bi, i, 0, 0)),
                   blk_t(A_WIDTH), blk_t(A_WIDTH), tok(R_QK_WIDTH), tok(R_V_WIDTH),
                   blk_t(R_QK_WIDTH), blk_t(R_QK_WIDTH)],
        out_shape=[jax.ShapeDtypeStruct((b, nb, TOKEN_TILE, A_WIDTH), BF16),
                   jax.ShapeDtypeStruct((b, nb, 1, A_WIDTH), F32),
                   jax.ShapeDtypeStruct((b, nb, A_WIDTH, TOKEN_TILE), BF16),
                   jax.ShapeDtypeStruct((b, nb, A_WIDTH, TOKEN_TILE), BF16),
                   jax.ShapeDtypeStruct((b, s, R_QK_WIDTH), BF16),
                   jax.ShapeDtypeStruct((b, s, R_V_WIDTH), BF16),
                   jax.ShapeDtypeStruct((b, nb, R_QK_WIDTH, TOKEN_TILE), BF16),
                   jax.ShapeDtypeStruct((b, nb, R_QK_WIDTH, TOKEN_TILE), BF16)],
        compiler_params=_params("parallel", "parallel"),
        name="qkv",
    )(x, g, w, wqt, wvt, wkt, *tabs)


def _moba_kernel(qt_ref, k_ref, vt_ref, km_ref, o_ref, qm_ref, m_ref, acc_ref, ot_ref, s_ref):
    i = pl.program_id(1)
    nb = km_ref.shape[1]
    t = MOBA_BLOCK
    dh = A_HEAD_DIM
    row = lax.broadcasted_iota(jnp.int32, (2 * dh, t), 0)
    blk = lax.broadcasted_iota(jnp.int32, (nb, t), 0)
    past = blk < i
    causal = (lax.broadcasted_iota(jnp.int32, (t, t), 0) <= lax.broadcasted_iota(jnp.int32, (t, t), 1))
    ones = jnp.ones((MOBA_ONES_ROWS, t), BF16)
    lanes = lambda hd: slice((hd // 2) * LANES, (hd // 2 + 1) * LANES)
    rows = lambda hd: slice(hd * dh, (hd + 1) * dh)
    row_heads = [(b, hd) for b in range(MOBA_ROWS) for hd in range(A_HEADS)]

    def select_blocks(b, hd):
        hh = hd % 2
        qt_pair = qt_ref[b, 0, lanes(hd), :]
        mine = (row >= hh * dh) & (row < (hh + 1) * dh)
        qt = jnp.where(mine, qt_pair, jnp.zeros_like(qt_pair))
        qm_ref[b, hd, 0:2 * dh, :] = qt

        km = km_ref[b, :, lanes(hd)]
        km_hi = km.astype(BF16)
        km_lo = (km - km_hi.astype(F32)).astype(BF16)
        bs = jnp.where(past, _dot(km_hi, qt) + _dot(km_lo, qt), -jnp.inf)
        picked = jnp.zeros((nb, t), jnp.bool_)
        for _ in range(MOBA_TOPK):
            best = jnp.max(bs, axis=0, keepdims=True)
            first = jnp.min(jnp.where(bs == best, blk, nb), axis=0, keepdims=True)
            hit = blk == first
            picked = picked | hit
            bs = jnp.where(hit, -jnp.inf, bs)
        bias = jnp.where(past & picked, 0.0, NEG).astype(BF16)
        qm_ref[b, hd, 2 * dh:, :] = jnp.concatenate([bias, jnp.zeros((2 * dh - nb, t), BF16)], axis=0)

    key_lane = lax.broadcasted_iota(jnp.int32, (t, 2 * dh), 1)

    def keys_with_block_column(b, j):
        onehot = jnp.where(key_lane == j, 1.0, 0.0).astype(BF16)
        return [jnp.concatenate([k_ref[b, j, :, lanes(2 * p)], onehot], axis=1) for p in range(A_HEADS // 2)]

    def pv(b, j, hd, p):
        vt_aug = jnp.concatenate([vt_ref[b, j, rows(hd), :], ones], axis=0)
        return _dot(vt_aug, p.astype(BF16))

    def own_unit(b, hd):
        def scores():
            return jnp.where(causal, _dot(k_ref[b, i, :, lanes(hd)], qm_ref[b, hd, 0:2 * dh, :]), NEG)

        def update(s, m0):
            m_ref[b, hd] = jnp.broadcast_to(m0, (8, t))
            acc_ref[b, hd] = pv(b, i, hd, jnp.exp2(s - m0))

        return scores, update

    def past_units(b, j):
        keys = keys_with_block_column(b, j)

        def unit(hd):
            def scores():
                return _dot(keys[hd // 2], qm_ref[b, hd])

            def update(s, m_blk):
                m_old = m_ref[b, hd]
                m_new = jnp.maximum(m_old, m_blk)
                alpha = jnp.exp2(m_old - m_new)
                m_ref[b, hd] = m_new
                acc_ref[b, hd] = alpha[0:1, :] * acc_ref[b, hd] + pv(b, j, hd, jnp.exp2(s - m_new[0:1, :]))

            return scores, update

        return [unit(hd) for hd in range(A_HEADS)]

    def pipelined(units):
        blk_max = {}
        slots = MOBA_LOOKAHEAD + 1
        for step in range(len(units) + MOBA_LOOKAHEAD):
            if step < len(units):
                sc = units[step][0]()
                s_ref[step % slots] = sc
                blk_max[step] = jnp.max(sc, axis=0, keepdims=True)
            if step >= MOBA_LOOKAHEAD:
                u = step - MOBA_LOOKAHEAD
                units[u][1](s_ref[u % slots], blk_max.pop(u))

    n_groups = i // MOBA_UNROLL
    for left in range(MOBA_UNROLL):
        @pl.when(i % MOBA_UNROLL == left)
        def _(left=left):
            for b, hd in row_heads:
                select_blocks(b, hd)
            units = [own_unit(b, hd) for b, hd in row_heads]
            for r in range(left):
                for b in range(MOBA_ROWS):
                    units += past_units(b, n_groups * MOBA_UNROLL + r)
            pipelined(units)

    def group_body(jj, carry):
        pipelined([u for r in range(MOBA_UNROLL) for b in range(MOBA_ROWS)
                   for u in past_units(b, MOBA_UNROLL * jj + r)])
        return carry

    lax.fori_loop(0, n_groups, group_body, 0)

    for b in range(MOBA_ROWS):
        for hd in range(A_HEADS):
            ot_ref[rows(hd), :] = acc_ref[b, hd, 0:dh, :] / acc_ref[b, hd, dh:dh + 1, :]
        o_ref[b] = ot_ref[...].T.astype(BF16)


def _moba(qt, k, vt, kmean):
    b, nb, t, w = k.shape
    r = MOBA_ROWS
    assert b % r == 0
    return pl.pallas_call(
        _moba_kernel,
        grid=(b // r, nb),
        in_specs=[pl.BlockSpec((r, 1, w, t), lambda bi, i: (bi, i, 0, 0)),
                  pl.BlockSpec((r, nb, t, w), lambda bi, i: (bi, 0, 0, 0)),
                  pl.BlockSpec((r, nb, w, t), lambda bi, i: (bi, 0, 0, 0)),
                  pl.BlockSpec((r, nb, w), lambda bi, i: (bi, 0, 0))],
        out_specs=pl.BlockSpec((r, t, w), lambda bi, i: (bi, i, 0)),
        out_shape=jax.ShapeDtypeStruct((b, nb * t, w), BF16),
        scratch_shapes=[pltpu.VMEM((r, A_HEADS, 4 * A_HEAD_DIM, t), BF16),
                        pltpu.VMEM((r, A_HEADS, 8, t), F32),
                        pltpu.VMEM((r, A_HEADS, A_HEAD_DIM + MOBA_ONES_ROWS, t), F32),
                        pltpu.VMEM((w, t), F32),
                        pltpu.VMEM((MOBA_LOOKAHEAD + 1, t, t), F32)],
        compiler_params=_params("parallel", "parallel"),
        name="moba",
    )(qt, k, vt, kmean)


def _mix_kernel(chunk_dec, x_ref, g_ref, rq_ref, rkt_ref, rktd_ref, rv_ref, ya_ref, dmask_ref, qdec_ref,
                w_ref, wa_ref, wb_ref, wo_ref, o_ref, state_ref, yb_ref):
    @pl.when(pl.program_id(1) == 0)
    def _():
        state_ref[...] = jnp.zeros_like(state_ref)

    x = x_ref[0]
    d = x.shape[1]
    hb = _rms(x, g_ref[...]).astype(BF16)
    c = R_CHUNK
    per_blk = TOKEN_TILE // c
    n_chunks = MIX_BLOCKS * per_blk
    rows = lambda ci: slice(ci * c, (ci + 1) * c)
    qk = lambda hd: slice(hd * R_QK_DIM, (hd + 1) * R_QK_DIM)
    vv = lambda hd: slice(hd * R_V_DIM, (hd + 1) * R_V_DIM)
    units = [(ci, hd) for ci in range(n_chunks) for hd in range(R_HEADS)]
    keys_t = lambda ref, ci, hd: ref[0, ci // per_blk, qk(hd), rows(ci % per_blk)]

    att = {u: _dot(rq_ref[0, rows(u[0]), qk(u[1])], keys_t(rkt_ref, *u)) for u in units}
    kv = {u: _dot(keys_t(rktd_ref, *u), rv_ref[0, rows(u[0]), vv(u[1])]) for u in units}
    rg = _dot(hb, w_ref[:, COL_GATES:COL_GATES + R_V_WIDTH])
    w = MIX_LATE_COLS
    late = ([lambda c=c: _dot(ya_ref[0], wa_ref[:, c:c + w]) for c in range(0, d, w)]
            + [lambda c=c: _dot(hb, w_ref[:, c:c + w])
               for c in range(COL_GATES + R_V_WIDTH, COL_GATES + R_V_WIDTH + 2 * d, w)])
    late_out = []

    cross = {}
    for hd in range(R_HEADS):
        st = state_ref[hd]
        for ci in range(n_chunks):
            cross[ci, hd] = _dot(rq_ref[0, rows(ci), qk(hd)], st.astype(BF16))
            st = chunk_dec[hd] * st + kv[ci, hd]
        state_ref[hd] = st

    gate = rg * _sigmoid(rg)
    for n, (ci, hd) in enumerate(units):
        inner = _dot((att[ci, hd] * dmask_ref[hd]).astype(BF16), rv_ref[0, rows(ci), vv(hd)])
        y = inner + cross[ci, hd] * qdec_ref[hd]
        mu = jnp.mean(y, axis=-1, keepdims=True)
        yc = y - mu
        var = jnp.mean(yc * yc, axis=-1, keepdims=True)
        yb_ref[rows(ci), vv(hd)] = (yc * lax.rsqrt(var + EPS) * gate[rows(ci), vv(hd)]).astype(BF16)
        for k in range(n * len(late) // len(units), (n + 1) * len(late) // len(units)):
            late_out.append(late[k]())

    pieces = d // w
    ta, ga, gb = (jnp.concatenate(late_out[k * pieces:(k + 1) * pieces], axis=1) for k in range(3))
    tb = _dot(yb_ref[...], wb_ref[...])
    merged = _sigmoid(ga) * ta + _sigmoid(gb) * tb
    o_ref[0] = x + _dot(merged.astype(BF16), wo_ref[...])


def _mix(x, g, rq, rkt, rktd, rv, ya, dmask, qdec, chunk_dec, w, wa, wb, wo):
    b, s, d = x.shape
    t = MIX_BLOCKS * TOKEN_TILE
    tok = lambda w: pl.BlockSpec((1, t, w), lambda bi, i: (bi, i, 0))
    blk_t = pl.BlockSpec((1, MIX_BLOCKS, R_QK_WIDTH, TOKEN_TILE), lambda bi, i: (bi, i, 0, 0))
    return pl.pallas_call(
        functools.partial(_mix_kernel, chunk_dec),
        grid=(b, s // t),
        in_specs=[tok(d), _full((1, d)), tok(R_QK_WIDTH), blk_t, blk_t, tok(R_V_WIDTH), tok(A_WIDTH),
                  _full(dmask.shape), _full(qdec.shape),
                  _full(w.shape), _full(wa.shape), _full(wb.shape), _full(wo.shape)],
        out_specs=tok(d),
        out_shape=jax.ShapeDtypeStruct((b, s, d), F32),
        scratch_shapes=[pltpu.VMEM((R_HEADS, R_QK_DIM, R_V_DIM), F32),
                        pltpu.VMEM((t, R_V_WIDTH), BF16)],
        compiler_params=_params("parallel", "arbitrary"),
        name="mix",
    )(x, g, rq, rkt, rktd, rv, ya, dmask, qdec, w, wa, wb, wo)


def _cross_kernel(x_ref, g_ref, m_ref, gm_ref, wkv_ref, wq_ref, wo_ref, o_ref, qk_ref, vo_ref, p_ref):
    d = x_ref.shape[2]
    dh = d // X_HEADS
    mlen = m_ref.shape[1]
    cols = lambda hd: slice(hd * dh, (hd + 1) * dh)
    mcols = lambda hd: slice(hd * mlen, (hd + 1) * mlen)

    @pl.when(pl.program_id(1) == 0)
    def _():
        mb = _rms(m_ref[0], gm_ref[...]).astype(BF16)
        kt = _dot(mb, wkv_ref[:, 0:d]).T.astype(BF16)
        v = _dot(mb, wkv_ref[:, d:]).astype(BF16)
        for hd in range(X_HEADS):
            qk_ref[:, mcols(hd)] = (_dot(wq_ref[:, cols(hd)], kt[cols(hd), :]) * dh ** -0.5).astype(BF16)
            vo_ref[mcols(hd), :] = _dot(v[:, cols(hd)], wo_ref[cols(hd), :]).astype(BF16)

    n_parts = CROSS_PARTS
    t = x_ref.shape[1] // n_parts
    part = lambda r: slice(r * t, (r + 1) * t)

    def scores_of(r):
        hb = _rms(x_ref[0, part(r), :], g_ref[...]).astype(BF16)
        return _dot(hb, qk_ref[...])

    nxt = scores_of(0)
    for r in range(n_parts):
        scores = nxt
        if r + 1 < n_parts:
            nxt = scores_of(r + 1)
        for hd in range(X_HEADS):
            s = scores[:, mcols(hd)]
            p = jnp.exp(s - jnp.max(s, axis=-1, keepdims=True))
            p_ref[part(r), mcols(hd)] = (p / jnp.sum(p, axis=-1, keepdims=True)).astype(BF16)
        o_ref[0, part(r), :] = x_ref[0, part(r), :] + _dot(p_ref[part(r), :], vo_ref[...])


def _cross(x, g, mem, gm, wkv, wq, wo):
    b, s, d = x.shape
    m = mem.shape[1]
    t = ROW_TILE
    tok = pl.BlockSpec((1, t, d), lambda bi, i: (bi, i, 0))
    return pl.pallas_call(
        _cross_kernel,
        grid=(b, s // t),
        in_specs=[tok, _full((1, d)), pl.BlockSpec((1, m, d), lambda bi, i: (bi, 0, 0)), _full((1, d)),
                  _full(wkv.shape), _full(wq.shape), _full(wo.shape)],
        out_specs=tok,
        out_shape=jax.ShapeDtypeStruct((b, s, d), F32),
        scratch_shapes=[pltpu.VMEM((d, X_HEADS * m), BF16),
                        pltpu.VMEM((X_HEADS * m, d), BF16),
                        pltpu.VMEM((t, X_HEADS * m), BF16)],
        compiler_params=_params("parallel", "arbitrary"),
        name="cross",
    )(x, g, mem, gm, wkv, wq, wo)


def _ffn_kernel(x_ref, g_ref, wg_ref, wu_ref, wd_ref, *rest):
    o_ref = rest[-1]
    x = x_ref[...]
    hb = _rms(x, g_ref[...]).astype(BF16)
    n_chunks = wg_ref.shape[1] // FF_CHUNK
    cols = lambda c: slice(c * FF_CHUNK, (c + 1) * FF_CHUNK)

    def up(c):
        return _dot(hb, wg_ref[:, cols(c)]), _dot(hb, wu_ref[:, cols(c)])

    acc = x
    nxt = up(0)
    for c in range(n_chunks):
        gate, upv = nxt
        if c + 1 < n_chunks:
            nxt = up(c + 1)
        act = gate * _sigmoid(gate) * upv
        acc = acc + _dot(act.astype(BF16), wd_ref[cols(c), :])
    o_ref[...] = _rms(acc, rest[0][...]) if len(rest) == 2 else acc


def _ffn(x2, g, wg, wu, wd, final_gain=None):
    n, d = x2.shape
    t = ROW_TILE
    tok = pl.BlockSpec((t, d), lambda i: (i, 0))
    final = () if final_gain is None else (final_gain,)
    return pl.pallas_call(
        _ffn_kernel,
        grid=(n // t,),
        in_specs=[tok, _full((1, d)), _full(wg.shape), _full(wu.shape), _full(wd.shape)] + [_full((1, d))] * len(final),
        out_specs=tok,
        out_shape=jax.ShapeDtypeStruct((n, d), F32),
        compiler_params=_params("parallel"),
        name="ffn",
    )(x2, g, wg, wu, wd, *final)


def _rope_tables(s):
    def cs(dim):
        inv = ROPE_THETA ** (-jnp.arange(0, dim, 2, dtype=F32) / dim)
        ang = jnp.arange(s).astype(F32)[:, None] * inv[None, :]
        return jnp.cos(ang), jnp.sin(ang)

    ca, sa = cs(A_HEAD_DIM)
    cr, sr = cs(R_QK_DIM)
    sign = lambda half: jnp.asarray(np.tile(np.repeat([-1.0, 1.0], half), LANES // (2 * half))[None, :], F32)
    return (jnp.tile(ca, (1, 4)), jnp.tile(sa, (1, 4)) * sign(A_HEAD_DIM // 2),
            jnp.tile(cr, (1, 2)), jnp.tile(sr, (1, 2)) * sign(R_QK_DIM // 2),
            ca.T, sa.T, cr.T, sr.T)


def _decay_tables(s):
    c = R_CHUNK
    log_g = np.log(1.0 - np.exp2(-5.0 - np.arange(R_HEADS, dtype=np.float64)))
    i = np.arange(c, dtype=np.float64)
    diff = i[:, None] - i[None, :]
    dmask = np.where(diff >= 0, np.exp(log_g[:, None, None] * np.maximum(diff, 0.0)), 0.0)
    q_dec = np.exp(log_g[:, None] * (i + 1.0))
    k_dec = np.exp(log_g[:, None] * (c - 1.0 - i))
    kdec_t = np.zeros((8, s), np.float64)
    kdec_t[:R_HEADS] = np.tile(k_dec, (1, s // c))
    qdec = np.broadcast_to(q_dec[:, :, None], (R_HEADS, c, R_V_DIM))
    chunk_dec = tuple(float(v) for v in np.exp(log_g * c))
    return (jnp.asarray(dmask, F32), jnp.asarray(qdec, F32), jnp.asarray(kdec_t, F32), chunk_dec)


def kernel(x, mem, norm_mix, w_in, w_branch_a, w_branch_b, w_out, norm_cross, norm_mem,
           w_xq, w_xkv, w_xo, norm_ffn, w_gate, w_up, w_down, norm_final):
    b, s, d = x.shape
    steps = (TOKEN_TILE * QKV_BLOCKS, TOKEN_TILE * MIX_BLOCKS, ROW_TILE)
    assert all(s % t == 0 for t in steps) and b % MOBA_ROWS == 0 and d % LANES == 0
    rope = _rope_tables(s)
    dmask, qdec, kdec_t, chunk_dec = _decay_tables(s)
    row = lambda v: v.reshape(1, d)
    assert w_in.shape[2] == COL_GATES + R_V_WIDTH + 2 * d

    for l in range(w_in.shape[0]):
        w = w_in[l].astype(BF16)
        wqt, wvt, wkt = (w[:, c:c + A_WIDTH].T for c in (COL_AQ, COL_AV, COL_RK))
        k_a, kmean, qt_a, vt_a, rq, rv, rkt, rktd = _qkv(x, row(norm_mix[l]), w, wqt, wvt, wkt, rope + (kdec_t,))
        y_a = _moba(qt_a, k_a, vt_a, kmean.reshape(b, s // MOBA_BLOCK, A_WIDTH))
        x = _mix(x, row(norm_mix[l]), rq, rkt, rktd, rv, y_a, dmask, qdec, chunk_dec,
                 w, w_branch_a[l].astype(BF16), w_branch_b[l].astype(BF16), w_out[l].astype(BF16))
        x = _cross(x, row(norm_cross[l]), mem, row(norm_mem[l]), w_xkv[l].astype(BF16),
                   w_xq[l].astype(BF16), w_xo[l].astype(BF16))
        last = l == w_in.shape[0] - 1
        x = _ffn(x.reshape(b * s, d), row(norm_ffn[l]), w_gate[l].astype(BF16), w_up[l].astype(BF16),
                 w_down[l].astype(BF16), row(norm_final) if last else None).reshape(b, s, d)
    return x
```

```python
import functools
import math

import numpy as np
import jax
import jax.numpy as jnp
from jax import lax
from jax.experimental import pallas as pl
from jax.experimental.pallas import tpu as pltpu

F32 = jnp.float32
BF16 = jnp.bfloat16

EPS = 1e-6
ROPE_THETA = 10000.0
A_HEADS = 8
A_HEAD_DIM = 64
A_WIDTH = A_HEADS * A_HEAD_DIM
MOBA_BLOCK = 256
MOBA_TOPK = 3
R_HEADS = 4
R_QK_DIM = 128
R_V_DIM = 256
R_QK_WIDTH = R_HEADS * R_QK_DIM
R_V_WIDTH = R_HEADS * R_V_DIM
R_CHUNK = 256
X_HEADS = 4
COL_AQ, COL_AK, COL_AV = 0, A_WIDTH, 2 * A_WIDTH
COL_RQ = 3 * A_WIDTH
COL_RK = COL_RQ + R_QK_WIDTH
COL_RV = COL_RK + R_QK_WIDTH
COL_GATES = COL_RV + R_V_WIDTH
FF_CHUNK = 256
TOKEN_TILE = MOBA_BLOCK
ROW_TILE = 1024
CROSS_PARTS = 4
QKV_BLOCKS = 4
MIX_LATE_COLS = 256
MIX_BLOCKS = 2
NEG = -1e30
LOG2_E = math.log2(math.e)
MOBA_LOOKAHEAD = 7
MOBA_ROWS = 2
MOBA_UNROLL = 4
MOBA_ONES_ROWS = 16
LANES = 128
VMEM_LIMIT = 56 * 1024 * 1024
assert 2 * A_HEAD_DIM == LANES and R_QK_DIM == LANES


def _dot(a, b):
    return jnp.dot(a, b, preferred_element_type=F32)


def _rms(x, g):
    return x * lax.rsqrt(jnp.mean(x * x, axis=-1, keepdims=True) + EPS) * g


def _sigmoid(x):
    return 1.0 / (1.0 + jnp.exp(-x))


def _params(*sem):
    return pltpu.CompilerParams(dimension_semantics=sem, vmem_limit_bytes=VMEM_LIMIT)


def _full(shape):
    n = len(shape)
    return pl.BlockSpec(shape, lambda *_: (0,) * n, pipeline_mode=pl.Buffered(1))


def _qkv_kernel(*refs):
    for sb in range(QKV_BLOCKS):
        _qkv_block(sb, *refs)


def _qkv_block(sb, x_ref, g_ref, w_ref, wqt_ref, wvt_ref, wkt_ref, cosa_ref, sina_ref, cosr_ref, sinr_ref,
               costa_ref, sinta_ref, costr_ref, sintr_ref, kdec_ref,
               k_ref, kmean_ref, qt_ref, vt_ref, rq_ref, rv_ref, rkt_ref, rktd_ref):
    t = TOKEN_TILE
    tok = slice(sb * t, (sb + 1) * t)
    h = _rms(x_ref[0, tok, :], g_ref[...])
    hb = h.astype(BF16)
    ht = h.T.astype(BF16)
    lane = lax.broadcasted_iota(jnp.int32, (t, LANES), 1)
    a_half, r_half = A_HEAD_DIM // 2, R_QK_DIM // 2
    first_half = (lane & a_half) == 0

    ak = _dot(hb, w_ref[:, COL_AK:COL_AV])
    cosa, sina = cosa_ref[tok, :], sina_ref[tok, :]
    for c in range(A_WIDTH // LANES):
        blk = ak[:, c * LANES:(c + 1) * LANES]
        rot = jnp.where(first_half, pltpu.roll(blk, LANES - a_half, 1), pltpu.roll(blk, a_half, 1))
        kr = blk * cosa + rot * sina
        k_ref[0, sb, :, c * LANES:(c + 1) * LANES] = kr.astype(BF16)
        kmean_ref[0, sb, :, c * LANES:(c + 1) * LANES] = jnp.mean(kr, axis=0, keepdims=True)

    rq = _dot(hb, w_ref[:, COL_RQ:COL_RK])
    cosr, sinr = cosr_ref[tok, :], sinr_ref[tok, :]
    for c in range(R_HEADS):
        blk = rq[:, c * R_QK_DIM:(c + 1) * R_QK_DIM]
        rq_ref[0, tok, c * R_QK_DIM:(c + 1) * R_QK_DIM] = (blk * cosr + pltpu.roll(blk, r_half, 1) * sinr).astype(BF16)

    rv_ref[0, tok, :] = _dot(hb, w_ref[:, COL_RV:COL_GATES]).astype(BF16)

    qt = _dot(wqt_ref[...], ht)
    cost, sint = costa_ref[:, tok], sinta_ref[:, tok]
    half = A_HEAD_DIM // 2
    scale_a = A_HEAD_DIM ** -0.5 * LOG2_E
    for hd in range(A_HEADS):
        x1 = qt[hd * A_HEAD_DIM:hd * A_HEAD_DIM + half]
        x2 = qt[hd * A_HEAD_DIM + half:(hd + 1) * A_HEAD_DIM]
        qt_ref[0, sb, hd * A_HEAD_DIM:hd * A_HEAD_DIM + half, :] = ((x1 * cost - x2 * sint) * scale_a).astype(BF16)
        qt_ref[0, sb, hd * A_HEAD_DIM + half:(hd + 1) * A_HEAD_DIM, :] = ((x2 * cost + x1 * sint) * scale_a).astype(BF16)

    vt_ref[0, sb] = _dot(wvt_ref[...], ht).astype(BF16)

    rkt = _dot(wkt_ref[...], ht)
    cost, sint = costr_ref[:, tok], sintr_ref[:, tok]
    half = R_QK_DIM // 2
    scale_r = R_QK_DIM ** -0.5
    for hd in range(R_HEADS):
        x1 = rkt[hd * R_QK_DIM:hd * R_QK_DIM + half]
        x2 = rkt[hd * R_QK_DIM + half:(hd + 1) * R_QK_DIM]
        o1 = (x1 * cost - x2 * sint) * scale_r
        o2 = (x2 * cost + x1 * sint) * scale_r
        dec = kdec_ref[hd:hd + 1, tok]
        rkt_ref[0, sb, hd * R_QK_DIM:hd * R_QK_DIM + half, :] = o1.astype(BF16)
        rkt_ref[0, sb, hd * R_QK_DIM + half:(hd + 1) * R_QK_DIM, :] = o2.astype(BF16)
        rktd_ref[0, sb, hd * R_QK_DIM:hd * R_QK_DIM + half, :] = (o1 * dec).astype(BF16)
        rktd_ref[0, sb, hd * R_QK_DIM + half:(hd + 1) * R_QK_DIM, :] = (o2 * dec).astype(BF16)


def _qkv(x, g, w, wqt, wvt, wkt, tabs):
    b, s, d = x.shape
    nb = s // TOKEN_TILE
    qb = QKV_BLOCKS
    t = qb * TOKEN_TILE
    tok = lambda w: pl.BlockSpec((1, t, w), lambda bi, i: (bi, i, 0))
    blk_t = lambda r: pl.BlockSpec((1, qb, r, TOKEN_TILE), lambda bi, i: (bi, i, 0, 0))
    nat_tab = pl.BlockSpec((t, LANES), lambda bi, i: (i, 0))
    tr_tab = lambda r: pl.BlockSpec((r, t), lambda bi, i: (0, i))
    return pl.pallas_call(
        _qkv_kernel,
        grid=(b, nb // qb),
        in_specs=[tok(d), _full((1, d)), _full(w.shape), _full(wqt.shape), _full(wvt.shape), _full(wkt.shape),
                  nat_tab, nat_tab, nat_tab, nat_tab,
                  tr_tab(A_HEAD_DIM // 2), tr_tab(A_HEAD_DIM // 2),
                  tr_tab(R_QK_DIM // 2), tr_tab(R_QK_DIM // 2), tr_tab(8)],
        out_specs=[pl.BlockSpec((1, qb, TOKEN_TILE, A_WIDTH), lambda bi, i: (bi, i, 0, 0)),
                   pl.BlockSpec((1, qb, 1, A_WIDTH), lambda bi, i: (bi, i, 0, 0)),
                   blk_t(A_WIDTH), blk_t(A_WIDTH), tok(R_QK_WIDTH), tok(R_V_WIDTH),
                   blk_t(R_QK_WIDTH), blk_t(R_QK_WIDTH)],
        out_shape=[jax.ShapeDtypeStruct((b, nb, TOKEN_TILE, A_WIDTH), BF16),
                   jax.ShapeDtypeStruct((b, nb, 1, A_WIDTH), F32),
                   jax.ShapeDtypeStruct((b, nb, A_WIDTH, TOKEN_TILE), BF16),
                   jax.ShapeDtypeStruct((b, nb, A_WIDTH, TOKEN_TILE), BF16),
                   jax.ShapeDtypeStruct((b, s, R_QK_WIDTH), BF16),
                   jax.ShapeDtypeStruct((b, s, R_V_WIDTH), BF16),
                   jax.ShapeDtypeStruct((b, nb, R_QK_WIDTH, TOKEN_TILE), BF16),
                   jax.ShapeDtypeStruct((b, nb, R_QK_WIDTH, TOKEN_TILE), BF16)],
        compiler_params=_params("parallel", "parallel"),
        name="qkv",
    )(x, g, w, wqt, wvt, wkt, *tabs)


def _moba_kernel(qt_ref, k_ref, vt_ref, km_ref, o_ref, qm_ref, m_ref, acc_ref, ot_ref, s_ref):
    i = pl.program_id(1)
    nb = km_ref.shape[1]
    t = MOBA_BLOCK
    dh = A_HEAD_DIM
    row = lax.broadcasted_iota(jnp.int32, (2 * dh, t), 0)
    blk = lax.broadcasted_iota(jnp.int32, (nb, t), 0)
    past = blk < i
    causal = (lax.broadcasted_iota(jnp.int32, (t, t), 0) <= lax.broadcasted_iota(jnp.int32, (t, t), 1))
    ones = jnp.ones((MOBA_ONES_ROWS, t), BF16)
    lanes = lambda hd: slice((hd // 2) * LANES, (hd // 2 + 1) * LANES)
    rows = lambda hd: slice(hd * dh, (hd + 1) * dh)
    row_heads = [(b, hd) for b in range(MOBA_ROWS) for hd in range(A_HEADS)]

    def select_blocks(b, hd):
        hh = hd % 2
        qt_pair = qt_ref[b, 0, lanes(hd), :]
        mine = (row >= hh * dh) & (row < (hh + 1) * dh)
        qt = jnp.where(mine, qt_pair, jnp.zeros_like(qt_pair))
        qm_ref[b, hd, 0:2 * dh, :] = qt

        km = km_ref[b, :, lanes(hd)]
        km_hi = km.astype(BF16)
        km_lo = (km - km_hi.astype(F32)).astype(BF16)
        bs = jnp.where(past, _dot(km_hi, qt) + _dot(km_lo, qt), -jnp.inf)
        picked = jnp.zeros((nb, t), jnp.bool_)
        for _ in range(MOBA_TOPK):
            best = jnp.max(bs, axis=0, keepdims=True)
            first = jnp.min(jnp.where(bs == best, blk, nb), axis=0, keepdims=True)
            hit = blk == first
            picked = picked | hit
            bs = jnp.where(hit, -jnp.inf, bs)
        bias = jnp.where(past & picked, 0.0, NEG).astype(BF16)
        qm_ref[b, hd, 2 * dh:, :] = jnp.concatenate([bias, jnp.zeros((2 * dh - nb, t), BF16)], axis=0)

    key_lane = lax.broadcasted_iota(jnp.int32, (t, 2 * dh), 1)

    def keys_with_block_column(b, j):
        onehot = jnp.where(key_lane == j, 1.0, 0.0).astype(BF16)
        return [jnp.concatenate([k_ref[b, j, :, lanes(2 * p)], onehot], axis=1) for p in range(A_HEADS // 2)]

    def pv(b, j, hd, p):
        vt_aug = jnp.concatenate([vt_ref[b, j, rows(hd), :], ones], axis=0)
        return _dot(vt_aug, p.astype(BF16))

    def own_unit(b, hd):
        def scores():
            return jnp.where(causal, _dot(k_ref[b, i, :, lanes(hd)], qm_ref[b, hd, 0:2 * dh, :]), NEG)

        def update(s, m0):
            m_ref[b, hd] = jnp.broadcast_to(m0, (8, t))
            acc_ref[b, hd] = pv(b, i, hd, jnp.exp2(s - m0))

        return scores, update

    def past_units(b, j):
        keys = keys_with_block_column(b, j)

        def unit(hd):
            def scores():
                return _dot(keys[hd // 2], qm_ref[b, hd])

            def update(s, m_blk):
                m_old = m_ref[b, hd]
                m_new = jnp.maximum(m_old, m_blk)
                alpha = jnp.exp2(m_old - m_new)
                m_ref[b, hd] = m_new
                acc_ref[b, hd] = alpha[0:1, :] * acc_ref[b, hd] + pv(b, j, hd, jnp.exp2(s - m_new[0:1, :]))

            return scores, update

        return [unit(hd) for hd in range(A_HEADS)]

    def pipelined(units):
        blk_max = {}
        slots = MOBA_LOOKAHEAD + 1
        for step in range(len(units) + MOBA_LOOKAHEAD):
            if step < len(units):
                sc = units[step][0]()
                s_ref[step % slots] = sc
                blk_max[step] = jnp.max(sc, axis=0, keepdims=True)
            if step >= MOBA_LOOKAHEAD:
                u = step - MOBA_LOOKAHEAD
                units[u][1](s_ref[u % slots], blk_max.pop(u))

    n_groups = i // MOBA_UNROLL
    for left in range(MOBA_UNROLL):
        @pl.when(i % MOBA_UNROLL == left)
        def _(left=left):
            for b, hd in row_heads:
                select_blocks(b, hd)
            units = [own_unit(b, hd) for b, hd in row_heads]
            for r in range(left):
                for b in range(MOBA_ROWS):
                    units += past_units(b, n_groups * MOBA_UNROLL + r)
            pipelined(units)

    def group_body(jj, carry):
        pipelined([u for r in range(MOBA_UNROLL) for b in range(MOBA_ROWS)
                   for u in past_units(b, MOBA_UNROLL * jj + r)])
        return carry

    lax.fori_loop(0, n_groups, group_body, 0)

    for b in range(MOBA_ROWS):
        for hd in range(A_HEADS):
            ot_ref[rows(hd), :] = acc_ref[b, hd, 0:dh, :] / acc_ref[b, hd, dh:dh + 1, :]
        o_ref[b] = ot_ref[...].T.astype(BF16)


def _moba(qt, k, vt, kmean):
    b, nb, t, w = k.shape
    r = MOBA_ROWS
    assert b % r == 0
    return pl.pallas_call(
        _moba_kernel,
        grid=(b // r, nb),
        in_specs=[pl.BlockSpec((r, 1, w, t), lambda bi, i: (bi, i, 0, 0)),
                  pl.BlockSpec((r, nb, t, w), lambda bi, i: (bi, 0, 0, 0)),
                  pl.BlockSpec((r, nb, w, t), lambda bi, i: (bi, 0, 0, 0)),
                  pl.BlockSpec((r, nb, w), lambda bi, i: (bi, 0, 0))],
        out_specs=pl.BlockSpec((r, t, w), lambda bi, i: (bi, i, 0)),
        out_shape=jax.ShapeDtypeStruct((b, nb * t, w), BF16),
        scratch_shapes=[pltpu.VMEM((r, A_HEADS, 4 * A_HEAD_DIM, t), BF16),
                        pltpu.VMEM((r, A_HEADS, 8, t), F32),
                        pltpu.VMEM((r, A_HEADS, A_HEAD_DIM + MOBA_ONES_ROWS, t), F32),
                        pltpu.VMEM((w, t), F32),
                        pltpu.VMEM((MOBA_LOOKAHEAD + 1, t, t), F32)],
        compiler_params=_params("parallel", "parallel"),
        name="moba",
    )(qt, k, vt, kmean)


def _mix_kernel(chunk_dec, x_ref, g_ref, rq_ref, rkt_ref, rktd_ref, rv_ref, ya_ref, dmask_ref, qdec_ref,
                w_ref, wa_ref, wb_ref, wo_ref, o_ref, state_ref, yb_ref):
    @pl.when(pl.program_id(1) == 0)
    def _():
        state_ref[...] = jnp.zeros_like(state_ref)

    x = x_ref[0]
    d = x.shape[1]
    hb = _rms(x, g_ref[...]).astype(BF16)
    c = R_CHUNK
    per_blk = TOKEN_TILE // c
    n_chunks = MIX_BLOCKS * per_blk
    rows = lambda ci: slice(ci * c, (ci + 1) * c)
    qk = lambda hd: slice(hd * R_QK_DIM, (hd + 1) * R_QK_DIM)
    vv = lambda hd: slice(hd * R_V_DIM, (hd + 1) * R_V_DIM)
    units = [(ci, hd) for ci in range(n_chunks) for hd in range(R_HEADS)]
    keys_t = lambda ref, ci, hd: ref[0, ci // per_blk, qk(hd), rows(ci % per_blk)]

    att = {u: _dot(rq_ref[0, rows(u[0]), qk(u[1])], keys_t(rkt_ref, *u)) for u in units}
    kv = {u: _dot(keys_t(rktd_ref, *u), rv_ref[0, rows(u[0]), vv(u[1])]) for u in units}
    rg = _dot(hb, w_ref[:, COL_GATES:COL_GATES + R_V_WIDTH])
    w = MIX_LATE_COLS
    late = ([lambda c=c: _dot(ya_ref[0], wa_ref[:, c:c + w]) for c in range(0, d, w)]
            + [lambda c=c: _dot(hb, w_ref[:, c:c + w])
               for c in range(COL_GATES + R_V_WIDTH, COL_GATES + R_V_WIDTH + 2 * d, w)])
    late_out = []

    cross = {}
    for hd in range(R_HEADS):
        st = state_ref[hd]
        for ci in range(n_chunks):
            cross[ci, hd] = _dot(rq_ref[0, rows(ci), qk(hd)], st.astype(BF16))
            st = chunk_dec[hd] * st + kv[ci, hd]
        state_ref[hd] = st

    gate = rg * _sigmoid(rg)
    for n, (ci, hd) in enumerate(units):
        inner = _dot((att[ci, hd] * dmask_ref[hd]).astype(BF16), rv_ref[0, rows(ci), vv(hd)])
        y = inner + cross[ci, hd] * qdec_ref[hd]
        mu = jnp.mean(y, axis=-1, keepdims=True)
        yc = y - mu
        var = jnp.mean(yc * yc, axis=-1, keepdims=True)
        yb_ref[rows(ci), vv(hd)] = (yc * lax.rsqrt(var + EPS) * gate[rows(ci), vv(hd)]).astype(BF16)
        for k in range(n * len(late) // len(units), (n + 1) * len(late) // len(units)):
            late_out.append(late[k]())

    pieces = d // w
    ta, ga, gb = (jnp.concatenate(late_out[k * pieces:(k + 1) * pieces], axis=1) for k in range(3))
    blocks = [slice(r * TOKEN_TILE, (r + 1) * TOKEN_TILE) for r in range(MIX_BLOCKS)]
    tb = [_dot(yb_ref[tok, :], wb_ref[...]) for tok in blocks]
    for tok, tb_blk in zip(blocks, tb):
        merged = _sigmoid(ga[tok]) * ta[tok] + _sigmoid(gb[tok]) * tb_blk
        o_ref[0, tok, :] = x[tok] + _dot(merged.astype(BF16), wo_ref[...])


def _mix(x, g, rq, rkt, rktd, rv, ya, dmask, qdec, chunk_dec, w, wa, wb, wo):
    b, s, d = x.shape
    t = MIX_BLOCKS * TOKEN_TILE
    tok = lambda w: pl.BlockSpec((1, t, w), lambda bi, i: (bi, i, 0))
    blk_t = pl.BlockSpec((1, MIX_BLOCKS, R_QK_WIDTH, TOKEN_TILE), lambda bi, i: (bi, i, 0, 0))
    return pl.pallas_call(
        functools.partial(_mix_kernel, chunk_dec),
        grid=(b, s // t),
        in_specs=[tok(d), _full((1, d)), tok(R_QK_WIDTH), blk_t, blk_t, tok(R_V_WIDTH), tok(A_WIDTH),
                  _full(dmask.shape), _full(qdec.shape),
                  _full(w.shape), _full(wa.shape), _full(wb.shape), _full(wo.shape)],
        out_specs=tok(d),
        out_shape=jax.ShapeDtypeStruct((b, s, d), F32),
        scratch_shapes=[pltpu.VMEM((R_HEADS, R_QK_DIM, R_V_DIM), F32),
                        pltpu.VMEM((t, R_V_WIDTH), BF16)],
        compiler_params=_params("parallel", "arbitrary"),
        name="mix",
    )(x, g, rq, rkt, rktd, rv, ya, dmask, qdec, w, wa, wb, wo)


def _cross_kernel(x_ref, g_ref, m_ref, gm_ref, wkv_ref, wq_ref, wo_ref, o_ref, qk_ref, vo_ref, p_ref):
    d = x_ref.shape[2]
    dh = d // X_HEADS
    mlen = m_ref.shape[1]
    cols = lambda hd: slice(hd * dh, (hd + 1) * dh)
    mcols = lambda hd: slice(hd * mlen, (hd + 1) * mlen)

    @pl.when(pl.program_id(1) == 0)
    def _():
        mb = _rms(m_ref[0], gm_ref[...]).astype(BF16)
        kt = _dot(mb, wkv_ref[:, 0:d]).T.astype(BF16)
        v = _dot(mb, wkv_ref[:, d:]).astype(BF16)
        for hd in range(X_HEADS):
            qk_ref[:, mcols(hd)] = (_dot(wq_ref[:, cols(hd)], kt[cols(hd), :]) * dh ** -0.5).astype(BF16)
            vo_ref[mcols(hd), :] = _dot(v[:, cols(hd)], wo_ref[cols(hd), :]).astype(BF16)

    n_parts = CROSS_PARTS
    t = x_ref.shape[1] // n_parts
    part = lambda r: slice(r * t, (r + 1) * t)

    def scores_of(r):
        hb = _rms(x_ref[0, part(r), :], g_ref[...]).astype(BF16)
        return _dot(hb, qk_ref[...])

    nxt = scores_of(0)
    for r in range(n_parts):
        scores = nxt
        if r + 1 < n_parts:
            nxt = scores_of(r + 1)
        for hd in range(X_HEADS):
            s = scores[:, mcols(hd)]
            p = jnp.exp(s - jnp.max(s, axis=-1, keepdims=True))
            p_ref[part(r), mcols(hd)] = (p / jnp.sum(p, axis=-1, keepdims=True)).astype(BF16)
        o_ref[0, part(r), :] = x_ref[0, part(r), :] + _dot(p_ref[part(r), :], vo_ref[...])


def _cross(x, g, mem, gm, wkv, wq, wo):
    b, s, d = x.shape
    m = mem.shape[1]
    t = ROW_TILE
    tok = pl.BlockSpec((1, t, d), lambda bi, i: (bi, i, 0))
    return pl.pallas_call(
        _cross_kernel,
        grid=(b, s // t),
        in_specs=[tok, _full((1, d)), pl.BlockSpec((1, m, d), lambda bi, i: (bi, 0, 0)), _full((1, d)),
                  _full(wkv.shape), _full(wq.shape), _full(wo.shape)],
        out_specs=tok,
        out_shape=jax.ShapeDtypeStruct((b, s, d), F32),
        scratch_shapes=[pltpu.VMEM((d, X_HEADS * m), BF16),
                        pltpu.VMEM((X_HEADS * m, d), BF16),
                        pltpu.VMEM((t, X_HEADS * m), BF16)],
        compiler_params=_params("parallel", "arbitrary"),
        name="cross",
    )(x, g, mem, gm, wkv, wq, wo)


def _ffn_kernel(x_ref, g_ref, wg_ref, wu_ref, wd_ref, *rest):
    o_ref = rest[-1]
    x = x_ref[...]
    hb = _rms(x, g_ref[...]).astype(BF16)
    n_chunks = wg_ref.shape[1] // FF_CHUNK
    cols = lambda c: slice(c * FF_CHUNK, (c + 1) * FF_CHUNK)

    def up(c):
        return _dot(hb, wg_ref[:, cols(c)]), _dot(hb, wu_ref[:, cols(c)])

    acc = x
    nxt = up(0)
    for c in range(n_chunks):
        gate, upv = nxt
        if c + 1 < n_chunks:
            nxt = up(c + 1)
        act = gate * _sigmoid(gate) * upv
        acc = acc + _dot(act.astype(BF16), wd_ref[cols(c), :])
    o_ref[...] = _rms(acc, rest[0][...]) if len(rest) == 2 else acc


def _ffn(x2, g, wg, wu, wd, final_gain=None):
    n, d = x2.shape
    t = ROW_TILE
    tok = pl.BlockSpec((t, d), lambda i: (i, 0))
    final = () if final_gain is None else (final_gain,)
    return pl.pallas_call(
        _ffn_kernel,
        grid=(n // t,),
        in_specs=[tok, _full((1, d)), _full(wg.shape), _full(wu.shape), _full(wd.shape)] + [_full((1, d))] * len(final),
        out_specs=tok,
        out_shape=jax.ShapeDtypeStruct((n, d), F32),
        compiler_params=_params("parallel"),
        name="ffn",
    )(x2, g, wg, wu, wd, *final)


def _rope_tables(s):
    def cs(dim):
        inv = ROPE_THETA ** (-jnp.arange(0, dim, 2, dtype=F32) / dim)
        ang = jnp.arange(s).astype(F32)[:, None] * inv[None, :]
        return jnp.cos(ang), jnp.sin(ang)

    ca, sa = cs(A_HEAD_DIM)
    cr, sr = cs(R_QK_DIM)
    sign = lambda half: jnp.asarray(np.tile(np.repeat([-1.0, 1.0], half), LANES // (2 * half))[None, :], F32)
    return (jnp.tile(ca, (1, 4)), jnp.tile(sa, (1, 4)) * sign(A_HEAD_DIM // 2),
            jnp.tile(cr, (1, 2)), jnp.tile(sr, (1, 2)) * sign(R_QK_DIM // 2),
            ca.T, sa.T, cr.T, sr.T)


def _decay_tables(s):
    c = R_CHUNK
    log_g = np.log(1.0 - np.exp2(-5.0 - np.arange(R_HEADS, dtype=np.float64)))
    i = np.arange(c, dtype=np.float64)
    diff = i[:, None] - i[None, :]
    dmask = np.where(diff >= 0, np.exp(log_g[:, None, None] * np.maximum(diff, 0.0)), 0.0)
    q_dec = np.exp(log_g[:, None] * (i + 1.0))
    k_dec = np.exp(log_g[:, None] * (c - 1.0 - i))
    kdec_t = np.zeros((8, s), np.float64)
    kdec_t[:R_HEADS] = np.tile(k_dec, (1, s // c))
    qdec = np.broadcast_to(q_dec[:, :, None], (R_HEADS, c, R_V_DIM))
    chunk_dec = tuple(float(v) for v in np.exp(log_g * c))
    return (jnp.asarray(dmask, F32), jnp.asarray(qdec, F32), jnp.asarray(kdec_t, F32), chunk_dec)


def kernel(x, mem, norm_mix, w_in, w_branch_a, w_branch_b, w_out, norm_cross, norm_mem,
           w_xq, w_xkv, w_xo, norm_ffn, w_gate, w_up, w_down, norm_final):
    b, s, d = x.shape
    steps = (TOKEN_TILE * QKV_BLOCKS, TOKEN_TILE * MIX_BLOCKS, ROW_TILE)
    assert all(s % t == 0 for t in steps) and b % MOBA_ROWS == 0 and d % LANES == 0
    rope = _rope_tables(s)
    dmask, qdec, kdec_t, chunk_dec = _decay_tables(s)
    row = lambda v: v.reshape(1, d)
    assert w_in.shape[2] == COL_GATES + R_V_WIDTH + 2 * d

    for l in range(w_in.shape[0]):
        w = w_in[l].astype(BF16)
        wqt, wvt, wkt = (w[:, c:c + A_WIDTH].T for c in (COL_AQ, COL_AV, COL_RK))
        k_a, kmean, qt_a, vt_a, rq, rv, rkt, rktd = _qkv(x, row(norm_mix[l]), w, wqt, wvt, wkt, rope + (kdec_t,))
        y_a = _moba(qt_a, k_a, vt_a, kmean.reshape(b, s // MOBA_BLOCK, A_WIDTH))
        x = _mix(x, row(norm_mix[l]), rq, rkt, rktd, rv, y_a, dmask, qdec, chunk_dec,
                 w, w_branch_a[l].astype(BF16), w_branch_b[l].astype(BF16), w_out[l].astype(BF16))
        x = _cross(x, row(norm_cross[l]), mem, row(norm_mem[l]), w_xkv[l].astype(BF16),
                   w_xq[l].astype(BF16), w_xo[l].astype(BF16))
        last = l == w_in.shape[0] - 1
        x = _ffn(x.reshape(b * s, d), row(norm_ffn[l]), w_gate[l].astype(BF16), w_up[l].astype(BF16),
                 w_down[l].astype(BF16), row(norm_final) if last else None).reshape(b, s, d)
    return x
```

```python
import functools
import math

import numpy as np
import jax
import jax.numpy as jnp
from jax import lax
from jax.experimental import pallas as pl
from jax.experimental.pallas import tpu as pltpu

F32 = jnp.float32
BF16 = jnp.bfloat16

EPS = 1e-6
ROPE_THETA = 10000.0
A_HEADS = 8
A_HEAD_DIM = 64
A_WIDTH = A_HEADS * A_HEAD_DIM
MOBA_BLOCK = 256
MOBA_TOPK = 3
R_HEADS = 4
R_QK_DIM = 128
R_V_DIM = 256
R_QK_WIDTH = R_HEADS * R_QK_DIM
R_V_WIDTH = R_HEADS * R_V_DIM
R_CHUNK = 256
X_HEADS = 4
COL_AQ, COL_AK, COL_AV = 0, A_WIDTH, 2 * A_WIDTH
COL_RQ = 3 * A_WIDTH
COL_RK = COL_RQ + R_QK_WIDTH
COL_RV = COL_RK + R_QK_WIDTH
COL_GATES = COL_RV + R_V_WIDTH

LANES = 128
SUBLANES = 8
BF16_ROWS = 16
VMEM_LIMIT = 56 * 1024 * 1024
assert 2 * A_HEAD_DIM == LANES and R_QK_DIM == LANES

TOKEN_TILE = MOBA_BLOCK
QKV_BLOCKS = 4
MIX_BLOCKS = 2
MIX_LATE_COLS = 256
ROW_TILE = 1024
CROSS_PARTS = 4
FF_CHUNK = 256
MOBA_ROWS = 2
MOBA_UNROLL = 4
MOBA_LOOKAHEAD = 7
MOBA_ONES_ROWS = BF16_ROWS

NEG = -1e30
LOG2_E = math.log2(math.e)


def _dot(a, b):
    return jnp.dot(a, b, preferred_element_type=F32)


def _rms(x, g):
    return x * lax.rsqrt(jnp.mean(x * x, axis=-1, keepdims=True) + EPS) * g


def _sigmoid(x):
    return 1.0 / (1.0 + jnp.exp(-x))


def _params(*sem):
    return pltpu.CompilerParams(dimension_semantics=sem, vmem_limit_bytes=VMEM_LIMIT)


def _full(shape):
    n = len(shape)
    return pl.BlockSpec(shape, lambda *_: (0,) * n, pipeline_mode=pl.Buffered(1))


def _qkv_kernel(*refs):
    for sb in range(QKV_BLOCKS):
        _qkv_block(sb, *refs)


def _qkv_block(sb, x_ref, g_ref, w_ref, wqt_ref, wvt_ref, wkt_ref, cosa_ref, sina_ref, cosr_ref, sinr_ref,
               costa_ref, sinta_ref, costr_ref, sintr_ref, kdec_ref,
               k_ref, kmean_ref, qt_ref, vt_ref, rq_ref, rv_ref, rkt_ref, rktd_ref):
    t = TOKEN_TILE
    tok = slice(sb * t, (sb + 1) * t)
    h = _rms(x_ref[0, tok, :], g_ref[...])
    hb = h.astype(BF16)
    ht = h.T.astype(BF16)
    lane = lax.broadcasted_iota(jnp.int32, (t, LANES), 1)
    a_half, r_half = A_HEAD_DIM // 2, R_QK_DIM // 2
    first_half = (lane & a_half) == 0

    ak = _dot(hb, w_ref[:, COL_AK:COL_AV])
    cosa, sina = cosa_ref[tok, :], sina_ref[tok, :]
    for c in range(A_WIDTH // LANES):
        blk = ak[:, c * LANES:(c + 1) * LANES]
        rot = jnp.where(first_half, pltpu.roll(blk, LANES - a_half, 1), pltpu.roll(blk, a_half, 1))
        kr = blk * cosa + rot * sina
        k_ref[0, sb, :, c * LANES:(c + 1) * LANES] = kr.astype(BF16)
        kmean_ref[0, sb, :, c * LANES:(c + 1) * LANES] = jnp.mean(kr, axis=0, keepdims=True)

    rq = _dot(hb, w_ref[:, COL_RQ:COL_RK])
    cosr, sinr = cosr_ref[tok, :], sinr_ref[tok, :]
    for c in range(R_HEADS):
        blk = rq[:, c * R_QK_DIM:(c + 1) * R_QK_DIM]
        rq_ref[0, tok, c * R_QK_DIM:(c + 1) * R_QK_DIM] = (blk * cosr + pltpu.roll(blk, r_half, 1) * sinr).astype(BF16)

    rv_ref[0, tok, :] = _dot(hb, w_ref[:, COL_RV:COL_GATES]).astype(BF16)

    qt = _dot(wqt_ref[...], ht)
    cost, sint = costa_ref[:, tok], sinta_ref[:, tok]
    half = A_HEAD_DIM // 2
    scale_a = A_HEAD_DIM ** -0.5 * LOG2_E
    for hd in range(A_HEADS):
        x1 = qt[hd * A_HEAD_DIM:hd * A_HEAD_DIM + half]
        x2 = qt[hd * A_HEAD_DIM + half:(hd + 1) * A_HEAD_DIM]
        qt_ref[0, sb, hd * A_HEAD_DIM:hd * A_HEAD_DIM + half, :] = ((x1 * cost - x2 * sint) * scale_a).astype(BF16)
        qt_ref[0, sb, hd * A_HEAD_DIM + half:(hd + 1) * A_HEAD_DIM, :] = ((x2 * cost + x1 * sint) * scale_a).astype(BF16)

    vt_ref[0, sb] = _dot(wvt_ref[...], ht).astype(BF16)

    rkt = _dot(wkt_ref[...], ht)
    cost, sint = costr_ref[:, tok], sintr_ref[:, tok]
    half = R_QK_DIM // 2
    scale_r = R_QK_DIM ** -0.5
    for hd in range(R_HEADS):
        x1 = rkt[hd * R_QK_DIM:hd * R_QK_DIM + half]
        x2 = rkt[hd * R_QK_DIM + half:(hd + 1) * R_QK_DIM]
        o1 = (x1 * cost - x2 * sint) * scale_r
        o2 = (x2 * cost + x1 * sint) * scale_r
        dec = kdec_ref[hd:hd + 1, tok]
        rkt_ref[0, sb, hd * R_QK_DIM:hd * R_QK_DIM + half, :] = o1.astype(BF16)
        rkt_ref[0, sb, hd * R_QK_DIM + half:(hd + 1) * R_QK_DIM, :] = o2.astype(BF16)
        rktd_ref[0, sb, hd * R_QK_DIM:hd * R_QK_DIM + half, :] = (o1 * dec).astype(BF16)
        rktd_ref[0, sb, hd * R_QK_DIM + half:(hd + 1) * R_QK_DIM, :] = (o2 * dec).astype(BF16)


def _qkv(x, g, w, wqt, wvt, wkt, tabs):
    b, s, d = x.shape
    nb = s // TOKEN_TILE
    qb = QKV_BLOCKS
    t = qb * TOKEN_TILE
    tok = lambda w: pl.BlockSpec((1, t, w), lambda bi, i: (bi, i, 0))
    blk_t = lambda r: pl.BlockSpec((1, qb, r, TOKEN_TILE), lambda bi, i: (bi, i, 0, 0))
    nat_tab = pl.BlockSpec((t, LANES), lambda bi, i: (i, 0))
    tr_tab = lambda r: pl.BlockSpec((r, t), lambda bi, i: (0, i))
    return pl.pallas_call(
        _qkv_kernel,
        grid=(b, nb // qb),
        in_specs=[tok(d), _full((1, d)), _full(w.shape), _full(wqt.shape), _full(wvt.shape), _full(wkt.shape),
                  nat_tab, nat_tab, nat_tab, nat_tab,
                  tr_tab(A_HEAD_DIM // 2), tr_tab(A_HEAD_DIM // 2),
                  tr_tab(R_QK_DIM // 2), tr_tab(R_QK_DIM // 2), tr_tab(SUBLANES)],
        out_specs=[pl.BlockSpec((1, qb, TOKEN_TILE, A_WIDTH), lambda bi, i: (bi, i, 0, 0)),
                   pl.BlockSpec((1, qb, 1, A_WIDTH), lambda bi, i: (bi, i, 0, 0)),
                   blk_t(A_WIDTH), blk_t(A_WIDTH), tok(R_QK_WIDTH), tok(R_V_WIDTH),
                   blk_t(R_QK_WIDTH), blk_t(R_QK_WIDTH)],
        out_shape=[jax.ShapeDtypeStruct((b, nb, TOKEN_TILE, A_WIDTH), BF16),
                   jax.ShapeDtypeStruct((b, nb, 1, A_WIDTH), F32),
                   jax.ShapeDtypeStruct((b, nb, A_WIDTH, TOKEN_TILE), BF16),
                   jax.ShapeDtypeStruct((b, nb, A_WIDTH, TOKEN_TILE), BF16),
                   jax.ShapeDtypeStruct((b, s, R_QK_WIDTH), BF16),
                   jax.ShapeDtypeStruct((b, s, R_V_WIDTH), BF16),
                   jax.ShapeDtypeStruct((b, nb, R_QK_WIDTH, TOKEN_TILE), BF16),
                   jax.ShapeDtypeStruct((b, nb, R_QK_WIDTH, TOKEN_TILE), BF16)],
        compiler_params=_params("parallel", "parallel"),
        name="qkv",
    )(x, g, w, wqt, wvt, wkt, *tabs)


def _moba_kernel(qt_ref, k_ref, vt_ref, km_ref, o_ref, qm_ref, m_ref, acc_ref, ot_ref, s_ref):
    i = pl.program_id(1)
    nb = km_ref.shape[1]
    t = MOBA_BLOCK
    dh = A_HEAD_DIM
    row = lax.broadcasted_iota(jnp.int32, (2 * dh, t), 0)
    blk = lax.broadcasted_iota(jnp.int32, (nb, t), 0)
    past = blk < i
    causal = (lax.broadcasted_iota(jnp.int32, (t, t), 0) <= lax.broadcasted_iota(jnp.int32, (t, t), 1))
    ones = jnp.ones((MOBA_ONES_ROWS, t), BF16)
    lanes = lambda hd: slice((hd // 2) * LANES, (hd // 2 + 1) * LANES)
    rows = lambda hd: slice(hd * dh, (hd + 1) * dh)
    row_heads = [(b, hd) for b in range(MOBA_ROWS) for hd in range(A_HEADS)]

    def select_blocks(b, hd):
        hh = hd % 2
        qt_pair = qt_ref[b, 0, lanes(hd), :]
        mine = (row >= hh * dh) & (row < (hh + 1) * dh)
        qt = jnp.where(mine, qt_pair, jnp.zeros_like(qt_pair))
        qm_ref[b, hd, 0:2 * dh, :] = qt

        km = km_ref[b, :, lanes(hd)]
        km_hi = km.astype(BF16)
        km_lo = (km - km_hi.astype(F32)).astype(BF16)
        bs = jnp.where(past, _dot(km_hi, qt) + _dot(km_lo, qt), -jnp.inf)
        picked = jnp.zeros((nb, t), jnp.bool_)
        for _ in range(MOBA_TOPK):
            best = jnp.max(bs, axis=0, keepdims=True)
            first = jnp.min(jnp.where(bs == best, blk, nb), axis=0, keepdims=True)
            hit = blk == first
            picked = picked | hit
            bs = jnp.where(hit, -jnp.inf, bs)
        bias = jnp.where(past & picked, 0.0, NEG).astype(BF16)
        qm_ref[b, hd, 2 * dh:, :] = jnp.concatenate([bias, jnp.zeros((2 * dh - nb, t), BF16)], axis=0)

    key_lane = lax.broadcasted_iota(jnp.int32, (t, 2 * dh), 1)

    def keys_with_block_column(b, j):
        onehot = jnp.where(key_lane == j, 1.0, 0.0).astype(BF16)
        return [jnp.concatenate([k_ref[b, j, :, lanes(2 * p)], onehot], axis=1) for p in range(A_HEADS // 2)]

    def pv(b, j, hd, p):
        vt_aug = jnp.concatenate([vt_ref[b, j, rows(hd), :], ones], axis=0)
        return _dot(vt_aug, p.astype(BF16))

    def own_unit(b, hd):
        def scores():
            return jnp.where(causal, _dot(k_ref[b, i, :, lanes(hd)], qm_ref[b, hd, 0:2 * dh, :]), NEG)

        def update(s, m0):
            m_ref[b, hd] = jnp.broadcast_to(m0, (SUBLANES, t))
            acc_ref[b, hd] = pv(b, i, hd, jnp.exp2(s - m0))

        return scores, update

    def past_units(b, j):
        keys = keys_with_block_column(b, j)

        def unit(hd):
            def scores():
                return _dot(keys[hd // 2], qm_ref[b, hd])

            def update(s, m_blk):
                m_old = m_ref[b, hd]
                m_new = jnp.maximum(m_old, m_blk)
                alpha = jnp.exp2(m_old - m_new)
                m_ref[b, hd] = m_new
                acc_ref[b, hd] = alpha[0:1, :] * acc_ref[b, hd] + pv(b, j, hd, jnp.exp2(s - m_new[0:1, :]))

            return scores, update

        return [unit(hd) for hd in range(A_HEADS)]

    def pipelined(units):
        blk_max = {}
        slots = MOBA_LOOKAHEAD + 1
        for step in range(len(units) + MOBA_LOOKAHEAD):
            if step < len(units):
                sc = units[step][0]()
                s_ref[step % slots] = sc
                blk_max[step] = jnp.max(sc, axis=0, keepdims=True)
            if step >= MOBA_LOOKAHEAD:
                u = step - MOBA_LOOKAHEAD
                units[u][1](s_ref[u % slots], blk_max.pop(u))

    n_groups = i // MOBA_UNROLL
    for left in range(MOBA_UNROLL):
        @pl.when(i % MOBA_UNROLL == left)
        def _(left=left):
            for b, hd in row_heads:
                select_blocks(b, hd)
            units = [own_unit(b, hd) for b, hd in row_heads]
            for r in range(left):
                for b in range(MOBA_ROWS):
                    units += past_units(b, n_groups * MOBA_UNROLL + r)
            pipelined(units)

    def group_body(jj, carry):
        pipelined([u for r in range(MOBA_UNROLL) for b in range(MOBA_ROWS)
                   for u in past_units(b, MOBA_UNROLL * jj + r)])
        return carry

    lax.fori_loop(0, n_groups, group_body, 0)

    for b in range(MOBA_ROWS):
        for hd in range(A_HEADS):
            ot_ref[rows(hd), :] = acc_ref[b, hd, 0:dh, :] / acc_ref[b, hd, dh:dh + 1, :]
        o_ref[b] = ot_ref[...].T.astype(BF16)


def _moba(qt, k, vt, kmean):
    b, nb, t, w = k.shape
    r = MOBA_ROWS
    assert b % r == 0
    return pl.pallas_call(
        _moba_kernel,
        grid=(b // r, nb),
        in_specs=[pl.BlockSpec((r, 1, w, t), lambda bi, i: (bi, i, 0, 0)),
                  pl.BlockSpec((r, nb, t, w), lambda bi, i: (bi, 0, 0, 0)),
                  pl.BlockSpec((r, nb, w, t), lambda bi, i: (bi, 0, 0, 0)),
                  pl.BlockSpec((r, nb, w), lambda bi, i: (bi, 0, 0))],
        out_specs=pl.BlockSpec((r, t, w), lambda bi, i: (bi, i, 0)),
        out_shape=jax.ShapeDtypeStruct((b, nb * t, w), BF16),
        scratch_shapes=[pltpu.VMEM((r, A_HEADS, 4 * A_HEAD_DIM, t), BF16),
                        pltpu.VMEM((r, A_HEADS, SUBLANES, t), F32),
                        pltpu.VMEM((r, A_HEADS, A_HEAD_DIM + MOBA_ONES_ROWS, t), F32),
                        pltpu.VMEM((w, t), F32),
                        pltpu.VMEM((MOBA_LOOKAHEAD + 1, t, t), F32)],
        compiler_params=_params("parallel", "parallel"),
        name="moba",
    )(qt, k, vt, kmean)


def _mix_kernel(chunk_dec, x_ref, g_ref, rq_ref, rkt_ref, rktd_ref, rv_ref, ya_ref, dmask_ref, qdec_ref,
                w_ref, wa_ref, wb_ref, wo_ref, o_ref, state_ref, yb_ref):
    @pl.when(pl.program_id(1) == 0)
    def _():
        state_ref[...] = jnp.zeros_like(state_ref)

    x = x_ref[0]
    d = x.shape[1]
    hb = _rms(x, g_ref[...]).astype(BF16)
    c = R_CHUNK
    per_blk = TOKEN_TILE // c
    n_chunks = MIX_BLOCKS * per_blk
    rows = lambda ci: slice(ci * c, (ci + 1) * c)
    qk = lambda hd: slice(hd * R_QK_DIM, (hd + 1) * R_QK_DIM)
    vv = lambda hd: slice(hd * R_V_DIM, (hd + 1) * R_V_DIM)
    units = [(ci, hd) for ci in range(n_chunks) for hd in range(R_HEADS)]
    keys_t = lambda ref, ci, hd: ref[0, ci // per_blk, qk(hd), rows(ci % per_blk)]

    att = {u: _dot(rq_ref[0, rows(u[0]), qk(u[1])], keys_t(rkt_ref, *u)) for u in units}
    kv = {u: _dot(keys_t(rktd_ref, *u), rv_ref[0, rows(u[0]), vv(u[1])]) for u in units}
    rg = _dot(hb, w_ref[:, COL_GATES:COL_GATES + R_V_WIDTH])
    w = MIX_LATE_COLS
    late = ([lambda c=c: _dot(ya_ref[0], wa_ref[:, c:c + w]) for c in range(0, d, w)]
            + [lambda c=c: _dot(hb, w_ref[:, c:c + w])
               for c in range(COL_GATES + R_V_WIDTH, COL_GATES + R_V_WIDTH + 2 * d, w)])
    late_out = []

    cross = {}
    for hd in range(R_HEADS):
        st = state_ref[hd]
        for ci in range(n_chunks):
            cross[ci, hd] = _dot(rq_ref[0, rows(ci), qk(hd)], st.astype(BF16))
            st = chunk_dec[hd] * st + kv[ci, hd]
        state_ref[hd] = st

    gate = rg * _sigmoid(rg)
    for n, (ci, hd) in enumerate(units):
        inner = _dot((att[ci, hd] * dmask_ref[hd]).astype(BF16), rv_ref[0, rows(ci), vv(hd)])
        y = inner + cross[ci, hd] * qdec_ref[hd]
        mu = jnp.mean(y, axis=-1, keepdims=True)
        yc = y - mu
        var = jnp.mean(yc * yc, axis=-1, keepdims=True)
        yb_ref[rows(ci), vv(hd)] = (yc * lax.rsqrt(var + EPS) * gate[rows(ci), vv(hd)]).astype(BF16)
        for k in range(n * len(late) // len(units), (n + 1) * len(late) // len(units)):
            late_out.append(late[k]())

    pieces = d // w
    ta, ga, gb = (jnp.concatenate(late_out[k * pieces:(k + 1) * pieces], axis=1) for k in range(3))
    blocks = [slice(r * TOKEN_TILE, (r + 1) * TOKEN_TILE) for r in range(MIX_BLOCKS)]
    tb = [_dot(yb_ref[tok, :], wb_ref[...]) for tok in blocks]
    for tok, tb_blk in zip(blocks, tb):
        merged = _sigmoid(ga[tok]) * ta[tok] + _sigmoid(gb[tok]) * tb_blk
        o_ref[0, tok, :] = x[tok] + _dot(merged.astype(BF16), wo_ref[...])


def _mix(x, g, rq, rkt, rktd, rv, ya, dmask, qdec, chunk_dec, w, wa, wb, wo):
    b, s, d = x.shape
    t = MIX_BLOCKS * TOKEN_TILE
    tok = lambda w: pl.BlockSpec((1, t, w), lambda bi, i: (bi, i, 0))
    blk_t = pl.BlockSpec((1, MIX_BLOCKS, R_QK_WIDTH, TOKEN_TILE), lambda bi, i: (bi, i, 0, 0))
    return pl.pallas_call(
        functools.partial(_mix_kernel, chunk_dec),
        grid=(b, s // t),
        in_specs=[tok(d), _full((1, d)), tok(R_QK_WIDTH), blk_t, blk_t, tok(R_V_WIDTH), tok(A_WIDTH),
                  _full(dmask.shape), _full(qdec.shape),
                  _full(w.shape), _full(wa.shape), _full(wb.shape), _full(wo.shape)],
        out_specs=tok(d),
        out_shape=jax.ShapeDtypeStruct((b, s, d), F32),
        scratch_shapes=[pltpu.VMEM((R_HEADS, R_QK_DIM, R_V_DIM), F32),
                        pltpu.VMEM((t, R_V_WIDTH), BF16)],
        compiler_params=_params("parallel", "arbitrary"),
        name="mix",
    )(x, g, rq, rkt, rktd, rv, ya, dmask, qdec, w, wa, wb, wo)


def _cross_kernel(x_ref, g_ref, m_ref, gm_ref, wkv_ref, wq_ref, wo_ref, o_ref, qk_ref, vo_ref, p_ref):
    d = x_ref.shape[2]
    dh = d // X_HEADS
    mlen = m_ref.shape[1]
    cols = lambda hd: slice(hd * dh, (hd + 1) * dh)
    mcols = lambda hd: slice(hd * mlen, (hd + 1) * mlen)

    @pl.when(pl.program_id(1) == 0)
    def _():
        mb = _rms(m_ref[0], gm_ref[...]).astype(BF16)
        kt = _dot(mb, wkv_ref[:, 0:d]).T.astype(BF16)
        v = _dot(mb, wkv_ref[:, d:]).astype(BF16)
        for hd in range(X_HEADS):
            qk_ref[:, mcols(hd)] = (_dot(wq_ref[:, cols(hd)], kt[cols(hd), :]) * dh ** -0.5).astype(BF16)
            vo_ref[mcols(hd), :] = _dot(v[:, cols(hd)], wo_ref[cols(hd), :]).astype(BF16)

    n_parts = CROSS_PARTS
    t = x_ref.shape[1] // n_parts
    part = lambda r: slice(r * t, (r + 1) * t)

    def scores_of(r):
        hb = _rms(x_ref[0, part(r), :], g_ref[...]).astype(BF16)
        return _dot(hb, qk_ref[...])

    nxt = scores_of(0)
    for r in range(n_parts):
        scores = nxt
        if r + 1 < n_parts:
            nxt = scores_of(r + 1)
        for hd in range(X_HEADS):
            s = scores[:, mcols(hd)]
            p = jnp.exp(s - jnp.max(s, axis=-1, keepdims=True))
            p_ref[part(r), mcols(hd)] = (p / jnp.sum(p, axis=-1, keepdims=True)).astype(BF16)
        o_ref[0, part(r), :] = x_ref[0, part(r), :] + _dot(p_ref[part(r), :], vo_ref[...])


def _cross(x, g, mem, gm, wkv, wq, wo):
    b, s, d = x.shape
    m = mem.shape[1]
    t = ROW_TILE
    tok = pl.BlockSpec((1, t, d), lambda bi, i: (bi, i, 0))
    return pl.pallas_call(
        _cross_kernel,
        grid=(b, s // t),
        in_specs=[tok, _full((1, d)), pl.BlockSpec((1, m, d), lambda bi, i: (bi, 0, 0)), _full((1, d)),
                  _full(wkv.shape), _full(wq.shape), _full(wo.shape)],
        out_specs=tok,
        out_shape=jax.ShapeDtypeStruct((b, s, d), F32),
        scratch_shapes=[pltpu.VMEM((d, X_HEADS * m), BF16),
                        pltpu.VMEM((X_HEADS * m, d), BF16),
                        pltpu.VMEM((t, X_HEADS * m), BF16)],
        compiler_params=_params("parallel", "arbitrary"),
        name="cross",
    )(x, g, mem, gm, wkv, wq, wo)


def _ffn_kernel(x_ref, g_ref, wg_ref, wu_ref, wd_ref, *rest):
    o_ref = rest[-1]
    x = x_ref[...]
    hb = _rms(x, g_ref[...]).astype(BF16)
    n_chunks = wg_ref.shape[1] // FF_CHUNK
    cols = lambda c: slice(c * FF_CHUNK, (c + 1) * FF_CHUNK)

    def up(c):
        return _dot(hb, wg_ref[:, cols(c)]), _dot(hb, wu_ref[:, cols(c)])

    acc = x
    nxt = up(0)
    for c in range(n_chunks):
        gate, upv = nxt
        if c + 1 < n_chunks:
            nxt = up(c + 1)
        act = gate * _sigmoid(gate) * upv
        acc = acc + _dot(act.astype(BF16), wd_ref[cols(c), :])
    o_ref[...] = _rms(acc, rest[0][...]) if len(rest) == 2 else acc


def _ffn(x2, g, wg, wu, wd, final_gain=None):
    n, d = x2.shape
    t = ROW_TILE
    tok = pl.BlockSpec((t, d), lambda i: (i, 0))
    final = () if final_gain is None else (final_gain,)
    return pl.pallas_call(
        _ffn_kernel,
        grid=(n // t,),
        in_specs=[tok, _full((1, d)), _full(wg.shape), _full(wu.shape), _full(wd.shape)] + [_full((1, d))] * len(final),
        out_specs=tok,
        out_shape=jax.ShapeDtypeStruct((n, d), F32),
        compiler_params=_params("parallel"),
        name="ffn",
    )(x2, g, wg, wu, wd, *final)


def _rope_tables(s):
    def cs(dim):
        inv = ROPE_THETA ** (-jnp.arange(0, dim, 2, dtype=F32) / dim)
        ang = jnp.arange(s).astype(F32)[:, None] * inv[None, :]
        return jnp.cos(ang), jnp.sin(ang)

    ca, sa = cs(A_HEAD_DIM)
    cr, sr = cs(R_QK_DIM)
    sign = lambda half: jnp.asarray(np.tile(np.repeat([-1.0, 1.0], half), LANES // (2 * half))[None, :], F32)
    return (jnp.tile(ca, (1, 4)), jnp.tile(sa, (1, 4)) * sign(A_HEAD_DIM // 2),
            jnp.tile(cr, (1, 2)), jnp.tile(sr, (1, 2)) * sign(R_QK_DIM // 2),
            ca.T, sa.T, cr.T, sr.T)


def _decay_tables(s):
    c = R_CHUNK
    log_g = np.log(1.0 - np.exp2(-5.0 - np.arange(R_HEADS, dtype=np.float64)))
    i = np.arange(c, dtype=np.float64)
    diff = i[:, None] - i[None, :]
    dmask = np.where(diff >= 0, np.exp(log_g[:, None, None] * np.maximum(diff, 0.0)), 0.0)
    q_dec = np.exp(log_g[:, None] * (i + 1.0))
    k_dec = np.exp(log_g[:, None] * (c - 1.0 - i))
    kdec_t = np.zeros((SUBLANES, s), np.float64)
    kdec_t[:R_HEADS] = np.tile(k_dec, (1, s // c))
    qdec = np.broadcast_to(q_dec[:, :, None], (R_HEADS, c, R_V_DIM))
    chunk_dec = tuple(float(v) for v in np.exp(log_g * c))
    return (jnp.asarray(dmask, F32), jnp.asarray(qdec, F32), jnp.asarray(kdec_t, F32), chunk_dec)


def kernel(x, mem, norm_mix, w_in, w_branch_a, w_branch_b, w_out, norm_cross, norm_mem,
           w_xq, w_xkv, w_xo, norm_ffn, w_gate, w_up, w_down, norm_final):
    b, s, d = x.shape
    steps = (TOKEN_TILE * QKV_BLOCKS, TOKEN_TILE * MIX_BLOCKS, ROW_TILE)
    assert all(s % t == 0 for t in steps) and b % MOBA_ROWS == 0 and d % LANES == 0
    rope = _rope_tables(s)
    dmask, qdec, kdec_t, chunk_dec = _decay_tables(s)
    row = lambda v: v.reshape(1, d)
    assert w_in.shape[2] == COL_GATES + R_V_WIDTH + 2 * d

    for l in range(w_in.shape[0]):
        w = w_in[l].astype(BF16)
        wqt, wvt, wkt = (w[:, c:c + n].T for c, n in ((COL_AQ, A_WIDTH), (COL_AV, A_WIDTH), (COL_RK, R_QK_WIDTH)))
        k_a, kmean, qt_a, vt_a, rq, rv, rkt, rktd = _qkv(x, row(norm_mix[l]), w, wqt, wvt, wkt, rope + (kdec_t,))
        y_a = _moba(qt_a, k_a, vt_a, kmean.reshape(b, s // MOBA_BLOCK, A_WIDTH))
        x = _mix(x, row(norm_mix[l]), rq, rkt, rktd, rv, y_a, dmask, qdec, chunk_dec,
                 w, w_branch_a[l].astype(BF16), w_branch_b[l].astype(BF16), w_out[l].astype(BF16))
        x = _cross(x, row(norm_cross[l]), mem, row(norm_mem[l]), w_xkv[l].astype(BF16),
                   w_xq[l].astype(BF16), w_xo[l].astype(BF16))
        last = l == w_in.shape[0] - 1
        x = _ffn(x.reshape(b * s, d), row(norm_ffn[l]), w_gate[l].astype(BF16), w_up[l].astype(BF16),
                 w_down[l].astype(BF16), row(norm_final) if last else None).reshape(b, s, d)
    return x
```

```python
import functools
import math

import numpy as np
import jax
import jax.numpy as jnp
from jax import lax
from jax.experimental import pallas as pl
from jax.experimental.pallas import tpu as pltpu

F32 = jnp.float32
BF16 = jnp.bfloat16

EPS = 1e-6
ROPE_THETA = 10000.0
A_HEADS = 8
A_HEAD_DIM = 64
A_WIDTH = A_HEADS * A_HEAD_DIM
MOBA_BLOCK = 256
MOBA_TOPK = 3
R_HEADS = 4
R_QK_DIM = 128
R_V_DIM = 256
R_QK_WIDTH = R_HEADS * R_QK_DIM
R_V_WIDTH = R_HEADS * R_V_DIM
R_CHUNK = 256
X_HEADS = 4
COL_AQ, COL_AK, COL_AV = 0, A_WIDTH, 2 * A_WIDTH
COL_RQ = 3 * A_WIDTH
COL_RK = COL_RQ + R_QK_WIDTH
COL_RV = COL_RK + R_QK_WIDTH
COL_GATES = COL_RV + R_V_WIDTH

LANES = 128
SUBLANES = 8
BF16_ROWS = 16
VMEM_LIMIT = 56 * 1024 * 1024
assert 2 * A_HEAD_DIM == LANES and R_QK_DIM == LANES

TOKEN_TILE = MOBA_BLOCK
QKV_BLOCKS = 4
MIX_BLOCKS = 2
MIX_LATE_COLS = 256
ROW_TILE = 1024
CROSS_PARTS = 4
FF_CHUNK = 256
MOBA_ROWS = 2
MOBA_UNROLL = 4
MOBA_LOOKAHEAD = 7
MOBA_ONES_ROWS = BF16_ROWS

NEG = -1e30
LOG2_E = math.log2(math.e)


def _dot(a, b):
    return jnp.dot(a, b, preferred_element_type=F32)


def _rms(x, g):
    return x * lax.rsqrt(jnp.mean(x * x, axis=-1, keepdims=True) + EPS) * g


def _sigmoid(x):
    return 1.0 / (1.0 + jnp.exp(-x))


def _params(*sem):
    return pltpu.CompilerParams(dimension_semantics=sem, vmem_limit_bytes=VMEM_LIMIT)


def _full(shape):
    n = len(shape)
    return pl.BlockSpec(shape, lambda *_: (0,) * n, pipeline_mode=pl.Buffered(1))


def _qkv_kernel(*refs):
    for sb in range(QKV_BLOCKS):
        _qkv_block(sb, *refs)


def _qkv_block(sb, x_ref, g_ref, w_ref, wqt_ref, wvt_ref, wkt_ref, cosa_ref, sina_ref, cosr_ref, sinr_ref,
               costa_ref, sinta_ref, costr_ref, sintr_ref, kdec_ref,
               k_ref, kmean_ref, qt_ref, vt_ref, rq_ref, rv_ref, rkt_ref, rktd_ref):
    t = TOKEN_TILE
    tok = slice(sb * t, (sb + 1) * t)
    h = _rms(x_ref[0, tok, :], g_ref[...])
    hb = h.astype(BF16)
    ht = h.T.astype(BF16)
    lane = lax.broadcasted_iota(jnp.int32, (t, LANES), 1)
    a_half, r_half = A_HEAD_DIM // 2, R_QK_DIM // 2
    first_half = (lane & a_half) == 0

    ak = _dot(hb, w_ref[:, COL_AK:COL_AV])
    cosa, sina = cosa_ref[tok, :], sina_ref[tok, :]
    for c in range(A_WIDTH // LANES):
        blk = ak[:, c * LANES:(c + 1) * LANES]
        rot = jnp.where(first_half, pltpu.roll(blk, LANES - a_half, 1), pltpu.roll(blk, a_half, 1))
        kr = blk * cosa + rot * sina
        k_ref[0, sb, :, c * LANES:(c + 1) * LANES] = kr.astype(BF16)
        kmean_ref[0, sb, :, c * LANES:(c + 1) * LANES] = jnp.mean(kr, axis=0, keepdims=True)

    rq = _dot(hb, w_ref[:, COL_RQ:COL_RK])
    cosr, sinr = cosr_ref[tok, :], sinr_ref[tok, :]
    for c in range(R_HEADS):
        blk = rq[:, c * R_QK_DIM:(c + 1) * R_QK_DIM]
        rq_ref[0, tok, c * R_QK_DIM:(c + 1) * R_QK_DIM] = (blk * cosr + pltpu.roll(blk, r_half, 1) * sinr).astype(BF16)

    rv_ref[0, tok, :] = _dot(hb, w_ref[:, COL_RV:COL_GATES]).astype(BF16)

    qt = _dot(wqt_ref[...], ht)
    cost, sint = costa_ref[:, tok], sinta_ref[:, tok]
    half = A_HEAD_DIM // 2
    scale_a = A_HEAD_DIM ** -0.5 * LOG2_E
    for hd in range(A_HEADS):
        x1 = qt[hd * A_HEAD_DIM:hd * A_HEAD_DIM + half]
        x2 = qt[hd * A_HEAD_DIM + half:(hd + 1) * A_HEAD_DIM]
        qt_ref[0, sb, hd * A_HEAD_DIM:hd * A_HEAD_DIM + half, :] = ((x1 * cost - x2 * sint) * scale_a).astype(BF16)
        qt_ref[0, sb, hd * A_HEAD_DIM + half:(hd + 1) * A_HEAD_DIM, :] = ((x2 * cost + x1 * sint) * scale_a).astype(BF16)

    vt_ref[0, sb] = _dot(wvt_ref[...], ht).astype(BF16)

    rkt = _dot(wkt_ref[...], ht)
    cost, sint = costr_ref[:, tok], sintr_ref[:, tok]
    half = R_QK_DIM // 2
    scale_r = R_QK_DIM ** -0.5
    for hd in range(R_HEADS):
        x1 = rkt[hd * R_QK_DIM:hd * R_QK_DIM + half]
        x2 = rkt[hd * R_QK_DIM + half:(hd + 1) * R_QK_DIM]
        o1 = (x1 * cost - x2 * sint) * scale_r
        o2 = (x2 * cost + x1 * sint) * scale_r
        dec = kdec_ref[hd:hd + 1, tok]
        rkt_ref[0, sb, hd * R_QK_DIM:hd * R_QK_DIM + half, :] = o1.astype(BF16)
        rkt_ref[0, sb, hd * R_QK_DIM + half:(hd + 1) * R_QK_DIM, :] = o2.astype(BF16)
        rktd_ref[0, sb, hd * R_QK_DIM:hd * R_QK_DIM + half, :] = (o1 * dec).astype(BF16)
        rktd_ref[0, sb, hd * R_QK_DIM + half:(hd + 1) * R_QK_DIM, :] = (o2 * dec).astype(BF16)


def _qkv(x, g, w, wqt, wvt, wkt, tabs):
    b, s, d = x.shape
    nb = s // TOKEN_TILE
    qb = QKV_BLOCKS
    t = qb * TOKEN_TILE
    tok = lambda w: pl.BlockSpec((1, t, w), lambda bi, i: (bi, i, 0))
    blk_t = lambda r: pl.BlockSpec((1, qb, r, TOKEN_TILE), lambda bi, i: (bi, i, 0, 0))
    nat_tab = pl.BlockSpec((t, LANES), lambda bi, i: (i, 0))
    tr_tab = lambda r: pl.BlockSpec((r, t), lambda bi, i: (0, i))
    return pl.pallas_call(
        _qkv_kernel,
        grid=(b, nb // qb),
        in_specs=[tok(d), _full((1, d)), _full(w.shape), _full(wqt.shape), _full(wvt.shape), _full(wkt.shape),
                  nat_tab, nat_tab, nat_tab, nat_tab,
                  tr_tab(A_HEAD_DIM // 2), tr_tab(A_HEAD_DIM // 2),
                  tr_tab(R_QK_DIM // 2), tr_tab(R_QK_DIM // 2), tr_tab(SUBLANES)],
        out_specs=[pl.BlockSpec((1, qb, TOKEN_TILE, A_WIDTH), lambda bi, i: (bi, i, 0, 0)),
                   pl.BlockSpec((1, qb, 1, A_WIDTH), lambda bi, i: (bi, i, 0, 0)),
                   blk_t(A_WIDTH), blk_t(A_WIDTH), tok(R_QK_WIDTH), tok(R_V_WIDTH),
                   blk_t(R_QK_WIDTH), blk_t(R_QK_WIDTH)],
        out_shape=[jax.ShapeDtypeStruct((b, nb, TOKEN_TILE, A_WIDTH), BF16),
                   jax.ShapeDtypeStruct((b, nb, 1, A_WIDTH), F32),
                   jax.ShapeDtypeStruct((b, nb, A_WIDTH, TOKEN_TILE), BF16),
                   jax.ShapeDtypeStruct((b, nb, A_WIDTH, TOKEN_TILE), BF16),
                   jax.ShapeDtypeStruct((b, s, R_QK_WIDTH), BF16),
                   jax.ShapeDtypeStruct((b, s, R_V_WIDTH), BF16),
                   jax.ShapeDtypeStruct((b, nb, R_QK_WIDTH, TOKEN_TILE), BF16),
                   jax.ShapeDtypeStruct((b, nb, R_QK_WIDTH, TOKEN_TILE), BF16)],
        compiler_params=_params("parallel", "parallel"),
        name="qkv",
    )(x, g, w, wqt, wvt, wkt, *tabs)


def _moba_kernel(qt_ref, k_ref, vt_ref, km_ref, o_ref, qm_ref, m_ref, acc_ref, ot_ref, s_ref):
    i = pl.program_id(1)
    nb = km_ref.shape[1]
    t = MOBA_BLOCK
    dh = A_HEAD_DIM
    row = lax.broadcasted_iota(jnp.int32, (2 * dh, t), 0)
    blk = lax.broadcasted_iota(jnp.int32, (nb, t), 0)
    past = blk < i
    causal = (lax.broadcasted_iota(jnp.int32, (t, t), 0) <= lax.broadcasted_iota(jnp.int32, (t, t), 1))
    ones = jnp.ones((MOBA_ONES_ROWS, t), BF16)
    lanes = lambda hd: slice((hd // 2) * LANES, (hd // 2 + 1) * LANES)
    rows = lambda hd: slice(hd * dh, (hd + 1) * dh)
    row_heads = [(b, hd) for b in range(MOBA_ROWS) for hd in range(A_HEADS)]

    def select_blocks(b, hd):
        hh = hd % 2
        qt_pair = qt_ref[b, 0, lanes(hd), :]
        mine = (row >= hh * dh) & (row < (hh + 1) * dh)
        qt = jnp.where(mine, qt_pair, jnp.zeros_like(qt_pair))
        qm_ref[b, hd, 0:2 * dh, :] = qt

        km = km_ref[b, :, lanes(hd)]
        km_hi = km.astype(BF16)
        km_lo = (km - km_hi.astype(F32)).astype(BF16)
        bs = jnp.where(past, _dot(km_hi, qt) + _dot(km_lo, qt), -jnp.inf)
        picked = jnp.zeros((nb, t), jnp.bool_)
        for _ in range(MOBA_TOPK):
            best = jnp.max(bs, axis=0, keepdims=True)
            first = jnp.min(jnp.where(bs == best, blk, nb), axis=0, keepdims=True)
            hit = blk == first
            picked = picked | hit
            bs = jnp.where(hit, -jnp.inf, bs)
        bias = jnp.where(past & picked, 0.0, NEG).astype(BF16)
        qm_ref[b, hd, 2 * dh:, :] = jnp.concatenate([bias, jnp.zeros((2 * dh - nb, t), BF16)], axis=0)

    key_lane = lax.broadcasted_iota(jnp.int32, (t, 2 * dh), 1)

    def keys_with_block_column(b, j):
        onehot = jnp.where(key_lane == j, 1.0, 0.0).astype(BF16)
        return [jnp.concatenate([k_ref[b, j, :, lanes(2 * p)], onehot], axis=1) for p in range(A_HEADS // 2)]

    def pv(b, j, hd, p):
        vt_aug = jnp.concatenate([vt_ref[b, j, rows(hd), :], ones], axis=0)
        return _dot(vt_aug, p.astype(BF16))

    def own_unit(b, hd):
        def scores():
            return jnp.where(causal, _dot(k_ref[b, i, :, lanes(hd)], qm_ref[b, hd, 0:2 * dh, :]), NEG)

        def update(s, m0):
            m_ref[b, hd] = jnp.broadcast_to(m0, (SUBLANES, t))
            acc_ref[b, hd] = pv(b, i, hd, jnp.exp2(s - m0))

        return scores, update

    def past_units(b, j):
        keys = keys_with_block_column(b, j)

        def unit(hd):
            def scores():
                return _dot(keys[hd // 2], qm_ref[b, hd])

            def update(s, m_blk):
                m_old = m_ref[b, hd]
                m_new = jnp.maximum(m_old, m_blk)
                alpha = jnp.exp2(m_old - m_new)
                m_ref[b, hd] = m_new
                acc_ref[b, hd] = alpha[0:1, :] * acc_ref[b, hd] + pv(b, j, hd, jnp.exp2(s - m_new[0:1, :]))

            return scores, update

        return [unit(hd) for hd in range(A_HEADS)]

    def pipelined(units):
        blk_max = {}
        slots = MOBA_LOOKAHEAD + 1
        for step in range(len(units) + MOBA_LOOKAHEAD):
            if step < len(units):
                sc = units[step][0]()
                s_ref[step % slots] = sc
                blk_max[step] = jnp.max(sc, axis=0, keepdims=True)
            if step >= MOBA_LOOKAHEAD:
                u = step - MOBA_LOOKAHEAD
                units[u][1](s_ref[u % slots], blk_max.pop(u))

    n_groups = i // MOBA_UNROLL
    for left in range(MOBA_UNROLL):
        @pl.when(i % MOBA_UNROLL == left)
        def _(left=left):
            for b, hd in row_heads:
                select_blocks(b, hd)
            units = [own_unit(b, hd) for b, hd in row_heads]
            for r in range(left):
                for b in range(MOBA_ROWS):
                    units += past_units(b, n_groups * MOBA_UNROLL + r)
            pipelined(units)

    def group_body(jj, carry):
        pipelined([u for r in range(MOBA_UNROLL) for b in range(MOBA_ROWS)
                   for u in past_units(b, MOBA_UNROLL * jj + r)])
        return carry

    lax.fori_loop(0, n_groups, group_body, 0)

    for b in range(MOBA_ROWS):
        for hd in range(A_HEADS):
            ot_ref[rows(hd), :] = acc_ref[b, hd, 0:dh, :] / acc_ref[b, hd, dh:dh + 1, :]
        o_ref[b] = ot_ref[...].T.astype(BF16)


def _moba(qt, k, vt, kmean):
    b, nb, t, w = k.shape
    r = MOBA_ROWS
    assert b % r == 0
    return pl.pallas_call(
        _moba_kernel,
        grid=(b // r, nb),
        in_specs=[pl.BlockSpec((r, 1, w, t), lambda bi, i: (bi, i, 0, 0)),
                  pl.BlockSpec((r, nb, t, w), lambda bi, i: (bi, 0, 0, 0)),
                  pl.BlockSpec((r, nb, w, t), lambda bi, i: (bi, 0, 0, 0)),
                  pl.BlockSpec((r, nb, w), lambda bi, i: (bi, 0, 0))],
        out_specs=pl.BlockSpec((r, t, w), lambda bi, i: (bi, i, 0)),
        out_shape=jax.ShapeDtypeStruct((b, nb * t, w), BF16),
        scratch_shapes=[pltpu.VMEM((r, A_HEADS, 4 * A_HEAD_DIM, t), BF16),
                        pltpu.VMEM((r, A_HEADS, SUBLANES, t), F32),
                        pltpu.VMEM((r, A_HEADS, A_HEAD_DIM + MOBA_ONES_ROWS, t), F32),
                        pltpu.VMEM((w, t), F32),
                        pltpu.VMEM((MOBA_LOOKAHEAD + 1, t, t), F32)],
        compiler_params=_params("parallel", "parallel"),
        name="moba",
    )(qt, k, vt, kmean)


def _mix_kernel(chunk_dec, x_ref, g_ref, rq_ref, rkt_ref, rktd_ref, rv_ref, ya_ref, dmask_ref, qdec_ref,
                w_ref, wa_ref, wb_ref, wo_ref, o_ref, state_ref, yb_ref):
    @pl.when(pl.program_id(1) == 0)
    def _():
        state_ref[...] = jnp.zeros_like(state_ref)

    x = x_ref[0]
    d = x.shape[1]
    hb = _rms(x, g_ref[...]).astype(BF16)
    c = R_CHUNK
    per_blk = TOKEN_TILE // c
    n_chunks = MIX_BLOCKS * per_blk
    rows = lambda ci: slice(ci * c, (ci + 1) * c)
    qk = lambda hd: slice(hd * R_QK_DIM, (hd + 1) * R_QK_DIM)
    vv = lambda hd: slice(hd * R_V_DIM, (hd + 1) * R_V_DIM)
    units = [(ci, hd) for ci in range(n_chunks) for hd in range(R_HEADS)]
    keys_t = lambda ref, ci, hd: ref[0, ci // per_blk, qk(hd), rows(ci % per_blk)]

    att = {u: _dot(rq_ref[0, rows(u[0]), qk(u[1])], keys_t(rkt_ref, *u)) for u in units}
    kv = {u: _dot(keys_t(rktd_ref, *u), rv_ref[0, rows(u[0]), vv(u[1])]) for u in units}
    rg = _dot(hb, w_ref[:, COL_GATES:COL_GATES + R_V_WIDTH])
    w = MIX_LATE_COLS
    late = ([lambda c=c: _dot(ya_ref[0], wa_ref[:, c:c + w]) for c in range(0, d, w)]
            + [lambda c=c: _dot(hb, w_ref[:, c:c + w])
               for c in range(COL_GATES + R_V_WIDTH, COL_GATES + R_V_WIDTH + 2 * d, w)])
    late_out = []

    cross = {}
    for hd in range(R_HEADS):
        st = state_ref[hd]
        for ci in range(n_chunks):
            cross[ci, hd] = _dot(rq_ref[0, rows(ci), qk(hd)], st.astype(BF16))
            st = chunk_dec[hd] * st + kv[ci, hd]
        state_ref[hd] = st

    gate = rg * _sigmoid(rg)
    for n, (ci, hd) in enumerate(units):
        inner = _dot((att[ci, hd] * dmask_ref[hd]).astype(BF16), rv_ref[0, rows(ci), vv(hd)])
        y = inner + cross[ci, hd] * qdec_ref[hd]
        mu = jnp.mean(y, axis=-1, keepdims=True)
        yc = y - mu
        var = jnp.mean(yc * yc, axis=-1, keepdims=True)
        yb_ref[rows(ci), vv(hd)] = (yc * lax.rsqrt(var + EPS) * gate[rows(ci), vv(hd)]).astype(BF16)
        for k in range(n * len(late) // len(units), (n + 1) * len(late) // len(units)):
            late_out.append(late[k]())

    pieces = d // w
    ta, ga, gb = (jnp.concatenate(late_out[k * pieces:(k + 1) * pieces], axis=1) for k in range(3))
    blocks = [slice(r * TOKEN_TILE, (r + 1) * TOKEN_TILE) for r in range(MIX_BLOCKS)]
    tb = [_dot(yb_ref[tok, :], wb_ref[...]) for tok in blocks]
    for tok, tb_blk in zip(blocks, tb):
        merged = _sigmoid(ga[tok]) * ta[tok] + _sigmoid(gb[tok]) * tb_blk
        o_ref[0, tok, :] = x[tok] + _dot(merged.astype(BF16), wo_ref[...])


def _mix(x, g, rq, rkt, rktd, rv, ya, dmask, qdec, chunk_dec, w, wa, wb, wo):
    b, s, d = x.shape
    t = MIX_BLOCKS * TOKEN_TILE
    tok = lambda w: pl.BlockSpec((1, t, w), lambda bi, i: (bi, i, 0))
    blk_t = pl.BlockSpec((1, MIX_BLOCKS, R_QK_WIDTH, TOKEN_TILE), lambda bi, i: (bi, i, 0, 0))
    return pl.pallas_call(
        functools.partial(_mix_kernel, chunk_dec),
        grid=(b, s // t),
        in_specs=[tok(d), _full((1, d)), tok(R_QK_WIDTH), blk_t, blk_t, tok(R_V_WIDTH), tok(A_WIDTH),
                  _full(dmask.shape), _full(qdec.shape),
                  _full(w.shape), _full(wa.shape), _full(wb.shape), _full(wo.shape)],
        out_specs=tok(d),
        out_shape=jax.ShapeDtypeStruct((b, s, d), F32),
        scratch_shapes=[pltpu.VMEM((R_HEADS, R_QK_DIM, R_V_DIM), F32),
                        pltpu.VMEM((t, R_V_WIDTH), BF16)],
        compiler_params=_params("parallel", "arbitrary"),
        name="mix",
    )(x, g, rq, rkt, rktd, rv, ya, dmask, qdec, w, wa, wb, wo)


def _cross_kernel(x_ref, g_ref, m_ref, gm_ref, wkv_ref, wq_ref, wo_ref, *rest):
    n_cast = (len(rest) - 4) // 2
    o_ref = rest[n_cast]
    qk_ref, vo_ref, p_ref = rest[-3:]
    for src, dst in zip(rest[:n_cast], rest[n_cast + 1:2 * n_cast + 1]):
        dst[...] = src[...].astype(BF16)
    d = x_ref.shape[2]
    dh = d // X_HEADS
    mlen = m_ref.shape[1]
    cols = lambda hd: slice(hd * dh, (hd + 1) * dh)
    mcols = lambda hd: slice(hd * mlen, (hd + 1) * mlen)

    @pl.when(pl.program_id(1) == 0)
    def _():
        mb = _rms(m_ref[0], gm_ref[...]).astype(BF16)
        kt = _dot(mb, wkv_ref[:, 0:d]).T.astype(BF16)
        v = _dot(mb, wkv_ref[:, d:]).astype(BF16)
        for hd in range(X_HEADS):
            qk_ref[:, mcols(hd)] = (_dot(wq_ref[:, cols(hd)], kt[cols(hd), :]) * dh ** -0.5).astype(BF16)
            vo_ref[mcols(hd), :] = _dot(v[:, cols(hd)], wo_ref[cols(hd), :]).astype(BF16)

    n_parts = CROSS_PARTS
    t = x_ref.shape[1] // n_parts
    part = lambda r: slice(r * t, (r + 1) * t)

    def scores_of(r):
        hb = _rms(x_ref[0, part(r), :], g_ref[...]).astype(BF16)
        return _dot(hb, qk_ref[...])

    nxt = scores_of(0)
    for r in range(n_parts):
        scores = nxt
        if r + 1 < n_parts:
            nxt = scores_of(r + 1)
        for hd in range(X_HEADS):
            s = scores[:, mcols(hd)]
            p = jnp.exp(s - jnp.max(s, axis=-1, keepdims=True))
            p_ref[part(r), mcols(hd)] = (p / jnp.sum(p, axis=-1, keepdims=True)).astype(BF16)
        o_ref[0, part(r), :] = x_ref[0, part(r), :] + _dot(p_ref[part(r), :], vo_ref[...])


def _cross(x, g, mem, gm, wkv, wq, wo, cast_ahead):
    b, s, d = x.shape
    m = mem.shape[1]
    t = ROW_TILE
    n_steps = b * (s // t)
    tok = pl.BlockSpec((1, t, d), lambda bi, i: (bi, i, 0))
    assert all(w.shape[0] % n_steps == 0 for w in cast_ahead)
    sliced = [w.reshape(n_steps, w.shape[0] // n_steps, w.shape[1]) for w in cast_ahead]
    slice_spec = lambda w: pl.BlockSpec((1,) + w.shape[1:], lambda bi, i: (bi * (s // t) + i, 0, 0))
    out = pl.pallas_call(
        _cross_kernel,
        grid=(b, s // t),
        in_specs=[tok, _full((1, d)), pl.BlockSpec((1, m, d), lambda bi, i: (bi, 0, 0)), _full((1, d)),
                  _full(wkv.shape), _full(wq.shape), _full(wo.shape)] + [slice_spec(w) for w in sliced],
        out_specs=[tok] + [slice_spec(w) for w in sliced],
        out_shape=[jax.ShapeDtypeStruct((b, s, d), F32)] + [jax.ShapeDtypeStruct(w.shape, BF16) for w in sliced],
        scratch_shapes=[pltpu.VMEM((d, X_HEADS * m), BF16),
                        pltpu.VMEM((X_HEADS * m, d), BF16),
                        pltpu.VMEM((t, X_HEADS * m), BF16)],
        compiler_params=_params("parallel", "arbitrary"),
        name="cross",
    )(x, g, mem, gm, wkv, wq, wo, *sliced)
    return out[0], [o.reshape(w.shape) for o, w in zip(out[1:], cast_ahead)]


def _ffn_kernel(x_ref, g_ref, wg_ref, wu_ref, wd_ref, *rest):
    o_ref = rest[-1]
    x = x_ref[...]
    hb = _rms(x, g_ref[...]).astype(BF16)
    n_chunks = wg_ref.shape[1] // FF_CHUNK
    cols = lambda c: slice(c * FF_CHUNK, (c + 1) * FF_CHUNK)

    def up(c):
        return _dot(hb, wg_ref[:, cols(c)]), _dot(hb, wu_ref[:, cols(c)])

    acc = x
    nxt = up(0)
    for c in range(n_chunks):
        gate, upv = nxt
        if c + 1 < n_chunks:
            nxt = up(c + 1)
        act = gate * _sigmoid(gate) * upv
        acc = acc + _dot(act.astype(BF16), wd_ref[cols(c), :])
    o_ref[...] = _rms(acc, rest[0][...]) if len(rest) == 2 else acc


def _ffn(x2, g, wg, wu, wd, final_gain=None):
    n, d = x2.shape
    t = ROW_TILE
    tok = pl.BlockSpec((t, d), lambda i: (i, 0))
    final = () if final_gain is None else (final_gain,)
    return pl.pallas_call(
        _ffn_kernel,
        grid=(n // t,),
        in_specs=[tok, _full((1, d)), _full(wg.shape), _full(wu.shape), _full(wd.shape)] + [_full((1, d))] * len(final),
        out_specs=tok,
        out_shape=jax.ShapeDtypeStruct((n, d), F32),
        compiler_params=_params("parallel"),
        name="ffn",
    )(x2, g, wg, wu, wd, *final)


def _rope_tables(s):
    def cs(dim):
        inv = ROPE_THETA ** (-jnp.arange(0, dim, 2, dtype=F32) / dim)
        ang = jnp.arange(s).astype(F32)[:, None] * inv[None, :]
        return jnp.cos(ang), jnp.sin(ang)

    ca, sa = cs(A_HEAD_DIM)
    cr, sr = cs(R_QK_DIM)
    sign = lambda half: jnp.asarray(np.tile(np.repeat([-1.0, 1.0], half), LANES // (2 * half))[None, :], F32)
    return (jnp.tile(ca, (1, 4)), jnp.tile(sa, (1, 4)) * sign(A_HEAD_DIM // 2),
            jnp.tile(cr, (1, 2)), jnp.tile(sr, (1, 2)) * sign(R_QK_DIM // 2),
            ca.T, sa.T, cr.T, sr.T)


def _decay_tables(s):
    c = R_CHUNK
    log_g = np.log(1.0 - np.exp2(-5.0 - np.arange(R_HEADS, dtype=np.float64)))
    i = np.arange(c, dtype=np.float64)
    diff = i[:, None] - i[None, :]
    dmask = np.where(diff >= 0, np.exp(log_g[:, None, None] * np.maximum(diff, 0.0)), 0.0)
    q_dec = np.exp(log_g[:, None] * (i + 1.0))
    k_dec = np.exp(log_g[:, None] * (c - 1.0 - i))
    kdec_t = np.zeros((SUBLANES, s), np.float64)
    kdec_t[:R_HEADS] = np.tile(k_dec, (1, s // c))
    qdec = np.broadcast_to(q_dec[:, :, None], (R_HEADS, c, R_V_DIM))
    chunk_dec = tuple(float(v) for v in np.exp(log_g * c))
    return (jnp.asarray(dmask, F32), jnp.asarray(qdec, F32), jnp.asarray(kdec_t, F32), chunk_dec)


def kernel(x, mem, norm_mix, w_in, w_branch_a, w_branch_b, w_out, norm_cross, norm_mem,
           w_xq, w_xkv, w_xo, norm_ffn, w_gate, w_up, w_down, norm_final):
    b, s, d = x.shape
    steps = (TOKEN_TILE * QKV_BLOCKS, TOKEN_TILE * MIX_BLOCKS, ROW_TILE)
    assert all(s % t == 0 for t in steps) and b % MOBA_ROWS == 0 and d % LANES == 0
    rope = _rope_tables(s)
    dmask, qdec, kdec_t, chunk_dec = _decay_tables(s)
    row = lambda v: v.reshape(1, d)
    assert w_in.shape[2] == COL_GATES + R_V_WIDTH + 2 * d

    for l in range(w_in.shape[0]):
        w = w_in[l].astype(BF16)
        wqt, wvt, wkt = (w[:, c:c + n].T for c, n in ((COL_AQ, A_WIDTH), (COL_AV, A_WIDTH), (COL_RK, R_QK_WIDTH)))
        k_a, kmean, qt_a, vt_a, rq, rv, rkt, rktd = _qkv(x, row(norm_mix[l]), w, wqt, wvt, wkt, rope + (kdec_t,))
        y_a = _moba(qt_a, k_a, vt_a, kmean.reshape(b, s // MOBA_BLOCK, A_WIDTH))
        x = _mix(x, row(norm_mix[l]), rq, rkt, rktd, rv, y_a, dmask, qdec, chunk_dec,
                 w, w_branch_a[l].astype(BF16), w_branch_b[l].astype(BF16), w_out[l].astype(BF16))
        x, (wg, wu, wd) = _cross(x, row(norm_cross[l]), mem, row(norm_mem[l]), w_xkv[l].astype(BF16),
                                 w_xq[l].astype(BF16), w_xo[l].astype(BF16), (w_gate[l], w_up[l], w_down[l]))
        last = l == w_in.shape[0] - 1
        x = _ffn(x.reshape(b * s, d), row(norm_ffn[l]), wg, wu, wd,
                 row(norm_final) if last else None).reshape(b, s, d)
    return x
```

```python
import functools
import math

import numpy as np
import jax
import jax.numpy as jnp
from jax import lax
from jax.experimental import pallas as pl
from jax.experimental.pallas import tpu as pltpu

F32 = jnp.float32
BF16 = jnp.bfloat16

EPS = 1e-6
ROPE_THETA = 10000.0
A_HEADS = 8
A_HEAD_DIM = 64
A_WIDTH = A_HEADS * A_HEAD_DIM
MOBA_BLOCK = 256
MOBA_TOPK = 3
R_HEADS = 4
R_QK_DIM = 128
R_V_DIM = 256
R_QK_WIDTH = R_HEADS * R_QK_DIM
R_V_WIDTH = R_HEADS * R_V_DIM
R_CHUNK = 256
X_HEADS = 4
COL_AQ, COL_AK, COL_AV = 0, A_WIDTH, 2 * A_WIDTH
COL_RQ = 3 * A_WIDTH
COL_RK = COL_RQ + R_QK_WIDTH
COL_RV = COL_RK + R_QK_WIDTH
COL_GATES = COL_RV + R_V_WIDTH

LANES = 128
SUBLANES = 8
BF16_ROWS = 16
VMEM_LIMIT = 56 * 1024 * 1024
assert 2 * A_HEAD_DIM == LANES and R_QK_DIM == LANES

TOKEN_TILE = MOBA_BLOCK
QKV_BLOCKS = 4
MIX_BLOCKS = 2
MIX_LATE_COLS = 256
ROW_TILE = 1024
CROSS_PARTS = 4
FF_CHUNK = 256
MOBA_ROWS = 2
MOBA_UNROLL = 4
MOBA_LOOKAHEAD = 7
MOBA_ONES_ROWS = BF16_ROWS

NEG = -1e30
LOG2_E = math.log2(math.e)


def _dot(a, b):
    return jnp.dot(a, b, preferred_element_type=F32)


def _rms(x, g):
    return x * lax.rsqrt(jnp.mean(x * x, axis=-1, keepdims=True) + EPS) * g


def _sigmoid(x):
    return 1.0 / (1.0 + jnp.exp(-x))


def _params(*sem):
    return pltpu.CompilerParams(dimension_semantics=sem, vmem_limit_bytes=VMEM_LIMIT)


def _cast_ahead(mats, n_steps, step_of):
    assert all(w.shape[0] % n_steps == 0 for w in mats)
    sliced = [w.reshape(n_steps, w.shape[0] // n_steps, w.shape[1]) for w in mats]
    specs = [pl.BlockSpec((1,) + w.shape[1:], lambda *ids: (step_of(*ids), 0, 0)) for w in sliced]
    return sliced, specs, [jax.ShapeDtypeStruct(w.shape, BF16) for w in sliced]


def _split_cast_refs(rest, n_out, n_scratch):
    n = (len(rest) - n_out - n_scratch) // 2
    outs = rest[n:n + n_out]
    scratch = rest[len(rest) - n_scratch:]
    for src, dst in zip(rest[:n], rest[n + n_out:2 * n + n_out]):
        dst[...] = src[...].astype(BF16)
    return outs, scratch


def _full(shape):
    n = len(shape)
    return pl.BlockSpec(shape, lambda *_: (0,) * n, pipeline_mode=pl.Buffered(1))


QKV_INPUTS, QKV_OUTPUTS = 15, 8


def _qkv_kernel(*refs):
    outs, _ = _split_cast_refs(refs[QKV_INPUTS:], QKV_OUTPUTS, 0)
    for sb in range(QKV_BLOCKS):
        _qkv_block(sb, *refs[:QKV_INPUTS], *outs)


def _qkv_block(sb, x_ref, g_ref, w_ref, wqt_ref, wvt_ref, wkt_ref, cosa_ref, sina_ref, cosr_ref, sinr_ref,
               costa_ref, sinta_ref, costr_ref, sintr_ref, kdec_ref,
               k_ref, kmean_ref, qt_ref, vt_ref, rq_ref, rv_ref, rkt_ref, rktd_ref):
    t = TOKEN_TILE
    tok = slice(sb * t, (sb + 1) * t)
    h = _rms(x_ref[0, tok, :], g_ref[...])
    hb = h.astype(BF16)
    ht = h.T.astype(BF16)
    lane = lax.broadcasted_iota(jnp.int32, (t, LANES), 1)
    a_half, r_half = A_HEAD_DIM // 2, R_QK_DIM // 2
    first_half = (lane & a_half) == 0

    ak = _dot(hb, w_ref[:, COL_AK:COL_AV])
    cosa, sina = cosa_ref[tok, :], sina_ref[tok, :]
    for c in range(A_WIDTH // LANES):
        blk = ak[:, c * LANES:(c + 1) * LANES]
        rot = jnp.where(first_half, pltpu.roll(blk, LANES - a_half, 1), pltpu.roll(blk, a_half, 1))
        kr = blk * cosa + rot * sina
        k_ref[0, sb, :, c * LANES:(c + 1) * LANES] = kr.astype(BF16)
        kmean_ref[0, sb, :, c * LANES:(c + 1) * LANES] = jnp.mean(kr, axis=0, keepdims=True)

    rq = _dot(hb, w_ref[:, COL_RQ:COL_RK])
    cosr, sinr = cosr_ref[tok, :], sinr_ref[tok, :]
    for c in range(R_HEADS):
        blk = rq[:, c * R_QK_DIM:(c + 1) * R_QK_DIM]
        rq_ref[0, tok, c * R_QK_DIM:(c + 1) * R_QK_DIM] = (blk * cosr + pltpu.roll(blk, r_half, 1) * sinr).astype(BF16)

    rv_ref[0, tok, :] = _dot(hb, w_ref[:, COL_RV:COL_GATES]).astype(BF16)

    qt = _dot(wqt_ref[...], ht)
    cost, sint = costa_ref[:, tok], sinta_ref[:, tok]
    half = A_HEAD_DIM // 2
    scale_a = A_HEAD_DIM ** -0.5 * LOG2_E
    for hd in range(A_HEADS):
        x1 = qt[hd * A_HEAD_DIM:hd * A_HEAD_DIM + half]
        x2 = qt[hd * A_HEAD_DIM + half:(hd + 1) * A_HEAD_DIM]
        qt_ref[0, sb, hd * A_HEAD_DIM:hd * A_HEAD_DIM + half, :] = ((x1 * cost - x2 * sint) * scale_a).astype(BF16)
        qt_ref[0, sb, hd * A_HEAD_DIM + half:(hd + 1) * A_HEAD_DIM, :] = ((x2 * cost + x1 * sint) * scale_a).astype(BF16)

    vt_ref[0, sb] = _dot(wvt_ref[...], ht).astype(BF16)

    rkt = _dot(wkt_ref[...], ht)
    cost, sint = costr_ref[:, tok], sintr_ref[:, tok]
    half = R_QK_DIM // 2
    scale_r = R_QK_DIM ** -0.5
    for hd in range(R_HEADS):
        x1 = rkt[hd * R_QK_DIM:hd * R_QK_DIM + half]
        x2 = rkt[hd * R_QK_DIM + half:(hd + 1) * R_QK_DIM]
        o1 = (x1 * cost - x2 * sint) * scale_r
        o2 = (x2 * cost + x1 * sint) * scale_r
        dec = kdec_ref[hd:hd + 1, tok]
        rkt_ref[0, sb, hd * R_QK_DIM:hd * R_QK_DIM + half, :] = o1.astype(BF16)
        rkt_ref[0, sb, hd * R_QK_DIM + half:(hd + 1) * R_QK_DIM, :] = o2.astype(BF16)
        rktd_ref[0, sb, hd * R_QK_DIM:hd * R_QK_DIM + half, :] = (o1 * dec).astype(BF16)
        rktd_ref[0, sb, hd * R_QK_DIM + half:(hd + 1) * R_QK_DIM, :] = (o2 * dec).astype(BF16)


def _qkv(x, g, w, wqt, wvt, wkt, tabs, cast_ahead):
    b, s, d = x.shape
    nb = s // TOKEN_TILE
    qb = QKV_BLOCKS
    t = qb * TOKEN_TILE
    steps = nb // qb
    tok = lambda w: pl.BlockSpec((1, t, w), lambda bi, i: (bi, i, 0))
    blk_t = lambda r: pl.BlockSpec((1, qb, r, TOKEN_TILE), lambda bi, i: (bi, i, 0, 0))
    nat_tab = pl.BlockSpec((t, LANES), lambda bi, i: (i, 0))
    tr_tab = lambda r: pl.BlockSpec((r, t), lambda bi, i: (0, i))
    sliced, cast_specs, cast_shapes = _cast_ahead(cast_ahead, b * steps, lambda bi, i: bi * steps + i)
    in_specs = [tok(d), _full((1, d)), _full(w.shape), _full(wqt.shape), _full(wvt.shape), _full(wkt.shape),
                nat_tab, nat_tab, nat_tab, nat_tab,
                tr_tab(A_HEAD_DIM // 2), tr_tab(A_HEAD_DIM // 2),
                tr_tab(R_QK_DIM // 2), tr_tab(R_QK_DIM // 2), tr_tab(SUBLANES)]
    assert len(in_specs) == QKV_INPUTS
    out = pl.pallas_call(
        _qkv_kernel,
        grid=(b, steps),
        in_specs=in_specs + cast_specs,
        out_specs=[pl.BlockSpec((1, qb, TOKEN_TILE, A_WIDTH), lambda bi, i: (bi, i, 0, 0)),
                   pl.BlockSpec((1, qb, 1, A_WIDTH), lambda bi, i: (bi, i, 0, 0)),
                   blk_t(A_WIDTH), blk_t(A_WIDTH), tok(R_QK_WIDTH), tok(R_V_WIDTH),
                   blk_t(R_QK_WIDTH), blk_t(R_QK_WIDTH)] + cast_specs,
        out_shape=[jax.ShapeDtypeStruct((b, nb, TOKEN_TILE, A_WIDTH), BF16),
                   jax.ShapeDtypeStruct((b, nb, 1, A_WIDTH), F32),
                   jax.ShapeDtypeStruct((b, nb, A_WIDTH, TOKEN_TILE), BF16),
                   jax.ShapeDtypeStruct((b, nb, A_WIDTH, TOKEN_TILE), BF16),
                   jax.ShapeDtypeStruct((b, s, R_QK_WIDTH), BF16),
                   jax.ShapeDtypeStruct((b, s, R_V_WIDTH), BF16),
                   jax.ShapeDtypeStruct((b, nb, R_QK_WIDTH, TOKEN_TILE), BF16),
                   jax.ShapeDtypeStruct((b, nb, R_QK_WIDTH, TOKEN_TILE), BF16)] + cast_shapes,
        compiler_params=_params("parallel", "parallel"),
        name="qkv",
    )(x, g, w, wqt, wvt, wkt, *tabs, *sliced)
    return out[:QKV_OUTPUTS], [o.reshape(w.shape) for o, w in zip(out[QKV_OUTPUTS:], cast_ahead)]


def _moba_kernel(qt_ref, k_ref, vt_ref, km_ref, o_ref, qm_ref, m_ref, acc_ref, ot_ref, s_ref):
    i = pl.program_id(1)
    nb = km_ref.shape[1]
    t = MOBA_BLOCK
    dh = A_HEAD_DIM
    row = lax.broadcasted_iota(jnp.int32, (2 * dh, t), 0)
    blk = lax.broadcasted_iota(jnp.int32, (nb, t), 0)
    past = blk < i
    causal = (lax.broadcasted_iota(jnp.int32, (t, t), 0) <= lax.broadcasted_iota(jnp.int32, (t, t), 1))
    ones = jnp.ones((MOBA_ONES_ROWS, t), BF16)
    lanes = lambda hd: slice((hd // 2) * LANES, (hd // 2 + 1) * LANES)
    rows = lambda hd: slice(hd * dh, (hd + 1) * dh)
    row_heads = [(b, hd) for b in range(MOBA_ROWS) for hd in range(A_HEADS)]

    def select_blocks(b, hd):
        hh = hd % 2
        qt_pair = qt_ref[b, 0, lanes(hd), :]
        mine = (row >= hh * dh) & (row < (hh + 1) * dh)
        qt = jnp.where(mine, qt_pair, jnp.zeros_like(qt_pair))
        qm_ref[b, hd, 0:2 * dh, :] = qt

        km = km_ref[b, :, lanes(hd)]
        km_hi = km.astype(BF16)
        km_lo = (km - km_hi.astype(F32)).astype(BF16)
        bs = jnp.where(past, _dot(km_hi, qt) + _dot(km_lo, qt), -jnp.inf)
        picked = jnp.zeros((nb, t), jnp.bool_)
        for _ in range(MOBA_TOPK):
            best = jnp.max(bs, axis=0, keepdims=True)
            first = jnp.min(jnp.where(bs == best, blk, nb), axis=0, keepdims=True)
            hit = blk == first
            picked = picked | hit
            bs = jnp.where(hit, -jnp.inf, bs)
        bias = jnp.where(past & picked, 0.0, NEG).astype(BF16)
        qm_ref[b, hd, 2 * dh:, :] = jnp.concatenate([bias, jnp.zeros((2 * dh - nb, t), BF16)], axis=0)

    key_lane = lax.broadcasted_iota(jnp.int32, (t, 2 * dh), 1)

    def keys_with_block_column(b, j):
        onehot = jnp.where(key_lane == j, 1.0, 0.0).astype(BF16)
        return [jnp.concatenate([k_ref[b, j, :, lanes(2 * p)], onehot], axis=1) for p in range(A_HEADS // 2)]

    def pv(b, j, hd, p):
        vt_aug = jnp.concatenate([vt_ref[b, j, rows(hd), :], ones], axis=0)
        return _dot(vt_aug, p.astype(BF16))

    def own_unit(b, hd):
        def scores():
            return jnp.where(causal, _dot(k_ref[b, i, :, lanes(hd)], qm_ref[b, hd, 0:2 * dh, :]), NEG)

        def update(s, m0):
            m_ref[b, hd] = jnp.broadcast_to(m0, (SUBLANES, t))
            acc_ref[b, hd] = pv(b, i, hd, jnp.exp2(s - m0))

        return scores, update

    def past_units(b, j):
        keys = keys_with_block_column(b, j)

        def unit(hd):
            def scores():
                return _dot(keys[hd // 2], qm_ref[b, hd])

            def update(s, m_blk):
                m_old = m_ref[b, hd]
                m_new = jnp.maximum(m_old, m_blk)
                alpha = jnp.exp2(m_old - m_new)
                m_ref[b, hd] = m_new
                acc_ref[b, hd] = alpha[0:1, :] * acc_ref[b, hd] + pv(b, j, hd, jnp.exp2(s - m_new[0:1, :]))

            return scores, update

        return [unit(hd) for hd in range(A_HEADS)]

    def pipelined(units):
        blk_max = {}
        slots = MOBA_LOOKAHEAD + 1
        for step in range(len(units) + MOBA_LOOKAHEAD):
            if step < len(units):
                sc = units[step][0]()
                s_ref[step % slots] = sc
                blk_max[step] = jnp.max(sc, axis=0, keepdims=True)
            if step >= MOBA_LOOKAHEAD:
                u = step - MOBA_LOOKAHEAD
                units[u][1](s_ref[u % slots], blk_max.pop(u))

    n_groups = i // MOBA_UNROLL
    for left in range(MOBA_UNROLL):
        @pl.when(i % MOBA_UNROLL == left)
        def _(left=left):
            for b, hd in row_heads:
                select_blocks(b, hd)
            units = [own_unit(b, hd) for b, hd in row_heads]
            for r in range(left):
                for b in range(MOBA_ROWS):
                    units += past_units(b, n_groups * MOBA_UNROLL + r)
            pipelined(units)

    def group_body(jj, carry):
        pipelined([u for r in range(MOBA_UNROLL) for b in range(MOBA_ROWS)
                   for u in past_units(b, MOBA_UNROLL * jj + r)])
        return carry

    lax.fori_loop(0, n_groups, group_body, 0)

    for b in range(MOBA_ROWS):
        for hd in range(A_HEADS):
            ot_ref[rows(hd), :] = acc_ref[b, hd, 0:dh, :] / acc_ref[b, hd, dh:dh + 1, :]
        o_ref[b] = ot_ref[...].T.astype(BF16)


def _moba(qt, k, vt, kmean):
    b, nb, t, w = k.shape
    r = MOBA_ROWS
    assert b % r == 0
    return pl.pallas_call(
        _moba_kernel,
        grid=(b // r, nb),
        in_specs=[pl.BlockSpec((r, 1, w, t), lambda bi, i: (bi, i, 0, 0)),
                  pl.BlockSpec((r, nb, t, w), lambda bi, i: (bi, 0, 0, 0)),
                  pl.BlockSpec((r, nb, w, t), lambda bi, i: (bi, 0, 0, 0)),
                  pl.BlockSpec((r, nb, w), lambda bi, i: (bi, 0, 0))],
        out_specs=pl.BlockSpec((r, t, w), lambda bi, i: (bi, i, 0)),
        out_shape=jax.ShapeDtypeStruct((b, nb * t, w), BF16),
        scratch_shapes=[pltpu.VMEM((r, A_HEADS, 4 * A_HEAD_DIM, t), BF16),
                        pltpu.VMEM((r, A_HEADS, SUBLANES, t), F32),
                        pltpu.VMEM((r, A_HEADS, A_HEAD_DIM + MOBA_ONES_ROWS, t), F32),
                        pltpu.VMEM((w, t), F32),
                        pltpu.VMEM((MOBA_LOOKAHEAD + 1, t, t), F32)],
        compiler_params=_params("parallel", "parallel"),
        name="moba",
    )(qt, k, vt, kmean)


def _mix_kernel(chunk_dec, x_ref, g_ref, rq_ref, rkt_ref, rktd_ref, rv_ref, ya_ref, dmask_ref, qdec_ref,
                w_ref, wa_ref, wb_ref, wo_ref, *rest):
    (o_ref,), (state_ref, yb_ref) = _split_cast_refs(rest, 1, 2)

    @pl.when(pl.program_id(1) == 0)
    def _():
        state_ref[...] = jnp.zeros_like(state_ref)

    x = x_ref[0]
    d = x.shape[1]
    hb = _rms(x, g_ref[...]).astype(BF16)
    c = R_CHUNK
    per_blk = TOKEN_TILE // c
    n_chunks = MIX_BLOCKS * per_blk
    rows = lambda ci: slice(ci * c, (ci + 1) * c)
    qk = lambda hd: slice(hd * R_QK_DIM, (hd + 1) * R_QK_DIM)
    vv = lambda hd: slice(hd * R_V_DIM, (hd + 1) * R_V_DIM)
    units = [(ci, hd) for ci in range(n_chunks) for hd in range(R_HEADS)]
    keys_t = lambda ref, ci, hd: ref[0, ci // per_blk, qk(hd), rows(ci % per_blk)]

    att = {u: _dot(rq_ref[0, rows(u[0]), qk(u[1])], keys_t(rkt_ref, *u)) for u in units}
    kv = {u: _dot(keys_t(rktd_ref, *u), rv_ref[0, rows(u[0]), vv(u[1])]) for u in units}
    rg = _dot(hb, w_ref[:, COL_GATES:COL_GATES + R_V_WIDTH])
    w = MIX_LATE_COLS
    late = ([lambda c=c: _dot(ya_ref[0], wa_ref[:, c:c + w]) for c in range(0, d, w)]
            + [lambda c=c: _dot(hb, w_ref[:, c:c + w])
               for c in range(COL_GATES + R_V_WIDTH, COL_GATES + R_V_WIDTH + 2 * d, w)])
    late_out = []

    cross = {}
    for hd in range(R_HEADS):
        st = state_ref[hd]
        for ci in range(n_chunks):
            cross[ci, hd] = _dot(rq_ref[0, rows(ci), qk(hd)], st.astype(BF16))
            st = chunk_dec[hd] * st + kv[ci, hd]
        state_ref[hd] = st

    gate = rg * _sigmoid(rg)
    for n, (ci, hd) in enumerate(units):
        inner = _dot((att[ci, hd] * dmask_ref[hd]).astype(BF16), rv_ref[0, rows(ci), vv(hd)])
        y = inner + cross[ci, hd] * qdec_ref[hd]
        mu = jnp.mean(y, axis=-1, keepdims=True)
        yc = y - mu
        var = jnp.mean(yc * yc, axis=-1, keepdims=True)
        yb_ref[rows(ci), vv(hd)] = (yc * lax.rsqrt(var + EPS) * gate[rows(ci), vv(hd)]).astype(BF16)
        for k in range(n * len(late) // len(units), (n + 1) * len(late) // len(units)):
            late_out.append(late[k]())

    pieces = d // w
    ta, ga, gb = (jnp.concatenate(late_out[k * pieces:(k + 1) * pieces], axis=1) for k in range(3))
    blocks = [slice(r * TOKEN_TILE, (r + 1) * TOKEN_TILE) for r in range(MIX_BLOCKS)]
    tb = [_dot(yb_ref[tok, :], wb_ref[...]) for tok in blocks]
    for tok, tb_blk in zip(blocks, tb):
        merged = _sigmoid(ga[tok]) * ta[tok] + _sigmoid(gb[tok]) * tb_blk
        o_ref[0, tok, :] = x[tok] + _dot(merged.astype(BF16), wo_ref[...])


def _mix(x, g, rq, rkt, rktd, rv, ya, dmask, qdec, chunk_dec, w, wa, wb, wo, cast_ahead):
    b, s, d = x.shape
    t = MIX_BLOCKS * TOKEN_TILE
    steps = s // t
    tok = lambda w: pl.BlockSpec((1, t, w), lambda bi, i: (bi, i, 0))
    blk_t = pl.BlockSpec((1, MIX_BLOCKS, R_QK_WIDTH, TOKEN_TILE), lambda bi, i: (bi, i, 0, 0))
    sliced, cast_specs, cast_shapes = _cast_ahead(cast_ahead, b * steps, lambda bi, i: bi * steps + i)
    out = pl.pallas_call(
        functools.partial(_mix_kernel, chunk_dec),
        grid=(b, steps),
        in_specs=[tok(d), _full((1, d)), tok(R_QK_WIDTH), blk_t, blk_t, tok(R_V_WIDTH), tok(A_WIDTH),
                  _full(dmask.shape), _full(qdec.shape),
                  _full(w.shape), _full(wa.shape), _full(wb.shape), _full(wo.shape)] + cast_specs,
        out_specs=[tok(d)] + cast_specs,
        out_shape=[jax.ShapeDtypeStruct((b, s, d), F32)] + cast_shapes,
        scratch_shapes=[pltpu.VMEM((R_HEADS, R_QK_DIM, R_V_DIM), F32),
                        pltpu.VMEM((t, R_V_WIDTH), BF16)],
        compiler_params=_params("parallel", "arbitrary"),
        name="mix",
    )(x, g, rq, rkt, rktd, rv, ya, dmask, qdec, w, wa, wb, wo, *sliced)
    return out[0], [o.reshape(w.shape) for o, w in zip(out[1:], cast_ahead)]


def _cross_kernel(x_ref, g_ref, m_ref, gm_ref, wkv_ref, wq_ref, wo_ref, *rest):
    (o_ref,), (qk_ref, vo_ref, p_ref) = _split_cast_refs(rest, 1, 3)
    d = x_ref.shape[2]
    dh = d // X_HEADS
    mlen = m_ref.shape[1]
    cols = lambda hd: slice(hd * dh, (hd + 1) * dh)
    mcols = lambda hd: slice(hd * mlen, (hd + 1) * mlen)

    @pl.when(pl.program_id(1) == 0)
    def _():
        mb = _rms(m_ref[0], gm_ref[...]).astype(BF16)
        kt = _dot(mb, wkv_ref[:, 0:d]).T.astype(BF16)
        v = _dot(mb, wkv_ref[:, d:]).astype(BF16)
        for hd in range(X_HEADS):
            qk_ref[:, mcols(hd)] = (_dot(wq_ref[:, cols(hd)], kt[cols(hd), :]) * dh ** -0.5).astype(BF16)
            vo_ref[mcols(hd), :] = _dot(v[:, cols(hd)], wo_ref[cols(hd), :]).astype(BF16)

    n_parts = CROSS_PARTS
    t = x_ref.shape[1] // n_parts
    part = lambda r: slice(r * t, (r + 1) * t)

    def scores_of(r):
        hb = _rms(x_ref[0, part(r), :], g_ref[...]).astype(BF16)
        return _dot(hb, qk_ref[...])

    nxt = scores_of(0)
    for r in range(n_parts):
        scores = nxt
        if r + 1 < n_parts:
            nxt = scores_of(r + 1)
        for hd in range(X_HEADS):
            s = scores[:, mcols(hd)]
            p = jnp.exp(s - jnp.max(s, axis=-1, keepdims=True))
            p_ref[part(r), mcols(hd)] = (p / jnp.sum(p, axis=-1, keepdims=True)).astype(BF16)
        o_ref[0, part(r), :] = x_ref[0, part(r), :] + _dot(p_ref[part(r), :], vo_ref[...])


def _cross(x, g, mem, gm, wkv, wq, wo, cast_ahead):
    b, s, d = x.shape
    m = mem.shape[1]
    t = ROW_TILE
    steps = s // t
    tok = pl.BlockSpec((1, t, d), lambda bi, i: (bi, i, 0))
    sliced, cast_specs, cast_shapes = _cast_ahead(cast_ahead, b * steps, lambda bi, i: bi * steps + i)
    out = pl.pallas_call(
        _cross_kernel,
        grid=(b, steps),
        in_specs=[tok, _full((1, d)), pl.BlockSpec((1, m, d), lambda bi, i: (bi, 0, 0)), _full((1, d)),
                  _full(wkv.shape), _full(wq.shape), _full(wo.shape)] + cast_specs,
        out_specs=[tok] + cast_specs,
        out_shape=[jax.ShapeDtypeStruct((b, s, d), F32)] + cast_shapes,
        scratch_shapes=[pltpu.VMEM((d, X_HEADS * m), BF16),
                        pltpu.VMEM((X_HEADS * m, d), BF16),
                        pltpu.VMEM((t, X_HEADS * m), BF16)],
        compiler_params=_params("parallel", "arbitrary"),
        name="cross",
    )(x, g, mem, gm, wkv, wq, wo, *sliced)
    return out[0], [o.reshape(w.shape) for o, w in zip(out[1:], cast_ahead)]


def _ffn_kernel(x_ref, g_ref, wg_ref, wu_ref, wd_ref, *rest):
    o_ref = rest[-1]
    x = x_ref[...]
    hb = _rms(x, g_ref[...]).astype(BF16)
    n_chunks = wg_ref.shape[1] // FF_CHUNK
    cols = lambda c: slice(c * FF_CHUNK, (c + 1) * FF_CHUNK)

    def up(c):
        return _dot(hb, wg_ref[:, cols(c)]), _dot(hb, wu_ref[:, cols(c)])

    acc = x
    nxt = up(0)
    for c in range(n_chunks):
        gate, upv = nxt
        if c + 1 < n_chunks:
            nxt = up(c + 1)
        act = gate * _sigmoid(gate) * upv
        acc = acc + _dot(act.astype(BF16), wd_ref[cols(c), :])
    o_ref[...] = _rms(acc, rest[0][...]) if len(rest) == 2 else acc


def _ffn(x2, g, wg, wu, wd, final_gain=None):
    n, d = x2.shape
    t = ROW_TILE
    tok = pl.BlockSpec((t, d), lambda i: (i, 0))
    final = () if final_gain is None else (final_gain,)
    return pl.pallas_call(
        _ffn_kernel,
        grid=(n // t,),
        in_specs=[tok, _full((1, d)), _full(wg.shape), _full(wu.shape), _full(wd.shape)] + [_full((1, d))] * len(final),
        out_specs=tok,
        out_shape=jax.ShapeDtypeStruct((n, d), F32),
        compiler_params=_params("parallel"),
        name="ffn",
    )(x2, g, wg, wu, wd, *final)


def _rope_tables(s):
    def cs(dim):
        inv = ROPE_THETA ** (-jnp.arange(0, dim, 2, dtype=F32) / dim)
        ang = jnp.arange(s).astype(F32)[:, None] * inv[None, :]
        return jnp.cos(ang), jnp.sin(ang)

    ca, sa = cs(A_HEAD_DIM)
    cr, sr = cs(R_QK_DIM)
    sign = lambda half: jnp.asarray(np.tile(np.repeat([-1.0, 1.0], half), LANES // (2 * half))[None, :], F32)
    return (jnp.tile(ca, (1, 4)), jnp.tile(sa, (1, 4)) * sign(A_HEAD_DIM // 2),
            jnp.tile(cr, (1, 2)), jnp.tile(sr, (1, 2)) * sign(R_QK_DIM // 2),
            ca.T, sa.T, cr.T, sr.T)


def _decay_tables(s):
    c = R_CHUNK
    log_g = np.log(1.0 - np.exp2(-5.0 - np.arange(R_HEADS, dtype=np.float64)))
    i = np.arange(c, dtype=np.float64)
    diff = i[:, None] - i[None, :]
    dmask = np.where(diff >= 0, np.exp(log_g[:, None, None] * np.maximum(diff, 0.0)), 0.0)
    q_dec = np.exp(log_g[:, None] * (i + 1.0))
    k_dec = np.exp(log_g[:, None] * (c - 1.0 - i))
    kdec_t = np.zeros((SUBLANES, s), np.float64)
    kdec_t[:R_HEADS] = np.tile(k_dec, (1, s // c))
    qdec = np.broadcast_to(q_dec[:, :, None], (R_HEADS, c, R_V_DIM))
    chunk_dec = tuple(float(v) for v in np.exp(log_g * c))
    return (jnp.asarray(dmask, F32), jnp.asarray(qdec, F32), jnp.asarray(kdec_t, F32), chunk_dec)


def kernel(x, mem, norm_mix, w_in, w_branch_a, w_branch_b, w_out, norm_cross, norm_mem,
           w_xq, w_xkv, w_xo, norm_ffn, w_gate, w_up, w_down, norm_final):
    b, s, d = x.shape
    steps = (TOKEN_TILE * QKV_BLOCKS, TOKEN_TILE * MIX_BLOCKS, ROW_TILE)
    assert all(s % t == 0 for t in steps) and b % MOBA_ROWS == 0 and d % LANES == 0
    rope = _rope_tables(s)
    dmask, qdec, kdec_t, chunk_dec = _decay_tables(s)
    row = lambda v: v.reshape(1, d)
    assert w_in.shape[2] == COL_GATES + R_V_WIDTH + 2 * d

    for l in range(w_in.shape[0]):
        w = w_in[l].astype(BF16)
        wqt, wvt, wkt = (w[:, c:c + n].T for c, n in ((COL_AQ, A_WIDTH), (COL_AV, A_WIDTH), (COL_RK, R_QK_WIDTH)))
        (k_a, kmean, qt_a, vt_a, rq, rv, rkt, rktd), (wa, wb, wo) = _qkv(
            x, row(norm_mix[l]), w, wqt, wvt, wkt, rope + (kdec_t,), (w_branch_a[l], w_branch_b[l], w_out[l]))
        y_a = _moba(qt_a, k_a, vt_a, kmean.reshape(b, s // MOBA_BLOCK, A_WIDTH))
        x, (wxkv, wxq, wxo) = _mix(x, row(norm_mix[l]), rq, rkt, rktd, rv, y_a, dmask, qdec, chunk_dec,
                                   w, wa, wb, wo, (w_xkv[l], w_xq[l], w_xo[l]))
        x, (wg, wu, wd) = _cross(x, row(norm_cross[l]), mem, row(norm_mem[l]), wxkv, wxq, wxo,
                                 (w_gate[l], w_up[l], w_down[l]))
        last = l == w_in.shape[0] - 1
        x = _ffn(x.reshape(b * s, d), row(norm_ffn[l]), wg, wu, wd,
                 row(norm_final) if last else None).reshape(b, s, d)
    return x
```

```python
import functools
import math

import numpy as np
import jax
import jax.numpy as jnp
from jax import lax
from jax.experimental import pallas as pl
from jax.experimental.pallas import tpu as pltpu

F32 = jnp.float32
BF16 = jnp.bfloat16

EPS = 1e-6
ROPE_THETA = 10000.0
A_HEADS = 8
A_HEAD_DIM = 64
A_WIDTH = A_HEADS * A_HEAD_DIM
MOBA_BLOCK = 256
MOBA_TOPK = 3
R_HEADS = 4
R_QK_DIM = 128
R_V_DIM = 256
R_QK_WIDTH = R_HEADS * R_QK_DIM
R_V_WIDTH = R_HEADS * R_V_DIM
R_CHUNK = 256
X_HEADS = 4
COL_AQ, COL_AK, COL_AV = 0, A_WIDTH, 2 * A_WIDTH
COL_RQ = 3 * A_WIDTH
COL_RK = COL_RQ + R_QK_WIDTH
COL_RV = COL_RK + R_QK_WIDTH
COL_GATES = COL_RV + R_V_WIDTH

LANES = 128
SUBLANES = 8
BF16_ROWS = 16
VMEM_LIMIT = 56 * 1024 * 1024
assert 2 * A_HEAD_DIM == LANES and R_QK_DIM == LANES

TOKEN_TILE = MOBA_BLOCK
QKV_BLOCKS = 4
MIX_BLOCKS = 2
MIX_LATE_COLS = 256
ROW_TILE = 1024
CROSS_PARTS = 4
FF_CHUNK = 256
MOBA_ROWS = 2
MOBA_UNROLL = 4
MOBA_LOOKAHEAD = 7
MOBA_ONES_ROWS = BF16_ROWS

NEG = -1e30
LOG2_E = math.log2(math.e)


def _dot(a, b):
    return jnp.dot(a, b, preferred_element_type=F32)


def _rms(x, g):
    return x * lax.rsqrt(jnp.mean(x * x, axis=-1, keepdims=True) + EPS) * g


def _sigmoid(x):
    return 1.0 / (1.0 + jnp.exp(-x))


def _params(*sem):
    return pltpu.CompilerParams(dimension_semantics=sem, vmem_limit_bytes=VMEM_LIMIT)


def _cast_ahead(mats, n_steps, step_of):
    assert all(w.shape[0] % n_steps == 0 for w in mats)
    sliced = [w.reshape(n_steps, w.shape[0] // n_steps, w.shape[1]) for w in mats]
    specs = [pl.BlockSpec((1,) + w.shape[1:], lambda *ids: (step_of(*ids), 0, 0)) for w in sliced]
    return sliced, specs, [jax.ShapeDtypeStruct(w.shape, BF16) for w in sliced]


def _split_cast_refs(rest, n_out, n_scratch):
    n = (len(rest) - n_out - n_scratch) // 2
    outs = rest[n:n + n_out]
    scratch = rest[len(rest) - n_scratch:]
    for src, dst in zip(rest[:n], rest[n + n_out:2 * n + n_out]):
        dst[...] = src[...].astype(BF16)
    return outs, scratch


def _full(shape):
    n = len(shape)
    return pl.BlockSpec(shape, lambda *_: (0,) * n, pipeline_mode=pl.Buffered(1))


QKV_INPUTS, QKV_OUTPUTS = 15, 8


def _qkv_kernel(*refs):
    outs, _ = _split_cast_refs(refs[QKV_INPUTS:], QKV_OUTPUTS, 0)
    for sb in range(QKV_BLOCKS):
        _qkv_block(sb, *refs[:QKV_INPUTS], *outs)


def _qkv_block(sb, x_ref, g_ref, w_ref, wqt_ref, wvt_ref, wkt_ref, cosa_ref, sina_ref, cosr_ref, sinr_ref,
               costa_ref, sinta_ref, costr_ref, sintr_ref, kdec_ref,
               k_ref, kmean_ref, qt_ref, vt_ref, rq_ref, rv_ref, rkt_ref, rktd_ref):
    t = TOKEN_TILE
    tok = slice(sb * t, (sb + 1) * t)
    h = _rms(x_ref[0, tok, :], g_ref[...])
    hb = h.astype(BF16)
    ht = h.T.astype(BF16)
    lane = lax.broadcasted_iota(jnp.int32, (t, LANES), 1)
    a_half, r_half = A_HEAD_DIM // 2, R_QK_DIM // 2
    first_half = (lane & a_half) == 0

    ak = _dot(hb, w_ref[:, COL_AK:COL_AV])
    cosa, sina = cosa_ref[tok, :], sina_ref[tok, :]
    for c in range(A_WIDTH // LANES):
        blk = ak[:, c * LANES:(c + 1) * LANES]
        rot = jnp.where(first_half, pltpu.roll(blk, LANES - a_half, 1), pltpu.roll(blk, a_half, 1))
        kr = blk * cosa + rot * sina
        k_ref[0, sb, :, c * LANES:(c + 1) * LANES] = kr.astype(BF16)
        kmean_ref[0, sb, :, c * LANES:(c + 1) * LANES] = jnp.mean(kr, axis=0, keepdims=True)

    rq = _dot(hb, w_ref[:, COL_RQ:COL_RK])
    cosr, sinr = cosr_ref[tok, :], sinr_ref[tok, :]
    for c in range(R_HEADS):
        blk = rq[:, c * R_QK_DIM:(c + 1) * R_QK_DIM]
        rq_ref[0, tok, c * R_QK_DIM:(c + 1) * R_QK_DIM] = (blk * cosr + pltpu.roll(blk, r_half, 1) * sinr).astype(BF16)

    rv_ref[0, tok, :] = _dot(hb, w_ref[:, COL_RV:COL_GATES]).astype(BF16)

    qt = _dot(wqt_ref[...], ht)
    cost, sint = costa_ref[:, tok], sinta_ref[:, tok]
    half = A_HEAD_DIM // 2
    scale_a = A_HEAD_DIM ** -0.5 * LOG2_E
    for hd in range(A_HEADS):
        x1 = qt[hd * A_HEAD_DIM:hd * A_HEAD_DIM + half]
        x2 = qt[hd * A_HEAD_DIM + half:(hd + 1) * A_HEAD_DIM]
        qt_ref[0, sb, hd * A_HEAD_DIM:hd * A_HEAD_DIM + half, :] = ((x1 * cost - x2 * sint) * scale_a).astype(BF16)
        qt_ref[0, sb, hd * A_HEAD_DIM + half:(hd + 1) * A_HEAD_DIM, :] = ((x2 * cost + x1 * sint) * scale_a).astype(BF16)

    vt_ref[0, sb] = _dot(wvt_ref[...], ht).astype(BF16)

    rkt = _dot(wkt_ref[...], ht)
    cost, sint = costr_ref[:, tok], sintr_ref[:, tok]
    half = R_QK_DIM // 2
    scale_r = R_QK_DIM ** -0.5
    for hd in range(R_HEADS):
        x1 = rkt[hd * R_QK_DIM:hd * R_QK_DIM + half]
        x2 = rkt[hd * R_QK_DIM + half:(hd + 1) * R_QK_DIM]
        o1 = (x1 * cost - x2 * sint) * scale_r
        o2 = (x2 * cost + x1 * sint) * scale_r
        dec = kdec_ref[hd:hd + 1, tok]
        rkt_ref[0, sb, hd * R_QK_DIM:hd * R_QK_DIM + half, :] = o1.astype(BF16)
        rkt_ref[0, sb, hd * R_QK_DIM + half:(hd + 1) * R_QK_DIM, :] = o2.astype(BF16)
        rktd_ref[0, sb, hd * R_QK_DIM:hd * R_QK_DIM + half, :] = (o1 * dec).astype(BF16)
        rktd_ref[0, sb, hd * R_QK_DIM + half:(hd + 1) * R_QK_DIM, :] = (o2 * dec).astype(BF16)


def _qkv(x, g, w, wqt, wvt, wkt, tabs, cast_ahead):
    b, s, d = x.shape
    nb = s // TOKEN_TILE
    qb = QKV_BLOCKS
    t = qb * TOKEN_TILE
    steps = nb // qb
    tok = lambda w: pl.BlockSpec((1, t, w), lambda i, bi: (bi, i, 0))
    blk_t = lambda r: pl.BlockSpec((1, qb, r, TOKEN_TILE), lambda i, bi: (bi, i, 0, 0))
    nat_tab = pl.BlockSpec((t, LANES), lambda i, bi: (i, 0))
    tr_tab = lambda r: pl.BlockSpec((r, t), lambda i, bi: (0, i))
    sliced, cast_specs, cast_shapes = _cast_ahead(cast_ahead, b * steps, lambda i, bi: i * b + bi)
    in_specs = [tok(d), _full((1, d)), _full(w.shape), _full(wqt.shape), _full(wvt.shape), _full(wkt.shape),
                nat_tab, nat_tab, nat_tab, nat_tab,
                tr_tab(A_HEAD_DIM // 2), tr_tab(A_HEAD_DIM // 2),
                tr_tab(R_QK_DIM // 2), tr_tab(R_QK_DIM // 2), tr_tab(SUBLANES)]
    assert len(in_specs) == QKV_INPUTS
    out = pl.pallas_call(
        _qkv_kernel,
        grid=(steps, b),
        in_specs=in_specs + cast_specs,
        out_specs=[pl.BlockSpec((1, qb, TOKEN_TILE, A_WIDTH), lambda i, bi: (bi, i, 0, 0)),
                   pl.BlockSpec((1, qb, 1, A_WIDTH), lambda i, bi: (bi, i, 0, 0)),
                   blk_t(A_WIDTH), blk_t(A_WIDTH), tok(R_QK_WIDTH), tok(R_V_WIDTH),
                   blk_t(R_QK_WIDTH), blk_t(R_QK_WIDTH)] + cast_specs,
        out_shape=[jax.ShapeDtypeStruct((b, nb, TOKEN_TILE, A_WIDTH), BF16),
                   jax.ShapeDtypeStruct((b, nb, 1, A_WIDTH), F32),
                   jax.ShapeDtypeStruct((b, nb, A_WIDTH, TOKEN_TILE), BF16),
                   jax.ShapeDtypeStruct((b, nb, A_WIDTH, TOKEN_TILE), BF16),
                   jax.ShapeDtypeStruct((b, s, R_QK_WIDTH), BF16),
                   jax.ShapeDtypeStruct((b, s, R_V_WIDTH), BF16),
                   jax.ShapeDtypeStruct((b, nb, R_QK_WIDTH, TOKEN_TILE), BF16),
                   jax.ShapeDtypeStruct((b, nb, R_QK_WIDTH, TOKEN_TILE), BF16)] + cast_shapes,
        compiler_params=_params("parallel", "parallel"),
        name="qkv",
    )(x, g, w, wqt, wvt, wkt, *tabs, *sliced)
    return out[:QKV_OUTPUTS], [o.reshape(w.shape) for o, w in zip(out[QKV_OUTPUTS:], cast_ahead)]


def _moba_kernel(qt_ref, k_ref, vt_ref, km_ref, o_ref, qm_ref, m_ref, acc_ref, ot_ref, s_ref):
    i = pl.program_id(1)
    nb = km_ref.shape[1]
    t = MOBA_BLOCK
    dh = A_HEAD_DIM
    row = lax.broadcasted_iota(jnp.int32, (2 * dh, t), 0)
    blk = lax.broadcasted_iota(jnp.int32, (nb, t), 0)
    past = blk < i
    causal = (lax.broadcasted_iota(jnp.int32, (t, t), 0) <= lax.broadcasted_iota(jnp.int32, (t, t), 1))
    ones = jnp.ones((MOBA_ONES_ROWS, t), BF16)
    lanes = lambda hd: slice((hd // 2) * LANES, (hd // 2 + 1) * LANES)
    rows = lambda hd: slice(hd * dh, (hd + 1) * dh)
    row_heads = [(b, hd) for b in range(MOBA_ROWS) for hd in range(A_HEADS)]

    def select_blocks(b, hd):
        hh = hd % 2
        qt_pair = qt_ref[b, 0, lanes(hd), :]
        mine = (row >= hh * dh) & (row < (hh + 1) * dh)
        qt = jnp.where(mine, qt_pair, jnp.zeros_like(qt_pair))
        qm_ref[b, hd, 0:2 * dh, :] = qt

        km = km_ref[b, :, lanes(hd)]
        km_hi = km.astype(BF16)
        km_lo = (km - km_hi.astype(F32)).astype(BF16)
        bs = jnp.where(past, _dot(km_hi, qt) + _dot(km_lo, qt), -jnp.inf)
        picked = jnp.zeros((nb, t), jnp.bool_)
        for _ in range(MOBA_TOPK):
            best = jnp.max(bs, axis=0, keepdims=True)
            first = jnp.min(jnp.where(bs == best, blk, nb), axis=0, keepdims=True)
            hit = blk == first
            picked = picked | hit
            bs = jnp.where(hit, -jnp.inf, bs)
        bias = jnp.where(past & picked, 0.0, NEG).astype(BF16)
        qm_ref[b, hd, 2 * dh:, :] = jnp.concatenate([bias, jnp.zeros((2 * dh - nb, t), BF16)], axis=0)

    key_lane = lax.broadcasted_iota(jnp.int32, (t, 2 * dh), 1)

    def keys_with_block_column(b, j):
        onehot = jnp.where(key_lane == j, 1.0, 0.0).astype(BF16)
        return [jnp.concatenate([k_ref[b, j, :, lanes(2 * p)], onehot], axis=1) for p in range(A_HEADS // 2)]

    def pv(b, j, hd, p):
        vt_aug = jnp.concatenate([vt_ref[b, j, rows(hd), :], ones], axis=0)
        return _dot(vt_aug, p.astype(BF16))

    def own_unit(b, hd):
        def scores():
            return jnp.where(causal, _dot(k_ref[b, i, :, lanes(hd)], qm_ref[b, hd, 0:2 * dh, :]), NEG)

        def update(s, m0):
            m_ref[b, hd] = jnp.broadcast_to(m0, (SUBLANES, t))
            acc_ref[b, hd] = pv(b, i, hd, jnp.exp2(s - m0))

        return scores, update

    def past_units(b, j):
        keys = keys_with_block_column(b, j)

        def unit(hd):
            def scores():
                return _dot(keys[hd // 2], qm_ref[b, hd])

            def update(s, m_blk):
                m_old = m_ref[b, hd]
                m_new = jnp.maximum(m_old, m_blk)
                alpha = jnp.exp2(m_old - m_new)
                m_ref[b, hd] = m_new
                acc_ref[b, hd] = alpha[0:1, :] * acc_ref[b, hd] + pv(b, j, hd, jnp.exp2(s - m_new[0:1, :]))

            return scores, update

        return [unit(hd) for hd in range(A_HEADS)]

    def pipelined(units):
        blk_max = {}
        slots = MOBA_LOOKAHEAD + 1
        for step in range(len(units) + MOBA_LOOKAHEAD):
            if step < len(units):
                sc = units[step][0]()
                s_ref[step % slots] = sc
                blk_max[step] = jnp.max(sc, axis=0, keepdims=True)
            if step >= MOBA_LOOKAHEAD:
                u = step - MOBA_LOOKAHEAD
                units[u][1](s_ref[u % slots], blk_max.pop(u))

    n_groups = i // MOBA_UNROLL
    for left in range(MOBA_UNROLL):
        @pl.when(i % MOBA_UNROLL == left)
        def _(left=left):
            for b, hd in row_heads:
                select_blocks(b, hd)
            units = [own_unit(b, hd) for b, hd in row_heads]
            for r in range(left):
                for b in range(MOBA_ROWS):
                    units += past_units(b, n_groups * MOBA_UNROLL + r)
            pipelined(units)

    def group_body(jj, carry):
        pipelined([u for r in range(MOBA_UNROLL) for b in range(MOBA_ROWS)
                   for u in past_units(b, MOBA_UNROLL * jj + r)])
        return carry

    lax.fori_loop(0, n_groups, group_body, 0)

    for b in range(MOBA_ROWS):
        for hd in range(A_HEADS):
            ot_ref[rows(hd), :] = acc_ref[b, hd, 0:dh, :] / acc_ref[b, hd, dh:dh + 1, :]
        o_ref[b] = ot_ref[...].T.astype(BF16)


def _moba(qt, k, vt, kmean):
    b, nb, t, w = k.shape
    r = MOBA_ROWS
    assert b % r == 0
    return pl.pallas_call(
        _moba_kernel,
        grid=(b // r, nb),
        in_specs=[pl.BlockSpec((r, 1, w, t), lambda bi, i: (bi, i, 0, 0)),
                  pl.BlockSpec((r, nb, t, w), lambda bi, i: (bi, 0, 0, 0)),
                  pl.BlockSpec((r, nb, w, t), lambda bi, i: (bi, 0, 0, 0)),
                  pl.BlockSpec((r, nb, w), lambda bi, i: (bi, 0, 0))],
        out_specs=pl.BlockSpec((r, t, w), lambda bi, i: (bi, i, 0)),
        out_shape=jax.ShapeDtypeStruct((b, nb * t, w), BF16),
        scratch_shapes=[pltpu.VMEM((r, A_HEADS, 4 * A_HEAD_DIM, t), BF16),
                        pltpu.VMEM((r, A_HEADS, SUBLANES, t), F32),
                        pltpu.VMEM((r, A_HEADS, A_HEAD_DIM + MOBA_ONES_ROWS, t), F32),
                        pltpu.VMEM((w, t), F32),
                        pltpu.VMEM((MOBA_LOOKAHEAD + 1, t, t), F32)],
        compiler_params=_params("parallel", "parallel"),
        name="moba",
    )(qt, k, vt, kmean)


def _mix_kernel(chunk_dec, x_ref, g_ref, rq_ref, rkt_ref, rktd_ref, rv_ref, ya_ref, dmask_ref, qdec_ref,
                w_ref, wa_ref, wb_ref, wo_ref, *rest):
    (o_ref,), (state_ref, yb_ref) = _split_cast_refs(rest, 1, 2)

    @pl.when(pl.program_id(1) == 0)
    def _():
        state_ref[...] = jnp.zeros_like(state_ref)

    x = x_ref[0]
    d = x.shape[1]
    hb = _rms(x, g_ref[...]).astype(BF16)
    c = R_CHUNK
    per_blk = TOKEN_TILE // c
    n_chunks = MIX_BLOCKS * per_blk
    rows = lambda ci: slice(ci * c, (ci + 1) * c)
    qk = lambda hd: slice(hd * R_QK_DIM, (hd + 1) * R_QK_DIM)
    vv = lambda hd: slice(hd * R_V_DIM, (hd + 1) * R_V_DIM)
    units = [(ci, hd) for ci in range(n_chunks) for hd in range(R_HEADS)]
    keys_t = lambda ref, ci, hd: ref[0, ci // per_blk, qk(hd), rows(ci % per_blk)]

    att = {u: _dot(rq_ref[0, rows(u[0]), qk(u[1])], keys_t(rkt_ref, *u)) for u in units}
    kv = {u: _dot(keys_t(rktd_ref, *u), rv_ref[0, rows(u[0]), vv(u[1])]) for u in units}
    rg = _dot(hb, w_ref[:, COL_GATES:COL_GATES + R_V_WIDTH])
    w = MIX_LATE_COLS
    late = ([lambda c=c: _dot(ya_ref[0], wa_ref[:, c:c + w]) for c in range(0, d, w)]
            + [lambda c=c: _dot(hb, w_ref[:, c:c + w])
               for c in range(COL_GATES + R_V_WIDTH, COL_GATES + R_V_WIDTH + 2 * d, w)])
    late_out = []

    cross = {}
    for hd in range(R_HEADS):
        st = state_ref[hd]
        for ci in range(n_chunks):
            cross[ci, hd] = _dot(rq_ref[0, rows(ci), qk(hd)], st.astype(BF16))
            st = chunk_dec[hd] * st + kv[ci, hd]
        state_ref[hd] = st

    gate = rg * _sigmoid(rg)
    for n, (ci, hd) in enumerate(units):
        inner = _dot((att[ci, hd] * dmask_ref[hd]).astype(BF16), rv_ref[0, rows(ci), vv(hd)])
        y = inner + cross[ci, hd] * qdec_ref[hd]
        mu = jnp.mean(y, axis=-1, keepdims=True)
        yc = y - mu
        var = jnp.mean(yc * yc, axis=-1, keepdims=True)
        yb_ref[rows(ci), vv(hd)] = (yc * lax.rsqrt(var + EPS) * gate[rows(ci), vv(hd)]).astype(BF16)
        for k in range(n * len(late) // len(units), (n + 1) * len(late) // len(units)):
            late_out.append(late[k]())

    pieces = d // w
    ta, ga, gb = (jnp.concatenate(late_out[k * pieces:(k + 1) * pieces], axis=1) for k in range(3))
    blocks = [slice(r * TOKEN_TILE, (r + 1) * TOKEN_TILE) for r in range(MIX_BLOCKS)]
    tb = [_dot(yb_ref[tok, :], wb_ref[...]) for tok in blocks]
    for tok, tb_blk in zip(blocks, tb):
        merged = _sigmoid(ga[tok]) * ta[tok] + _sigmoid(gb[tok]) * tb_blk
        o_ref[0, tok, :] = x[tok] + _dot(merged.astype(BF16), wo_ref[...])


def _mix(x, g, rq, rkt, rktd, rv, ya, dmask, qdec, chunk_dec, w, wa, wb, wo, cast_ahead):
    b, s, d = x.shape
    t = MIX_BLOCKS * TOKEN_TILE
    steps = s // t
    tok = lambda w: pl.BlockSpec((1, t, w), lambda bi, i: (bi, i, 0))
    blk_t = pl.BlockSpec((1, MIX_BLOCKS, R_QK_WIDTH, TOKEN_TILE), lambda bi, i: (bi, i, 0, 0))
    sliced, cast_specs, cast_shapes = _cast_ahead(cast_ahead, b * steps, lambda bi, i: bi * steps + i)
    out = pl.pallas_call(
        functools.partial(_mix_kernel, chunk_dec),
        grid=(b, steps),
        in_specs=[tok(d), _full((1, d)), tok(R_QK_WIDTH), blk_t, blk_t, tok(R_V_WIDTH), tok(A_WIDTH),
                  _full(dmask.shape), _full(qdec.shape),
                  _full(w.shape), _full(wa.shape), _full(wb.shape), _full(wo.shape)] + cast_specs,
        out_specs=[tok(d)] + cast_specs,
        out_shape=[jax.ShapeDtypeStruct((b, s, d), F32)] + cast_shapes,
        scratch_shapes=[pltpu.VMEM((R_HEADS, R_QK_DIM, R_V_DIM), F32),
                        pltpu.VMEM((t, R_V_WIDTH), BF16)],
        compiler_params=_params("parallel", "arbitrary"),
        name="mix",
    )(x, g, rq, rkt, rktd, rv, ya, dmask, qdec, w, wa, wb, wo, *sliced)
    return out[0], [o.reshape(w.shape) for o, w in zip(out[1:], cast_ahead)]


def _cross_kernel(x_ref, g_ref, m_ref, gm_ref, wkv_ref, wq_ref, wo_ref, *rest):
    (o_ref,), (qk_ref, vo_ref, p_ref) = _split_cast_refs(rest, 1, 3)
    d = x_ref.shape[2]
    dh = d // X_HEADS
    mlen = m_ref.shape[1]
    cols = lambda hd: slice(hd * dh, (hd + 1) * dh)
    mcols = lambda hd: slice(hd * mlen, (hd + 1) * mlen)

    @pl.when(pl.program_id(1) == 0)
    def _():
        mb = _rms(m_ref[0], gm_ref[...]).astype(BF16)
        kt = _dot(mb, wkv_ref[:, 0:d]).T.astype(BF16)
        v = _dot(mb, wkv_ref[:, d:]).astype(BF16)
        for hd in range(X_HEADS):
            qk_ref[:, mcols(hd)] = (_dot(wq_ref[:, cols(hd)], kt[cols(hd), :]) * dh ** -0.5).astype(BF16)
            vo_ref[mcols(hd), :] = _dot(v[:, cols(hd)], wo_ref[cols(hd), :]).astype(BF16)

    n_parts = CROSS_PARTS
    t = x_ref.shape[1] // n_parts
    part = lambda r: slice(r * t, (r + 1) * t)

    def scores_of(r):
        hb = _rms(x_ref[0, part(r), :], g_ref[...]).astype(BF16)
        return _dot(hb, qk_ref[...])

    nxt = scores_of(0)
    for r in range(n_parts):
        scores = nxt
        if r + 1 < n_parts:
            nxt = scores_of(r + 1)
        for hd in range(X_HEADS):
            s = scores[:, mcols(hd)]
            p = jnp.exp(s - jnp.max(s, axis=-1, keepdims=True))
            p_ref[part(r), mcols(hd)] = (p / jnp.sum(p, axis=-1, keepdims=True)).astype(BF16)
        o_ref[0, part(r), :] = x_ref[0, part(r), :] + _dot(p_ref[part(r), :], vo_ref[...])


def _cross(x, g, mem, gm, wkv, wq, wo, cast_ahead):
    b, s, d = x.shape
    m = mem.shape[1]
    t = ROW_TILE
    steps = s // t
    tok = pl.BlockSpec((1, t, d), lambda bi, i: (bi, i, 0))
    sliced, cast_specs, cast_shapes = _cast_ahead(cast_ahead, b * steps, lambda bi, i: bi * steps + i)
    out = pl.pallas_call(
        _cross_kernel,
        grid=(b, steps),
        in_specs=[tok, _full((1, d)), pl.BlockSpec((1, m, d), lambda bi, i: (bi, 0, 0)), _full((1, d)),
                  _full(wkv.shape), _full(wq.shape), _full(wo.shape)] + cast_specs,
        out_specs=[tok] + cast_specs,
        out_shape=[jax.ShapeDtypeStruct((b, s, d), F32)] + cast_shapes,
        scratch_shapes=[pltpu.VMEM((d, X_HEADS * m), BF16),
                        pltpu.VMEM((X_HEADS * m, d), BF16),
                        pltpu.VMEM((t, X_HEADS * m), BF16)],
        compiler_params=_params("parallel", "arbitrary"),
        name="cross",
    )(x, g, mem, gm, wkv, wq, wo, *sliced)
    return out[0], [o.reshape(w.shape) for o, w in zip(out[1:], cast_ahead)]


def _ffn_kernel(x_ref, g_ref, wg_ref, wu_ref, wd_ref, *rest):
    o_ref = rest[-1]
    x = x_ref[...]
    hb = _rms(x, g_ref[...]).astype(BF16)
    n_chunks = wg_ref.shape[1] // FF_CHUNK
    cols = lambda c: slice(c * FF_CHUNK, (c + 1) * FF_CHUNK)

    def up(c):
        return _dot(hb, wg_ref[:, cols(c)]), _dot(hb, wu_ref[:, cols(c)])

    acc = x
    nxt = up(0)
    for c in range(n_chunks):
        gate, upv = nxt
        if c + 1 < n_chunks:
            nxt = up(c + 1)
        act = gate * _sigmoid(gate) * upv
        acc = acc + _dot(act.astype(BF16), wd_ref[cols(c), :])
    o_ref[...] = _rms(acc, rest[0][...]) if len(rest) == 2 else acc


def _ffn(x2, g, wg, wu, wd, final_gain=None):
    n, d = x2.shape
    t = ROW_TILE
    tok = pl.BlockSpec((t, d), lambda i: (i, 0))
    final = () if final_gain is None else (final_gain,)
    return pl.pallas_call(
        _ffn_kernel,
        grid=(n // t,),
        in_specs=[tok, _full((1, d)), _full(wg.shape), _full(wu.shape), _full(wd.shape)] + [_full((1, d))] * len(final),
        out_specs=tok,
        out_shape=jax.ShapeDtypeStruct((n, d), F32),
        compiler_params=_params("parallel"),
        name="ffn",
    )(x2, g, wg, wu, wd, *final)


def _rope_tables(s):
    def cs(dim):
        inv = ROPE_THETA ** (-jnp.arange(0, dim, 2, dtype=F32) / dim)
        ang = jnp.arange(s).astype(F32)[:, None] * inv[None, :]
        return jnp.cos(ang), jnp.sin(ang)

    ca, sa = cs(A_HEAD_DIM)
    cr, sr = cs(R_QK_DIM)
    sign = lambda half: jnp.asarray(np.tile(np.repeat([-1.0, 1.0], half), LANES // (2 * half))[None, :], F32)
    return (jnp.tile(ca, (1, 4)), jnp.tile(sa, (1, 4)) * sign(A_HEAD_DIM // 2),
            jnp.tile(cr, (1, 2)), jnp.tile(sr, (1, 2)) * sign(R_QK_DIM // 2),
            ca.T, sa.T, cr.T, sr.T)


def _decay_tables(s):
    c = R_CHUNK
    log_g = np.log(1.0 - np.exp2(-5.0 - np.arange(R_HEADS, dtype=np.float64)))
    i = np.arange(c, dtype=np.float64)
    diff = i[:, None] - i[None, :]
    dmask = np.where(diff >= 0, np.exp(log_g[:, None, None] * np.maximum(diff, 0.0)), 0.0)
    q_dec = np.exp(log_g[:, None] * (i + 1.0))
    k_dec = np.exp(log_g[:, None] * (c - 1.0 - i))
    kdec_t = np.zeros((SUBLANES, s), np.float64)
    kdec_t[:R_HEADS] = np.tile(k_dec, (1, s // c))
    qdec = np.broadcast_to(q_dec[:, :, None], (R_HEADS, c, R_V_DIM))
    chunk_dec = tuple(float(v) for v in np.exp(log_g * c))
    return (jnp.asarray(dmask, F32), jnp.asarray(qdec, F32), jnp.asarray(kdec_t, F32), chunk_dec)


def kernel(x, mem, norm_mix, w_in, w_branch_a, w_branch_b, w_out, norm_cross, norm_mem,
           w_xq, w_xkv, w_xo, norm_ffn, w_gate, w_up, w_down, norm_final):
    b, s, d = x.shape
    steps = (TOKEN_TILE * QKV_BLOCKS, TOKEN_TILE * MIX_BLOCKS, ROW_TILE)
    assert all(s % t == 0 for t in steps) and b % MOBA_ROWS == 0 and d % LANES == 0
    rope = _rope_tables(s)
    dmask, qdec, kdec_t, chunk_dec = _decay_tables(s)
    row = lambda v: v.reshape(1, d)
    assert w_in.shape[2] == COL_GATES + R_V_WIDTH + 2 * d

    for l in range(w_in.shape[0]):
        w = w_in[l].astype(BF16)
        wqt, wvt, wkt = (w[:, c:c + n].T for c, n in ((COL_AQ, A_WIDTH), (COL_AV, A_WIDTH), (COL_RK, R_QK_WIDTH)))
        (k_a, kmean, qt_a, vt_a, rq, rv, rkt, rktd), (wa, wb, wo) = _qkv(
            x, row(norm_mix[l]), w, wqt, wvt, wkt, rope + (kdec_t,), (w_branch_a[l], w_branch_b[l], w_out[l]))
        y_a = _moba(qt_a, k_a, vt_a, kmean.reshape(b, s // MOBA_BLOCK, A_WIDTH))
        x, (wxkv, wxq, wxo) = _mix(x, row(norm_mix[l]), rq, rkt, rktd, rv, y_a, dmask, qdec, chunk_dec,
                                   w, wa, wb, wo, (w_xkv[l], w_xq[l], w_xo[l]))
        x, (wg, wu, wd) = _cross(x, row(norm_cross[l]), mem, row(norm_mem[l]), wxkv, wxq, wxo,
                                 (w_gate[l], w_up[l], w_down[l]))
        last = l == w_in.shape[0] - 1
        x = _ffn(x.reshape(b * s, d), row(norm_ffn[l]), wg, wu, wd,
                 row(norm_final) if last else None).reshape(b, s, d)
    return x
```

```python
import functools
import math

import numpy as np
import jax
import jax.numpy as jnp
from jax import lax
from jax.experimental import pallas as pl
from jax.experimental.pallas import tpu as pltpu

F32 = jnp.float32
BF16 = jnp.bfloat16

EPS = 1e-6
ROPE_THETA = 10000.0
A_HEADS = 8
A_HEAD_DIM = 64
A_WIDTH = A_HEADS * A_HEAD_DIM
MOBA_BLOCK = 256
MOBA_TOPK = 3
R_HEADS = 4
R_QK_DIM = 128
R_V_DIM = 256
R_QK_WIDTH = R_HEADS * R_QK_DIM
R_V_WIDTH = R_HEADS * R_V_DIM
R_CHUNK = 256
X_HEADS = 4
COL_AQ, COL_AK, COL_AV = 0, A_WIDTH, 2 * A_WIDTH
COL_RQ = 3 * A_WIDTH
COL_RK = COL_RQ + R_QK_WIDTH
COL_RV = COL_RK + R_QK_WIDTH
COL_GATES = COL_RV + R_V_WIDTH

LANES = 128
SUBLANES = 8
BF16_ROWS = 16
VMEM_LIMIT = 56 * 1024 * 1024
assert 2 * A_HEAD_DIM == LANES and R_QK_DIM == LANES

TOKEN_TILE = MOBA_BLOCK
QKV_BLOCKS = 4
MIX_BLOCKS = 2
MIX_LATE_COLS = 256
ROW_TILE = 1024
CROSS_PARTS = 4
FF_CHUNK = 256
MOBA_ROWS = 2
MOBA_UNROLL = 4
MOBA_LOOKAHEAD = 7
MOBA_ONES_ROWS = BF16_ROWS

NEG = -1e30
LOG2_E = math.log2(math.e)


def _dot(a, b):
    return jnp.dot(a, b, preferred_element_type=F32)


def _rms(x, g):
    return x * lax.rsqrt(jnp.mean(x * x, axis=-1, keepdims=True) + EPS) * g


def _sigmoid(x):
    return 1.0 / (1.0 + jnp.exp(-x))


def _params(*sem):
    return pltpu.CompilerParams(dimension_semantics=sem, vmem_limit_bytes=VMEM_LIMIT)


def _cast_ahead(mats, n_steps, step_of):
    assert all(w.shape[0] % n_steps == 0 for w in mats)
    sliced = [w.reshape(n_steps, w.shape[0] // n_steps, w.shape[1]) for w in mats]
    specs = [pl.BlockSpec((1,) + w.shape[1:], lambda *ids: (step_of(*ids), 0, 0)) for w in sliced]
    return sliced, specs, [jax.ShapeDtypeStruct(w.shape, BF16) for w in sliced]


def _split_cast_refs(rest, n_out, n_scratch):
    n = (len(rest) - n_out - n_scratch) // 2
    outs = rest[n:n + n_out]
    scratch = rest[len(rest) - n_scratch:]
    for src, dst in zip(rest[:n], rest[n + n_out:2 * n + n_out]):
        dst[...] = src[...].astype(BF16)
    return outs, scratch


def _full(shape):
    n = len(shape)
    return pl.BlockSpec(shape, lambda *_: (0,) * n, pipeline_mode=pl.Buffered(1))


QKV_INPUTS, QKV_OUTPUTS = 15, 8


def _qkv_kernel(*refs):
    outs, _ = _split_cast_refs(refs[QKV_INPUTS:], QKV_OUTPUTS, 0)
    for sb in range(QKV_BLOCKS):
        _qkv_block(sb, *refs[:QKV_INPUTS], *outs)


def _qkv_block(sb, x_ref, g_ref, w_ref, wqt_ref, wvt_ref, wkt_ref, cosa_ref, sina_ref, cosr_ref, sinr_ref,
               costa_ref, sinta_ref, costr_ref, sintr_ref, kdec_ref,
               k_ref, kmean_ref, qt_ref, vt_ref, rq_ref, rv_ref, rkt_ref, rktd_ref):
    t = TOKEN_TILE
    tok = slice(sb * t, (sb + 1) * t)
    h = _rms(x_ref[0, tok, :], g_ref[...])
    hb = h.astype(BF16)
    ht = h.T.astype(BF16)
    lane = lax.broadcasted_iota(jnp.int32, (t, LANES), 1)
    a_half, r_half = A_HEAD_DIM // 2, R_QK_DIM // 2
    first_half = (lane & a_half) == 0

    ak = _dot(hb, w_ref[:, COL_AK:COL_AV])
    cosa, sina = cosa_ref[tok, :], sina_ref[tok, :]
    for c in range(A_WIDTH // LANES):
        blk = ak[:, c * LANES:(c + 1) * LANES]
        rot = jnp.where(first_half, pltpu.roll(blk, LANES - a_half, 1), pltpu.roll(blk, a_half, 1))
        kr = blk * cosa + rot * sina
        k_ref[0, sb, :, c * LANES:(c + 1) * LANES] = kr.astype(BF16)
        kmean_ref[0, sb, :, c * LANES:(c + 1) * LANES] = jnp.mean(kr, axis=0, keepdims=True)

    rq = _dot(hb, w_ref[:, COL_RQ:COL_RK])
    cosr, sinr = cosr_ref[tok, :], sinr_ref[tok, :]
    for c in range(R_HEADS):
        blk = rq[:, c * R_QK_DIM:(c + 1) * R_QK_DIM]
        rq_ref[0, tok, c * R_QK_DIM:(c + 1) * R_QK_DIM] = (blk * cosr + pltpu.roll(blk, r_half, 1) * sinr).astype(BF16)

    rv_ref[0, tok, :] = _dot(hb, w_ref[:, COL_RV:COL_GATES]).astype(BF16)

    qt = _dot(wqt_ref[...], ht)
    cost, sint = costa_ref[:, tok], sinta_ref[:, tok]
    half = A_HEAD_DIM // 2
    scale_a = A_HEAD_DIM ** -0.5 * LOG2_E
    for hd in range(A_HEADS):
        x1 = qt[hd * A_HEAD_DIM:hd * A_HEAD_DIM + half]
        x2 = qt[hd * A_HEAD_DIM + half:(hd + 1) * A_HEAD_DIM]
        qt_ref[0, sb, hd * A_HEAD_DIM:hd * A_HEAD_DIM + half, :] = ((x1 * cost - x2 * sint) * scale_a).astype(BF16)
        qt_ref[0, sb, hd * A_HEAD_DIM + half:(hd + 1) * A_HEAD_DIM, :] = ((x2 * cost + x1 * sint) * scale_a).astype(BF16)

    vt_ref[0, sb] = _dot(wvt_ref[...], ht).astype(BF16)

    rkt = _dot(wkt_ref[...], ht)
    cost, sint = costr_ref[:, tok], sintr_ref[:, tok]
    half = R_QK_DIM // 2
    scale_r = R_QK_DIM ** -0.5
    for hd in range(R_HEADS):
        x1 = rkt[hd * R_QK_DIM:hd * R_QK_DIM + half]
        x2 = rkt[hd * R_QK_DIM + half:(hd + 1) * R_QK_DIM]
        o1 = (x1 * cost - x2 * sint) * scale_r
        o2 = (x2 * cost + x1 * sint) * scale_r
        dec = kdec_ref[hd:hd + 1, tok]
        rkt_ref[0, sb, hd * R_QK_DIM:hd * R_QK_DIM + half, :] = o1.astype(BF16)
        rkt_ref[0, sb, hd * R_QK_DIM + half:(hd + 1) * R_QK_DIM, :] = o2.astype(BF16)
        rktd_ref[0, sb, hd * R_QK_DIM:hd * R_QK_DIM + half, :] = (o1 * dec).astype(BF16)
        rktd_ref[0, sb, hd * R_QK_DIM + half:(hd + 1) * R_QK_DIM, :] = (o2 * dec).astype(BF16)


def _qkv(x, g, w, wqt, wvt, wkt, tabs, cast_ahead):
    b, s, d = x.shape
    nb = s // TOKEN_TILE
    qb = QKV_BLOCKS
    t = qb * TOKEN_TILE
    steps = nb // qb
    tok = lambda w: pl.BlockSpec((1, t, w), lambda bi, i: (bi, i, 0))
    blk_t = lambda r: pl.BlockSpec((1, qb, r, TOKEN_TILE), lambda bi, i: (bi, i, 0, 0))
    nat_tab = pl.BlockSpec((t, LANES), lambda bi, i: (i, 0))
    tr_tab = lambda r: pl.BlockSpec((r, t), lambda bi, i: (0, i))
    sliced, cast_specs, cast_shapes = _cast_ahead(cast_ahead, b * steps, lambda bi, i: bi * steps + i)
    in_specs = [tok(d), _full((1, d)), _full(w.shape), _full(wqt.shape), _full(wvt.shape), _full(wkt.shape),
                nat_tab, nat_tab, nat_tab, nat_tab,
                tr_tab(A_HEAD_DIM // 2), tr_tab(A_HEAD_DIM // 2),
                tr_tab(R_QK_DIM // 2), tr_tab(R_QK_DIM // 2), tr_tab(SUBLANES)]
    assert len(in_specs) == QKV_INPUTS
    out = pl.pallas_call(
        _qkv_kernel,
        grid=(b, steps),
        in_specs=in_specs + cast_specs,
        out_specs=[pl.BlockSpec((1, qb, TOKEN_TILE, A_WIDTH), lambda bi, i: (bi, i, 0, 0)),
                   pl.BlockSpec((1, qb, 1, A_WIDTH), lambda bi, i: (bi, i, 0, 0)),
                   blk_t(A_WIDTH), blk_t(A_WIDTH), tok(R_QK_WIDTH), tok(R_V_WIDTH),
                   blk_t(R_QK_WIDTH), blk_t(R_QK_WIDTH)] + cast_specs,
        out_shape=[jax.ShapeDtypeStruct((b, nb, TOKEN_TILE, A_WIDTH), BF16),
                   jax.ShapeDtypeStruct((b, nb, 1, A_WIDTH), F32),
                   jax.ShapeDtypeStruct((b, nb, A_WIDTH, TOKEN_TILE), BF16),
                   jax.ShapeDtypeStruct((b, nb, A_WIDTH, TOKEN_TILE), BF16),
                   jax.ShapeDtypeStruct((b, s, R_QK_WIDTH), BF16),
                   jax.ShapeDtypeStruct((b, s, R_V_WIDTH), BF16),
                   jax.ShapeDtypeStruct((b, nb, R_QK_WIDTH, TOKEN_TILE), BF16),
                   jax.ShapeDtypeStruct((b, nb, R_QK_WIDTH, TOKEN_TILE), BF16)] + cast_shapes,
        compiler_params=_params("parallel", "parallel"),
        name="qkv",
    )(x, g, w, wqt, wvt, wkt, *tabs, *sliced)
    return out[:QKV_OUTPUTS], [o.reshape(w.shape) for o, w in zip(out[QKV_OUTPUTS:], cast_ahead)]


def _moba_kernel(qt_ref, k_ref, vt_ref, km_ref, o_ref, qm_ref, m_ref, acc_ref, ot_ref, s_ref):
    i = pl.program_id(1)
    nb = km_ref.shape[1]
    t = MOBA_BLOCK
    dh = A_HEAD_DIM
    row = lax.broadcasted_iota(jnp.int32, (2 * dh, t), 0)
    blk = lax.broadcasted_iota(jnp.int32, (nb, t), 0)
    past = blk < i
    causal = (lax.broadcasted_iota(jnp.int32, (t, t), 0) <= lax.broadcasted_iota(jnp.int32, (t, t), 1))
    ones = jnp.ones((MOBA_ONES_ROWS, t), BF16)
    lanes = lambda hd: slice((hd // 2) * LANES, (hd // 2 + 1) * LANES)
    rows = lambda hd: slice(hd * dh, (hd + 1) * dh)
    row_heads = [(b, hd) for b in range(MOBA_ROWS) for hd in range(A_HEADS)]

    def select_blocks(b, hd):
        hh = hd % 2
        qt_pair = qt_ref[b, 0, lanes(hd), :]
        mine = (row >= hh * dh) & (row < (hh + 1) * dh)
        qt = jnp.where(mine, qt_pair, jnp.zeros_like(qt_pair))
        qm_ref[b, hd, 0:2 * dh, :] = qt

        km = km_ref[b, :, lanes(hd)]
        km_hi = km.astype(BF16)
        km_lo = (km - km_hi.astype(F32)).astype(BF16)
        bs = jnp.where(past, _dot(km_hi, qt) + _dot(km_lo, qt), -jnp.inf)
        picked = jnp.zeros((nb, t), jnp.bool_)
        for _ in range(MOBA_TOPK):
            best = jnp.max(bs, axis=0, keepdims=True)
            first = jnp.min(jnp.where(bs == best, blk, nb), axis=0, keepdims=True)
            hit = blk == first
            picked = picked | hit
            bs = jnp.where(hit, -jnp.inf, bs)
        bias = jnp.where(past & picked, 0.0, NEG).astype(BF16)
        qm_ref[b, hd, 2 * dh:, :] = jnp.concatenate([bias, jnp.zeros((2 * dh - nb, t), BF16)], axis=0)

    key_lane = lax.broadcasted_iota(jnp.int32, (t, 2 * dh), 1)

    def keys_with_block_column(b, j):
        onehot = jnp.where(key_lane == j, 1.0, 0.0).astype(BF16)
        return [jnp.concatenate([k_ref[b, j, :, lanes(2 * p)], onehot], axis=1) for p in range(A_HEADS // 2)]

    def pv(b, j, hd, p):
        vt_aug = jnp.concatenate([vt_ref[b, j, rows(hd), :], ones], axis=0)
        return _dot(vt_aug, p.astype(BF16))

    def own_unit(b, hd):
        def scores():
            return jnp.where(causal, _dot(k_ref[b, i, :, lanes(hd)], qm_ref[b, hd, 0:2 * dh, :]), NEG)

        def update(s, m0):
            m_ref[b, hd] = jnp.broadcast_to(m0, (SUBLANES, t))
            acc_ref[b, hd] = pv(b, i, hd, jnp.exp2(s - m0))

        return scores, update

    def past_units(b, j):
        keys = keys_with_block_column(b, j)

        def unit(hd):
            def scores():
                return _dot(keys[hd // 2], qm_ref[b, hd])

            def update(s, m_blk):
                m_old = m_ref[b, hd]
                m_new = jnp.maximum(m_old, m_blk)
                alpha = jnp.exp2(m_old - m_new)
                m_ref[b, hd] = m_new
                acc_ref[b, hd] = alpha[0:1, :] * acc_ref[b, hd] + pv(b, j, hd, jnp.exp2(s - m_new[0:1, :]))

            return scores, update

        return [unit(hd) for hd in range(A_HEADS)]

    def pipelined(units):
        blk_max = {}
        slots = MOBA_LOOKAHEAD + 1
        for step in range(len(units) + MOBA_LOOKAHEAD):
            if step < len(units):
                sc = units[step][0]()
                s_ref[step % slots] = sc
                blk_max[step] = jnp.max(sc, axis=0, keepdims=True)
            if step >= MOBA_LOOKAHEAD:
                u = step - MOBA_LOOKAHEAD
                units[u][1](s_ref[u % slots], blk_max.pop(u))

    n_groups = i // MOBA_UNROLL
    for left in range(MOBA_UNROLL):
        @pl.when(i % MOBA_UNROLL == left)
        def _(left=left):
            for b, hd in row_heads:
                select_blocks(b, hd)
            units = [own_unit(b, hd) for b, hd in row_heads]
            for r in range(left):
                for b in range(MOBA_ROWS):
                    units += past_units(b, n_groups * MOBA_UNROLL + r)
            pipelined(units)

    def group_body(jj, carry):
        pipelined([u for r in range(MOBA_UNROLL) for b in range(MOBA_ROWS)
                   for u in past_units(b, MOBA_UNROLL * jj + r)])
        return carry

    lax.fori_loop(0, n_groups, group_body, 0)

    for b in range(MOBA_ROWS):
        for hd in range(A_HEADS):
            ot_ref[rows(hd), :] = acc_ref[b, hd, 0:dh, :] / acc_ref[b, hd, dh:dh + 1, :]
        o_ref[b] = ot_ref[...].T.astype(BF16)


def _moba(qt, k, vt, kmean):
    b, nb, t, w = k.shape
    r = MOBA_ROWS
    assert b % r == 0
    return pl.pallas_call(
        _moba_kernel,
        grid=(b // r, nb),
        in_specs=[pl.BlockSpec((r, 1, w, t), lambda bi, i: (bi, i, 0, 0)),
                  pl.BlockSpec((r, nb, t, w), lambda bi, i: (bi, 0, 0, 0)),
                  pl.BlockSpec((r, nb, w, t), lambda bi, i: (bi, 0, 0, 0)),
                  pl.BlockSpec((r, nb, w), lambda bi, i: (bi, 0, 0))],
        out_specs=pl.BlockSpec((r, t, w), lambda bi, i: (bi, i, 0)),
        out_shape=jax.ShapeDtypeStruct((b, nb * t, w), BF16),
        scratch_shapes=[pltpu.VMEM((r, A_HEADS, 4 * A_HEAD_DIM, t), BF16),
                        pltpu.VMEM((r, A_HEADS, SUBLANES, t), F32),
                        pltpu.VMEM((r, A_HEADS, A_HEAD_DIM + MOBA_ONES_ROWS, t), F32),
                        pltpu.VMEM((w, t), F32),
                        pltpu.VMEM((MOBA_LOOKAHEAD + 1, t, t), F32)],
        compiler_params=_params("parallel", "parallel"),
        name="moba",
    )(qt, k, vt, kmean)


def _mix_kernel(chunk_dec, x_ref, g_ref, rq_ref, rkt_ref, rktd_ref, rv_ref, ya_ref, dmask_ref, qdec_ref,
                w_ref, wa_ref, wb_ref, wo_ref, *rest):
    (o_ref,), (state_ref, yb_ref) = _split_cast_refs(rest, 1, 2)

    @pl.when(pl.program_id(1) == 0)
    def _():
        state_ref[...] = jnp.zeros_like(state_ref)

    x = x_ref[0]
    d = x.shape[1]
    hb = _rms(x, g_ref[...]).astype(BF16)
    c = R_CHUNK
    per_blk = TOKEN_TILE // c
    n_chunks = MIX_BLOCKS * per_blk
    rows = lambda ci: slice(ci * c, (ci + 1) * c)
    qk = lambda hd: slice(hd * R_QK_DIM, (hd + 1) * R_QK_DIM)
    vv = lambda hd: slice(hd * R_V_DIM, (hd + 1) * R_V_DIM)
    units = [(ci, hd) for ci in range(n_chunks) for hd in range(R_HEADS)]
    keys_t = lambda ref, ci, hd: ref[0, ci // per_blk, qk(hd), rows(ci % per_blk)]

    att = {u: _dot(rq_ref[0, rows(u[0]), qk(u[1])], keys_t(rkt_ref, *u)) for u in units}
    kv = {u: _dot(keys_t(rktd_ref, *u), rv_ref[0, rows(u[0]), vv(u[1])]) for u in units}
    rg = _dot(hb, w_ref[:, COL_GATES:COL_GATES + R_V_WIDTH])
    w = MIX_LATE_COLS
    late = ([lambda c=c: _dot(ya_ref[0], wa_ref[:, c:c + w]) for c in range(0, d, w)]
            + [lambda c=c: _dot(hb, w_ref[:, c:c + w])
               for c in range(COL_GATES + R_V_WIDTH, COL_GATES + R_V_WIDTH + 2 * d, w)])
    late_out = []

    cross = {}
    for hd in range(R_HEADS):
        st = state_ref[hd]
        for ci in range(n_chunks):
            cross[ci, hd] = _dot(rq_ref[0, rows(ci), qk(hd)], st.astype(BF16))
            st = chunk_dec[hd] * st + kv[ci, hd]
        state_ref[hd] = st

    gate = rg * _sigmoid(rg)
    for n, (ci, hd) in enumerate(units):
        inner = _dot((att[ci, hd] * dmask_ref[hd]).astype(BF16), rv_ref[0, rows(ci), vv(hd)])
        y = inner + cross[ci, hd] * qdec_ref[hd]
        mu = jnp.mean(y, axis=-1, keepdims=True)
        yc = y - mu
        var = jnp.mean(yc * yc, axis=-1, keepdims=True)
        yb_ref[rows(ci), vv(hd)] = (yc * lax.rsqrt(var + EPS) * gate[rows(ci), vv(hd)]).astype(BF16)
        for k in range(n * len(late) // len(units), (n + 1) * len(late) // len(units)):
            late_out.append(late[k]())

    pieces = d // w
    ta, ga, gb = (jnp.concatenate(late_out[k * pieces:(k + 1) * pieces], axis=1) for k in range(3))
    blocks = [slice(r * TOKEN_TILE, (r + 1) * TOKEN_TILE) for r in range(MIX_BLOCKS)]
    tb = [_dot(yb_ref[tok, :], wb_ref[...]) for tok in blocks]
    for tok, tb_blk in zip(blocks, tb):
        merged = _sigmoid(ga[tok]) * ta[tok] + _sigmoid(gb[tok]) * tb_blk
        o_ref[0, tok, :] = x[tok] + _dot(merged.astype(BF16), wo_ref[...])


def _mix(x, g, rq, rkt, rktd, rv, ya, dmask, qdec, chunk_dec, w, wa, wb, wo, cast_ahead):
    b, s, d = x.shape
    t = MIX_BLOCKS * TOKEN_TILE
    steps = s // t
    tok = lambda w: pl.BlockSpec((1, t, w), lambda bi, i: (bi, i, 0))
    blk_t = pl.BlockSpec((1, MIX_BLOCKS, R_QK_WIDTH, TOKEN_TILE), lambda bi, i: (bi, i, 0, 0))
    sliced, cast_specs, cast_shapes = _cast_ahead(cast_ahead, b * steps, lambda bi, i: bi * steps + i)
    out = pl.pallas_call(
        functools.partial(_mix_kernel, chunk_dec),
        grid=(b, steps),
        in_specs=[tok(d), _full((1, d)), tok(R_QK_WIDTH), blk_t, blk_t, tok(R_V_WIDTH), tok(A_WIDTH),
                  _full(dmask.shape), _full(qdec.shape),
                  _full(w.shape), _full(wa.shape), _full(wb.shape), _full(wo.shape)] + cast_specs,
        out_specs=[tok(d)] + cast_specs,
        out_shape=[jax.ShapeDtypeStruct((b, s, d), F32)] + cast_shapes,
        scratch_shapes=[pltpu.VMEM((R_HEADS, R_QK_DIM, R_V_DIM), F32),
                        pltpu.VMEM((t, R_V_WIDTH), BF16)],
        compiler_params=_params("parallel", "arbitrary"),
        name="mix",
    )(x, g, rq, rkt, rktd, rv, ya, dmask, qdec, w, wa, wb, wo, *sliced)
    return out[0], [o.reshape(w.shape) for o, w in zip(out[1:], cast_ahead)]


def _cross_kernel(x_ref, g_ref, m_ref, gm_ref, wkv_ref, wq_ref, wo_ref, *rest):
    (o_ref,), (qk_ref, vo_ref, p_ref) = _split_cast_refs(rest, 1, 3)
    d = x_ref.shape[2]
    dh = d // X_HEADS
    mlen = m_ref.shape[1]
    cols = lambda hd: slice(hd * dh, (hd + 1) * dh)
    mcols = lambda hd: slice(hd * mlen, (hd + 1) * mlen)

    @pl.when(pl.program_id(1) == 0)
    def _():
        mb = _rms(m_ref[0], gm_ref[...]).astype(BF16)
        kt = _dot(mb, wkv_ref[:, 0:d]).T.astype(BF16)
        v = _dot(mb, wkv_ref[:, d:]).astype(BF16)
        for hd in range(X_HEADS):
            qk_ref[:, mcols(hd)] = (_dot(wq_ref[:, cols(hd)], kt[cols(hd), :]) * dh ** -0.5).astype(BF16)
            vo_ref[mcols(hd), :] = _dot(v[:, cols(hd)], wo_ref[cols(hd), :]).astype(BF16)

    n_parts = CROSS_PARTS
    t = x_ref.shape[1] // n_parts
    part = lambda r: slice(r * t, (r + 1) * t)

    def scores_of(r):
        hb = _rms(x_ref[0, part(r), :], g_ref[...]).astype(BF16)
        return _dot(hb, qk_ref[...])

    nxt = scores_of(0)
    for r in range(n_parts):
        scores = nxt
        if r + 1 < n_parts:
            nxt = scores_of(r + 1)
        for hd in range(X_HEADS):
            s = scores[:, mcols(hd)]
            p = jnp.exp(s - jnp.max(s, axis=-1, keepdims=True))
            p_ref[part(r), mcols(hd)] = (p / jnp.sum(p, axis=-1, keepdims=True)).astype(BF16)
        o_ref[0, part(r), :] = x_ref[0, part(r), :] + _dot(p_ref[part(r), :], vo_ref[...])


def _cross(x, g, mem, gm, wkv, wq, wo, cast_ahead):
    b, s, d = x.shape
    m = mem.shape[1]
    t = ROW_TILE
    steps = s // t
    tok = pl.BlockSpec((1, t, d), lambda bi, i: (bi, i, 0))
    sliced, cast_specs, cast_shapes = _cast_ahead(cast_ahead, b * steps, lambda bi, i: bi * steps + i)
    out = pl.pallas_call(
        _cross_kernel,
        grid=(b, steps),
        in_specs=[tok, _full((1, d)), pl.BlockSpec((1, m, d), lambda bi, i: (bi, 0, 0)), _full((1, d)),
                  _full(wkv.shape), _full(wq.shape), _full(wo.shape)] + cast_specs,
        out_specs=[tok] + cast_specs,
        out_shape=[jax.ShapeDtypeStruct((b, s, d), F32)] + cast_shapes,
        scratch_shapes=[pltpu.VMEM((d, X_HEADS * m), BF16),
                        pltpu.VMEM((X_HEADS * m, d), BF16),
                        pltpu.VMEM((t, X_HEADS * m), BF16)],
        compiler_params=_params("parallel", "arbitrary"),
        name="cross",
    )(x, g, mem, gm, wkv, wq, wo, *sliced)
    return out[0], [o.reshape(w.shape) for o, w in zip(out[1:], cast_ahead)]


def _ffn_kernel(x_ref, g_ref, wg_ref, wu_ref, wd_ref, *rest):
    o_ref = rest[-1]
    x = x_ref[...]
    hb = _rms(x, g_ref[...]).astype(BF16)
    n_chunks = wg_ref.shape[1] // FF_CHUNK
    cols = lambda c: slice(c * FF_CHUNK, (c + 1) * FF_CHUNK)

    def up(c):
        return _dot(hb, wg_ref[:, cols(c)]), _dot(hb, wu_ref[:, cols(c)])

    acc = x
    nxt = up(0)
    for c in range(n_chunks):
        gate, upv = nxt
        if c + 1 < n_chunks:
            nxt = up(c + 1)
        act = gate * _sigmoid(gate) * upv
        acc = acc + _dot(act.astype(BF16), wd_ref[cols(c), :])
    o_ref[...] = _rms(acc, rest[0][...]) if len(rest) == 2 else acc


def _ffn(x2, g, wg, wu, wd, final_gain=None):
    n, d = x2.shape
    t = ROW_TILE
    tok = pl.BlockSpec((t, d), lambda i: (i, 0))
    final = () if final_gain is None else (final_gain,)
    return pl.pallas_call(
        _ffn_kernel,
        grid=(n // t,),
        in_specs=[tok, _full((1, d)), _full(wg.shape), _full(wu.shape), _full(wd.shape)] + [_full((1, d))] * len(final),
        out_specs=tok,
        out_shape=jax.ShapeDtypeStruct((n, d), F32),
        compiler_params=_params("parallel"),
        name="ffn",
    )(x2, g, wg, wu, wd, *final)


def _rope_tables(s):
    def cs(dim):
        inv = ROPE_THETA ** (-jnp.arange(0, dim, 2, dtype=F32) / dim)
        ang = jnp.arange(s).astype(F32)[:, None] * inv[None, :]
        return jnp.cos(ang), jnp.sin(ang)

    with jax.ensure_compile_time_eval():
        ca, sa = cs(A_HEAD_DIM)
        cr, sr = cs(R_QK_DIM)
        sign = lambda half: jnp.asarray(np.tile(np.repeat([-1.0, 1.0], half), LANES // (2 * half))[None, :], F32)
        return (jnp.tile(ca, (1, 4)), jnp.tile(sa, (1, 4)) * sign(A_HEAD_DIM // 2),
                jnp.tile(cr, (1, 2)), jnp.tile(sr, (1, 2)) * sign(R_QK_DIM // 2),
                ca.T, sa.T, cr.T, sr.T)


def _decay_tables(s):
    c = R_CHUNK
    log_g = np.log(1.0 - np.exp2(-5.0 - np.arange(R_HEADS, dtype=np.float64)))
    i = np.arange(c, dtype=np.float64)
    diff = i[:, None] - i[None, :]
    dmask = np.where(diff >= 0, np.exp(log_g[:, None, None] * np.maximum(diff, 0.0)), 0.0)
    q_dec = np.exp(log_g[:, None] * (i + 1.0))
    k_dec = np.exp(log_g[:, None] * (c - 1.0 - i))
    kdec_t = np.zeros((SUBLANES, s), np.float64)
    kdec_t[:R_HEADS] = np.tile(k_dec, (1, s // c))
    qdec = np.broadcast_to(q_dec[:, :, None], (R_HEADS, c, R_V_DIM))
    chunk_dec = tuple(float(v) for v in np.exp(log_g * c))
    return (jnp.asarray(dmask, F32), jnp.asarray(qdec, F32), jnp.asarray(kdec_t, F32), chunk_dec)


def kernel(x, mem, norm_mix, w_in, w_branch_a, w_branch_b, w_out, norm_cross, norm_mem,
           w_xq, w_xkv, w_xo, norm_ffn, w_gate, w_up, w_down, norm_final):
    b, s, d = x.shape
    steps = (TOKEN_TILE * QKV_BLOCKS, TOKEN_TILE * MIX_BLOCKS, ROW_TILE)
    assert all(s % t == 0 for t in steps) and b % MOBA_ROWS == 0 and d % LANES == 0
    rope = _rope_tables(s)
    dmask, qdec, kdec_t, chunk_dec = _decay_tables(s)
    row = lambda v: v.reshape(1, d)
    assert w_in.shape[2] == COL_GATES + R_V_WIDTH + 2 * d

    for l in range(w_in.shape[0]):
        w = w_in[l].astype(BF16)
        wqt, wvt, wkt = (w[:, c:c + n].T for c, n in ((COL_AQ, A_WIDTH), (COL_AV, A_WIDTH), (COL_RK, R_QK_WIDTH)))
        (k_a, kmean, qt_a, vt_a, rq, rv, rkt, rktd), (wa, wb, wo) = _qkv(
            x, row(norm_mix[l]), w, wqt, wvt, wkt, rope + (kdec_t,), (w_branch_a[l], w_branch_b[l], w_out[l]))
        y_a = _moba(qt_a, k_a, vt_a, kmean.reshape(b, s // MOBA_BLOCK, A_WIDTH))
        x, (wxkv, wxq, wxo) = _mix(x, row(norm_mix[l]), rq, rkt, rktd, rv, y_a, dmask, qdec, chunk_dec,
                                   w, wa, wb, wo, (w_xkv[l], w_xq[l], w_xo[l]))
        x, (wg, wu, wd) = _cross(x, row(norm_cross[l]), mem, row(norm_mem[l]), wxkv, wxq, wxo,
                                 (w_gate[l], w_up[l], w_down[l]))
        last = l == w_in.shape[0] - 1
        x = _ffn(x.reshape(b * s, d), row(norm_ffn[l]), wg, wu, wd,
                 row(norm_final) if last else None).reshape(b, s, d)
    return x
```

```python
import functools
import math

import numpy as np
import jax
import jax.numpy as jnp
from jax import lax
from jax.experimental import pallas as pl
from jax.experimental.pallas import tpu as pltpu

F32 = jnp.float32
BF16 = jnp.bfloat16

EPS = 1e-6
ROPE_THETA = 10000.0
A_HEADS = 8
A_HEAD_DIM = 64
A_WIDTH = A_HEADS * A_HEAD_DIM
MOBA_BLOCK = 256
MOBA_TOPK = 3
R_HEADS = 4
R_QK_DIM = 128
R_V_DIM = 256
R_QK_WIDTH = R_HEADS * R_QK_DIM
R_V_WIDTH = R_HEADS * R_V_DIM
R_CHUNK = 256
X_HEADS = 4
COL_AQ, COL_AK, COL_AV = 0, A_WIDTH, 2 * A_WIDTH
COL_RQ = 3 * A_WIDTH
COL_RK = COL_RQ + R_QK_WIDTH
COL_RV = COL_RK + R_QK_WIDTH
COL_GATES = COL_RV + R_V_WIDTH

LANES = 128
SUBLANES = 8
BF16_ROWS = 16
VMEM_LIMIT = 56 * 1024 * 1024
assert 2 * A_HEAD_DIM == LANES and R_QK_DIM == LANES

TOKEN_TILE = MOBA_BLOCK
QKV_BLOCKS = 4
MIX_BLOCKS = 2
MIX_LATE_COLS = 256
ROW_TILE = 1024
CROSS_PARTS = 4
FF_CHUNK = 256
MOBA_ROWS = 2
MOBA_UNROLL = 4
MOBA_LOOKAHEAD = 7
MOBA_ONES_ROWS = BF16_ROWS

NEG = -1e30
LOG2_E = math.log2(math.e)


def _dot(a, b):
    return jnp.dot(a, b, preferred_element_type=F32)


def _rms(x, g):
    return x * lax.rsqrt(jnp.mean(x * x, axis=-1, keepdims=True) + EPS) * g


def _sigmoid(x):
    return 1.0 / (1.0 + jnp.exp(-x))


def _params(*sem):
    return pltpu.CompilerParams(dimension_semantics=sem, vmem_limit_bytes=VMEM_LIMIT)


def _cast_ahead(mats, n_steps, step_of):
    assert all(w.shape[0] % n_steps == 0 for w in mats)
    sliced = [w.reshape(n_steps, w.shape[0] // n_steps, w.shape[1]) for w in mats]
    specs = [pl.BlockSpec((1,) + w.shape[1:], lambda *ids: (step_of(*ids), 0, 0)) for w in sliced]
    return sliced, specs, [jax.ShapeDtypeStruct(w.shape, BF16) for w in sliced]


def _split_cast_refs(rest, n_out, n_scratch):
    n = (len(rest) - n_out - n_scratch) // 2
    outs = rest[n:n + n_out]
    scratch = rest[len(rest) - n_scratch:]
    for src, dst in zip(rest[:n], rest[n + n_out:2 * n + n_out]):
        dst[...] = src[...].astype(BF16)
    return outs, scratch


def _full(shape):
    n = len(shape)
    return pl.BlockSpec(shape, lambda *_: (0,) * n, pipeline_mode=pl.Buffered(1))


QKV_INPUTS, QKV_OUTPUTS = 12, 8
QKV_T_COLS = ((COL_AQ, A_WIDTH), (COL_AV, A_WIDTH), (COL_RK, R_QK_WIDTH))


def _qkv_kernel(*refs):
    outs, wt_refs = _split_cast_refs(refs[QKV_INPUTS:], QKV_OUTPUTS, len(QKV_T_COLS))
    w_ref = refs[2]

    @pl.when((pl.program_id(0) == 0) & (pl.program_id(1) == 0))
    def _():
        for wt_ref, (col, n) in zip(wt_refs, QKV_T_COLS):
            for c in range(0, n, TOKEN_TILE):
                wt_ref[c:c + TOKEN_TILE, :] = w_ref[:, col + c:col + c + TOKEN_TILE].astype(F32).T.astype(BF16)

    for sb in range(QKV_BLOCKS):
        _qkv_block(sb, *refs[:3], *wt_refs, *refs[3:QKV_INPUTS], *outs)


def _qkv_block(sb, x_ref, g_ref, w_ref, wqt_ref, wvt_ref, wkt_ref, cosa_ref, sina_ref, cosr_ref, sinr_ref,
               costa_ref, sinta_ref, costr_ref, sintr_ref, kdec_ref,
               k_ref, kmean_ref, qt_ref, vt_ref, rq_ref, rv_ref, rkt_ref, rktd_ref):
    t = TOKEN_TILE
    tok = slice(sb * t, (sb + 1) * t)
    h = _rms(x_ref[0, tok, :], g_ref[...])
    hb = h.astype(BF16)
    ht = h.T.astype(BF16)
    lane = lax.broadcasted_iota(jnp.int32, (t, LANES), 1)
    a_half, r_half = A_HEAD_DIM // 2, R_QK_DIM // 2
    first_half = (lane & a_half) == 0

    ak = _dot(hb, w_ref[:, COL_AK:COL_AV])
    cosa, sina = cosa_ref[tok, :], sina_ref[tok, :]
    for c in range(A_WIDTH // LANES):
        blk = ak[:, c * LANES:(c + 1) * LANES]
        rot = jnp.where(first_half, pltpu.roll(blk, LANES - a_half, 1), pltpu.roll(blk, a_half, 1))
        kr = blk * cosa + rot * sina
        k_ref[0, sb, :, c * LANES:(c + 1) * LANES] = kr.astype(BF16)
        kmean_ref[0, sb, :, c * LANES:(c + 1) * LANES] = jnp.mean(kr, axis=0, keepdims=True)

    rq = _dot(hb, w_ref[:, COL_RQ:COL_RK])
    cosr, sinr = cosr_ref[tok, :], sinr_ref[tok, :]
    for c in range(R_HEADS):
        blk = rq[:, c * R_QK_DIM:(c + 1) * R_QK_DIM]
        rq_ref[0, tok, c * R_QK_DIM:(c + 1) * R_QK_DIM] = (blk * cosr + pltpu.roll(blk, r_half, 1) * sinr).astype(BF16)

    rv_ref[0, tok, :] = _dot(hb, w_ref[:, COL_RV:COL_GATES]).astype(BF16)

    qt = _dot(wqt_ref[...], ht)
    cost, sint = costa_ref[:, tok], sinta_ref[:, tok]
    half = A_HEAD_DIM // 2
    scale_a = A_HEAD_DIM ** -0.5 * LOG2_E
    for hd in range(A_HEADS):
        x1 = qt[hd * A_HEAD_DIM:hd * A_HEAD_DIM + half]
        x2 = qt[hd * A_HEAD_DIM + half:(hd + 1) * A_HEAD_DIM]
        qt_ref[0, sb, hd * A_HEAD_DIM:hd * A_HEAD_DIM + half, :] = ((x1 * cost - x2 * sint) * scale_a).astype(BF16)
        qt_ref[0, sb, hd * A_HEAD_DIM + half:(hd + 1) * A_HEAD_DIM, :] = ((x2 * cost + x1 * sint) * scale_a).astype(BF16)

    vt_ref[0, sb] = _dot(wvt_ref[...], ht).astype(BF16)

    rkt = _dot(wkt_ref[...], ht)
    cost, sint = costr_ref[:, tok], sintr_ref[:, tok]
    half = R_QK_DIM // 2
    scale_r = R_QK_DIM ** -0.5
    for hd in range(R_HEADS):
        x1 = rkt[hd * R_QK_DIM:hd * R_QK_DIM + half]
        x2 = rkt[hd * R_QK_DIM + half:(hd + 1) * R_QK_DIM]
        o1 = (x1 * cost - x2 * sint) * scale_r
        o2 = (x2 * cost + x1 * sint) * scale_r
        dec = kdec_ref[hd:hd + 1, tok]
        rkt_ref[0, sb, hd * R_QK_DIM:hd * R_QK_DIM + half, :] = o1.astype(BF16)
        rkt_ref[0, sb, hd * R_QK_DIM + half:(hd + 1) * R_QK_DIM, :] = o2.astype(BF16)
        rktd_ref[0, sb, hd * R_QK_DIM:hd * R_QK_DIM + half, :] = (o1 * dec).astype(BF16)
        rktd_ref[0, sb, hd * R_QK_DIM + half:(hd + 1) * R_QK_DIM, :] = (o2 * dec).astype(BF16)


def _qkv(x, g, w, tabs, cast_ahead):
    b, s, d = x.shape
    nb = s // TOKEN_TILE
    qb = QKV_BLOCKS
    t = qb * TOKEN_TILE
    steps = nb // qb
    tok = lambda w: pl.BlockSpec((1, t, w), lambda bi, i: (bi, i, 0))
    blk_t = lambda r: pl.BlockSpec((1, qb, r, TOKEN_TILE), lambda bi, i: (bi, i, 0, 0))
    nat_tab = pl.BlockSpec((t, LANES), lambda bi, i: (i, 0))
    tr_tab = lambda r: pl.BlockSpec((r, t), lambda bi, i: (0, i))
    sliced, cast_specs, cast_shapes = _cast_ahead(cast_ahead, b * steps, lambda bi, i: bi * steps + i)
    in_specs = [tok(d), _full((1, d)), _full(w.shape),
                nat_tab, nat_tab, nat_tab, nat_tab,
                tr_tab(A_HEAD_DIM // 2), tr_tab(A_HEAD_DIM // 2),
                tr_tab(R_QK_DIM // 2), tr_tab(R_QK_DIM // 2), tr_tab(SUBLANES)]
    assert len(in_specs) == QKV_INPUTS
    out = pl.pallas_call(
        _qkv_kernel,
        grid=(b, steps),
        in_specs=in_specs + cast_specs,
        out_specs=[pl.BlockSpec((1, qb, TOKEN_TILE, A_WIDTH), lambda bi, i: (bi, i, 0, 0)),
                   pl.BlockSpec((1, qb, 1, A_WIDTH), lambda bi, i: (bi, i, 0, 0)),
                   blk_t(A_WIDTH), blk_t(A_WIDTH), tok(R_QK_WIDTH), tok(R_V_WIDTH),
                   blk_t(R_QK_WIDTH), blk_t(R_QK_WIDTH)] + cast_specs,
        out_shape=[jax.ShapeDtypeStruct((b, nb, TOKEN_TILE, A_WIDTH), BF16),
                   jax.ShapeDtypeStruct((b, nb, 1, A_WIDTH), F32),
                   jax.ShapeDtypeStruct((b, nb, A_WIDTH, TOKEN_TILE), BF16),
                   jax.ShapeDtypeStruct((b, nb, A_WIDTH, TOKEN_TILE), BF16),
                   jax.ShapeDtypeStruct((b, s, R_QK_WIDTH), BF16),
                   jax.ShapeDtypeStruct((b, s, R_V_WIDTH), BF16),
                   jax.ShapeDtypeStruct((b, nb, R_QK_WIDTH, TOKEN_TILE), BF16),
                   jax.ShapeDtypeStruct((b, nb, R_QK_WIDTH, TOKEN_TILE), BF16)] + cast_shapes,
        scratch_shapes=[pltpu.VMEM((n, d), BF16) for _, n in QKV_T_COLS],
        compiler_params=_params("arbitrary", "arbitrary"),
        name="qkv",
    )(x, g, w, *tabs, *sliced)
    return out[:QKV_OUTPUTS], [o.reshape(w.shape) for o, w in zip(out[QKV_OUTPUTS:], cast_ahead)]


def _moba_kernel(qt_ref, k_ref, vt_ref, km_ref, o_ref, qm_ref, m_ref, acc_ref, ot_ref, s_ref):
    i = pl.program_id(1)
    nb = km_ref.shape[1]
    t = MOBA_BLOCK
    dh = A_HEAD_DIM
    row = lax.broadcasted_iota(jnp.int32, (2 * dh, t), 0)
    blk = lax.broadcasted_iota(jnp.int32, (nb, t), 0)
    past = blk < i
    causal = (lax.broadcasted_iota(jnp.int32, (t, t), 0) <= lax.broadcasted_iota(jnp.int32, (t, t), 1))
    ones = jnp.ones((MOBA_ONES_ROWS, t), BF16)
    lanes = lambda hd: slice((hd // 2) * LANES, (hd // 2 + 1) * LANES)
    rows = lambda hd: slice(hd * dh, (hd + 1) * dh)
    row_heads = [(b, hd) for b in range(MOBA_ROWS) for hd in range(A_HEADS)]

    def select_blocks(b, hd):
        hh = hd % 2
        qt_pair = qt_ref[b, 0, lanes(hd), :]
        mine = (row >= hh * dh) & (row < (hh + 1) * dh)
        qt = jnp.where(mine, qt_pair, jnp.zeros_like(qt_pair))
        qm_ref[b, hd, 0:2 * dh, :] = qt

        km = km_ref[b, :, lanes(hd)]
        km_hi = km.astype(BF16)
        km_lo = (km - km_hi.astype(F32)).astype(BF16)
        bs = jnp.where(past, _dot(km_hi, qt) + _dot(km_lo, qt), -jnp.inf)
        picked = jnp.zeros((nb, t), jnp.bool_)
        for _ in range(MOBA_TOPK):
            best = jnp.max(bs, axis=0, keepdims=True)
            first = jnp.min(jnp.where(bs == best, blk, nb), axis=0, keepdims=True)
            hit = blk == first
            picked = picked | hit
            bs = jnp.where(hit, -jnp.inf, bs)
        bias = jnp.where(past & picked, 0.0, NEG).astype(BF16)
        qm_ref[b, hd, 2 * dh:, :] = jnp.concatenate([bias, jnp.zeros((2 * dh - nb, t), BF16)], axis=0)

    key_lane = lax.broadcasted_iota(jnp.int32, (t, 2 * dh), 1)

    def keys_with_block_column(b, j):
        onehot = jnp.where(key_lane == j, 1.0, 0.0).astype(BF16)
        return [jnp.concatenate([k_ref[b, j, :, lanes(2 * p)], onehot], axis=1) for p in range(A_HEADS // 2)]

    def pv(b, j, hd, p):
        vt_aug = jnp.concatenate([vt_ref[b, j, rows(hd), :], ones], axis=0)
        return _dot(vt_aug, p.astype(BF16))

    def own_unit(b, hd):
        def scores():
            return jnp.where(causal, _dot(k_ref[b, i, :, lanes(hd)], qm_ref[b, hd, 0:2 * dh, :]), NEG)

        def update(s, m0):
            m_ref[b, hd] = jnp.broadcast_to(m0, (SUBLANES, t))
            acc_ref[b, hd] = pv(b, i, hd, jnp.exp2(s - m0))

        return scores, update

    def past_units(b, j):
        keys = keys_with_block_column(b, j)

        def unit(hd):
            def scores():
                return _dot(keys[hd // 2], qm_ref[b, hd])

            def update(s, m_blk):
                m_old = m_ref[b, hd]
                m_new = jnp.maximum(m_old, m_blk)
                alpha = jnp.exp2(m_old - m_new)
                m_ref[b, hd] = m_new
                acc_ref[b, hd] = alpha[0:1, :] * acc_ref[b, hd] + pv(b, j, hd, jnp.exp2(s - m_new[0:1, :]))

            return scores, update

        return [unit(hd) for hd in range(A_HEADS)]

    def pipelined(units):
        blk_max = {}
        slots = MOBA_LOOKAHEAD + 1
        for step in range(len(units) + MOBA_LOOKAHEAD):
            if step < len(units):
                sc = units[step][0]()
                s_ref[step % slots] = sc
                blk_max[step] = jnp.max(sc, axis=0, keepdims=True)
            if step >= MOBA_LOOKAHEAD:
                u = step - MOBA_LOOKAHEAD
                units[u][1](s_ref[u % slots], blk_max.pop(u))

    n_groups = i // MOBA_UNROLL
    for left in range(MOBA_UNROLL):
        @pl.when(i % MOBA_UNROLL == left)
        def _(left=left):
            for b, hd in row_heads:
                select_blocks(b, hd)
            units = [own_unit(b, hd) for b, hd in row_heads]
            for r in range(left):
                for b in range(MOBA_ROWS):
                    units += past_units(b, n_groups * MOBA_UNROLL + r)
            pipelined(units)

    def group_body(jj, carry):
        pipelined([u for r in range(MOBA_UNROLL) for b in range(MOBA_ROWS)
                   for u in past_units(b, MOBA_UNROLL * jj + r)])
        return carry

    lax.fori_loop(0, n_groups, group_body, 0)

    for b in range(MOBA_ROWS):
        for hd in range(A_HEADS):
            ot_ref[rows(hd), :] = acc_ref[b, hd, 0:dh, :] / acc_ref[b, hd, dh:dh + 1, :]
        o_ref[b] = ot_ref[...].T.astype(BF16)


def _moba(qt, k, vt, kmean):
    b, nb, t, w = k.shape
    r = MOBA_ROWS
    assert b % r == 0
    return pl.pallas_call(
        _moba_kernel,
        grid=(b // r, nb),
        in_specs=[pl.BlockSpec((r, 1, w, t), lambda bi, i: (bi, i, 0, 0)),
                  pl.BlockSpec((r, nb, t, w), lambda bi, i: (bi, 0, 0, 0)),
                  pl.BlockSpec((r, nb, w, t), lambda bi, i: (bi, 0, 0, 0)),
                  pl.BlockSpec((r, nb, w), lambda bi, i: (bi, 0, 0))],
        out_specs=pl.BlockSpec((r, t, w), lambda bi, i: (bi, i, 0)),
        out_shape=jax.ShapeDtypeStruct((b, nb * t, w), BF16),
        scratch_shapes=[pltpu.VMEM((r, A_HEADS, 4 * A_HEAD_DIM, t), BF16),
                        pltpu.VMEM((r, A_HEADS, SUBLANES, t), F32),
                        pltpu.VMEM((r, A_HEADS, A_HEAD_DIM + MOBA_ONES_ROWS, t), F32),
                        pltpu.VMEM((w, t), F32),
                        pltpu.VMEM((MOBA_LOOKAHEAD + 1, t, t), F32)],
        compiler_params=_params("parallel", "parallel"),
        name="moba",
    )(qt, k, vt, kmean)


def _mix_kernel(chunk_dec, x_ref, g_ref, rq_ref, rkt_ref, rktd_ref, rv_ref, ya_ref, dmask_ref, qdec_ref,
                w_ref, wa_ref, wb_ref, wo_ref, *rest):
    (o_ref,), (state_ref, yb_ref) = _split_cast_refs(rest, 1, 2)

    @pl.when(pl.program_id(1) == 0)
    def _():
        state_ref[...] = jnp.zeros_like(state_ref)

    x = x_ref[0]
    d = x.shape[1]
    hb = _rms(x, g_ref[...]).astype(BF16)
    c = R_CHUNK
    per_blk = TOKEN_TILE // c
    n_chunks = MIX_BLOCKS * per_blk
    rows = lambda ci: slice(ci * c, (ci + 1) * c)
    qk = lambda hd: slice(hd * R_QK_DIM, (hd + 1) * R_QK_DIM)
    vv = lambda hd: slice(hd * R_V_DIM, (hd + 1) * R_V_DIM)
    units = [(ci, hd) for ci in range(n_chunks) for hd in range(R_HEADS)]
    keys_t = lambda ref, ci, hd: ref[0, ci // per_blk, qk(hd), rows(ci % per_blk)]

    att = {u: _dot(rq_ref[0, rows(u[0]), qk(u[1])], keys_t(rkt_ref, *u)) for u in units}
    kv = {u: _dot(keys_t(rktd_ref, *u), rv_ref[0, rows(u[0]), vv(u[1])]) for u in units}
    rg = _dot(hb, w_ref[:, COL_GATES:COL_GATES + R_V_WIDTH])
    w = MIX_LATE_COLS
    late = ([lambda c=c: _dot(ya_ref[0], wa_ref[:, c:c + w]) for c in range(0, d, w)]
            + [lambda c=c: _dot(hb, w_ref[:, c:c + w])
               for c in range(COL_GATES + R_V_WIDTH, COL_GATES + R_V_WIDTH + 2 * d, w)])
    late_out = []

    cross = {}
    for hd in range(R_HEADS):
        st = state_ref[hd]
        for ci in range(n_chunks):
            cross[ci, hd] = _dot(rq_ref[0, rows(ci), qk(hd)], st.astype(BF16))
            st = chunk_dec[hd] * st + kv[ci, hd]
        state_ref[hd] = st

    gate = rg * _sigmoid(rg)
    for n, (ci, hd) in enumerate(units):
        inner = _dot((att[ci, hd] * dmask_ref[hd]).astype(BF16), rv_ref[0, rows(ci), vv(hd)])
        y = inner + cross[ci, hd] * qdec_ref[hd]
        mu = jnp.mean(y, axis=-1, keepdims=True)
        yc = y - mu
        var = jnp.mean(yc * yc, axis=-1, keepdims=True)
        yb_ref[rows(ci), vv(hd)] = (yc * lax.rsqrt(var + EPS) * gate[rows(ci), vv(hd)]).astype(BF16)
        for k in range(n * len(late) // len(units), (n + 1) * len(late) // len(units)):
            late_out.append(late[k]())

    pieces = d // w
    ta, ga, gb = (jnp.concatenate(late_out[k * pieces:(k + 1) * pieces], axis=1) for k in range(3))
    blocks = [slice(r * TOKEN_TILE, (r + 1) * TOKEN_TILE) for r in range(MIX_BLOCKS)]
    tb = [_dot(yb_ref[tok, :], wb_ref[...]) for tok in blocks]
    for tok, tb_blk in zip(blocks, tb):
        merged = _sigmoid(ga[tok]) * ta[tok] + _sigmoid(gb[tok]) * tb_blk
        o_ref[0, tok, :] = x[tok] + _dot(merged.astype(BF16), wo_ref[...])


def _mix(x, g, rq, rkt, rktd, rv, ya, dmask, qdec, chunk_dec, w, wa, wb, wo, cast_ahead):
    b, s, d = x.shape
    t = MIX_BLOCKS * TOKEN_TILE
    steps = s // t
    tok = lambda w: pl.BlockSpec((1, t, w), lambda bi, i: (bi, i, 0))
    blk_t = pl.BlockSpec((1, MIX_BLOCKS, R_QK_WIDTH, TOKEN_TILE), lambda bi, i: (bi, i, 0, 0))
    sliced, cast_specs, cast_shapes = _cast_ahead(cast_ahead, b * steps, lambda bi, i: bi * steps + i)
    out = pl.pallas_call(
        functools.partial(_mix_kernel, chunk_dec),
        grid=(b, steps),
        in_specs=[tok(d), _full((1, d)), tok(R_QK_WIDTH), blk_t, blk_t, tok(R_V_WIDTH), tok(A_WIDTH),
                  _full(dmask.shape), _full(qdec.shape),
                  _full(w.shape), _full(wa.shape), _full(wb.shape), _full(wo.shape)] + cast_specs,
        out_specs=[tok(d)] + cast_specs,
        out_shape=[jax.ShapeDtypeStruct((b, s, d), F32)] + cast_shapes,
        scratch_shapes=[pltpu.VMEM((R_HEADS, R_QK_DIM, R_V_DIM), F32),
                        pltpu.VMEM((t, R_V_WIDTH), BF16)],
        compiler_params=_params("parallel", "arbitrary"),
        name="mix",
    )(x, g, rq, rkt, rktd, rv, ya, dmask, qdec, w, wa, wb, wo, *sliced)
    return out[0], [o.reshape(w.shape) for o, w in zip(out[1:], cast_ahead)]


def _cross_kernel(x_ref, g_ref, m_ref, gm_ref, wkv_ref, wq_ref, wo_ref, *rest):
    (o_ref,), (qk_ref, vo_ref, p_ref) = _split_cast_refs(rest, 1, 3)
    d = x_ref.shape[2]
    dh = d // X_HEADS
    mlen = m_ref.shape[1]
    cols = lambda hd: slice(hd * dh, (hd + 1) * dh)
    mcols = lambda hd: slice(hd * mlen, (hd + 1) * mlen)

    @pl.when(pl.program_id(1) == 0)
    def _():
        mb = _rms(m_ref[0], gm_ref[...]).astype(BF16)
        kt = _dot(mb, wkv_ref[:, 0:d]).T.astype(BF16)
        v = _dot(mb, wkv_ref[:, d:]).astype(BF16)
        for hd in range(X_HEADS):
            qk_ref[:, mcols(hd)] = (_dot(wq_ref[:, cols(hd)], kt[cols(hd), :]) * dh ** -0.5).astype(BF16)
            vo_ref[mcols(hd), :] = _dot(v[:, cols(hd)], wo_ref[cols(hd), :]).astype(BF16)

    n_parts = CROSS_PARTS
    t = x_ref.shape[1] // n_parts
    part = lambda r: slice(r * t, (r + 1) * t)

    def scores_of(r):
        hb = _rms(x_ref[0, part(r), :], g_ref[...]).astype(BF16)
        return _dot(hb, qk_ref[...])

    nxt = scores_of(0)
    for r in range(n_parts):
        scores = nxt
        if r + 1 < n_parts:
            nxt = scores_of(r + 1)
        for hd in range(X_HEADS):
            s = scores[:, mcols(hd)]
            p = jnp.exp(s - jnp.max(s, axis=-1, keepdims=True))
            p_ref[part(r), mcols(hd)] = (p / jnp.sum(p, axis=-1, keepdims=True)).astype(BF16)
        o_ref[0, part(r), :] = x_ref[0, part(r), :] + _dot(p_ref[part(r), :], vo_ref[...])


def _cross(x, g, mem, gm, wkv, wq, wo, cast_ahead):
    b, s, d = x.shape
    m = mem.shape[1]
    t = ROW_TILE
    steps = s // t
    tok = pl.BlockSpec((1, t, d), lambda bi, i: (bi, i, 0))
    sliced, cast_specs, cast_shapes = _cast_ahead(cast_ahead, b * steps, lambda bi, i: bi * steps + i)
    out = pl.pallas_call(
        _cross_kernel,
        grid=(b, steps),
        in_specs=[tok, _full((1, d)), pl.BlockSpec((1, m, d), lambda bi, i: (bi, 0, 0)), _full((1, d)),
                  _full(wkv.shape), _full(wq.shape), _full(wo.shape)] + cast_specs,
        out_specs=[tok] + cast_specs,
        out_shape=[jax.ShapeDtypeStruct((b, s, d), F32)] + cast_shapes,
        scratch_shapes=[pltpu.VMEM((d, X_HEADS * m), BF16),
                        pltpu.VMEM((X_HEADS * m, d), BF16),
                        pltpu.VMEM((t, X_HEADS * m), BF16)],
        compiler_params=_params("parallel", "arbitrary"),
        name="cross",
    )(x, g, mem, gm, wkv, wq, wo, *sliced)
    return out[0], [o.reshape(w.shape) for o, w in zip(out[1:], cast_ahead)]


def _ffn_kernel(x_ref, g_ref, wg_ref, wu_ref, wd_ref, *rest):
    o_ref = rest[-1]
    x = x_ref[...]
    hb = _rms(x, g_ref[...]).astype(BF16)
    n_chunks = wg_ref.shape[1] // FF_CHUNK
    cols = lambda c: slice(c * FF_CHUNK, (c + 1) * FF_CHUNK)

    def up(c):
        return _dot(hb, wg_ref[:, cols(c)]), _dot(hb, wu_ref[:, cols(c)])

    acc = x
    nxt = up(0)
    for c in range(n_chunks):
        gate, upv = nxt
        if c + 1 < n_chunks:
            nxt = up(c + 1)
        act = gate * _sigmoid(gate) * upv
        acc = acc + _dot(act.astype(BF16), wd_ref[cols(c), :])
    o_ref[...] = _rms(acc, rest[0][...]) if len(rest) == 2 else acc


def _ffn(x2, g, wg, wu, wd, final_gain=None):
    n, d = x2.shape
    t = ROW_TILE
    tok = pl.BlockSpec((t, d), lambda i: (i, 0))
    final = () if final_gain is None else (final_gain,)
    return pl.pallas_call(
        _ffn_kernel,
        grid=(n // t,),
        in_specs=[tok, _full((1, d)), _full(wg.shape), _full(wu.shape), _full(wd.shape)] + [_full((1, d))] * len(final),
        out_specs=tok,
        out_shape=jax.ShapeDtypeStruct((n, d), F32),
        compiler_params=_params("parallel"),
        name="ffn",
    )(x2, g, wg, wu, wd, *final)


def _rope_tables(s):
    def cs(dim):
        inv = ROPE_THETA ** (-jnp.arange(0, dim, 2, dtype=F32) / dim)
        ang = jnp.arange(s).astype(F32)[:, None] * inv[None, :]
        return jnp.cos(ang), jnp.sin(ang)

    with jax.ensure_compile_time_eval():
        ca, sa = cs(A_HEAD_DIM)
        cr, sr = cs(R_QK_DIM)
        sign = lambda half: jnp.asarray(np.tile(np.repeat([-1.0, 1.0], half), LANES // (2 * half))[None, :], F32)
        return (jnp.tile(ca, (1, 4)), jnp.tile(sa, (1, 4)) * sign(A_HEAD_DIM // 2),
                jnp.tile(cr, (1, 2)), jnp.tile(sr, (1, 2)) * sign(R_QK_DIM // 2),
                ca.T, sa.T, cr.T, sr.T)


def _decay_tables(s):
    c = R_CHUNK
    log_g = np.log(1.0 - np.exp2(-5.0 - np.arange(R_HEADS, dtype=np.float64)))
    i = np.arange(c, dtype=np.float64)
    diff = i[:, None] - i[None, :]
    dmask = np.where(diff >= 0, np.exp(log_g[:, None, None] * np.maximum(diff, 0.0)), 0.0)
    q_dec = np.exp(log_g[:, None] * (i + 1.0))
    k_dec = np.exp(log_g[:, None] * (c - 1.0 - i))
    kdec_t = np.zeros((SUBLANES, s), np.float64)
    kdec_t[:R_HEADS] = np.tile(k_dec, (1, s // c))
    qdec = np.broadcast_to(q_dec[:, :, None], (R_HEADS, c, R_V_DIM))
    chunk_dec = tuple(float(v) for v in np.exp(log_g * c))
    return (jnp.asarray(dmask, F32), jnp.asarray(qdec, F32), jnp.asarray(kdec_t, F32), chunk_dec)


def kernel(x, mem, norm_mix, w_in, w_branch_a, w_branch_b, w_out, norm_cross, norm_mem,
           w_xq, w_xkv, w_xo, norm_ffn, w_gate, w_up, w_down, norm_final):
    b, s, d = x.shape
    steps = (TOKEN_TILE * QKV_BLOCKS, TOKEN_TILE * MIX_BLOCKS, ROW_TILE)
    assert all(s % t == 0 for t in steps) and b % MOBA_ROWS == 0 and d % LANES == 0
    rope = _rope_tables(s)
    dmask, qdec, kdec_t, chunk_dec = _decay_tables(s)
    row = lambda v: v.reshape(1, d)
    assert w_in.shape[2] == COL_GATES + R_V_WIDTH + 2 * d

    for l in range(w_in.shape[0]):
        w = w_in[l].astype(BF16)
        (k_a, kmean, qt_a, vt_a, rq, rv, rkt, rktd), (wa, wb, wo) = _qkv(
            x, row(norm_mix[l]), w, rope + (kdec_t,), (w_branch_a[l], w_branch_b[l], w_out[l]))
        y_a = _moba(qt_a, k_a, vt_a, kmean.reshape(b, s // MOBA_BLOCK, A_WIDTH))
        x, (wxkv, wxq, wxo) = _mix(x, row(norm_mix[l]), rq, rkt, rktd, rv, y_a, dmask, qdec, chunk_dec,
                                   w, wa, wb, wo, (w_xkv[l], w_xq[l], w_xo[l]))
        x, (wg, wu, wd) = _cross(x, row(norm_cross[l]), mem, row(norm_mem[l]), wxkv, wxq, wxo,
                                 (w_gate[l], w_up[l], w_down[l]))
        last = l == w_in.shape[0] - 1
        x = _ffn(x.reshape(b * s, d), row(norm_ffn[l]), wg, wu, wd,
                 row(norm_final) if last else None).reshape(b, s, d)
    return x
```

```python
import functools
import math

import numpy as np
import jax
import jax.numpy as jnp
from jax import lax
from jax.experimental import pallas as pl
from jax.experimental.pallas import tpu as pltpu

F32 = jnp.float32
BF16 = jnp.bfloat16

EPS = 1e-6
ROPE_THETA = 10000.0
A_HEADS = 8
A_HEAD_DIM = 64
A_WIDTH = A_HEADS * A_HEAD_DIM
MOBA_BLOCK = 256
MOBA_TOPK = 3
R_HEADS = 4
R_QK_DIM = 128
R_V_DIM = 256
R_QK_WIDTH = R_HEADS * R_QK_DIM
R_V_WIDTH = R_HEADS * R_V_DIM
R_CHUNK = 256
X_HEADS = 4
COL_AQ, COL_AK, COL_AV = 0, A_WIDTH, 2 * A_WIDTH
COL_RQ = 3 * A_WIDTH
COL_RK = COL_RQ + R_QK_WIDTH
COL_RV = COL_RK + R_QK_WIDTH
COL_GATES = COL_RV + R_V_WIDTH

LANES = 128
SUBLANES = 8
BF16_ROWS = 16
VMEM_LIMIT = 56 * 1024 * 1024
assert 2 * A_HEAD_DIM == LANES and R_QK_DIM == LANES

TOKEN_TILE = MOBA_BLOCK
QKV_BLOCKS = 4
MIX_BLOCKS = 2
MIX_LATE_COLS = 256
ROW_TILE = 1024
CROSS_PARTS = 4
FF_CHUNK = 256
MOBA_ROWS = 2
MOBA_UNROLL = 4
MOBA_LOOKAHEAD = 7
MOBA_ONES_ROWS = BF16_ROWS

NEG = -1e30
LOG2_E = math.log2(math.e)


def _dot(a, b):
    return jnp.dot(a, b, preferred_element_type=F32)


def _rms(x, g):
    return x * lax.rsqrt(jnp.mean(x * x, axis=-1, keepdims=True) + EPS) * g


def _sigmoid(x):
    return 1.0 / (1.0 + jnp.exp(-x))


def _params(*sem):
    return pltpu.CompilerParams(dimension_semantics=sem, vmem_limit_bytes=VMEM_LIMIT)


def _cast_ahead(mats, n_steps, step_of):
    assert all(w.shape[0] % n_steps == 0 for w in mats)
    sliced = [w.reshape(n_steps, w.shape[0] // n_steps, w.shape[1]) for w in mats]
    specs = [pl.BlockSpec((1,) + w.shape[1:], lambda *ids: (step_of(*ids), 0, 0)) for w in sliced]
    return sliced, specs, [jax.ShapeDtypeStruct(w.shape, BF16) for w in sliced]


def _split_cast_refs(rest, n_out, n_scratch):
    n = (len(rest) - n_out - n_scratch) // 2
    outs = rest[n:n + n_out]
    scratch = rest[len(rest) - n_scratch:]
    for src, dst in zip(rest[:n], rest[n + n_out:2 * n + n_out]):
        dst[...] = src[...].astype(BF16)
    return outs, scratch


def _full(shape):
    n = len(shape)
    return pl.BlockSpec(shape, lambda *_: (0,) * n, pipeline_mode=pl.Buffered(1))


QKV_INPUTS, QKV_OUTPUTS = 12, 8
QKV_T_COLS = ((COL_AQ, A_WIDTH), (COL_AV, A_WIDTH), (COL_RK, R_QK_WIDTH))


def _qkv_kernel(*refs):
    outs, wt_refs = _split_cast_refs(refs[QKV_INPUTS:], QKV_OUTPUTS, len(QKV_T_COLS))
    w_ref = refs[2]

    @pl.when((pl.program_id(0) == 0) & (pl.program_id(1) == 0))
    def _():
        for wt_ref, (col, n) in zip(wt_refs, QKV_T_COLS):
            for c in range(0, n, TOKEN_TILE):
                wt_ref[c:c + TOKEN_TILE, :] = w_ref[:, col + c:col + c + TOKEN_TILE].astype(F32).T.astype(BF16)

    for sb in range(QKV_BLOCKS):
        _qkv_block(sb, *refs[:3], *wt_refs, *refs[3:QKV_INPUTS], *outs)


def _qkv_block(sb, x_ref, g_ref, w_ref, wqt_ref, wvt_ref, wkt_ref, cosa_ref, sina_ref, cosr_ref, sinr_ref,
               costa_ref, sinta_ref, costr_ref, sintr_ref, kdec_ref,
               k_ref, kmean_ref, qt_ref, vt_ref, rq_ref, rv_ref, rkt_ref, rktd_ref):
    t = TOKEN_TILE
    tok = slice(sb * t, (sb + 1) * t)
    h = _rms(x_ref[0, tok, :], g_ref[...])
    hb = h.astype(BF16)
    ht = h.T.astype(BF16)
    lane = lax.broadcasted_iota(jnp.int32, (t, LANES), 1)
    a_half, r_half = A_HEAD_DIM // 2, R_QK_DIM // 2
    first_half = (lane & a_half) == 0

    ak = _dot(hb, w_ref[:, COL_AK:COL_AV])
    cosa, sina = cosa_ref[tok, :], sina_ref[tok, :]
    for c in range(A_WIDTH // LANES):
        blk = ak[:, c * LANES:(c + 1) * LANES]
        rot = jnp.where(first_half, pltpu.roll(blk, LANES - a_half, 1), pltpu.roll(blk, a_half, 1))
        kr = blk * cosa + rot * sina
        k_ref[0, sb, :, c * LANES:(c + 1) * LANES] = kr.astype(BF16)
        kmean_ref[0, sb, :, c * LANES:(c + 1) * LANES] = jnp.mean(kr, axis=0, keepdims=True)

    rq = _dot(hb, w_ref[:, COL_RQ:COL_RK])
    cosr, sinr = cosr_ref[tok, :], sinr_ref[tok, :]
    for c in range(R_HEADS):
        blk = rq[:, c * R_QK_DIM:(c + 1) * R_QK_DIM]
        rq_ref[0, tok, c * R_QK_DIM:(c + 1) * R_QK_DIM] = (blk * cosr + pltpu.roll(blk, r_half, 1) * sinr).astype(BF16)

    rv_ref[0, tok, :] = _dot(hb, w_ref[:, COL_RV:COL_GATES]).astype(BF16)

    qt = _dot(wqt_ref[...], ht)
    cost, sint = costa_ref[:, tok], sinta_ref[:, tok]
    half = A_HEAD_DIM // 2
    scale_a = A_HEAD_DIM ** -0.5 * LOG2_E
    for hd in range(A_HEADS):
        x1 = qt[hd * A_HEAD_DIM:hd * A_HEAD_DIM + half]
        x2 = qt[hd * A_HEAD_DIM + half:(hd + 1) * A_HEAD_DIM]
        qt_ref[0, sb, hd * A_HEAD_DIM:hd * A_HEAD_DIM + half, :] = ((x1 * cost - x2 * sint) * scale_a).astype(BF16)
        qt_ref[0, sb, hd * A_HEAD_DIM + half:(hd + 1) * A_HEAD_DIM, :] = ((x2 * cost + x1 * sint) * scale_a).astype(BF16)

    vt_ref[0, sb] = _dot(wvt_ref[...], ht).astype(BF16)

    rkt = _dot(wkt_ref[...], ht)
    cost, sint = costr_ref[:, tok], sintr_ref[:, tok]
    half = R_QK_DIM // 2
    scale_r = R_QK_DIM ** -0.5
    for hd in range(R_HEADS):
        x1 = rkt[hd * R_QK_DIM:hd * R_QK_DIM + half]
        x2 = rkt[hd * R_QK_DIM + half:(hd + 1) * R_QK_DIM]
        o1 = (x1 * cost - x2 * sint) * scale_r
        o2 = (x2 * cost + x1 * sint) * scale_r
        dec = kdec_ref[hd:hd + 1, tok]
        rkt_ref[0, sb, hd * R_QK_DIM:hd * R_QK_DIM + half, :] = o1.astype(BF16)
        rkt_ref[0, sb, hd * R_QK_DIM + half:(hd + 1) * R_QK_DIM, :] = o2.astype(BF16)
        rktd_ref[0, sb, hd * R_QK_DIM:hd * R_QK_DIM + half, :] = (o1 * dec).astype(BF16)
        rktd_ref[0, sb, hd * R_QK_DIM + half:(hd + 1) * R_QK_DIM, :] = (o2 * dec).astype(BF16)


def _qkv(x, g, w, tabs, cast_ahead):
    b, s, d = x.shape
    nb = s // TOKEN_TILE
    qb = QKV_BLOCKS
    t = qb * TOKEN_TILE
    steps = nb // qb
    tok = lambda w: pl.BlockSpec((1, t, w), lambda bi, i: (bi, i, 0))
    blk_t = lambda r: pl.BlockSpec((1, qb, r, TOKEN_TILE), lambda bi, i: (bi, i, 0, 0))
    nat_tab = pl.BlockSpec((t, LANES), lambda bi, i: (i, 0))
    tr_tab = lambda r: pl.BlockSpec((r, t), lambda bi, i: (0, i))
    sliced, cast_specs, cast_shapes = _cast_ahead(cast_ahead, b * steps, lambda bi, i: bi * steps + i)
    in_specs = [tok(d), _full((1, d)), _full(w.shape),
                nat_tab, nat_tab, nat_tab, nat_tab,
                tr_tab(A_HEAD_DIM // 2), tr_tab(A_HEAD_DIM // 2),
                tr_tab(R_QK_DIM // 2), tr_tab(R_QK_DIM // 2), tr_tab(SUBLANES)]
    assert len(in_specs) == QKV_INPUTS
    out = pl.pallas_call(
        _qkv_kernel,
        grid=(b, steps),
        in_specs=in_specs + cast_specs,
        out_specs=[pl.BlockSpec((1, qb, TOKEN_TILE, A_WIDTH), lambda bi, i: (bi, i, 0, 0)),
                   pl.BlockSpec((1, qb, 1, A_WIDTH), lambda bi, i: (bi, i, 0, 0)),
                   blk_t(A_WIDTH), blk_t(A_WIDTH), tok(R_QK_WIDTH), tok(R_V_WIDTH),
                   blk_t(R_QK_WIDTH), blk_t(R_QK_WIDTH)] + cast_specs,
        out_shape=[jax.ShapeDtypeStruct((b, nb, TOKEN_TILE, A_WIDTH), BF16),
                   jax.ShapeDtypeStruct((b, nb, 1, A_WIDTH), F32),
                   jax.ShapeDtypeStruct((b, nb, A_WIDTH, TOKEN_TILE), BF16),
                   jax.ShapeDtypeStruct((b, nb, A_WIDTH, TOKEN_TILE), BF16),
                   jax.ShapeDtypeStruct((b, s, R_QK_WIDTH), BF16),
                   jax.ShapeDtypeStruct((b, s, R_V_WIDTH), BF16),
                   jax.ShapeDtypeStruct((b, nb, R_QK_WIDTH, TOKEN_TILE), BF16),
                   jax.ShapeDtypeStruct((b, nb, R_QK_WIDTH, TOKEN_TILE), BF16)] + cast_shapes,
        scratch_shapes=[pltpu.VMEM((n, d), BF16) for _, n in QKV_T_COLS],
        compiler_params=_params("arbitrary", "arbitrary"),
        name="qkv",
    )(x, g, w, *tabs, *sliced)
    return out[:QKV_OUTPUTS], [o.reshape(w.shape) for o, w in zip(out[QKV_OUTPUTS:], cast_ahead)]


def _moba_kernel(qt_ref, k_ref, vt_ref, km_ref, o_ref, qm_ref, m_ref, acc_ref, ot_ref, s_ref):
    i = pl.program_id(1)
    nb = km_ref.shape[1]
    t = MOBA_BLOCK
    dh = A_HEAD_DIM
    row = lax.broadcasted_iota(jnp.int32, (2 * dh, t), 0)
    blk = lax.broadcasted_iota(jnp.int32, (nb, t), 0)
    past = blk < i
    causal = (lax.broadcasted_iota(jnp.int32, (t, t), 0) <= lax.broadcasted_iota(jnp.int32, (t, t), 1))
    ones = jnp.ones((MOBA_ONES_ROWS, t), BF16)
    lanes = lambda hd: slice((hd // 2) * LANES, (hd // 2 + 1) * LANES)
    rows = lambda hd: slice(hd * dh, (hd + 1) * dh)
    row_heads = [(b, hd) for b in range(MOBA_ROWS) for hd in range(A_HEADS)]

    def select_blocks(b, hd):
        hh = hd % 2
        qt_pair = qt_ref[b, 0, lanes(hd), :]
        mine = (row >= hh * dh) & (row < (hh + 1) * dh)
        qt = jnp.where(mine, qt_pair, jnp.zeros_like(qt_pair))
        qm_ref[b, hd, 0:2 * dh, :] = qt

        km = km_ref[b, :, lanes(hd)]
        km_hi = km.astype(BF16)
        km_lo = (km - km_hi.astype(F32)).astype(BF16)
        bs = jnp.where(past, _dot(km_hi, qt) + _dot(km_lo, qt), -jnp.inf)
        picked = jnp.zeros((nb, t), jnp.bool_)
        for _ in range(MOBA_TOPK):
            best = jnp.max(bs, axis=0, keepdims=True)
            first = jnp.min(jnp.where(bs == best, blk, nb), axis=0, keepdims=True)
            hit = blk == first
            picked = picked | hit
            bs = jnp.where(hit, -jnp.inf, bs)
        bias = jnp.where(past & picked, 0.0, NEG).astype(BF16)
        qm_ref[b, hd, 2 * dh:, :] = jnp.concatenate([bias, jnp.zeros((2 * dh - nb, t), BF16)], axis=0)

    key_lane = lax.broadcasted_iota(jnp.int32, (t, 2 * dh), 1)

    def keys_with_block_column(b, j):
        onehot = jnp.where(key_lane == j, 1.0, 0.0).astype(BF16)
        return [jnp.concatenate([k_ref[b, j, :, lanes(2 * p)], onehot], axis=1) for p in range(A_HEADS // 2)]

    def pv(b, j, hd, p):
        vt_aug = jnp.concatenate([vt_ref[b, j, rows(hd), :], ones], axis=0)
        return _dot(vt_aug, p.astype(BF16))

    def own_unit(b, hd):
        def scores():
            return jnp.where(causal, _dot(k_ref[b, i, :, lanes(hd)], qm_ref[b, hd, 0:2 * dh, :]), NEG)

        def update(s, m0):
            m_ref[b, hd] = jnp.broadcast_to(m0, (SUBLANES, t))
            acc_ref[b, hd] = pv(b, i, hd, jnp.exp2(s - m0))

        return scores, update

    def past_units(b, j):
        keys = keys_with_block_column(b, j)

        def unit(hd):
            def scores():
                return _dot(keys[hd // 2], qm_ref[b, hd])

            def update(s, m_blk):
                m_old = m_ref[b, hd]
                m_new = jnp.maximum(m_old, m_blk)
                alpha = jnp.exp2(m_old - m_new)
                m_ref[b, hd] = m_new
                acc_ref[b, hd] = alpha[0:1, :] * acc_ref[b, hd] + pv(b, j, hd, jnp.exp2(s - m_new[0:1, :]))

            return scores, update

        return [unit(hd) for hd in range(A_HEADS)]

    def pipelined(units):
        blk_max = {}
        slots = MOBA_LOOKAHEAD + 1
        for step in range(len(units) + MOBA_LOOKAHEAD):
            if step < len(units):
                sc = units[step][0]()
                s_ref[step % slots] = sc
                blk_max[step] = jnp.max(sc, axis=0, keepdims=True)
            if step >= MOBA_LOOKAHEAD:
                u = step - MOBA_LOOKAHEAD
                units[u][1](s_ref[u % slots], blk_max.pop(u))

    n_groups = i // MOBA_UNROLL
    for left in range(MOBA_UNROLL):
        @pl.when(i % MOBA_UNROLL == left)
        def _(left=left):
            for b, hd in row_heads:
                select_blocks(b, hd)
            units = [own_unit(b, hd) for b, hd in row_heads]
            for r in range(left):
                for b in range(MOBA_ROWS):
                    units += past_units(b, n_groups * MOBA_UNROLL + r)
            pipelined(units)

    def group_body(jj, carry):
        pipelined([u for r in range(MOBA_UNROLL) for b in range(MOBA_ROWS)
                   for u in past_units(b, MOBA_UNROLL * jj + r)])
        return carry

    lax.fori_loop(0, n_groups, group_body, 0)

    for b in range(MOBA_ROWS):
        for hd in range(A_HEADS):
            ot_ref[rows(hd), :] = acc_ref[b, hd, 0:dh, :] / acc_ref[b, hd, dh:dh + 1, :]
        o_ref[b] = ot_ref[...].T.astype(BF16)


def _moba(qt, k, vt, kmean):
    b, nb, t, w = k.shape
    r = MOBA_ROWS
    assert b % r == 0
    return pl.pallas_call(
        _moba_kernel,
        grid=(b // r, nb),
        in_specs=[pl.BlockSpec((r, 1, w, t), lambda bi, i: (bi, i, 0, 0)),
                  pl.BlockSpec((r, nb, t, w), lambda bi, i: (bi, 0, 0, 0)),
                  pl.BlockSpec((r, nb, w, t), lambda bi, i: (bi, 0, 0, 0)),
                  pl.BlockSpec((r, nb, w), lambda bi, i: (bi, 0, 0))],
        out_specs=pl.BlockSpec((r, t, w), lambda bi, i: (bi, i, 0)),
        out_shape=jax.ShapeDtypeStruct((b, nb * t, w), BF16),
        scratch_shapes=[pltpu.VMEM((r, A_HEADS, 4 * A_HEAD_DIM, t), BF16),
                        pltpu.VMEM((r, A_HEADS, SUBLANES, t), F32),
                        pltpu.VMEM((r, A_HEADS, A_HEAD_DIM + MOBA_ONES_ROWS, t), F32),
                        pltpu.VMEM((w, t), F32),
                        pltpu.VMEM((MOBA_LOOKAHEAD + 1, t, t), F32)],
        compiler_params=_params("parallel", "parallel"),
        name="moba",
    )(qt, k, vt, kmean)


def _mix_kernel(chunk_dec, x_ref, g_ref, rq_ref, rkt_ref, rktd_ref, rv_ref, ya_ref, dmask_ref, qdec_ref,
                w_ref, wa_ref, wb_ref, wo_ref, *rest):
    (o_ref,), (state_ref, yb_ref) = _split_cast_refs(rest, 1, 2)

    @pl.when(pl.program_id(1) == 0)
    def _():
        state_ref[...] = jnp.zeros_like(state_ref)

    x = x_ref[0]
    d = x.shape[1]
    hb = _rms(x, g_ref[...]).astype(BF16)
    c = R_CHUNK
    per_blk = TOKEN_TILE // c
    n_chunks = MIX_BLOCKS * per_blk
    rows = lambda ci: slice(ci * c, (ci + 1) * c)
    qk = lambda hd: slice(hd * R_QK_DIM, (hd + 1) * R_QK_DIM)
    vv = lambda hd: slice(hd * R_V_DIM, (hd + 1) * R_V_DIM)
    units = [(ci, hd) for ci in range(n_chunks) for hd in range(R_HEADS)]
    keys_t = lambda ref, ci, hd: ref[0, ci // per_blk, qk(hd), rows(ci % per_blk)]

    att = {u: _dot(rq_ref[0, rows(u[0]), qk(u[1])], keys_t(rkt_ref, *u)) for u in units}
    kv = {u: _dot(keys_t(rktd_ref, *u), rv_ref[0, rows(u[0]), vv(u[1])]) for u in units}
    rg = _dot(hb, w_ref[:, COL_GATES:COL_GATES + R_V_WIDTH])
    w = MIX_LATE_COLS
    late = ([lambda c=c: _dot(ya_ref[0], wa_ref[:, c:c + w]) for c in range(0, d, w)]
            + [lambda c=c: _dot(hb, w_ref[:, c:c + w])
               for c in range(COL_GATES + R_V_WIDTH, COL_GATES + R_V_WIDTH + 2 * d, w)])
    late_out = []

    cross = {}
    for hd in range(R_HEADS):
        st = state_ref[hd]
        for ci in range(n_chunks):
            cross[ci, hd] = _dot(rq_ref[0, rows(ci), qk(hd)], st.astype(BF16))
            st = chunk_dec[hd] * st + kv[ci, hd]
        state_ref[hd] = st

    gate = rg * _sigmoid(rg)
    for n, (ci, hd) in enumerate(units):
        inner = _dot((att[ci, hd] * dmask_ref[hd]).astype(BF16), rv_ref[0, rows(ci), vv(hd)])
        y = inner + cross[ci, hd] * qdec_ref[hd]
        mu = jnp.mean(y, axis=-1, keepdims=True)
        yc = y - mu
        var = jnp.mean(yc * yc, axis=-1, keepdims=True)
        yb_ref[rows(ci), vv(hd)] = (yc * lax.rsqrt(var + EPS) * gate[rows(ci), vv(hd)]).astype(BF16)
        for k in range(n * len(late) // len(units), (n + 1) * len(late) // len(units)):
            late_out.append(late[k]())

    pieces = d // w
    ta, ga, gb = (jnp.concatenate(late_out[k * pieces:(k + 1) * pieces], axis=1) for k in range(3))
    blocks = [slice(r * TOKEN_TILE, (r + 1) * TOKEN_TILE) for r in range(MIX_BLOCKS)]
    tb = [_dot(yb_ref[tok, :], wb_ref[...]) for tok in blocks]
    for tok, tb_blk in zip(blocks, tb):
        merged = _sigmoid(ga[tok]) * ta[tok] + _sigmoid(gb[tok]) * tb_blk
        o_ref[0, tok, :] = x[tok] + _dot(merged.astype(BF16), wo_ref[...])


def _mix(x, g, rq, rkt, rktd, rv, ya, dmask, qdec, chunk_dec, w, wa, wb, wo, cast_ahead):
    b, s, d = x.shape
    t = MIX_BLOCKS * TOKEN_TILE
    steps = s // t
    tok = lambda w: pl.BlockSpec((1, t, w), lambda bi, i: (bi, i, 0))
    blk_t = pl.BlockSpec((1, MIX_BLOCKS, R_QK_WIDTH, TOKEN_TILE), lambda bi, i: (bi, i, 0, 0))
    sliced, cast_specs, cast_shapes = _cast_ahead(cast_ahead, b * steps, lambda bi, i: bi * steps + i)
    out = pl.pallas_call(
        functools.partial(_mix_kernel, chunk_dec),
        grid=(b, steps),
        in_specs=[tok(d), _full((1, d)), tok(R_QK_WIDTH), blk_t, blk_t, tok(R_V_WIDTH), tok(A_WIDTH),
                  _full(dmask.shape), _full(qdec.shape),
                  _full(w.shape), _full(wa.shape), _full(wb.shape), _full(wo.shape)] + cast_specs,
        out_specs=[tok(d)] + cast_specs,
        out_shape=[jax.ShapeDtypeStruct((b, s, d), F32)] + cast_shapes,
        scratch_shapes=[pltpu.VMEM((R_HEADS, R_QK_DIM, R_V_DIM), F32),
                        pltpu.VMEM((t, R_V_WIDTH), BF16)],
        compiler_params=_params("parallel", "arbitrary"),
        name="mix",
    )(x, g, rq, rkt, rktd, rv, ya, dmask, qdec, w, wa, wb, wo, *sliced)
    return out[0], [o.reshape(w.shape) for o, w in zip(out[1:], cast_ahead)]


def _cross_kernel(x_ref, g_ref, m_ref, gm_ref, wkv_ref, wq_ref, wo_ref, *rest):
    (o_ref,), (qk_ref, vo_ref, p_ref) = _split_cast_refs(rest, 1, 3)
    d = x_ref.shape[2]
    dh = d // X_HEADS
    mlen = m_ref.shape[1]
    cols = lambda hd: slice(hd * dh, (hd + 1) * dh)
    mcols = lambda hd: slice(hd * mlen, (hd + 1) * mlen)

    @pl.when(pl.program_id(1) == 0)
    def _():
        mb = _rms(m_ref[0], gm_ref[...]).astype(BF16)
        kt = _dot(mb, wkv_ref[:, 0:d]).T.astype(BF16)
        v = _dot(mb, wkv_ref[:, d:]).astype(BF16)
        for hd in range(X_HEADS):
            qk_ref[:, mcols(hd)] = (_dot(wq_ref[:, cols(hd)], kt[cols(hd), :]) * dh ** -0.5).astype(BF16)
            vo_ref[mcols(hd), :] = _dot(v[:, cols(hd)], wo_ref[cols(hd), :]).astype(BF16)

    n_parts = CROSS_PARTS
    t = x_ref.shape[1] // n_parts
    part = lambda r: slice(r * t, (r + 1) * t)

    def scores_of(r):
        hb = _rms(x_ref[0, part(r), :], g_ref[...]).astype(BF16)
        return _dot(hb, qk_ref[...])

    nxt = scores_of(0)
    for r in range(n_parts):
        scores = nxt
        if r + 1 < n_parts:
            nxt = scores_of(r + 1)
        for hd in range(X_HEADS):
            s = scores[:, mcols(hd)]
            p = jnp.exp(s - jnp.max(s, axis=-1, keepdims=True))
            p_ref[part(r), mcols(hd)] = (p / jnp.sum(p, axis=-1, keepdims=True)).astype(BF16)
        o_ref[0, part(r), :] = x_ref[0, part(r), :] + _dot(p_ref[part(r), :], vo_ref[...])


def _cross(x, g, mem, gm, wkv, wq, wo, cast_ahead):
    b, s, d = x.shape
    m = mem.shape[1]
    t = ROW_TILE
    steps = s // t
    tok = pl.BlockSpec((1, t, d), lambda bi, i: (bi, i, 0))
    sliced, cast_specs, cast_shapes = _cast_ahead(cast_ahead, b * steps, lambda bi, i: bi * steps + i)
    out = pl.pallas_call(
        _cross_kernel,
        grid=(b, steps),
        in_specs=[tok, _full((1, d)), pl.BlockSpec((1, m, d), lambda bi, i: (bi, 0, 0)), _full((1, d)),
                  _full(wkv.shape), _full(wq.shape), _full(wo.shape)] + cast_specs,
        out_specs=[tok] + cast_specs,
        out_shape=[jax.ShapeDtypeStruct((b, s, d), F32)] + cast_shapes,
        scratch_shapes=[pltpu.VMEM((d, X_HEADS * m), BF16),
                        pltpu.VMEM((X_HEADS * m, d), BF16),
                        pltpu.VMEM((t, X_HEADS * m), BF16)],
        compiler_params=_params("parallel", "arbitrary"),
        name="cross",
    )(x, g, mem, gm, wkv, wq, wo, *sliced)
    return out[0], [o.reshape(w.shape) for o, w in zip(out[1:], cast_ahead)]


def _ffn_kernel(x_ref, g_ref, wg_ref, wu_ref, wd_ref, *rest):
    o_ref = rest[-1]
    x = x_ref[...]
    hb = _rms(x, g_ref[...]).astype(BF16)
    n_chunks = wg_ref.shape[1] // FF_CHUNK
    cols = lambda c: slice(c * FF_CHUNK, (c + 1) * FF_CHUNK)

    def up(c):
        return _dot(hb, wg_ref[:, cols(c)]), _dot(hb, wu_ref[:, cols(c)])

    acc = x
    nxt = up(0)
    for c in range(n_chunks):
        gate, upv = nxt
        if c + 1 < n_chunks:
            nxt = up(c + 1)
        act = gate * _sigmoid(gate) * upv
        acc = acc + _dot(act.astype(BF16), wd_ref[cols(c), :])
    o_ref[...] = _rms(acc, rest[0][...]) if len(rest) == 2 else acc


def _ffn(x2, g, wg, wu, wd, final_gain=None):
    n, d = x2.shape
    t = ROW_TILE
    tok = pl.BlockSpec((t, d), lambda i: (i, 0))
    final = () if final_gain is None else (final_gain,)
    return pl.pallas_call(
        _ffn_kernel,
        grid=(n // t,),
        in_specs=[tok, _full((1, d)), _full(wg.shape), _full(wu.shape), _full(wd.shape)] + [_full((1, d))] * len(final),
        out_specs=tok,
        out_shape=jax.ShapeDtypeStruct((n, d), F32),
        compiler_params=_params("parallel"),
        name="ffn",
    )(x2, g, wg, wu, wd, *final)


def _rope_tables(s):
    def cs(dim):
        inv = ROPE_THETA ** (-np.arange(0, dim, 2, dtype=np.float64) / dim)
        ang = np.arange(s, dtype=np.float64)[:, None] * inv[None, :]
        return np.cos(ang), np.sin(ang)

    ca, sa = cs(A_HEAD_DIM)
    cr, sr = cs(R_QK_DIM)
    sign = lambda half: np.tile(np.repeat([-1.0, 1.0], half), LANES // (2 * half))[None, :]
    tabs = (np.tile(ca, (1, 4)), np.tile(sa, (1, 4)) * sign(A_HEAD_DIM // 2),
            np.tile(cr, (1, 2)), np.tile(sr, (1, 2)) * sign(R_QK_DIM // 2),
            ca.T, sa.T, cr.T, sr.T)
    return tuple(jnp.asarray(np.ascontiguousarray(t), F32) for t in tabs)


def _decay_tables(s):
    c = R_CHUNK
    log_g = np.log(1.0 - np.exp2(-5.0 - np.arange(R_HEADS, dtype=np.float64)))
    i = np.arange(c, dtype=np.float64)
    diff = i[:, None] - i[None, :]
    dmask = np.where(diff >= 0, np.exp(log_g[:, None, None] * np.maximum(diff, 0.0)), 0.0)
    q_dec = np.exp(log_g[:, None] * (i + 1.0))
    k_dec = np.exp(log_g[:, None] * (c - 1.0 - i))
    kdec_t = np.zeros((SUBLANES, s), np.float64)
    kdec_t[:R_HEADS] = np.tile(k_dec, (1, s // c))
    qdec = np.broadcast_to(q_dec[:, :, None], (R_HEADS, c, R_V_DIM))
    chunk_dec = tuple(float(v) for v in np.exp(log_g * c))
    return (jnp.asarray(dmask, F32), jnp.asarray(qdec, F32), jnp.asarray(kdec_t, F32), chunk_dec)


def kernel(x, mem, norm_mix, w_in, w_branch_a, w_branch_b, w_out, norm_cross, norm_mem,
           w_xq, w_xkv, w_xo, norm_ffn, w_gate, w_up, w_down, norm_final):
    b, s, d = x.shape
    steps = (TOKEN_TILE * QKV_BLOCKS, TOKEN_TILE * MIX_BLOCKS, ROW_TILE)
    assert all(s % t == 0 for t in steps) and b % MOBA_ROWS == 0 and d % LANES == 0
    rope = _rope_tables(s)
    dmask, qdec, kdec_t, chunk_dec = _decay_tables(s)
    row = lambda v: v.reshape(1, d)
    assert w_in.shape[2] == COL_GATES + R_V_WIDTH + 2 * d

    for l in range(w_in.shape[0]):
        w = w_in[l].astype(BF16)
        (k_a, kmean, qt_a, vt_a, rq, rv, rkt, rktd), (wa, wb, wo) = _qkv(
            x, row(norm_mix[l]), w, rope + (kdec_t,), (w_branch_a[l], w_branch_b[l], w_out[l]))
        y_a = _moba(qt_a, k_a, vt_a, kmean.reshape(b, s // MOBA_BLOCK, A_WIDTH))
        x, (wxkv, wxq, wxo) = _mix(x, row(norm_mix[l]), rq, rkt, rktd, rv, y_a, dmask, qdec, chunk_dec,
                                   w, wa, wb, wo, (w_xkv[l], w_xq[l], w_xo[l]))
        x, (wg, wu, wd) = _cross(x, row(norm_cross[l]), mem, row(norm_mem[l]), wxkv, wxq, wxo,
                                 (w_gate[l], w_up[l], w_down[l]))
        last = l == w_in.shape[0] - 1
        x = _ffn(x.reshape(b * s, d), row(norm_ffn[l]), wg, wu, wd,
                 row(norm_final) if last else None).reshape(b, s, d)
    return x
```

```python
import functools
import math

import numpy as np
import jax
import jax.numpy as jnp
from jax import lax
from jax.experimental import pallas as pl
from jax.experimental.pallas import tpu as pltpu

F32 = jnp.float32
BF16 = jnp.bfloat16

EPS = 1e-6
ROPE_THETA = 10000.0
A_HEADS = 8
A_HEAD_DIM = 64
A_WIDTH = A_HEADS * A_HEAD_DIM
MOBA_BLOCK = 256
MOBA_TOPK = 3
R_HEADS = 4
R_QK_DIM = 128
R_V_DIM = 256
R_QK_WIDTH = R_HEADS * R_QK_DIM
R_V_WIDTH = R_HEADS * R_V_DIM
R_CHUNK = 256
X_HEADS = 4
COL_AQ, COL_AK, COL_AV = 0, A_WIDTH, 2 * A_WIDTH
COL_RQ = 3 * A_WIDTH
COL_RK = COL_RQ + R_QK_WIDTH
COL_RV = COL_RK + R_QK_WIDTH
COL_GATES = COL_RV + R_V_WIDTH

LANES = 128
SUBLANES = 8
BF16_ROWS = 16
VMEM_LIMIT = 56 * 1024 * 1024
assert 2 * A_HEAD_DIM == LANES and R_QK_DIM == LANES

TOKEN_TILE = MOBA_BLOCK
QKV_BLOCKS = 4
MIX_BLOCKS = 2
MIX_LATE_COLS = 256
ROW_TILE = 1024
CROSS_PARTS = 4
FF_CHUNK = 256
MOBA_ROWS = 2
MOBA_UNROLL = 4
MOBA_LOOKAHEAD = 7
MOBA_ONES_ROWS = BF16_ROWS

NEG = -1e30
LOG2_E = math.log2(math.e)


def _dot(a, b):
    return jnp.dot(a, b, preferred_element_type=F32)


def _rms(x, g):
    return x * lax.rsqrt(jnp.mean(x * x, axis=-1, keepdims=True) + EPS) * g


def _sigmoid(x):
    return 1.0 / (1.0 + jnp.exp(-x))


def _params(*sem, fused_input=None, n_inputs=0):
    fuse = None if fused_input is None else [i == fused_input for i in range(n_inputs)]
    return pltpu.CompilerParams(dimension_semantics=sem, vmem_limit_bytes=VMEM_LIMIT, allow_input_fusion=fuse)


def _cast_ahead(mats, n_steps, step_of):
    assert all(w.shape[0] % n_steps == 0 for w in mats)
    sliced = [w.reshape(n_steps, w.shape[0] // n_steps, w.shape[1]) for w in mats]
    specs = [pl.BlockSpec((1,) + w.shape[1:], lambda *ids: (step_of(*ids), 0, 0)) for w in sliced]
    return sliced, specs, [jax.ShapeDtypeStruct(w.shape, BF16) for w in sliced]


def _split_cast_refs(rest, n_out, n_scratch):
    n = (len(rest) - n_out - n_scratch) // 2
    outs = rest[n:n + n_out]
    scratch = rest[len(rest) - n_scratch:]
    for src, dst in zip(rest[:n], rest[n + n_out:2 * n + n_out]):
        dst[...] = src[...].astype(BF16)
    return outs, scratch


def _full(shape):
    n = len(shape)
    return pl.BlockSpec(shape, lambda *_: (0,) * n, pipeline_mode=pl.Buffered(1))


QKV_INPUTS, QKV_OUTPUTS = 12, 8
QKV_T_COLS = ((COL_AQ, A_WIDTH), (COL_AV, A_WIDTH), (COL_RK, R_QK_WIDTH))


def _qkv_kernel(*refs):
    outs, wt_refs = _split_cast_refs(refs[QKV_INPUTS:], QKV_OUTPUTS, len(QKV_T_COLS))
    w_ref = refs[2]

    @pl.when((pl.program_id(0) == 0) & (pl.program_id(1) == 0))
    def _():
        for wt_ref, (col, n) in zip(wt_refs, QKV_T_COLS):
            for c in range(0, n, TOKEN_TILE):
                wt_ref[c:c + TOKEN_TILE, :] = w_ref[:, col + c:col + c + TOKEN_TILE].astype(F32).T.astype(BF16)

    for sb in range(QKV_BLOCKS):
        _qkv_block(sb, *refs[:3], *wt_refs, *refs[3:QKV_INPUTS], *outs)


def _qkv_block(sb, x_ref, g_ref, w_ref, wqt_ref, wvt_ref, wkt_ref, cosa_ref, sina_ref, cosr_ref, sinr_ref,
               costa_ref, sinta_ref, costr_ref, sintr_ref, kdec_ref,
               k_ref, kmean_ref, qt_ref, vt_ref, rq_ref, rv_ref, rkt_ref, rktd_ref):
    t = TOKEN_TILE
    tok = slice(sb * t, (sb + 1) * t)
    h = _rms(x_ref[0, tok, :], g_ref[...])
    hb = h.astype(BF16)
    ht = h.T.astype(BF16)
    lane = lax.broadcasted_iota(jnp.int32, (t, LANES), 1)
    a_half, r_half = A_HEAD_DIM // 2, R_QK_DIM // 2
    first_half = (lane & a_half) == 0

    ak = _dot(hb, w_ref[:, COL_AK:COL_AV])
    cosa, sina = cosa_ref[tok, :], sina_ref[tok, :]
    for c in range(A_WIDTH // LANES):
        blk = ak[:, c * LANES:(c + 1) * LANES]
        rot = jnp.where(first_half, pltpu.roll(blk, LANES - a_half, 1), pltpu.roll(blk, a_half, 1))
        kr = blk * cosa + rot * sina
        k_ref[0, sb, :, c * LANES:(c + 1) * LANES] = kr.astype(BF16)
        kmean_ref[0, sb, :, c * LANES:(c + 1) * LANES] = jnp.mean(kr, axis=0, keepdims=True)

    rq = _dot(hb, w_ref[:, COL_RQ:COL_RK])
    cosr, sinr = cosr_ref[tok, :], sinr_ref[tok, :]
    for c in range(R_HEADS):
        blk = rq[:, c * R_QK_DIM:(c + 1) * R_QK_DIM]
        rq_ref[0, tok, c * R_QK_DIM:(c + 1) * R_QK_DIM] = (blk * cosr + pltpu.roll(blk, r_half, 1) * sinr).astype(BF16)

    rv_ref[0, tok, :] = _dot(hb, w_ref[:, COL_RV:COL_GATES]).astype(BF16)

    qt = _dot(wqt_ref[...], ht)
    cost, sint = costa_ref[:, tok], sinta_ref[:, tok]
    half = A_HEAD_DIM // 2
    scale_a = A_HEAD_DIM ** -0.5 * LOG2_E
    for hd in range(A_HEADS):
        x1 = qt[hd * A_HEAD_DIM:hd * A_HEAD_DIM + half]
        x2 = qt[hd * A_HEAD_DIM + half:(hd + 1) * A_HEAD_DIM]
        qt_ref[0, sb, hd * A_HEAD_DIM:hd * A_HEAD_DIM + half, :] = ((x1 * cost - x2 * sint) * scale_a).astype(BF16)
        qt_ref[0, sb, hd * A_HEAD_DIM + half:(hd + 1) * A_HEAD_DIM, :] = ((x2 * cost + x1 * sint) * scale_a).astype(BF16)

    vt_ref[0, sb] = _dot(wvt_ref[...], ht).astype(BF16)

    rkt = _dot(wkt_ref[...], ht)
    cost, sint = costr_ref[:, tok], sintr_ref[:, tok]
    half = R_QK_DIM // 2
    scale_r = R_QK_DIM ** -0.5
    for hd in range(R_HEADS):
        x1 = rkt[hd * R_QK_DIM:hd * R_QK_DIM + half]
        x2 = rkt[hd * R_QK_DIM + half:(hd + 1) * R_QK_DIM]
        o1 = (x1 * cost - x2 * sint) * scale_r
        o2 = (x2 * cost + x1 * sint) * scale_r
        dec = kdec_ref[hd:hd + 1, tok]
        rkt_ref[0, sb, hd * R_QK_DIM:hd * R_QK_DIM + half, :] = o1.astype(BF16)
        rkt_ref[0, sb, hd * R_QK_DIM + half:(hd + 1) * R_QK_DIM, :] = o2.astype(BF16)
        rktd_ref[0, sb, hd * R_QK_DIM:hd * R_QK_DIM + half, :] = (o1 * dec).astype(BF16)
        rktd_ref[0, sb, hd * R_QK_DIM + half:(hd + 1) * R_QK_DIM, :] = (o2 * dec).astype(BF16)


def _qkv(x, g, w, tabs, cast_ahead):
    b, s, d = x.shape
    nb = s // TOKEN_TILE
    qb = QKV_BLOCKS
    t = qb * TOKEN_TILE
    steps = nb // qb
    tok = lambda w: pl.BlockSpec((1, t, w), lambda bi, i: (bi, i, 0))
    blk_t = lambda r: pl.BlockSpec((1, qb, r, TOKEN_TILE), lambda bi, i: (bi, i, 0, 0))
    nat_tab = pl.BlockSpec((t, LANES), lambda bi, i: (i, 0))
    tr_tab = lambda r: pl.BlockSpec((r, t), lambda bi, i: (0, i))
    sliced, cast_specs, cast_shapes = _cast_ahead(cast_ahead, b * steps, lambda bi, i: bi * steps + i)
    in_specs = [tok(d), _full((1, d)), _full(w.shape),
                nat_tab, nat_tab, nat_tab, nat_tab,
                tr_tab(A_HEAD_DIM // 2), tr_tab(A_HEAD_DIM // 2),
                tr_tab(R_QK_DIM // 2), tr_tab(R_QK_DIM // 2), tr_tab(SUBLANES)]
    assert len(in_specs) == QKV_INPUTS
    out = pl.pallas_call(
        _qkv_kernel,
        grid=(b, steps),
        in_specs=in_specs + cast_specs,
        out_specs=[pl.BlockSpec((1, qb, TOKEN_TILE, A_WIDTH), lambda bi, i: (bi, i, 0, 0)),
                   pl.BlockSpec((1, qb, 1, A_WIDTH), lambda bi, i: (bi, i, 0, 0)),
                   blk_t(A_WIDTH), blk_t(A_WIDTH), tok(R_QK_WIDTH), tok(R_V_WIDTH),
                   blk_t(R_QK_WIDTH), blk_t(R_QK_WIDTH)] + cast_specs,
        out_shape=[jax.ShapeDtypeStruct((b, nb, TOKEN_TILE, A_WIDTH), BF16),
                   jax.ShapeDtypeStruct((b, nb, 1, A_WIDTH), F32),
                   jax.ShapeDtypeStruct((b, nb, A_WIDTH, TOKEN_TILE), BF16),
                   jax.ShapeDtypeStruct((b, nb, A_WIDTH, TOKEN_TILE), BF16),
                   jax.ShapeDtypeStruct((b, s, R_QK_WIDTH), BF16),
                   jax.ShapeDtypeStruct((b, s, R_V_WIDTH), BF16),
                   jax.ShapeDtypeStruct((b, nb, R_QK_WIDTH, TOKEN_TILE), BF16),
                   jax.ShapeDtypeStruct((b, nb, R_QK_WIDTH, TOKEN_TILE), BF16)] + cast_shapes,
        scratch_shapes=[pltpu.VMEM((n, d), BF16) for _, n in QKV_T_COLS],
        compiler_params=_params("arbitrary", "arbitrary", fused_input=2, n_inputs=QKV_INPUTS + len(sliced)),
        name="qkv",
    )(x, g, w, *tabs, *sliced)
    return out[:QKV_OUTPUTS], [o.reshape(w.shape) for o, w in zip(out[QKV_OUTPUTS:], cast_ahead)]


def _moba_kernel(qt_ref, k_ref, vt_ref, km_ref, o_ref, qm_ref, m_ref, acc_ref, ot_ref, s_ref):
    i = pl.program_id(1)
    nb = km_ref.shape[1]
    t = MOBA_BLOCK
    dh = A_HEAD_DIM
    row = lax.broadcasted_iota(jnp.int32, (2 * dh, t), 0)
    blk = lax.broadcasted_iota(jnp.int32, (nb, t), 0)
    past = blk < i
    causal = (lax.broadcasted_iota(jnp.int32, (t, t), 0) <= lax.broadcasted_iota(jnp.int32, (t, t), 1))
    ones = jnp.ones((MOBA_ONES_ROWS, t), BF16)
    lanes = lambda hd: slice((hd // 2) * LANES, (hd // 2 + 1) * LANES)
    rows = lambda hd: slice(hd * dh, (hd + 1) * dh)
    row_heads = [(b, hd) for b in range(MOBA_ROWS) for hd in range(A_HEADS)]

    def select_blocks(b, hd):
        hh = hd % 2
        qt_pair = qt_ref[b, 0, lanes(hd), :]
        mine = (row >= hh * dh) & (row < (hh + 1) * dh)
        qt = jnp.where(mine, qt_pair, jnp.zeros_like(qt_pair))
        qm_ref[b, hd, 0:2 * dh, :] = qt

        km = km_ref[b, :, lanes(hd)]
        km_hi = km.astype(BF16)
        km_lo = (km - km_hi.astype(F32)).astype(BF16)
        bs = jnp.where(past, _dot(km_hi, qt) + _dot(km_lo, qt), -jnp.inf)
        picked = jnp.zeros((nb, t), jnp.bool_)
        for _ in range(MOBA_TOPK):
            best = jnp.max(bs, axis=0, keepdims=True)
            first = jnp.min(jnp.where(bs == best, blk, nb), axis=0, keepdims=True)
            hit = blk == first
            picked = picked | hit
            bs = jnp.where(hit, -jnp.inf, bs)
        bias = jnp.where(past & picked, 0.0, NEG).astype(BF16)
        qm_ref[b, hd, 2 * dh:, :] = jnp.concatenate([bias, jnp.zeros((2 * dh - nb, t), BF16)], axis=0)

    key_lane = lax.broadcasted_iota(jnp.int32, (t, 2 * dh), 1)

    def keys_with_block_column(b, j):
        onehot = jnp.where(key_lane == j, 1.0, 0.0).astype(BF16)
        return [jnp.concatenate([k_ref[b, j, :, lanes(2 * p)], onehot], axis=1) for p in range(A_HEADS // 2)]

    def pv(b, j, hd, p):
        vt_aug = jnp.concatenate([vt_ref[b, j, rows(hd), :], ones], axis=0)
        return _dot(vt_aug, p.astype(BF16))

    def own_unit(b, hd):
        def scores():
            return jnp.where(causal, _dot(k_ref[b, i, :, lanes(hd)], qm_ref[b, hd, 0:2 * dh, :]), NEG)

        def update(s, m0):
            m_ref[b, hd] = jnp.broadcast_to(m0, (SUBLANES, t))
            acc_ref[b, hd] = pv(b, i, hd, jnp.exp2(s - m0))

        return scores, update

    def past_units(b, j):
        keys = keys_with_block_column(b, j)

        def unit(hd):
            def scores():
                return _dot(keys[hd // 2], qm_ref[b, hd])

            def update(s, m_blk):
                m_old = m_ref[b, hd]
                m_new = jnp.maximum(m_old, m_blk)
                alpha = jnp.exp2(m_old - m_new)
                m_ref[b, hd] = m_new
                acc_ref[b, hd] = alpha[0:1, :] * acc_ref[b, hd] + pv(b, j, hd, jnp.exp2(s - m_new[0:1, :]))

            return scores, update

        return [unit(hd) for hd in range(A_HEADS)]

    def pipelined(units):
        blk_max = {}
        slots = MOBA_LOOKAHEAD + 1
        for step in range(len(units) + MOBA_LOOKAHEAD):
            if step < len(units):
                sc = units[step][0]()
                s_ref[step % slots] = sc
                blk_max[step] = jnp.max(sc, axis=0, keepdims=True)
            if step >= MOBA_LOOKAHEAD:
                u = step - MOBA_LOOKAHEAD
                units[u][1](s_ref[u % slots], blk_max.pop(u))

    n_groups = i // MOBA_UNROLL
    for left in range(MOBA_UNROLL):
        @pl.when(i % MOBA_UNROLL == left)
        def _(left=left):
            for b, hd in row_heads:
                select_blocks(b, hd)
            units = [own_unit(b, hd) for b, hd in row_heads]
            for r in range(left):
                for b in range(MOBA_ROWS):
                    units += past_units(b, n_groups * MOBA_UNROLL + r)
            pipelined(units)

    def group_body(jj, carry):
        pipelined([u for r in range(MOBA_UNROLL) for b in range(MOBA_ROWS)
                   for u in past_units(b, MOBA_UNROLL * jj + r)])
        return carry

    lax.fori_loop(0, n_groups, group_body, 0)

    for b in range(MOBA_ROWS):
        for hd in range(A_HEADS):
            ot_ref[rows(hd), :] = acc_ref[b, hd, 0:dh, :] / acc_ref[b, hd, dh:dh + 1, :]
        o_ref[b] = ot_ref[...].T.astype(BF16)


def _moba(qt, k, vt, kmean):
    b, nb, t, w = k.shape
    r = MOBA_ROWS
    assert b % r == 0
    return pl.pallas_call(
        _moba_kernel,
        grid=(b // r, nb),
        in_specs=[pl.BlockSpec((r, 1, w, t), lambda bi, i: (bi, i, 0, 0)),
                  pl.BlockSpec((r, nb, t, w), lambda bi, i: (bi, 0, 0, 0)),
                  pl.BlockSpec((r, nb, w, t), lambda bi, i: (bi, 0, 0, 0)),
                  pl.BlockSpec((r, nb, w), lambda bi, i: (bi, 0, 0))],
        out_specs=pl.BlockSpec((r, t, w), lambda bi, i: (bi, i, 0)),
        out_shape=jax.ShapeDtypeStruct((b, nb * t, w), BF16),
        scratch_shapes=[pltpu.VMEM((r, A_HEADS, 4 * A_HEAD_DIM, t), BF16),
                        pltpu.VMEM((r, A_HEADS, SUBLANES, t), F32),
                        pltpu.VMEM((r, A_HEADS, A_HEAD_DIM + MOBA_ONES_ROWS, t), F32),
                        pltpu.VMEM((w, t), F32),
                        pltpu.VMEM((MOBA_LOOKAHEAD + 1, t, t), F32)],
        compiler_params=_params("parallel", "parallel"),
        name="moba",
    )(qt, k, vt, kmean)


def _mix_kernel(chunk_dec, x_ref, g_ref, rq_ref, rkt_ref, rktd_ref, rv_ref, ya_ref, dmask_ref, qdec_ref,
                w_ref, wa_ref, wb_ref, wo_ref, *rest):
    (o_ref,), (state_ref, yb_ref) = _split_cast_refs(rest, 1, 2)

    @pl.when(pl.program_id(1) == 0)
    def _():
        state_ref[...] = jnp.zeros_like(state_ref)

    x = x_ref[0]
    d = x.shape[1]
    hb = _rms(x, g_ref[...]).astype(BF16)
    c = R_CHUNK
    per_blk = TOKEN_TILE // c
    n_chunks = MIX_BLOCKS * per_blk
    rows = lambda ci: slice(ci * c, (ci + 1) * c)
    qk = lambda hd: slice(hd * R_QK_DIM, (hd + 1) * R_QK_DIM)
    vv = lambda hd: slice(hd * R_V_DIM, (hd + 1) * R_V_DIM)
    units = [(ci, hd) for ci in range(n_chunks) for hd in range(R_HEADS)]
    keys_t = lambda ref, ci, hd: ref[0, ci // per_blk, qk(hd), rows(ci % per_blk)]

    att = {u: _dot(rq_ref[0, rows(u[0]), qk(u[1])], keys_t(rkt_ref, *u)) for u in units}
    kv = {u: _dot(keys_t(rktd_ref, *u), rv_ref[0, rows(u[0]), vv(u[1])]) for u in units}
    rg = _dot(hb, w_ref[:, COL_GATES:COL_GATES + R_V_WIDTH])
    w = MIX_LATE_COLS
    late = ([lambda c=c: _dot(ya_ref[0], wa_ref[:, c:c + w]) for c in range(0, d, w)]
            + [lambda c=c: _dot(hb, w_ref[:, c:c + w])
               for c in range(COL_GATES + R_V_WIDTH, COL_GATES + R_V_WIDTH + 2 * d, w)])
    late_out = []

    cross = {}
    for hd in range(R_HEADS):
        st = state_ref[hd]
        for ci in range(n_chunks):
            cross[ci, hd] = _dot(rq_ref[0, rows(ci), qk(hd)], st.astype(BF16))
            st = chunk_dec[hd] * st + kv[ci, hd]
        state_ref[hd] = st

    gate = rg * _sigmoid(rg)
    for n, (ci, hd) in enumerate(units):
        inner = _dot((att[ci, hd] * dmask_ref[hd]).astype(BF16), rv_ref[0, rows(ci), vv(hd)])
        y = inner + cross[ci, hd] * qdec_ref[hd]
        mu = jnp.mean(y, axis=-1, keepdims=True)
        yc = y - mu
        var = jnp.mean(yc * yc, axis=-1, keepdims=True)
        yb_ref[rows(ci), vv(hd)] = (yc * lax.rsqrt(var + EPS) * gate[rows(ci), vv(hd)]).astype(BF16)
        for k in range(n * len(late) // len(units), (n + 1) * len(late) // len(units)):
            late_out.append(late[k]())

    pieces = d // w
    ta, ga, gb = (jnp.concatenate(late_out[k * pieces:(k + 1) * pieces], axis=1) for k in range(3))
    blocks = [slice(r * TOKEN_TILE, (r + 1) * TOKEN_TILE) for r in range(MIX_BLOCKS)]
    tb = [_dot(yb_ref[tok, :], wb_ref[...]) for tok in blocks]
    for tok, tb_blk in zip(blocks, tb):
        merged = _sigmoid(ga[tok]) * ta[tok] + _sigmoid(gb[tok]) * tb_blk
        o_ref[0, tok, :] = x[tok] + _dot(merged.astype(BF16), wo_ref[...])


def _mix(x, g, rq, rkt, rktd, rv, ya, dmask, qdec, chunk_dec, w, wa, wb, wo, cast_ahead):
    b, s, d = x.shape
    t = MIX_BLOCKS * TOKEN_TILE
    steps = s // t
    tok = lambda w: pl.BlockSpec((1, t, w), lambda bi, i: (bi, i, 0))
    blk_t = pl.BlockSpec((1, MIX_BLOCKS, R_QK_WIDTH, TOKEN_TILE), lambda bi, i: (bi, i, 0, 0))
    sliced, cast_specs, cast_shapes = _cast_ahead(cast_ahead, b * steps, lambda bi, i: bi * steps + i)
    out = pl.pallas_call(
        functools.partial(_mix_kernel, chunk_dec),
        grid=(b, steps),
        in_specs=[tok(d), _full((1, d)), tok(R_QK_WIDTH), blk_t, blk_t, tok(R_V_WIDTH), tok(A_WIDTH),
                  _full(dmask.shape), _full(qdec.shape),
                  _full(w.shape), _full(wa.shape), _full(wb.shape), _full(wo.shape)] + cast_specs,
        out_specs=[tok(d)] + cast_specs,
        out_shape=[jax.ShapeDtypeStruct((b, s, d), F32)] + cast_shapes,
        scratch_shapes=[pltpu.VMEM((R_HEADS, R_QK_DIM, R_V_DIM), F32),
                        pltpu.VMEM((t, R_V_WIDTH), BF16)],
        compiler_params=_params("parallel", "arbitrary", fused_input=9, n_inputs=13 + len(sliced)),
        name="mix",
    )(x, g, rq, rkt, rktd, rv, ya, dmask, qdec, w, wa, wb, wo, *sliced)
    return out[0], [o.reshape(w.shape) for o, w in zip(out[1:], cast_ahead)]


def _cross_kernel(x_ref, g_ref, m_ref, gm_ref, wkv_ref, wq_ref, wo_ref, *rest):
    (o_ref,), (qk_ref, vo_ref, p_ref) = _split_cast_refs(rest, 1, 3)
    d = x_ref.shape[2]
    dh = d // X_HEADS
    mlen = m_ref.shape[1]
    cols = lambda hd: slice(hd * dh, (hd + 1) * dh)
    mcols = lambda hd: slice(hd * mlen, (hd + 1) * mlen)

    @pl.when(pl.program_id(1) == 0)
    def _():
        mb = _rms(m_ref[0], gm_ref[...]).astype(BF16)
        kt = _dot(mb, wkv_ref[:, 0:d]).T.astype(BF16)
        v = _dot(mb, wkv_ref[:, d:]).astype(BF16)
        for hd in range(X_HEADS):
            qk_ref[:, mcols(hd)] = (_dot(wq_ref[:, cols(hd)], kt[cols(hd), :]) * dh ** -0.5).astype(BF16)
            vo_ref[mcols(hd), :] = _dot(v[:, cols(hd)], wo_ref[cols(hd), :]).astype(BF16)

    n_parts = CROSS_PARTS
    t = x_ref.shape[1] // n_parts
    part = lambda r: slice(r * t, (r + 1) * t)

    def scores_of(r):
        hb = _rms(x_ref[0, part(r), :], g_ref[...]).astype(BF16)
        return _dot(hb, qk_ref[...])

    nxt = scores_of(0)
    for r in range(n_parts):
        scores = nxt
        if r + 1 < n_parts:
            nxt = scores_of(r + 1)
        for hd in range(X_HEADS):
            s = scores[:, mcols(hd)]
            p = jnp.exp(s - jnp.max(s, axis=-1, keepdims=True))
            p_ref[part(r), mcols(hd)] = (p / jnp.sum(p, axis=-1, keepdims=True)).astype(BF16)
        o_ref[0, part(r), :] = x_ref[0, part(r), :] + _dot(p_ref[part(r), :], vo_ref[...])


def _cross(x, g, mem, gm, wkv, wq, wo, cast_ahead):
    b, s, d = x.shape
    m = mem.shape[1]
    t = ROW_TILE
    steps = s // t
    tok = pl.BlockSpec((1, t, d), lambda bi, i: (bi, i, 0))
    sliced, cast_specs, cast_shapes = _cast_ahead(cast_ahead, b * steps, lambda bi, i: bi * steps + i)
    out = pl.pallas_call(
        _cross_kernel,
        grid=(b, steps),
        in_specs=[tok, _full((1, d)), pl.BlockSpec((1, m, d), lambda bi, i: (bi, 0, 0)), _full((1, d)),
                  _full(wkv.shape), _full(wq.shape), _full(wo.shape)] + cast_specs,
        out_specs=[tok] + cast_specs,
        out_shape=[jax.ShapeDtypeStruct((b, s, d), F32)] + cast_shapes,
        scratch_shapes=[pltpu.VMEM((d, X_HEADS * m), BF16),
                        pltpu.VMEM((X_HEADS * m, d), BF16),
                        pltpu.VMEM((t, X_HEADS * m), BF16)],
        compiler_params=_params("parallel", "arbitrary"),
        name="cross",
    )(x, g, mem, gm, wkv, wq, wo, *sliced)
    return out[0], [o.reshape(w.shape) for o, w in zip(out[1:], cast_ahead)]


def _ffn_kernel(x_ref, g_ref, wg_ref, wu_ref, wd_ref, *rest):
    o_ref = rest[-1]
    x = x_ref[...]
    hb = _rms(x, g_ref[...]).astype(BF16)
    n_chunks = wg_ref.shape[1] // FF_CHUNK
    cols = lambda c: slice(c * FF_CHUNK, (c + 1) * FF_CHUNK)

    def up(c):
        return _dot(hb, wg_ref[:, cols(c)]), _dot(hb, wu_ref[:, cols(c)])

    acc = x
    nxt = up(0)
    for c in range(n_chunks):
        gate, upv = nxt
        if c + 1 < n_chunks:
            nxt = up(c + 1)
        act = gate * _sigmoid(gate) * upv
        acc = acc + _dot(act.astype(BF16), wd_ref[cols(c), :])
    o_ref[...] = _rms(acc, rest[0][...]) if len(rest) == 2 else acc


def _ffn(x2, g, wg, wu, wd, final_gain=None):
    n, d = x2.shape
    t = ROW_TILE
    tok = pl.BlockSpec((t, d), lambda i: (i, 0))
    final = () if final_gain is None else (final_gain,)
    return pl.pallas_call(
        _ffn_kernel,
        grid=(n // t,),
        in_specs=[tok, _full((1, d)), _full(wg.shape), _full(wu.shape), _full(wd.shape)] + [_full((1, d))] * len(final),
        out_specs=tok,
        out_shape=jax.ShapeDtypeStruct((n, d), F32),
        compiler_params=_params("parallel"),
        name="ffn",
    )(x2, g, wg, wu, wd, *final)


def _rope_tables(s):
    def cs(dim):
        inv = ROPE_THETA ** (-np.arange(0, dim, 2, dtype=np.float64) / dim)
        ang = np.arange(s, dtype=np.float64)[:, None] * inv[None, :]
        return np.cos(ang), np.sin(ang)

    ca, sa = cs(A_HEAD_DIM)
    cr, sr = cs(R_QK_DIM)
    sign = lambda half: np.tile(np.repeat([-1.0, 1.0], half), LANES // (2 * half))[None, :]
    tabs = (np.tile(ca, (1, 4)), np.tile(sa, (1, 4)) * sign(A_HEAD_DIM // 2),
            np.tile(cr, (1, 2)), np.tile(sr, (1, 2)) * sign(R_QK_DIM // 2),
            ca.T, sa.T, cr.T, sr.T)
    return tuple(jnp.asarray(np.ascontiguousarray(t), F32) for t in tabs)


def _decay_tables(s):
    c = R_CHUNK
    log_g = np.log(1.0 - np.exp2(-5.0 - np.arange(R_HEADS, dtype=np.float64)))
    i = np.arange(c, dtype=np.float64)
    diff = i[:, None] - i[None, :]
    dmask = np.where(diff >= 0, np.exp(log_g[:, None, None] * np.maximum(diff, 0.0)), 0.0)
    q_dec = np.exp(log_g[:, None] * (i + 1.0))
    k_dec = np.exp(log_g[:, None] * (c - 1.0 - i))
    kdec_t = np.zeros((SUBLANES, s), np.float64)
    kdec_t[:R_HEADS] = np.tile(k_dec, (1, s // c))
    qdec = np.broadcast_to(q_dec[:, :, None], (R_HEADS, c, R_V_DIM))
    chunk_dec = tuple(float(v) for v in np.exp(log_g * c))
    return (jnp.asarray(dmask, F32), jnp.asarray(qdec, F32), jnp.asarray(kdec_t, F32), chunk_dec)


def kernel(x, mem, norm_mix, w_in, w_branch_a, w_branch_b, w_out, norm_cross, norm_mem,
           w_xq, w_xkv, w_xo, norm_ffn, w_gate, w_up, w_down, norm_final):
    b, s, d = x.shape
    steps = (TOKEN_TILE * QKV_BLOCKS, TOKEN_TILE * MIX_BLOCKS, ROW_TILE)
    assert all(s % t == 0 for t in steps) and b % MOBA_ROWS == 0 and d % LANES == 0
    rope = _rope_tables(s)
    dmask, qdec, kdec_t, chunk_dec = _decay_tables(s)
    row = lambda v: v.reshape(1, d)
    assert w_in.shape[2] == COL_GATES + R_V_WIDTH + 2 * d

    for l in range(w_in.shape[0]):
        w = w_in[l].astype(BF16)
        (k_a, kmean, qt_a, vt_a, rq, rv, rkt, rktd), (wa, wb, wo) = _qkv(
            x, row(norm_mix[l]), w, rope + (kdec_t,), (w_branch_a[l], w_branch_b[l], w_out[l]))
        y_a = _moba(qt_a, k_a, vt_a, kmean.reshape(b, s // MOBA_BLOCK, A_WIDTH))
        x, (wxkv, wxq, wxo) = _mix(x, row(norm_mix[l]), rq, rkt, rktd, rv, y_a, dmask, qdec, chunk_dec,
                                   w, wa, wb, wo, (w_xkv[l], w_xq[l], w_xo[l]))
        x, (wg, wu, wd) = _cross(x, row(norm_cross[l]), mem, row(norm_mem[l]), wxkv, wxq, wxo,
                                 (w_gate[l], w_up[l], w_down[l]))
        last = l == w_in.shape[0] - 1
        x = _ffn(x.reshape(b * s, d), row(norm_ffn[l]), wg, wu, wd,
                 row(norm_final) if last else None).reshape(b, s, d)
    return x
```

```python
import functools
import math

import numpy as np
import jax
import jax.numpy as jnp
from jax import lax
from jax.experimental import pallas as pl
from jax.experimental.pallas import tpu as pltpu

F32 = jnp.float32
BF16 = jnp.bfloat16

EPS = 1e-6
ROPE_THETA = 10000.0
A_HEADS = 8
A_HEAD_DIM = 64
A_WIDTH = A_HEADS * A_HEAD_DIM
MOBA_BLOCK = 256
MOBA_TOPK = 3
R_HEADS = 4
R_QK_DIM = 128
R_V_DIM = 256
R_QK_WIDTH = R_HEADS * R_QK_DIM
R_V_WIDTH = R_HEADS * R_V_DIM
R_CHUNK = 256
X_HEADS = 4
COL_AQ, COL_AK, COL_AV = 0, A_WIDTH, 2 * A_WIDTH
COL_RQ = 3 * A_WIDTH
COL_RK = COL_RQ + R_QK_WIDTH
COL_RV = COL_RK + R_QK_WIDTH
COL_GATES = COL_RV + R_V_WIDTH

LANES = 128
SUBLANES = 8
BF16_ROWS = 16
VMEM_LIMIT = 56 * 1024 * 1024
assert 2 * A_HEAD_DIM == LANES and R_QK_DIM == LANES

TOKEN_TILE = MOBA_BLOCK
QKV_BLOCKS = 4
MIX_BLOCKS = 2
MIX_LATE_COLS = 256
ROW_TILE = 1024
CROSS_PARTS = 4
FF_CHUNK = 256
MOBA_ROWS = 2
MOBA_UNROLL = 4
MOBA_LOOKAHEAD = 7
MOBA_ONES_ROWS = BF16_ROWS

NEG = -1e30
LOG2_E = math.log2(math.e)


def _dot(a, b):
    return jnp.dot(a, b, preferred_element_type=F32)


def _rms(x, g):
    return x * lax.rsqrt(jnp.mean(x * x, axis=-1, keepdims=True) + EPS) * g


def _sigmoid(x):
    return 1.0 / (1.0 + jnp.exp(-x))


def _params(*sem, fused_input=None, n_inputs=0):
    fuse = None if fused_input is None else [i == fused_input for i in range(n_inputs)]
    return pltpu.CompilerParams(dimension_semantics=sem, vmem_limit_bytes=VMEM_LIMIT, allow_input_fusion=fuse)


def _cast_ahead(mats, n_steps, step_of):
    assert all(w.shape[0] % n_steps == 0 for w in mats)
    sliced = [w.reshape(n_steps, w.shape[0] // n_steps, w.shape[1]) for w in mats]
    specs = [pl.BlockSpec((1,) + w.shape[1:], lambda *ids: (step_of(*ids), 0, 0)) for w in sliced]
    return sliced, specs, [jax.ShapeDtypeStruct(w.shape, BF16) for w in sliced]


def _split_cast_refs(rest, n_out, n_scratch):
    n = (len(rest) - n_out - n_scratch) // 2
    outs = rest[n:n + n_out]
    scratch = rest[len(rest) - n_scratch:]
    for src, dst in zip(rest[:n], rest[n + n_out:2 * n + n_out]):
        dst[...] = src[...].astype(BF16)
    return outs, scratch


def _full(shape):
    n = len(shape)
    return pl.BlockSpec(shape, lambda *_: (0,) * n, pipeline_mode=pl.Buffered(1))


QKV_INPUTS, QKV_OUTPUTS = 12, 8
QKV_T_COLS = ((COL_AQ, A_WIDTH), (COL_AV, A_WIDTH), (COL_RK, R_QK_WIDTH))


def _qkv_kernel(*refs):
    outs, wt_refs = _split_cast_refs(refs[QKV_INPUTS:], QKV_OUTPUTS, len(QKV_T_COLS))
    w_ref = refs[2]

    @pl.when((pl.program_id(0) == 0) & (pl.program_id(1) == 0))
    def _():
        for wt_ref, (col, n) in zip(wt_refs, QKV_T_COLS):
            for c in range(0, n, TOKEN_TILE):
                wt_ref[c:c + TOKEN_TILE, :] = w_ref[:, col + c:col + c + TOKEN_TILE].astype(F32).T.astype(BF16)

    for sb in range(QKV_BLOCKS):
        _qkv_block(sb, *refs[:3], *wt_refs, *refs[3:QKV_INPUTS], *outs)


def _qkv_block(sb, x_ref, g_ref, w_ref, wqt_ref, wvt_ref, wkt_ref, cosa_ref, sina_ref, cosr_ref, sinr_ref,
               costa_ref, sinta_ref, costr_ref, sintr_ref, kdec_ref,
               k_ref, kmean_ref, qt_ref, vt_ref, rq_ref, rv_ref, rkt_ref, rktd_ref):
    t = TOKEN_TILE
    tok = slice(sb * t, (sb + 1) * t)
    h = _rms(x_ref[0, tok, :], g_ref[...])
    hb = h.astype(BF16)
    ht = h.T.astype(BF16)
    lane = lax.broadcasted_iota(jnp.int32, (t, LANES), 1)
    a_half, r_half = A_HEAD_DIM // 2, R_QK_DIM // 2
    first_half = (lane & a_half) == 0

    ak = _dot(hb, w_ref[:, COL_AK:COL_AV])
    cosa, sina = cosa_ref[tok, :], sina_ref[tok, :]
    for c in range(A_WIDTH // LANES):
        blk = ak[:, c * LANES:(c + 1) * LANES]
        rot = jnp.where(first_half, pltpu.roll(blk, LANES - a_half, 1), pltpu.roll(blk, a_half, 1))
        kr = blk * cosa + rot * sina
        k_ref[0, sb, :, c * LANES:(c + 1) * LANES] = kr.astype(BF16)
        kmean_ref[0, sb, :, c * LANES:(c + 1) * LANES] = jnp.mean(kr, axis=0, keepdims=True)

    rq = _dot(hb, w_ref[:, COL_RQ:COL_RK])
    cosr, sinr = cosr_ref[tok, :], sinr_ref[tok, :]
    for c in range(R_HEADS):
        blk = rq[:, c * R_QK_DIM:(c + 1) * R_QK_DIM]
        rq_ref[0, tok, c * R_QK_DIM:(c + 1) * R_QK_DIM] = (blk * cosr + pltpu.roll(blk, r_half, 1) * sinr).astype(BF16)

    rv_ref[0, tok, :] = _dot(hb, w_ref[:, COL_RV:COL_GATES]).astype(BF16)

    qt = _dot(wqt_ref[...], ht)
    cost, sint = costa_ref[:, tok], sinta_ref[:, tok]
    half = A_HEAD_DIM // 2
    scale_a = A_HEAD_DIM ** -0.5 * LOG2_E
    for hd in range(A_HEADS):
        x1 = qt[hd * A_HEAD_DIM:hd * A_HEAD_DIM + half]
        x2 = qt[hd * A_HEAD_DIM + half:(hd + 1) * A_HEAD_DIM]
        qt_ref[0, sb, hd * A_HEAD_DIM:hd * A_HEAD_DIM + half, :] = ((x1 * cost - x2 * sint) * scale_a).astype(BF16)
        qt_ref[0, sb, hd * A_HEAD_DIM + half:(hd + 1) * A_HEAD_DIM, :] = ((x2 * cost + x1 * sint) * scale_a).astype(BF16)

    vt_ref[0, sb] = _dot(wvt_ref[...], ht).astype(BF16)

    rkt = _dot(wkt_ref[...], ht)
    cost, sint = costr_ref[:, tok], sintr_ref[:, tok]
    half = R_QK_DIM // 2
    scale_r = R_QK_DIM ** -0.5
    for hd in range(R_HEADS):
        x1 = rkt[hd * R_QK_DIM:hd * R_QK_DIM + half]
        x2 = rkt[hd * R_QK_DIM + half:(hd + 1) * R_QK_DIM]
        o1 = (x1 * cost - x2 * sint) * scale_r
        o2 = (x2 * cost + x1 * sint) * scale_r
        dec = kdec_ref[hd:hd + 1, tok]
        rkt_ref[0, sb, hd * R_QK_DIM:hd * R_QK_DIM + half, :] = o1.astype(BF16)
        rkt_ref[0, sb, hd * R_QK_DIM + half:(hd + 1) * R_QK_DIM, :] = o2.astype(BF16)
        rktd_ref[0, sb, hd * R_QK_DIM:hd * R_QK_DIM + half, :] = (o1 * dec).astype(BF16)
        rktd_ref[0, sb, hd * R_QK_DIM + half:(hd + 1) * R_QK_DIM, :] = (o2 * dec).astype(BF16)


def _qkv(x, g, w, tabs, cast_ahead):
    b, s, d = x.shape
    nb = s // TOKEN_TILE
    qb = QKV_BLOCKS
    t = qb * TOKEN_TILE
    steps = nb // qb
    tok = lambda w: pl.BlockSpec((1, t, w), lambda bi, i: (bi, i, 0))
    blk_t = lambda r: pl.BlockSpec((1, qb, r, TOKEN_TILE), lambda bi, i: (bi, i, 0, 0))
    nat_tab = pl.BlockSpec((t, LANES), lambda bi, i: (i, 0))
    tr_tab = lambda r: pl.BlockSpec((r, t), lambda bi, i: (0, i))
    sliced, cast_specs, cast_shapes = _cast_ahead(cast_ahead, b * steps, lambda bi, i: bi * steps + i)
    in_specs = [tok(d), _full((1, d)), _full(w.shape),
                nat_tab, nat_tab, nat_tab, nat_tab,
                tr_tab(A_HEAD_DIM // 2), tr_tab(A_HEAD_DIM // 2),
                tr_tab(R_QK_DIM // 2), tr_tab(R_QK_DIM // 2), tr_tab(SUBLANES)]
    assert len(in_specs) == QKV_INPUTS
    out = pl.pallas_call(
        _qkv_kernel,
        grid=(b, steps),
        in_specs=in_specs + cast_specs,
        out_specs=[pl.BlockSpec((1, qb, TOKEN_TILE, A_WIDTH), lambda bi, i: (bi, i, 0, 0)),
                   pl.BlockSpec((1, qb, 1, A_WIDTH), lambda bi, i: (bi, i, 0, 0)),
                   blk_t(A_WIDTH), blk_t(A_WIDTH), tok(R_QK_WIDTH), tok(R_V_WIDTH),
                   blk_t(R_QK_WIDTH), blk_t(R_QK_WIDTH)] + cast_specs,
        out_shape=[jax.ShapeDtypeStruct((b, nb, TOKEN_TILE, A_WIDTH), BF16),
                   jax.ShapeDtypeStruct((b, nb, 1, A_WIDTH), F32),
                   jax.ShapeDtypeStruct((b, nb, A_WIDTH, TOKEN_TILE), BF16),
                   jax.ShapeDtypeStruct((b, nb, A_WIDTH, TOKEN_TILE), BF16),
                   jax.ShapeDtypeStruct((b, s, R_QK_WIDTH), BF16),
                   jax.ShapeDtypeStruct((b, s, R_V_WIDTH), BF16),
                   jax.ShapeDtypeStruct((b, nb, R_QK_WIDTH, TOKEN_TILE), BF16),
                   jax.ShapeDtypeStruct((b, nb, R_QK_WIDTH, TOKEN_TILE), BF16)] + cast_shapes,
        scratch_shapes=[pltpu.VMEM((n, d), BF16) for _, n in QKV_T_COLS],
        compiler_params=_params("arbitrary", "arbitrary", fused_input=2, n_inputs=QKV_INPUTS + len(sliced)),
        name="qkv",
    )(x, g, w, *tabs, *sliced)
    return out[:QKV_OUTPUTS], [o.reshape(w.shape) for o, w in zip(out[QKV_OUTPUTS:], cast_ahead)]


def _moba_kernel(qt_ref, k_ref, vt_ref, km_ref, o_ref, qm_ref, m_ref, acc_ref, ot_ref, s_ref):
    i = pl.program_id(1)
    nb = km_ref.shape[1]
    t = MOBA_BLOCK
    dh = A_HEAD_DIM
    row = lax.broadcasted_iota(jnp.int32, (2 * dh, t), 0)
    blk = lax.broadcasted_iota(jnp.int32, (nb, t), 0)
    past = blk < i
    causal = (lax.broadcasted_iota(jnp.int32, (t, t), 0) <= lax.broadcasted_iota(jnp.int32, (t, t), 1))
    ones = jnp.ones((MOBA_ONES_ROWS, t), BF16)
    lanes = lambda hd: slice((hd // 2) * LANES, (hd // 2 + 1) * LANES)
    rows = lambda hd: slice(hd * dh, (hd + 1) * dh)
    row_heads = [(b, hd) for b in range(MOBA_ROWS) for hd in range(A_HEADS)]

    def select_blocks(b, hd):
        hh = hd % 2
        qt_pair = qt_ref[b, 0, lanes(hd), :]
        mine = (row >= hh * dh) & (row < (hh + 1) * dh)
        qt = jnp.where(mine, qt_pair, jnp.zeros_like(qt_pair))
        qm_ref[b, hd, 0:2 * dh, :] = qt

        km = km_ref[b, :, lanes(hd)]
        km_hi = km.astype(BF16)
        km_lo = (km - km_hi.astype(F32)).astype(BF16)
        bs = jnp.where(past, _dot(km_hi, qt) + _dot(km_lo, qt), -jnp.inf)
        picked = jnp.zeros((nb, t), jnp.bool_)
        for _ in range(MOBA_TOPK):
            best = jnp.max(bs, axis=0, keepdims=True)
            first = jnp.min(jnp.where(bs == best, blk, nb), axis=0, keepdims=True)
            hit = blk == first
            picked = picked | hit
            bs = jnp.where(hit, -jnp.inf, bs)
        bias = jnp.where(past & picked, 0.0, NEG).astype(BF16)
        qm_ref[b, hd, 2 * dh:, :] = jnp.concatenate([bias, jnp.zeros((2 * dh - nb, t), BF16)], axis=0)

    key_lane = lax.broadcasted_iota(jnp.int32, (t, 2 * dh), 1)

    def keys_with_block_column(b, j):
        onehot = jnp.where(key_lane == j, 1.0, 0.0).astype(BF16)
        return [jnp.concatenate([k_ref[b, j, :, lanes(2 * p)], onehot], axis=1) for p in range(A_HEADS // 2)]

    def pv(b, j, hd, p):
        vt_aug = jnp.concatenate([vt_ref[b, j, rows(hd), :], ones], axis=0)
        return _dot(vt_aug, p.astype(BF16))

    def own_unit(b, hd):
        def scores():
            return jnp.where(causal, _dot(k_ref[b, i, :, lanes(hd)], qm_ref[b, hd, 0:2 * dh, :]), NEG)

        def update(s, m0):
            m_ref[b, hd] = jnp.broadcast_to(m0, (SUBLANES, t))
            acc_ref[b, hd] = pv(b, i, hd, jnp.exp2(s - m0))

        return scores, update

    def past_units(b, j):
        keys = keys_with_block_column(b, j)

        def unit(hd):
            def scores():
                return _dot(keys[hd // 2], qm_ref[b, hd])

            def update(s, m_blk):
                m_old = m_ref[b, hd]
                m_new = jnp.maximum(m_old, m_blk)
                alpha = jnp.exp2(m_old - m_new)
                m_ref[b, hd] = m_new
                acc_ref[b, hd] = alpha[0:1, :] * acc_ref[b, hd] + pv(b, j, hd, jnp.exp2(s - m_new[0:1, :]))

            return scores, update

        return [unit(hd) for hd in range(A_HEADS)]

    def pipelined(units):
        blk_max = {}
        slots = MOBA_LOOKAHEAD + 1
        for step in range(len(units) + MOBA_LOOKAHEAD):
            if step < len(units):
                sc = units[step][0]()
                s_ref[step % slots] = sc
                blk_max[step] = jnp.max(sc, axis=0, keepdims=True)
            if step >= MOBA_LOOKAHEAD:
                u = step - MOBA_LOOKAHEAD
                units[u][1](s_ref[u % slots], blk_max.pop(u))

    n_groups = i // MOBA_UNROLL
    for left in range(MOBA_UNROLL):
        @pl.when(i % MOBA_UNROLL == left)
        def _(left=left):
            for b, hd in row_heads:
                select_blocks(b, hd)
            units = [own_unit(b, hd) for b, hd in row_heads]
            for r in range(left):
                for b in range(MOBA_ROWS):
                    units += past_units(b, n_groups * MOBA_UNROLL + r)
            pipelined(units)

    def group_body(jj, carry):
        pipelined([u for r in range(MOBA_UNROLL) for b in range(MOBA_ROWS)
                   for u in past_units(b, MOBA_UNROLL * jj + r)])
        return carry

    lax.fori_loop(0, n_groups, group_body, 0)

    for b in range(MOBA_ROWS):
        for hd in range(A_HEADS):
            ot_ref[rows(hd), :] = acc_ref[b, hd, 0:dh, :] / acc_ref[b, hd, dh:dh + 1, :]
        o_ref[b] = ot_ref[...].T.astype(BF16)


def _moba(qt, k, vt, kmean):
    b, nb, t, w = k.shape
    r = MOBA_ROWS
    assert b % r == 0
    return pl.pallas_call(
        _moba_kernel,
        grid=(b // r, nb),
        in_specs=[pl.BlockSpec((r, 1, w, t), lambda bi, i: (bi, i, 0, 0)),
                  pl.BlockSpec((r, nb, t, w), lambda bi, i: (bi, 0, 0, 0)),
                  pl.BlockSpec((r, nb, w, t), lambda bi, i: (bi, 0, 0, 0)),
                  pl.BlockSpec((r, nb, w), lambda bi, i: (bi, 0, 0))],
        out_specs=pl.BlockSpec((r, t, w), lambda bi, i: (bi, i, 0)),
        out_shape=jax.ShapeDtypeStruct((b, nb * t, w), BF16),
        scratch_shapes=[pltpu.VMEM((r, A_HEADS, 4 * A_HEAD_DIM, t), BF16),
                        pltpu.VMEM((r, A_HEADS, SUBLANES, t), F32),
                        pltpu.VMEM((r, A_HEADS, A_HEAD_DIM + MOBA_ONES_ROWS, t), F32),
                        pltpu.VMEM((w, t), F32),
                        pltpu.VMEM((MOBA_LOOKAHEAD + 1, t, t), F32)],
        compiler_params=_params("parallel", "parallel"),
        name="moba",
    )(qt, k, vt, kmean)


def _mix_kernel(chunk_dec, x_ref, g_ref, rq_ref, rkt_ref, rktd_ref, rv_ref, ya_ref, dmask_ref, qdec_ref,
                w_ref, wa_ref, wb_ref, wo_ref, *rest):
    (o_ref,), (state_ref, yb_ref) = _split_cast_refs(rest, 1, 2)

    @pl.when(pl.program_id(1) == 0)
    def _():
        state_ref[...] = jnp.zeros_like(state_ref)

    x = x_ref[0]
    d = x.shape[1]
    hb = _rms(x, g_ref[...]).astype(BF16)
    c = R_CHUNK
    per_blk = TOKEN_TILE // c
    n_chunks = MIX_BLOCKS * per_blk
    rows = lambda ci: slice(ci * c, (ci + 1) * c)
    qk = lambda hd: slice(hd * R_QK_DIM, (hd + 1) * R_QK_DIM)
    vv = lambda hd: slice(hd * R_V_DIM, (hd + 1) * R_V_DIM)
    units = [(ci, hd) for ci in range(n_chunks) for hd in range(R_HEADS)]
    keys_t = lambda ref, ci, hd: ref[0, ci // per_blk, qk(hd), rows(ci % per_blk)]

    att = {u: _dot(rq_ref[0, rows(u[0]), qk(u[1])], keys_t(rkt_ref, *u)) for u in units}
    kv = {u: _dot(keys_t(rktd_ref, *u), rv_ref[0, rows(u[0]), vv(u[1])]) for u in units}
    rg = _dot(hb, w_ref[:, 0:R_V_WIDTH])
    w = MIX_LATE_COLS
    late = ([lambda c=c: _dot(ya_ref[0], wa_ref[:, c:c + w]) for c in range(0, d, w)]
            + [lambda c=c: _dot(hb, w_ref[:, c:c + w])
               for c in range(R_V_WIDTH, R_V_WIDTH + 2 * d, w)])
    late_out = []

    cross = {}
    for hd in range(R_HEADS):
        st = state_ref[hd]
        for ci in range(n_chunks):
            cross[ci, hd] = _dot(rq_ref[0, rows(ci), qk(hd)], st.astype(BF16))
            st = chunk_dec[hd] * st + kv[ci, hd]
        state_ref[hd] = st

    gate = rg * _sigmoid(rg)
    for n, (ci, hd) in enumerate(units):
        inner = _dot((att[ci, hd] * dmask_ref[hd]).astype(BF16), rv_ref[0, rows(ci), vv(hd)])
        y = inner + cross[ci, hd] * qdec_ref[hd]
        mu = jnp.mean(y, axis=-1, keepdims=True)
        yc = y - mu
        var = jnp.mean(yc * yc, axis=-1, keepdims=True)
        yb_ref[rows(ci), vv(hd)] = (yc * lax.rsqrt(var + EPS) * gate[rows(ci), vv(hd)]).astype(BF16)
        for k in range(n * len(late) // len(units), (n + 1) * len(late) // len(units)):
            late_out.append(late[k]())

    pieces = d // w
    ta, ga, gb = (jnp.concatenate(late_out[k * pieces:(k + 1) * pieces], axis=1) for k in range(3))
    blocks = [slice(r * TOKEN_TILE, (r + 1) * TOKEN_TILE) for r in range(MIX_BLOCKS)]
    tb = [_dot(yb_ref[tok, :], wb_ref[...]) for tok in blocks]
    for tok, tb_blk in zip(blocks, tb):
        merged = _sigmoid(ga[tok]) * ta[tok] + _sigmoid(gb[tok]) * tb_blk
        o_ref[0, tok, :] = x[tok] + _dot(merged.astype(BF16), wo_ref[...])


def _mix(x, g, rq, rkt, rktd, rv, ya, dmask, qdec, chunk_dec, w, wa, wb, wo, cast_ahead):
    b, s, d = x.shape
    t = MIX_BLOCKS * TOKEN_TILE
    steps = s // t
    tok = lambda w: pl.BlockSpec((1, t, w), lambda bi, i: (bi, i, 0))
    blk_t = pl.BlockSpec((1, MIX_BLOCKS, R_QK_WIDTH, TOKEN_TILE), lambda bi, i: (bi, i, 0, 0))
    sliced, cast_specs, cast_shapes = _cast_ahead(cast_ahead, b * steps, lambda bi, i: bi * steps + i)
    out = pl.pallas_call(
        functools.partial(_mix_kernel, chunk_dec),
        grid=(b, steps),
        in_specs=[tok(d), _full((1, d)), tok(R_QK_WIDTH), blk_t, blk_t, tok(R_V_WIDTH), tok(A_WIDTH),
                  _full(dmask.shape), _full(qdec.shape),
                  _full(w.shape), _full(wa.shape), _full(wb.shape), _full(wo.shape)] + cast_specs,
        out_specs=[tok(d)] + cast_specs,
        out_shape=[jax.ShapeDtypeStruct((b, s, d), F32)] + cast_shapes,
        scratch_shapes=[pltpu.VMEM((R_HEADS, R_QK_DIM, R_V_DIM), F32),
                        pltpu.VMEM((t, R_V_WIDTH), BF16)],
        compiler_params=_params("parallel", "arbitrary", fused_input=9, n_inputs=13 + len(sliced)),
        name="mix",
    )(x, g, rq, rkt, rktd, rv, ya, dmask, qdec, w, wa, wb, wo, *sliced)
    return out[0], [o.reshape(w.shape) for o, w in zip(out[1:], cast_ahead)]


def _cross_kernel(x_ref, g_ref, m_ref, gm_ref, wkv_ref, wq_ref, wo_ref, *rest):
    (o_ref,), (qk_ref, vo_ref, p_ref) = _split_cast_refs(rest, 1, 3)
    d = x_ref.shape[2]
    dh = d // X_HEADS
    mlen = m_ref.shape[1]
    cols = lambda hd: slice(hd * dh, (hd + 1) * dh)
    mcols = lambda hd: slice(hd * mlen, (hd + 1) * mlen)

    @pl.when(pl.program_id(1) == 0)
    def _():
        mb = _rms(m_ref[0], gm_ref[...]).astype(BF16)
        kt = _dot(mb, wkv_ref[:, 0:d]).T.astype(BF16)
        v = _dot(mb, wkv_ref[:, d:]).astype(BF16)
        for hd in range(X_HEADS):
            qk_ref[:, mcols(hd)] = (_dot(wq_ref[:, cols(hd)], kt[cols(hd), :]) * dh ** -0.5).astype(BF16)
            vo_ref[mcols(hd), :] = _dot(v[:, cols(hd)], wo_ref[cols(hd), :]).astype(BF16)

    n_parts = CROSS_PARTS
    t = x_ref.shape[1] // n_parts
    part = lambda r: slice(r * t, (r + 1) * t)

    def scores_of(r):
        hb = _rms(x_ref[0, part(r), :], g_ref[...]).astype(BF16)
        return _dot(hb, qk_ref[...])

    nxt = scores_of(0)
    for r in range(n_parts):
        scores = nxt
        if r + 1 < n_parts:
            nxt = scores_of(r + 1)
        for hd in range(X_HEADS):
            s = scores[:, mcols(hd)]
            p = jnp.exp(s - jnp.max(s, axis=-1, keepdims=True))
            p_ref[part(r), mcols(hd)] = (p / jnp.sum(p, axis=-1, keepdims=True)).astype(BF16)
        o_ref[0, part(r), :] = x_ref[0, part(r), :] + _dot(p_ref[part(r), :], vo_ref[...])


def _cross(x, g, mem, gm, wkv, wq, wo, cast_ahead):
    b, s, d = x.shape
    m = mem.shape[1]
    t = ROW_TILE
    steps = s // t
    tok = pl.BlockSpec((1, t, d), lambda bi, i: (bi, i, 0))
    sliced, cast_specs, cast_shapes = _cast_ahead(cast_ahead, b * steps, lambda bi, i: bi * steps + i)
    out = pl.pallas_call(
        _cross_kernel,
        grid=(b, steps),
        in_specs=[tok, _full((1, d)), pl.BlockSpec((1, m, d), lambda bi, i: (bi, 0, 0)), _full((1, d)),
                  _full(wkv.shape), _full(wq.shape), _full(wo.shape)] + cast_specs,
        out_specs=[tok] + cast_specs,
        out_shape=[jax.ShapeDtypeStruct((b, s, d), F32)] + cast_shapes,
        scratch_shapes=[pltpu.VMEM((d, X_HEADS * m), BF16),
                        pltpu.VMEM((X_HEADS * m, d), BF16),
                        pltpu.VMEM((t, X_HEADS * m), BF16)],
        compiler_params=_params("parallel", "arbitrary"),
        name="cross",
    )(x, g, mem, gm, wkv, wq, wo, *sliced)
    return out[0], [o.reshape(w.shape) for o, w in zip(out[1:], cast_ahead)]


def _ffn_kernel(x_ref, g_ref, wg_ref, wu_ref, wd_ref, *rest):
    o_ref = rest[-1]
    x = x_ref[...]
    hb = _rms(x, g_ref[...]).astype(BF16)
    n_chunks = wg_ref.shape[1] // FF_CHUNK
    cols = lambda c: slice(c * FF_CHUNK, (c + 1) * FF_CHUNK)

    def up(c):
        return _dot(hb, wg_ref[:, cols(c)]), _dot(hb, wu_ref[:, cols(c)])

    acc = x
    nxt = up(0)
    for c in range(n_chunks):
        gate, upv = nxt
        if c + 1 < n_chunks:
            nxt = up(c + 1)
        act = gate * _sigmoid(gate) * upv
        acc = acc + _dot(act.astype(BF16), wd_ref[cols(c), :])
    o_ref[...] = _rms(acc, rest[0][...]) if len(rest) == 2 else acc


def _ffn(x2, g, wg, wu, wd, final_gain=None):
    n, d = x2.shape
    t = ROW_TILE
    tok = pl.BlockSpec((t, d), lambda i: (i, 0))
    final = () if final_gain is None else (final_gain,)
    return pl.pallas_call(
        _ffn_kernel,
        grid=(n // t,),
        in_specs=[tok, _full((1, d)), _full(wg.shape), _full(wu.shape), _full(wd.shape)] + [_full((1, d))] * len(final),
        out_specs=tok,
        out_shape=jax.ShapeDtypeStruct((n, d), F32),
        compiler_params=_params("parallel"),
        name="ffn",
    )(x2, g, wg, wu, wd, *final)


def _rope_tables(s):
    def cs(dim):
        inv = ROPE_THETA ** (-np.arange(0, dim, 2, dtype=np.float64) / dim)
        ang = np.arange(s, dtype=np.float64)[:, None] * inv[None, :]
        return np.cos(ang), np.sin(ang)

    ca, sa = cs(A_HEAD_DIM)
    cr, sr = cs(R_QK_DIM)
    sign = lambda half: np.tile(np.repeat([-1.0, 1.0], half), LANES // (2 * half))[None, :]
    tabs = (np.tile(ca, (1, 4)), np.tile(sa, (1, 4)) * sign(A_HEAD_DIM // 2),
            np.tile(cr, (1, 2)), np.tile(sr, (1, 2)) * sign(R_QK_DIM // 2),
            ca.T, sa.T, cr.T, sr.T)
    return tuple(jnp.asarray(np.ascontiguousarray(t), F32) for t in tabs)


def _decay_tables(s):
    c = R_CHUNK
    log_g = np.log(1.0 - np.exp2(-5.0 - np.arange(R_HEADS, dtype=np.float64)))
    i = np.arange(c, dtype=np.float64)
    diff = i[:, None] - i[None, :]
    dmask = np.where(diff >= 0, np.exp(log_g[:, None, None] * np.maximum(diff, 0.0)), 0.0)
    q_dec = np.exp(log_g[:, None] * (i + 1.0))
    k_dec = np.exp(log_g[:, None] * (c - 1.0 - i))
    kdec_t = np.zeros((SUBLANES, s), np.float64)
    kdec_t[:R_HEADS] = np.tile(k_dec, (1, s // c))
    qdec = np.broadcast_to(q_dec[:, :, None], (R_HEADS, c, R_V_DIM))
    chunk_dec = tuple(float(v) for v in np.exp(log_g * c))
    return (jnp.asarray(dmask, F32), jnp.asarray(qdec, F32), jnp.asarray(kdec_t, F32), chunk_dec)


def kernel(x, mem, norm_mix, w_in, w_branch_a, w_branch_b, w_out, norm_cross, norm_mem,
           w_xq, w_xkv, w_xo, norm_ffn, w_gate, w_up, w_down, norm_final):
    b, s, d = x.shape
    steps = (TOKEN_TILE * QKV_BLOCKS, TOKEN_TILE * MIX_BLOCKS, ROW_TILE)
    assert all(s % t == 0 for t in steps) and b % MOBA_ROWS == 0 and d % LANES == 0
    rope = _rope_tables(s)
    dmask, qdec, kdec_t, chunk_dec = _decay_tables(s)
    row = lambda v: v.reshape(1, d)
    assert w_in.shape[2] == COL_GATES + R_V_WIDTH + 2 * d

    for l in range(w_in.shape[0]):
        w = w_in[l][:, :COL_GATES].astype(BF16)
        w_gates = w_in[l][:, COL_GATES:].astype(BF16)
        (k_a, kmean, qt_a, vt_a, rq, rv, rkt, rktd), (wa, wb, wo) = _qkv(
            x, row(norm_mix[l]), w, rope + (kdec_t,), (w_branch_a[l], w_branch_b[l], w_out[l]))
        y_a = _moba(qt_a, k_a, vt_a, kmean.reshape(b, s // MOBA_BLOCK, A_WIDTH))
        x, (wxkv, wxq, wxo) = _mix(x, row(norm_mix[l]), rq, rkt, rktd, rv, y_a, dmask, qdec, chunk_dec,
                                   w_gates, wa, wb, wo, (w_xkv[l], w_xq[l], w_xo[l]))
        x, (wg, wu, wd) = _cross(x, row(norm_cross[l]), mem, row(norm_mem[l]), wxkv, wxq, wxo,
                                 (w_gate[l], w_up[l], w_down[l]))
        last = l == w_in.shape[0] - 1
        x = _ffn(x.reshape(b * s, d), row(norm_ffn[l]), wg, wu, wd,
                 row(norm_final) if last else None).reshape(b, s, d)
    return x
```
